```python
import math
import numpy as np
import jax
import jax.numpy as jnp
from jax import lax

D_MODEL = 1024
BATCH = 32
SEQ = 2048
DEPTH = 2

HEAD_DIM = 64
Q_BLOCK = 128
SB_HEADS = 8
SB_WIDTH = SB_HEADS * HEAD_DIM
NSA_HEADS = 8
NSA_KV_GROUPS = 2
NSA_REP = NSA_HEADS // NSA_KV_GROUPS
NSA_WIDTH = NSA_HEADS * HEAD_DIM
NSA_KV_WIDTH = NSA_KV_GROUPS * HEAD_DIM
NSA_N_BRANCH = 3
CMP_LEN = 32
CMP_STRIDE = 16
CMP_HIDDEN = 256
SLC_LEN = 64
SLC_TOPN = 8
SLC_LOCAL = 2
SLC_FORCE_BONUS = 1e4
WINDOW = 512
ROPE_THETA = 500000.0
ROPE_DIM = HEAD_DIM // 4
SSM_GROUP = 16
SSM_GROUPS = D_MODEL // SSM_GROUP
SSM_STATE = 64
DT_MIN = 1e-3
DT_MAX = 1e-1
N_EXPERTS = 64
N_EXPERT_GROUPS = 8
TOPK_GROUPS = 4
TOP_K = 6
EXPERT_HIDDEN = 256
SHARED_HIDDEN = 256
ROUTED_SCALE = 2.5
MOE_BLOCK = 128
DN_ALPHA = (2 * DEPTH) ** 0.25
DN_BETA = (8 * DEPTH) ** -0.25
LN_EPS = 1e-5
NEG_INF = -1e30

kernel_name = "hybrid_sb_nsa_s5_moe_deepnorm"


def _mix0_layout():
    sizes = [SB_WIDTH] * 3 + [NSA_WIDTH] + [NSA_KV_WIDTH] * 6 + [NSA_HEADS * NSA_N_BRANCH]
    value_cols = (2, 5, 7, 9)
    scales = [DN_BETA if i in value_cols else 1.0 for i in range(len(sizes))]
    return sizes, scales


def layer_norm(x, g, b):
    xf = x.astype(jnp.float32)
    mu = jnp.mean(xf, axis=-1, keepdims=True)
    var = jnp.mean(jnp.square(xf - mu), axis=-1, keepdims=True)
    y = (xf - mu) * lax.rsqrt(var + LN_EPS)
    return (y * g.astype(jnp.float32) + b.astype(jnp.float32)).astype(x.dtype)


def masked_softmax(s, mask):
    s = jnp.where(mask, s.astype(jnp.float32), NEG_INF)
    p = jax.nn.softmax(s, axis=-1)
    return jnp.where(mask, p, 0.0)


def rope_tables(S):
    inv = jnp.power(ROPE_THETA, -jnp.arange(0, ROPE_DIM, 2, dtype=jnp.float32) / ROPE_DIM)
    ang = jnp.arange(S, dtype=jnp.float32)[:, None] * inv[None, :]
    return jnp.cos(ang), jnp.sin(ang)


def rope_partial(x, cos, sin):
    half = ROPE_DIM // 2
    c = cos.astype(x.dtype)
    s = sin.astype(x.dtype)
    x1 = x[..., :half]
    x2 = x[..., half:ROPE_DIM]
    return jnp.concatenate([x1 * c - x2 * s, x2 * c + x1 * s, x[..., ROPE_DIM:]], axis=-1)


def hybrid_attention_mixer(x, w_in, cmp_pos_k, cmp_pos_v, cmp_k_w1, cmp_k_w2,
                           cmp_v_w1, cmp_v_w2, w_out):
    B, S, _ = x.shape
    dt = x.dtype
    G, R, d = NSA_KV_GROUPS, NSA_REP, HEAD_DIM
    nq = S // Q_BLOCK
    scale = HEAD_DIM ** -0.5
    sizes, _ = _mix0_layout()
    splits = np.cumsum(sizes)[:-1].tolist()
    proj = x @ w_in
    (q_a, k_a, v_a, q_b, kc, vc, ks, vs, kw, vw, g_raw) = jnp.split(proj, splits, axis=-1)

    def heads(t, n):
        return t.reshape(B, S, n, d).transpose(0, 2, 1, 3)

    q_a, k_a, v_a = heads(q_a, SB_HEADS), heads(k_a, SB_HEADS), heads(v_a, SB_HEADS)
    qa_blocks = q_a.reshape(B, SB_HEADS, nq, Q_BLOCK, d).transpose(2, 0, 1, 3, 4)
    key_idx = jnp.arange(S)

    def sb_block(args):
        q_blk, i = args
        t = i * Q_BLOCK + jnp.arange(Q_BLOCK)
        z = jnp.einsum('bhqd,bhkd->bhqk', q_blk, k_a).astype(jnp.float32) * scale
        causal = key_idx[None, :] < t[:, None]
        log_1m = jnp.where(causal, -jax.nn.softplus(z), 0.0)
        suffix = lax.cumsum(log_1m, axis=z.ndim - 1, reverse=True) - log_1m
        w = jnp.where(causal, jnp.exp(jax.nn.log_sigmoid(z) + suffix), 0.0)
        return jnp.einsum('bhqk,bhkd->bhqd', w.astype(dt), v_a)

    o_a = lax.map(sb_block, (qa_blocks, jnp.arange(nq)))
    o_a = o_a.transpose(1, 0, 3, 2, 4).reshape(B, S, SB_WIDTH)

    cos, sin = rope_tables(S)
    qn = rope_partial(heads(q_b, NSA_HEADS), cos, sin).reshape(B, G, R, S, d)
    qn_blocks = qn.reshape(B, G, R, nq, Q_BLOCK, d).transpose(3, 0, 1, 2, 4, 5)
    gates = jax.nn.sigmoid(g_raw).reshape(B, S, NSA_HEADS, NSA_N_BRANCH).transpose(0, 2, 1, 3)
    gate_blocks = gates.reshape(B, G, R, nq, Q_BLOCK, NSA_N_BRANCH).transpose(3, 0, 1, 2, 4, 5)

    n_cmp = (S - CMP_LEN) // CMP_STRIDE + 1
    n_slc = S // SLC_LEN
    cmp_start = np.arange(n_cmp) * CMP_STRIDE
    cmp_stop = cmp_start + CMP_LEN - 1
    slc_start = np.arange(n_slc) * SLC_LEN
    slc_stop = slc_start + SLC_LEN - 1
    cmp_idx = cmp_start[:, None] + np.arange(CMP_LEN)[None, :]
    overlap = jnp.asarray(((cmp_start[:, None] <= slc_stop[None, :]) &
                           (cmp_stop[:, None] >= slc_start[None, :])).astype(np.float32))
    cmp_end = jnp.asarray(cmp_stop)
    kc_tok = rope_partial(heads(kc, G), cos, sin)
    vc_tok = heads(vc, G)
    kc_blk = (kc_tok[:, :, cmp_idx] + cmp_pos_k).reshape(B, G, n_cmp, CMP_LEN * d)
    vc_blk = (vc_tok[:, :, cmp_idx] + cmp_pos_v).reshape(B, G, n_cmp, CMP_LEN * d)
    k_cmp = jax.nn.gelu(kc_blk @ cmp_k_w1) @ cmp_k_w2
    v_cmp = jax.nn.gelu(vc_blk @ cmp_v_w1) @ cmp_v_w2

    k_slc = rope_partial(heads(ks, G), cos, sin).reshape(B, G, n_slc, SLC_LEN, d)
    v_slc = heads(vs, G).reshape(B, G, n_slc, SLC_LEN, d)
    n_top = min(SLC_TOPN, n_slc)
    blk_ids = jnp.arange(n_slc)
    bi = jnp.arange(B)[:, None, None, None]
    gi = jnp.arange(G)[None, :, None, None]

    pad = ((0, 0), (0, 0), (WINDOW, 0), (0, 0))
    k_win_pad = jnp.pad(rope_partial(heads(kw, G), cos, sin), pad)
    v_win_pad = jnp.pad(heads(vw, G), pad)

    def nsa_block(args):
        q_blk, g_blk, i = args
        t = i * Q_BLOCK + jnp.arange(Q_BLOCK)
        s_c = jnp.einsum('bgrqd,bgnd->bgrqn', q_blk, k_cmp) * scale
        p_c = masked_softmax(s_c, cmp_end[None, :] <= t[:, None])
        o_c = jnp.einsum('bgrqn,bgnd->bgrqd', p_c.astype(dt), v_cmp)
        imp = jnp.einsum('bgrqn,nm->bgqm', p_c, overlap)
        cur = t // SLC_LEN
        valid = blk_ids[None, :] <= cur[:, None]
        forced = (blk_ids[None, :] == 0) | (valid & (blk_ids[None, :] > cur[:, None] - SLC_LOCAL))
        imp = jnp.where(forced, imp + SLC_FORCE_BONUS, jnp.where(valid, imp, NEG_INF))
        _, sel = lax.top_k(imp, n_top)
        k_sel = k_slc[bi, gi, sel]
        v_sel = v_slc[bi, gi, sel]
        s_s = jnp.einsum('bgrqd,bgqnld->bgrqnl', q_blk, k_sel) * scale
        key_pos = sel[..., None] * SLC_LEN + jnp.arange(SLC_LEN)
        m_s = (key_pos <= t[None, None, :, None, None]).reshape(B, G, 1, Q_BLOCK, n_top * SLC_LEN)
        p_s = masked_softmax(s_s.reshape(B, G, R, Q_BLOCK, n_top * SLC_LEN), m_s)
        p_s = p_s.reshape(B, G, R, Q_BLOCK, n_top, SLC_LEN).astype(dt)
        o_s = jnp.einsum('bgrqnl,bgqnld->bgrqd', p_s, v_sel)
        k_w = lax.dynamic_slice_in_dim(k_win_pad, i * Q_BLOCK, Q_BLOCK + WINDOW, axis=2)
        v_w = lax.dynamic_slice_in_dim(v_win_pad, i * Q_BLOCK, Q_BLOCK + WINDOW, axis=2)
        kp = i * Q_BLOCK - WINDOW + jnp.arange(Q_BLOCK + WINDOW)
        m_w = (kp[None, :] <= t[:, None]) & (kp[None, :] > t[:, None] - WINDOW) & (kp[None, :] >= 0)
        s_w = jnp.einsum('bgrqd,bgkd->bgrqk', q_blk, k_w) * scale
        p_w = masked_softmax(s_w, m_w)
        o_w = jnp.einsum('bgrqk,bgkd->bgrqd', p_w.astype(dt), v_w)
        return g_blk[..., 0:1] * o_c + g_blk[..., 1:2] * o_s + g_blk[..., 2:3] * o_w

    o_b = lax.map(nsa_block, (qn_blocks, gate_blocks, jnp.arange(nq)))
    o_b = o_b.transpose(1, 0, 4, 2, 3, 5).reshape(B, S, NSA_WIDTH)

    return jnp.concatenate([o_a, o_b], axis=-1) @ w_out


def s5_mixer(x, w_in, lambda_re, lambda_im, b_re, b_im, c_re, c_im, d_skip, log_dt,
             w_glu, w_out):
    B, S, _ = x.shape
    f32 = jnp.float32
    u = (x @ w_in).astype(f32).reshape(B, S, SSM_GROUPS, SSM_GROUP)
    lre = lambda_re.astype(f32)
    lim = lambda_im.astype(f32)
    step = jnp.exp(log_dt.astype(f32))[:, None]
    mag = jnp.exp(lre * step)
    abar_re = mag * jnp.cos(lim * step)
    abar_im = mag * jnp.sin(lim * step)
    den = lre * lre + lim * lim
    zoh_re = ((abar_re - 1.0) * lre + abar_im * lim) / den
    zoh_im = (abar_im * lre - (abar_re - 1.0) * lim) / den
    bre, bim = b_re.astype(f32), b_im.astype(f32)
    bbar_re = zoh_re[..., None] * bre - zoh_im[..., None] * bim
    bbar_im = zoh_re[..., None] * bim + zoh_im[..., None] * bre
    cre, cim = c_re.astype(f32), c_im.astype(f32)
    dsk = d_skip.astype(f32)

    def combine(e1, e2):
        a1r, a1i, b1r, b1i = e1
        a2r, a2i, b2r, b2i = e2
        return (a2r * a1r - a2i * a1i, a2r * a1i + a2i * a1r,
                a2r * b1r - a2i * b1i + b2r, a2r * b1i + a2i * b1r + b2i)

    def scan_one(u_b):
        bu_re = jnp.einsum('sgh,gph->sgp', u_b, bbar_re)
        bu_im = jnp.einsum('sgh,gph->sgp', u_b, bbar_im)
        a_re = jnp.broadcast_to(abar_re, bu_re.shape)
        a_im = jnp.broadcast_to(abar_im, bu_im.shape)
        _, _, h_re, h_im = lax.associative_scan(combine, (a_re, a_im, bu_re, bu_im), axis=0)
        return (jnp.einsum('ghp,sgp->sgh', cre, h_re) - jnp.einsum('ghp,sgp->sgh', cim, h_im)
                + dsk * u_b)

    y = lax.map(scan_one, u).reshape(B, S, D_MODEL)
    y = jax.nn.gelu(y).astype(x.dtype)
    y = y * jax.nn.sigmoid(y @ w_glu)
    return y @ w_out


def moe_ffn(x, router, router_bias, w_gate, w_up, w_down, sh_gate, sh_up, sh_down):
    B, S, D = x.shape
    T = B * S
    xf = x.reshape(T, D)
    scores = jax.nn.sigmoid((xf @ router).astype(jnp.float32))
    biased = scores + router_bias.astype(jnp.float32)
    per_grp = N_EXPERTS // N_EXPERT_GROUPS
    grp_score = lax.top_k(biased.reshape(T, N_EXPERT_GROUPS, per_grp), 2)[0].sum(-1)
    _, top_grp = lax.top_k(grp_score, TOPK_GROUPS)
    grp_keep = jnp.any(top_grp[:, :, None] == jnp.arange(N_EXPERT_GROUPS)[None, None, :], axis=1)
    expert_keep = jnp.repeat(grp_keep, per_grp, axis=1)
    _, eidx = lax.top_k(jnp.where(expert_keep, biased, -jnp.inf), TOP_K)
    gate = jnp.take_along_axis(scores, eidx, axis=1)
    gate = gate / jnp.sum(gate, axis=-1, keepdims=True) * ROUTED_SCALE

    n_assign = T * TOP_K
    flat_e = eidx.reshape(-1)
    flat_tok = jnp.repeat(jnp.arange(T, dtype=jnp.int32), TOP_K)
    flat_g = gate.reshape(-1)
    order = jnp.argsort(flat_e)
    se, stok, sg = flat_e[order], flat_tok[order], flat_g[order]
    counts = jnp.bincount(flat_e, length=N_EXPERTS)
    padded = (counts + MOE_BLOCK - 1) // MOE_BLOCK * MOE_BLOCK
    pad_end = jnp.cumsum(padded)
    pad_start = pad_end - padded
    start = jnp.cumsum(counts) - counts
    dest = pad_start[se] + jnp.arange(n_assign) - start[se]
    n_blocks = n_assign // MOE_BLOCK + N_EXPERTS
    n_rows = n_blocks * MOE_BLOCK
    buf_tok = jnp.zeros((n_rows,), jnp.int32).at[dest].set(stok)
    buf_g = jnp.zeros((n_rows,), jnp.float32).at[dest].set(sg)
    blk_e = jnp.minimum(jnp.searchsorted(pad_end, jnp.arange(n_blocks) * MOE_BLOCK, side='right'),
                        N_EXPERTS - 1)

    def expert_block(args):
        tok, g, e = args
        h = xf[tok]
        a = jax.nn.silu(h @ w_gate[e]) * (h @ w_up[e])
        return (a @ w_down[e]) * g[:, None].astype(x.dtype)

    y_blk = lax.map(expert_block, (buf_tok.reshape(n_blocks, MOE_BLOCK),
                                   buf_g.reshape(n_blocks, MOE_BLOCK), blk_e))
    routed = jnp.zeros_like(xf).at[buf_tok].add(y_blk.reshape(n_rows, D))
    shared = (jax.nn.silu(xf @ sh_gate) * (xf @ sh_up)) @ sh_down
    return (routed + shared).reshape(B, S, D)


def setup_inputs(seed: int = 0) -> dict:
    key = jax.random.key(seed)
    keys = iter(jax.random.split(key, 64))

    def nrm(shape, scale):
        return jax.random.normal(next(keys), shape, jnp.float32) * scale

    def add_ln(p, name):
        p[name + "_g"] = 1.0 + nrm((D_MODEL,), 0.01)
        p[name + "_b"] = nrm((D_MODEL,), 0.01)

    def add_moe(p, pre):
        p[pre + "_router"] = nrm((D_MODEL, N_EXPERTS), D_MODEL ** -0.5)
        p[pre + "_router_bias"] = nrm((N_EXPERTS,), 0.01)
        p[pre + "_w_gate"] = nrm((N_EXPERTS, D_MODEL, EXPERT_HIDDEN), D_MODEL ** -0.5)
        p[pre + "_w_up"] = nrm((N_EXPERTS, D_MODEL, EXPERT_HIDDEN), D_MODEL ** -0.5)
        p[pre + "_w_down"] = nrm((N_EXPERTS, EXPERT_HIDDEN, D_MODEL), EXPERT_HIDDEN ** -0.5 * DN_BETA)
        p[pre + "_sh_gate"] = nrm((D_MODEL, SHARED_HIDDEN), D_MODEL ** -0.5)
        p[pre + "_sh_up"] = nrm((D_MODEL, SHARED_HIDDEN), D_MODEL ** -0.5)
        p[pre + "_sh_down"] = nrm((SHARED_HIDDEN, D_MODEL), SHARED_HIDDEN ** -0.5 * DN_BETA)

    p = {}
    p["x"] = nrm((BATCH, SEQ, D_MODEL), 1.0)
    sizes, scales = _mix0_layout()
    col_scale = jnp.concatenate([jnp.full((n,), s, jnp.float32) for n, s in zip(sizes, scales)])
    p["l0_w_in"] = nrm((D_MODEL, sum(sizes)), D_MODEL ** -0.5) * col_scale
    p["l0_cmp_pos_k"] = nrm((CMP_LEN, HEAD_DIM), 0.1)
    p["l0_cmp_pos_v"] = nrm((CMP_LEN, HEAD_DIM), 0.1)
    p["l0_cmp_k_w1"] = nrm((CMP_LEN * HEAD_DIM, CMP_HIDDEN), (CMP_LEN * HEAD_DIM) ** -0.5)
    p["l0_cmp_k_w2"] = nrm((CMP_HIDDEN, HEAD_DIM), CMP_HIDDEN ** -0.5)
    p["l0_cmp_v_w1"] = nrm((CMP_LEN * HEAD_DIM, CMP_HIDDEN), (CMP_LEN * HEAD_DIM) ** -0.5)
    p["l0_cmp_v_w2"] = nrm((CMP_HIDDEN, HEAD_DIM), CMP_HIDDEN ** -0.5)
    p["l0_w_out"] = nrm((SB_WIDTH + NSA_WIDTH, D_MODEL), (SB_WIDTH + NSA_WIDTH) ** -0.5 * DN_BETA)
    add_ln(p, "l0_ln1")
    add_moe(p, "l0")
    add_ln(p, "l0_ln2")
    p["l1_w_in"] = nrm((D_MODEL, D_MODEL), D_MODEL ** -0.5)
    p["l1_lambda_re"] = -0.5 + nrm((SSM_GROUPS, SSM_STATE), 0.02)
    p["l1_lambda_im"] = math.pi * jnp.arange(SSM_STATE, dtype=jnp.float32)[None, :] + nrm((SSM_GROUPS, SSM_STATE), 0.02)
    p["l1_b_re"] = nrm((SSM_GROUPS, SSM_STATE, SSM_GROUP), (2 * SSM_GROUP) ** -0.5)
    p["l1_b_im"] = nrm((SSM_GROUPS, SSM_STATE, SSM_GROUP), (2 * SSM_GROUP) ** -0.5)
    p["l1_c_re"] = nrm((SSM_GROUPS, SSM_GROUP, SSM_STATE), (2 * SSM_STATE) ** -0.5)
    p["l1_c_im"] = nrm((SSM_GROUPS, SSM_GROUP, SSM_STATE), (2 * SSM_STATE) ** -0.5)
    p["l1_d"] = nrm((SSM_GROUPS, SSM_GROUP), 1.0)
    p["l1_log_dt"] = jax.random.uniform(next(keys), (SSM_GROUPS,), jnp.float32,
                                        math.log(DT_MIN), math.log(DT_MAX))
    p["l1_w_glu"] = nrm((D_MODEL, D_MODEL), D_MODEL ** -0.5)
    p["l1_w_out"] = nrm((D_MODEL, D_MODEL), D_MODEL ** -0.5 * DN_BETA)
    add_ln(p, "l1_ln1")
    add_moe(p, "l1")
    add_ln(p, "l1_ln2")
    return p


def reference(x, l0_w_in, l0_cmp_pos_k, l0_cmp_pos_v, l0_cmp_k_w1, l0_cmp_k_w2, l0_cmp_v_w1,
              l0_cmp_v_w2, l0_w_out, l0_ln1_g, l0_ln1_b, l0_router, l0_router_bias, l0_w_gate,
              l0_w_up, l0_w_down, l0_sh_gate, l0_sh_up, l0_sh_down, l0_ln2_g, l0_ln2_b,
              l1_w_in, l1_lambda_re, l1_lambda_im, l1_b_re, l1_b_im, l1_c_re, l1_c_im, l1_d,
              l1_log_dt, l1_w_glu, l1_w_out, l1_ln1_g, l1_ln1_b, l1_router, l1_router_bias,
              l1_w_gate, l1_w_up, l1_w_down, l1_sh_gate, l1_sh_up, l1_sh_down, l1_ln2_g,
              l1_ln2_b):
    mixer_params = [
        (l0_w_in, l0_cmp_pos_k, l0_cmp_pos_v, l0_cmp_k_w1, l0_cmp_k_w2, l0_cmp_v_w1,
         l0_cmp_v_w2, l0_w_out),
        (l1_w_in, l1_lambda_re, l1_lambda_im, l1_b_re, l1_b_im, l1_c_re, l1_c_im, l1_d,
         l1_log_dt, l1_w_glu, l1_w_out),
    ]
    norm1 = [(l0_ln1_g, l0_ln1_b), (l1_ln1_g, l1_ln1_b)]
    moe_params = [
        (l0_router, l0_router_bias, l0_w_gate, l0_w_up, l0_w_down, l0_sh_gate, l0_sh_up, l0_sh_down),
        (l1_router, l1_router_bias, l1_w_gate, l1_w_up, l1_w_down, l1_sh_gate, l1_sh_up, l1_sh_down),
    ]
    norm2 = [(l0_ln2_g, l0_ln2_b), (l1_ln2_g, l1_ln2_b)]
    h = x
    for layer in range(DEPTH):
        if layer % 2 == 0:
            mixed = hybrid_attention_mixer(h, *mixer_params[layer])
        else:
            mixed = s5_mixer(h, *mixer_params[layer])
        h = layer_norm(DN_ALPHA * h + mixed, *norm1[layer])
        h = layer_norm(DN_ALPHA * h + moe_ffn(h, *moe_params[layer]), *norm2[layer])
    return h
```

```python
import functools
import math

import numpy as np
import jax
import jax.numpy as jnp
from jax import lax
from jax.experimental import pallas as pl
from jax.experimental.pallas import tpu as pltpu

F32 = jnp.float32
BF = jnp.bfloat16

D_MODEL = 1024
DEPTH = 2
HEAD_DIM = 64
LANES = 128
SB_HEADS = 8
SB_WIDTH = SB_HEADS * HEAD_DIM
NSA_HEADS = 8
NSA_KV_GROUPS = 2
NSA_REP = NSA_HEADS // NSA_KV_GROUPS
NSA_WIDTH = NSA_HEADS * HEAD_DIM
NSA_N_BRANCH = 3
CMP_LEN = 32
CMP_STRIDE = 16
CMP_HIDDEN = 256
SLC_LEN = 64
SLC_TOPN = 8
SLC_LOCAL = 2
SLC_FORCE_BONUS = 1e4
WINDOW = 512
ROPE_THETA = 500000.0
ROPE_DIM = HEAD_DIM // 4
SSM_GROUP = 16
SSM_GROUPS = D_MODEL // SSM_GROUP
SSM_STATE = 64
N_EXPERTS = 64
N_EXPERT_GROUPS = 8
TOPK_GROUPS = 4
TOP_K = 6
EXPERT_HIDDEN = 256
SHARED_HIDDEN = 256
ROUTED_SCALE = 2.5
DN_ALPHA = (2 * DEPTH) ** 0.25
LN_EPS = 1e-5
NEG_INF = -1e30
ATT_SCALE = HEAD_DIM ** -0.5

VMEM_LIMIT = 56 * 1024 * 1024


def _cparams(sem):
    return pltpu.CompilerParams(dimension_semantics=sem, vmem_limit_bytes=VMEM_LIMIT)


def _dot(a, b):
    return jnp.dot(a, b, preferred_element_type=F32)


def _dot_nt(a, b):
    return lax.dot_general(a, b, (((1,), (1,)), ((), ())), preferred_element_type=F32)


def _split2(x):
    hi = x.astype(BF)
    lo = (x - hi.astype(F32)).astype(BF)
    return hi, lo


def _layer_norm(h, g, b):
    mu = jnp.mean(h, axis=-1, keepdims=True)
    d = h - mu
    var = jnp.mean(d * d, axis=-1, keepdims=True)
    return d * lax.rsqrt(var + LN_EPS) * g + b


N_PLAIN = 3 * SB_WIDTH + 3 * LANES + LANES
N_ROPE = NSA_WIDTH + 3 * LANES


def _proj0_body(x_ref, wp_ref, wr_ref, wrr_ref, cos_ref, sin_ref,
                qkva_ref, vals_ref, gate_ref, qb_ref, kr_ref):
    xb = x_ref[...].astype(BF)
    a0, a1, a2 = 3 * SB_WIDTH, 3 * SB_WIDTH + 3 * LANES, N_PLAIN
    qkva_ref[...] = _dot(xb, wp_ref[:, 0:a0]).astype(BF)
    vals_ref[...] = _dot(xb, wp_ref[:, a0:a1]).astype(BF)
    gate_ref[...] = jax.nn.sigmoid(_dot(xb, wp_ref[:, a1:a2]))
    y = _dot(xb, wr_ref[...]) * cos_ref[...] + _dot(xb, wrr_ref[...]) * sin_ref[...]
    qb_ref[...] = y[:, 0:NSA_WIDTH].astype(BF)
    kr_ref[...] = y[:, NSA_WIDTH:N_ROPE].astype(BF)


def _proj0(x2, wp, wr, wrr, cos_t, sin_t, seq):
    T = x2.shape[0]
    tm = 512
    nseq = seq // tm
    row = lambda i: (i, 0)
    full = lambda i: (0, 0)
    tab = lambda i: (i % nseq, 0)
    return pl.pallas_call(
        _proj0_body,
        grid=(T // tm,),
        in_specs=[
            pl.BlockSpec((tm, D_MODEL), row),
            pl.BlockSpec((D_MODEL, N_PLAIN), full),
            pl.BlockSpec((D_MODEL, N_ROPE), full),
            pl.BlockSpec((D_MODEL, N_ROPE), full),
            pl.BlockSpec((tm, N_ROPE), tab),
            pl.BlockSpec((tm, N_ROPE), tab),
        ],
        out_specs=[
            pl.BlockSpec((tm, 3 * SB_WIDTH), row),
            pl.BlockSpec((tm, 3 * LANES), row),
            pl.BlockSpec((tm, LANES), row),
            pl.BlockSpec((tm, NSA_WIDTH), row),
            pl.BlockSpec((tm, 3 * LANES), row),
        ],
        out_shape=[
            jax.ShapeDtypeStruct((T, 3 * SB_WIDTH), BF),
            jax.ShapeDtypeStruct((T, 3 * LANES), BF),
            jax.ShapeDtypeStruct((T, LANES), F32),
            jax.ShapeDtypeStruct((T, NSA_WIDTH), BF),
            jax.ShapeDtypeStruct((T, 3 * LANES), BF),
        ],
        compiler_params=_cparams(("parallel",)),
        name="proj0",
    )(x2, wp, wr, wrr, cos_t, sin_t)


def _sb_body(q_ref, k_ref, v_ref, o_ref, *, tq):
    i = pl.program_id(2)
    q = q_ref[0]
    lane = lax.broadcasted_iota(jnp.int32, (1, LANES), 1)
    row = lax.broadcasted_iota(jnp.int32, (tq, tq), 0)
    col = lax.broadcasted_iota(jnp.int32, (tq, tq), 1)
    tri = jnp.where(row > col, 1.0, 0.0).astype(BF)
    diag_causal = col < row

    def block(j, qh, carry, acc, masked):
        off = pl.multiple_of(j * tq, tq)
        k = k_ref[0, pl.ds(off, tq), :]
        v = v_ref[0, pl.ds(off, tq), :]
        z = _dot_nt(qh, k) * ATT_SCALE
        soft = jnp.log(1.0 + jnp.exp(-jnp.abs(z)))
        log_1m = -(jnp.maximum(z, 0.0) + soft)
        log_sig = jnp.minimum(z, 0.0) - soft
        if masked:
            log_1m = jnp.where(diag_causal, log_1m, 0.0)
        hi, lo = _split2(log_1m)
        suffix = _dot(hi, tri) + _dot(lo, tri)
        w = jnp.exp(log_sig + suffix + carry)
        if masked:
            w = jnp.where(diag_causal, w, 0.0)
        acc = acc + _dot(w.astype(BF), v)
        carry = carry + jnp.sum(log_1m, axis=-1, keepdims=True)
        return carry, acc

    out = jnp.zeros((tq, LANES), F32)
    for hh in range(2):
        hm = (lane // HEAD_DIM) == hh
        qh = jnp.where(hm, q, jnp.zeros_like(q))
        carry, acc = block(i, qh, jnp.zeros((tq, 1), F32), jnp.zeros((tq, LANES), F32), True)

        def body(jj, c, qh=qh):
            return block(i - 1 - jj, qh, c[0], c[1], False)

        carry, acc = lax.fori_loop(0, i, body, (carry, acc))
        out = jnp.where(hm, acc, out)
    o_ref[0] = out.astype(BF)


def _sb_attention(qkva3):
    B, S, _ = qkva3.shape
    tq = 256
    npair = SB_WIDTH // LANES
    return pl.pallas_call(
        functools.partial(_sb_body, tq=tq),
        grid=(B, npair, S // tq),
        in_specs=[
            pl.BlockSpec((1, tq, LANES), lambda b, p, i: (b, i, p)),
            pl.BlockSpec((1, S, LANES), lambda b, p, i: (b, 0, npair + p)),
            pl.BlockSpec((1, S, LANES), lambda b, p, i: (b, 0, 2 * npair + p)),
        ],
        out_specs=pl.BlockSpec((1, tq, LANES), lambda b, p, i: (b, i, p)),
        out_shape=jax.ShapeDtypeStruct((B, S, SB_WIDTH), BF),
        compiler_params=_cparams(("parallel", "parallel", "arbitrary")),
        name="sb_attention",
    )(qkva3, qkva3, qkva3)


def _cmp_body(ak_ref, av_ref, posk_ref, posv_ref, w1k_ref, w1kt_ref, w1kb_ref, w2k_ref,
              w1v_ref, w1vt_ref, w1vb_ref, w2v_ref, kc_ref, vc_ref):
    def one(a_ref, pos_ref, w1_ref, w1t_ref, w1b_ref, w2_ref, o_ref):
        a = a_ref[0]
        n = a.shape[0]
        bias = _dot(pos_ref[...], w1_ref[...])[0:1]
        out = jnp.zeros((n, LANES), F32)
        for g in range(NSA_KV_GROUPS):
            top = _dot(a, w1t_ref[g])
            bot = _dot(a, w1b_ref[g])
            h = top + pltpu.roll(bot, n - 1, 0) + bias
            out = out + _dot(jax.nn.gelu(h).astype(BF), w2_ref[g])
        o_ref[0] = out.astype(BF)

    one(ak_ref, posk_ref, w1k_ref, w1kt_ref, w1kb_ref, w2k_ref, kc_ref)
    one(av_ref, posv_ref, w1v_ref, w1vt_ref, w1vb_ref, w2v_ref, vc_ref)


def _compress(ak, av, posk, posv, wk, wv):
    B, n, width = ak.shape
    blk = pl.BlockSpec((1, n, width), lambda b: (b, 0, 0))
    c2 = lambda shp: pl.BlockSpec(shp, lambda b: (0, 0))
    c3 = lambda shp: pl.BlockSpec(shp, lambda b: (0, 0, 0))
    wspecs = [c2((CMP_LEN * HEAD_DIM, CMP_HIDDEN)), c3((2, width, CMP_HIDDEN)),
              c3((2, width, CMP_HIDDEN)), c3((2, CMP_HIDDEN, LANES))]
    out = pl.BlockSpec((1, n, LANES), lambda b: (b, 0, 0))
    return pl.pallas_call(
        _cmp_body,
        grid=(B,),
        in_specs=[blk, blk, c2((8, CMP_LEN * HEAD_DIM)), c2((8, CMP_LEN * HEAD_DIM))] + wspecs + wspecs,
        out_specs=[out, out],
        out_shape=[jax.ShapeDtypeStruct((B, n, LANES), BF)] * 2,
        compiler_params=_cparams(("parallel",)),
        name="nsa_compress",
    )(ak, av, posk, posv, *wk, *wv)


def _online_softmax_step(s3, ok, v, state):
    m, l, acc = state
    R, tq, tk = s3.shape
    s3 = jnp.where(ok[None], s3, NEG_INF)
    m_new = jnp.maximum(m, jnp.max(s3, axis=-1, keepdims=True))
    alpha = jnp.exp(m - m_new)
    p = jnp.where(ok[None], jnp.exp(s3 - m_new), 0.0)
    l = alpha * l + jnp.sum(p, axis=-1, keepdims=True)
    acc = alpha.reshape(R * tq, 1) * acc + _dot(p.reshape(R * tq, tk).astype(BF), v)
    return m_new, l, acc


def _nsa_body(q_ref, kc_ref, vc_ref, ks_ref, vs_ref, kw_ref, vw_ref, g_ref, ovl_ref, e_ref,
              gexp_ref, o_ref, *, tq, tk):
    i = pl.program_id(1)
    R = NSA_REP
    t0 = i * tq
    lane = lax.broadcasted_iota(jnp.int32, (1, LANES), 1)
    q = q_ref[0]
    ghi, glo = _split2(g_ref[0])
    gx = _dot(ghi, gexp_ref[...]) + _dot(glo, gexp_ref[...])
    t_col = t0 + lax.broadcasted_iota(jnp.int32, (tq, 1), 0)
    t_row = t0 + lax.broadcasted_iota(jnp.int32, (1, tq), 1)
    n_idx = lax.broadcasted_iota(jnp.int32, (1, LANES), 1)
    cmp_ok = (CMP_STRIDE * n_idx + (CMP_LEN - 1)) <= t_col
    blk = lax.broadcasted_iota(jnp.int32, (LANES, 1), 0)
    cur = t_row // SLC_LEN
    valid = blk <= cur
    forced = (blk == 0) | (valid & (blk > cur - SLC_LOCAL))
    ovl = ovl_ref[...]
    init = (jnp.full((R, tq, 1), NEG_INF, F32), jnp.zeros((R, tq, 1), F32),
            jnp.zeros((R * tq, LANES), F32))
    branches = []
    for gi in range(NSA_KV_GROUPS):
        gm = (lane // HEAD_DIM) == gi
        qs = jnp.concatenate(
            [jnp.where(gm, q[:, LANES * r:LANES * (r + 1)], jnp.zeros((tq, LANES), BF)) for r in range(R)],
            axis=0)
        s3 = (_dot_nt(qs, kc_ref[0]) * ATT_SCALE).reshape(R, tq, LANES)
        sm = jnp.where(cmp_ok[None], s3, NEG_INF)
        e = jnp.exp(sm - jnp.max(sm, axis=-1, keepdims=True))
        p = e / jnp.sum(e, axis=-1, keepdims=True)
        pc = jnp.where(cmp_ok[None], p, 0.0)
        o_c = _dot(pc.reshape(R * tq, LANES).astype(BF), vc_ref[0])
        psum = pc[0] + pc[1] + pc[2] + pc[3]
        p1 = psum.astype(BF)
        r1 = psum - p1.astype(F32)
        p2 = r1.astype(BF)
        p3 = (r1 - p2.astype(F32)).astype(BF)
        imp = _dot_nt(ovl, p1) + _dot_nt(ovl, p2) + _dot_nt(ovl, p3)
        imp = jnp.where(forced, imp + SLC_FORCE_BONUS, jnp.where(valid, imp, NEG_INF))
        sel = jnp.zeros((LANES, tq), F32)
        for _ in range(SLC_TOPN):
            mx = jnp.max(imp, axis=0, keepdims=True)
            idx = jnp.min(jnp.where(imp == mx, blk, LANES), axis=0, keepdims=True)
            hit = blk == idx
            sel = jnp.where(hit, 1.0, sel)
            imp = jnp.where(hit, -jnp.inf, imp)
        sel_t = sel.T.astype(BF)

        def sel_step(j, state, qs=qs, sel_t=sel_t):
            off = pl.multiple_of(j * tk, tk)
            s = (_dot_nt(qs, ks_ref[0, pl.ds(off, tk), :]) * ATT_SCALE).reshape(R, tq, tk)
            chosen = _dot(sel_t, e_ref[:, pl.ds(off, tk)])
            key = off + lax.broadcasted_iota(jnp.int32, (1, tk), 1)
            ok = jnp.where(key <= t_col, chosen, 0.0) > 0.5
            return _online_softmax_step(s, ok, vs_ref[0, pl.ds(off, tk), :], state)

        _, l_s, acc_s = lax.fori_loop(0, (t0 + tq + tk - 1) // tk, sel_step, init)
        o_s = acc_s / l_s.reshape(R * tq, 1)

        jlo = jnp.maximum(i - WINDOW // tq, 0)

        def win_step(jj, state, qs=qs, jlo=jlo):
            off = pl.multiple_of((jlo + jj) * tq, tq)
            s = (_dot_nt(qs, kw_ref[0, pl.ds(off, tq), :]) * ATT_SCALE).reshape(R, tq, tq)
            kp = off + lax.broadcasted_iota(jnp.int32, (1, tq), 1)
            ok = jnp.where(kp <= t_col, kp, t_col - WINDOW) > t_col - WINDOW
            return _online_softmax_step(s, ok, vw_ref[0, pl.ds(off, tq), :], state)

        _, l_w, acc_w = lax.fori_loop(0, i - jlo + 1, win_step, init)
        o_w = acc_w / l_w.reshape(R * tq, 1)
        branches.append((o_c, o_s, o_w))

    g0 = (lane // HEAD_DIM) == 0
    for r in range(R):
        rows = slice(r * tq, (r + 1) * tq)
        out = jnp.zeros((tq, LANES), F32)
        for br in range(NSA_N_BRANCH):
            c = r * NSA_N_BRANCH + br
            both = jnp.where(g0, branches[0][br][rows], branches[1][br][rows])
            out = out + gx[:, c * LANES:(c + 1) * LANES] * both
        o_ref[0, :, r * LANES:(r + 1) * LANES] = out.astype(BF)


def _nsa_attention(qb3, kc, vc, kr3, vals3, gate3, ovl, expand, gexp):
    B, S, _ = qb3.shape
    tq, tk = 128, 256
    n = kc.shape[1]
    tok = lambda c: pl.BlockSpec((1, S, LANES), lambda b, i, c=c: (b, 0, c))
    return pl.pallas_call(
        functools.partial(_nsa_body, tq=tq, tk=tk),
        grid=(B, S // tq),
        in_specs=[
            pl.BlockSpec((1, tq, NSA_WIDTH), lambda b, i: (b, i, 0)),
            pl.BlockSpec((1, n, LANES), lambda b, i: (b, 0, 0)),
            pl.BlockSpec((1, n, LANES), lambda b, i: (b, 0, 0)),
            tok(1), tok(1), tok(2), tok(2),
            pl.BlockSpec((1, tq, LANES), lambda b, i: (b, i, 0)),
            pl.BlockSpec((LANES, LANES), lambda b, i: (0, 0)),
            pl.BlockSpec((LANES, S), lambda b, i: (0, 0)),
            pl.BlockSpec((LANES, NSA_REP * NSA_N_BRANCH * LANES), lambda b, i: (0, 0)),
        ],
        out_specs=pl.BlockSpec((1, tq, NSA_WIDTH), lambda b, i: (b, i, 0)),
        out_shape=jax.ShapeDtypeStruct((B, S, NSA_WIDTH), BF),
        compiler_params=_cparams(("parallel", "arbitrary")),
        name="nsa_attention",
    )(qb3, kc, vc, kr3, vals3, kr3, vals3, gate3, ovl, expand, gexp)


def _outproj_ln_body(oa_ref, ob_ref, x_ref, wa_ref, wb_ref, g_ref, b_ref, o_ref):
    y = _dot(oa_ref[...], wa_ref[...]) + _dot(ob_ref[...], wb_ref[...])
    o_ref[...] = _layer_norm(DN_ALPHA * x_ref[...] + y, g_ref[...], b_ref[...])


def _outproj_ln(oa, ob, x2, wa, wb, g, b):
    T = x2.shape[0]
    tm = 512
    row = lambda i: (i, 0)
    full = lambda i: (0, 0)
    return pl.pallas_call(
        _outproj_ln_body,
        grid=(T // tm,),
        in_specs=[
            pl.BlockSpec((tm, oa.shape[1]), row),
            pl.BlockSpec((tm, ob.shape[1]), row),
            pl.BlockSpec((tm, D_MODEL), row),
            pl.BlockSpec(wa.shape, full),
            pl.BlockSpec(wb.shape, full),
            pl.BlockSpec((1, D_MODEL), full),
            pl.BlockSpec((1, D_MODEL), full),
        ],
        out_specs=pl.BlockSpec((tm, D_MODEL), row),
        out_shape=jax.ShapeDtypeStruct((T, D_MODEL), F32),
        compiler_params=_cparams(("parallel",)),
        name="outproj_ln",
    )(oa, ob, x2, wa, wb, g, b)


def _rope_tables(seq):
    inv = jnp.power(ROPE_THETA, -jnp.arange(0, ROPE_DIM, 2, dtype=F32) / ROPE_DIM)
    ang = jnp.arange(seq, dtype=F32)[:, None] * inv[None, :]
    half = ROPE_DIM // 2
    rest = HEAD_DIM - ROPE_DIM
    cos_h = jnp.concatenate([jnp.cos(ang), jnp.cos(ang), jnp.ones((seq, rest), F32)], axis=1)
    sin_h = jnp.concatenate([jnp.sin(ang), jnp.sin(ang), jnp.zeros((seq, rest), F32)], axis=1)
    reps = N_ROPE // HEAD_DIM
    del half
    return jnp.tile(cos_h, (1, reps)), jnp.tile(sin_h, (1, reps))


def _rot_cols(w):
    k, n = w.shape
    w3 = w.reshape(k, n // HEAD_DIM, HEAD_DIM)
    half = ROPE_DIM // 2
    rot = jnp.concatenate([-w3[..., half:ROPE_DIM], w3[..., :half],
                           jnp.zeros_like(w3[..., ROPE_DIM:])], axis=-1)
    return rot.reshape(k, n)


def _nsa_head_perm():
    cols = []
    for r in range(NSA_REP):
        for g in range(NSA_KV_GROUPS):
            h = g * NSA_REP + r
            cols.extend(range(h * HEAD_DIM, (h + 1) * HEAD_DIM))
    return np.asarray(cols)


def _mixer0(x2, batch, seq, w_in, cmp_pos_k, cmp_pos_v, cmp_k_w1, cmp_k_w2, cmp_v_w1, cmp_v_w2,
            w_out, ln_g, ln_b):
    T = x2.shape[0]
    sizes = [SB_WIDTH] * 3 + [NSA_WIDTH] + [NSA_KV_GROUPS * HEAD_DIM] * 6 + [NSA_HEADS * NSA_N_BRANCH]
    offs = np.concatenate([[0], np.cumsum(sizes)])
    col = lambda j: w_in[:, offs[j]:offs[j + 1]]
    perm = _nsa_head_perm()
    ngate = sizes[-1]
    wp = jnp.concatenate([col(0), col(1), col(2), col(5), col(7), col(9),
                          jnp.pad(col(10), ((0, 0), (0, LANES - ngate)))], axis=1)
    wr = jnp.concatenate([col(3)[:, perm], col(4), col(6), col(8)], axis=1)
    wrr = _rot_cols(wr)
    cos_t, sin_t = _rope_tables(seq)
    qkva, vals, gates, qb, kr = _proj0(x2, wp.astype(BF), wr.astype(BF), wrr.astype(BF), cos_t, sin_t, seq)

    r3 = lambda a: a.reshape(batch, seq, a.shape[-1])
    o_a = _sb_attention(r3(qkva))

    ncmp = seq // CMP_STRIDE
    ak = kr[:, 0:LANES].reshape(batch, ncmp, CMP_STRIDE * LANES)
    av = vals[:, 0:LANES].reshape(batch, ncmp, CMP_STRIDE * LANES)

    def cmp_weights(w1, w2):
        w1r = w1.reshape(2, CMP_STRIDE, HEAD_DIM, CMP_HIDDEN)
        tops, bots, w2s = [], [], []
        for g in range(NSA_KV_GROUPS):
            ext = jnp.zeros((2, CMP_STRIDE, NSA_KV_GROUPS, HEAD_DIM, CMP_HIDDEN), F32).at[:, :, g].set(w1r)
            ext = ext.reshape(2, CMP_STRIDE * LANES, CMP_HIDDEN)
            tops.append(ext[0])
            bots.append(ext[1])
            w2s.append(jnp.zeros((CMP_HIDDEN, LANES), F32).at[:, g * HEAD_DIM:(g + 1) * HEAD_DIM].set(w2))
        return (w1.astype(BF), jnp.stack(tops).astype(BF), jnp.stack(bots).astype(BF),
                jnp.stack(w2s).astype(BF))

    posk = jnp.broadcast_to(cmp_pos_k.reshape(1, -1), (8, CMP_LEN * HEAD_DIM)).astype(BF)
    posv = jnp.broadcast_to(cmp_pos_v.reshape(1, -1), (8, CMP_LEN * HEAD_DIM)).astype(BF)
    kc, vc = _compress(ak, av, posk, posv, cmp_weights(cmp_k_w1, cmp_k_w2), cmp_weights(cmp_v_w1, cmp_v_w2))
    assert ncmp <= LANES
    if ncmp < LANES:
        kc = jnp.pad(kc, ((0, 0), (0, LANES - ncmp), (0, 0)))
        vc = jnp.pad(vc, ((0, 0), (0, LANES - ncmp), (0, 0)))

    n_slc = seq // SLC_LEN
    cmp_start = np.arange(ncmp) * CMP_STRIDE
    slc_start = np.arange(LANES) * SLC_LEN
    ovl = ((cmp_start[None, :] <= slc_start[:, None] + SLC_LEN - 1)
           & (cmp_start[None, :] + CMP_LEN - 1 >= slc_start[:, None])
           & (np.arange(LANES)[:, None] < n_slc)).astype(np.float32)
    ovl = np.pad(ovl, ((0, 0), (0, LANES - ncmp))) if ncmp < LANES else ovl
    expand = (np.arange(seq)[None, :] // SLC_LEN == np.arange(LANES)[:, None]).astype(np.float32)
    gexp = np.zeros((LANES, NSA_REP * NSA_N_BRANCH * LANES), np.float32)
    for r in range(NSA_REP):
        for br in range(NSA_N_BRANCH):
            c = r * NSA_N_BRANCH + br
            for g in range(NSA_KV_GROUPS):
                src = (g * NSA_REP + r) * NSA_N_BRANCH + br
                gexp[src, c * LANES + g * HEAD_DIM:c * LANES + (g + 1) * HEAD_DIM] = 1.0
    o_b = _nsa_attention(r3(qb), kc, vc, r3(kr), r3(vals), r3(gates), jnp.asarray(ovl, BF),
                         jnp.asarray(expand, BF), jnp.asarray(gexp, BF))

    wa = w_out[:SB_WIDTH].astype(BF)
    wb = w_out[SB_WIDTH:][perm].astype(BF)
    return _outproj_ln(o_a.reshape(T, SB_WIDTH), o_b.reshape(T, NSA_WIDTH), x2, wa, wb,
                       ln_g.reshape(1, -1), ln_b.reshape(1, -1))


MOE_TILE = 1024
ROW_ALIGN = 8
RANK_CHUNK = 256


def _top_rows(vals, ids, n_ids, count):
    hits = []
    for _ in range(count):
        mx = jnp.max(vals, axis=0, keepdims=True)
        idx = jnp.min(jnp.where(vals == mx, ids, n_ids), axis=0, keepdims=True)
        hit = ids == idx
        hits.append(hit)
        vals = jnp.where(hit, -jnp.inf, vals)
    return hits


def _router_body(h_ref, rt_ref, rb_ref, pos_ref, gate_ref, meta_ref, *, tm):
    E, NG = N_EXPERTS, N_EXPERT_GROUPS
    per = E // NG
    hh, hl = _split2(h_ref[...])
    rh, rl = _split2(rt_ref[...])
    logits = _dot_nt(rh, hh) + _dot_nt(rh, hl) + _dot_nt(rl, hh)
    scores = jax.nn.sigmoid(logits)
    biased = scores + rb_ref[...]
    i8 = lax.broadcasted_iota(jnp.int32, (per, tm), 0)
    gs = []
    for g in range(NG):
        v = biased[g * per:(g + 1) * per]
        m1 = jnp.max(v, axis=0, keepdims=True)
        a1 = jnp.min(jnp.where(v == m1, i8, per), axis=0, keepdims=True)
        m2 = jnp.max(jnp.where(i8 == a1, -jnp.inf, v), axis=0, keepdims=True)
        gs.append(m1 + m2)
    gs = jnp.concatenate(gs, axis=0)
    gi = lax.broadcasted_iota(jnp.int32, (NG, tm), 0)
    ghits = _top_rows(gs, gi, NG, TOPK_GROUPS)
    gkeep = jnp.zeros((NG, tm), F32)
    for hit in ghits:
        gkeep = jnp.where(hit, 1.0, gkeep)
    ekeep = jnp.concatenate([jnp.broadcast_to(gkeep[g:g + 1], (per, tm)) for g in range(NG)], axis=0)
    ei = lax.broadcasted_iota(jnp.int32, (E, tm), 0)
    hits = _top_rows(jnp.where(ekeep > 0.5, biased, -jnp.inf), ei, E, TOP_K)
    gates = [jnp.sum(jnp.where(hit, scores, 0.0), axis=0, keepdims=True) for hit in hits]
    gsum = gates[0]
    for gk in gates[1:]:
        gsum = gsum + gk
    gates = [gk / gsum * ROUTED_SCALE for gk in gates]

    member = jnp.zeros((E, tm), F32)
    for hit in hits:
        member = jnp.where(hit, 1.0, member)
    cnt_col = jnp.sum(member, axis=1, keepdims=True)
    pad_col = jnp.floor((cnt_col + (ROW_ALIGN - 1)) * (1.0 / ROW_ALIGN)) * ROW_ALIGN
    sub_e = lax.broadcasted_iota(jnp.int32, (E, LANES), 0)
    lane_e = lax.broadcasted_iota(jnp.int32, (E, LANES), 1)
    cnt_row = jnp.sum(jnp.where(sub_e == lane_e, cnt_col, 0.0), axis=0, keepdims=True)
    pad_row = jnp.sum(jnp.where(sub_e == lane_e, pad_col, 0.0), axis=0, keepdims=True)
    off_row = jnp.sum(jnp.where(sub_e < lane_e, pad_col, 0.0), axis=0, keepdims=True)
    off_col = jnp.sum(jnp.where(lane_e < sub_e, pad_row, 0.0), axis=1, keepdims=True)
    r_i = lax.broadcasted_iota(jnp.int32, (RANK_CHUNK, RANK_CHUNK), 0)
    c_i = lax.broadcasted_iota(jnp.int32, (RANK_CHUNK, RANK_CHUNK), 1)
    before = jnp.where(r_i < c_i, 1.0, 0.0).astype(BF)
    running = off_col
    ranks = []
    for c in range(tm // RANK_CHUNK):
        mc = member[:, c * RANK_CHUNK:(c + 1) * RANK_CHUNK]
        ranks.append(_dot(mc.astype(BF), before) + running)
        running = running + jnp.sum(mc, axis=1, keepdims=True)
    slot = jnp.concatenate(ranks, axis=1)
    pos = [jnp.sum(jnp.where(hit, slot, 0.0), axis=0, keepdims=True) for hit in hits]
    zrow = jnp.zeros((1, tm), F32)
    pos_ref[...] = jnp.concatenate(pos + [zrow, zrow], axis=0).astype(jnp.int32)
    gate_ref[...] = jnp.concatenate(gates + [zrow, zrow], axis=0)
    z128 = jnp.zeros((1, LANES), F32)
    meta_ref[0] = jnp.concatenate([off_row, cnt_row] + [z128] * 6, axis=0).astype(jnp.int32)


def _router(h2, router_t, bias_col):
    T = h2.shape[0]
    tm = MOE_TILE
    nt = T // tm
    return pl.pallas_call(
        functools.partial(_router_body, tm=tm),
        grid=(nt,),
        in_specs=[
            pl.BlockSpec((tm, D_MODEL), lambda i: (i, 0)),
            pl.BlockSpec((N_EXPERTS, D_MODEL), lambda i: (0, 0)),
            pl.BlockSpec((N_EXPERTS, 1), lambda i: (0, 0)),
        ],
        out_specs=[
            pl.BlockSpec((8, tm), lambda i: (0, i)),
            pl.BlockSpec((8, tm), lambda i: (0, i)),
            pl.BlockSpec((1, 8, LANES), lambda i: (i, 0, 0)),
        ],
        out_shape=[
            jax.ShapeDtypeStruct((8, T), jnp.int32),
            jax.ShapeDtypeStruct((8, T), F32),
            jax.ShapeDtypeStruct((nt, 8, LANES), jnp.int32),
        ],
        compiler_params=_cparams(("parallel",)),
        name="moe_router",
    )(h2, router_t, bias_col)


EXPERTS_PER_STEP = 4
EXPERT_CHUNK = 128
COMBINE_SUB = 256
HALF = D_MODEL // 2
NWORD = HALF // LANES
HI_MASK = 0xFFFF0000


def _pack_pairs(a, b):
    lo = pltpu.bitcast(a.astype(BF).astype(F32), jnp.uint32)
    hi = pltpu.bitcast(b.astype(BF).astype(F32), jnp.uint32)
    return (lo >> 16) | (hi & jnp.uint32(HI_MASK))


def _unpack_lo(w):
    return pltpu.bitcast(w << 16, F32)


def _unpack_hi(w):
    return pltpu.bitcast(w & jnp.uint32(HI_MASK), F32)


def _swiglu(xb, wgu, wd, hidden):
    gu = _dot(xb, wgu)
    a = jax.nn.silu(gu[:, :hidden]) * gu[:, hidden:]
    return _dot(a.astype(BF), wd)


def _moe_body(off_ref, cnt_ref, pos_ref, x_ref, gcol_ref, wgu_ref, wd_ref, wsgu_ref, wsd_ref,
              g_ref, b_ref, o_ref, src_ref, xs_ref, z_ref, *, tm, eb, ch, sub, unroll):
    i = pl.program_id(0)
    j = pl.program_id(1)

    @pl.when(j == 0)
    def _dispatch():
        x = x_ref[...]
        for c in range(NWORD):
            src_ref[c] = _pack_pairs(x[:, c * LANES:(c + 1) * LANES],
                                     x[:, HALF + c * LANES:HALF + (c + 1) * LANES])
        xs_ref[...] = jnp.zeros_like(xs_ref)

        def tok(tb, carry):
            for u in range(unroll):
                t = tb * unroll + u
                for k in range(TOP_K):
                    p = pos_ref[k, t]
                    for c in range(NWORD):
                        xs_ref[c, pl.ds(p, 1), :] = src_ref[c, pl.ds(t, 1), :]
            return carry

        lax.fori_loop(0, tm // unroll, tok, 0)

    for el in range(eb):
        e = i * N_EXPERTS + j * eb + el
        off = off_ref[e]
        cnt = cnt_ref[e]

        def chunk(c, carry, el=el, off=off, cnt=cnt):
            r0 = pl.multiple_of(off + c * ch, ROW_ALIGN)
            words = [xs_ref[cc, pl.ds(r0, ch), :] for cc in range(NWORD)]
            xb = jnp.concatenate([_unpack_lo(w).astype(BF) for w in words]
                                 + [_unpack_hi(w).astype(BF) for w in words], axis=1)
            y = _swiglu(xb, wgu_ref[el], wd_ref[el], EXPERT_HIDDEN)
            keep = (c * ch + lax.broadcasted_iota(jnp.int32, (ch, 1), 0)) < cnt
            for cc in range(NWORD):
                packed = _pack_pairs(y[:, cc * LANES:(cc + 1) * LANES],
                                     y[:, HALF + cc * LANES:HALF + (cc + 1) * LANES])
                xs_ref[cc, pl.ds(r0, ch), :] = jnp.where(keep, packed, words[cc])
            return carry

        lax.fori_loop(0, (cnt + ch - 1) // ch, chunk, 0)

    @pl.when(j == pl.num_programs(1) - 1)
    def _combine():
        for sb in range(tm // sub):
            def tok(tb, carry, sb=sb):
                for u in range(unroll):
                    tl = tb * unroll + u
                    for k in range(TOP_K):
                        p = pos_ref[k, sb * sub + tl]
                        for c in range(NWORD):
                            z_ref[k, c, pl.ds(tl, 1), :] = xs_ref[c, pl.ds(p, 1), :]
                return carry

            lax.fori_loop(0, sub // unroll, tok, 0)
            rows = slice(sb * sub, (sb + 1) * sub)
            gcol = gcol_ref[rows, :]
            lo = [jnp.zeros((sub, LANES), F32) for _ in range(NWORD)]
            hi = [jnp.zeros((sub, LANES), F32) for _ in range(NWORD)]
            for k in range(TOP_K):
                gk = gcol[:, k:k + 1]
                for c in range(NWORD):
                    w = z_ref[k, c]
                    lo[c] = lo[c] + gk * _unpack_lo(w)
                    hi[c] = hi[c] + gk * _unpack_hi(w)
            routed = jnp.concatenate(lo + hi, axis=1)
            x = x_ref[rows, :]
            shared = _swiglu(x.astype(BF), wsgu_ref[...], wsd_ref[...], SHARED_HIDDEN)
            o_ref[rows, :] = _layer_norm(DN_ALPHA * x + routed + shared, g_ref[...], b_ref[...])


def _moe_experts(h2, off, cnt, pos, gcol, wgu, wd, wsgu, wsd, g, b):
    T = h2.shape[0]
    tm, eb, ch, sub = MOE_TILE, EXPERTS_PER_STEP, EXPERT_CHUNK, COMBINE_SUB
    nt = T // tm
    rows = TOP_K * tm + N_EXPERTS * ROW_ALIGN + ch
    hidden2 = wgu.shape[-1]
    return pl.pallas_call(
        functools.partial(_moe_body, tm=tm, eb=eb, ch=ch, sub=sub, unroll=4),
        grid_spec=pltpu.PrefetchScalarGridSpec(
            num_scalar_prefetch=2,
            grid=(nt, N_EXPERTS // eb),
            in_specs=[
                pl.BlockSpec((8, tm), lambda i, j, *_: (0, i), memory_space=pltpu.SMEM),
                pl.BlockSpec((tm, D_MODEL), lambda i, j, *_: (i, 0)),
                pl.BlockSpec((tm, 8), lambda i, j, *_: (i, 0)),
                pl.BlockSpec((eb, D_MODEL, hidden2), lambda i, j, *_: (j, 0, 0)),
                pl.BlockSpec((eb, hidden2 // 2, D_MODEL), lambda i, j, *_: (j, 0, 0)),
                pl.BlockSpec(wsgu.shape, lambda i, j, *_: (0, 0)),
                pl.BlockSpec(wsd.shape, lambda i, j, *_: (0, 0)),
                pl.BlockSpec((1, D_MODEL), lambda i, j, *_: (0, 0)),
                pl.BlockSpec((1, D_MODEL), lambda i, j, *_: (0, 0)),
            ],
            out_specs=pl.BlockSpec((tm, D_MODEL), lambda i, j, *_: (i, 0)),
            scratch_shapes=[
                pltpu.VMEM((NWORD, tm, LANES), jnp.uint32),
                pltpu.VMEM((NWORD, rows, LANES), jnp.uint32),
                pltpu.VMEM((TOP_K, NWORD, sub, LANES), jnp.uint32),
            ],
        ),
        out_shape=jax.ShapeDtypeStruct((T, D_MODEL), F32),
        compiler_params=_cparams(("parallel", "arbitrary")),
        name="moe_experts",
    )(off, cnt, pos, h2, gcol, wgu, wd, wsgu, wsd, g, b)


S5_BATCH = 8
S5_STEPS = 32
S5_SLABS = 4
SLAB_CH = D_MODEL // S5_SLABS
SLAB_ST = SSM_GROUPS * SSM_STATE // S5_SLABS
N_STATE = SSM_GROUPS * SSM_STATE
SCAN_LANES = 512
SCAN_UNROLL = 8


def _s5_disc_body(lre_ref, lim_ref, ldt_ref, bre_ref, bim_ref, are_ref, aim_ref, bbre_ref, bbim_ref):
    lre, lim = lre_ref[...], lim_ref[...]
    step = jnp.exp(ldt_ref[...])
    mag = jnp.exp(lre * step)
    a_re = mag * jnp.cos(lim * step)
    a_im = mag * jnp.sin(lim * step)
    den = lre * lre + lim * lim
    zoh_re = ((a_re - 1.0) * lre + a_im * lim) / den
    zoh_im = (a_im * lre - (a_re - 1.0) * lim) / den
    are_ref[...] = a_re
    aim_ref[...] = a_im
    bbre_ref[...] = zoh_re * bre_ref[...] - zoh_im * bim_ref[...]
    bbim_ref[...] = zoh_re * bim_ref[...] + zoh_im * bre_ref[...]


def _s5_discretize(lambda_re, lambda_im, log_dt, b_re, b_im):
    col = lambda a: a.reshape(N_STATE, 1)
    ldt = jnp.broadcast_to(log_dt[:, None], (SSM_GROUPS, SSM_STATE))
    mat = lambda a: a.reshape(N_STATE, SSM_GROUP)
    c1 = jax.ShapeDtypeStruct((N_STATE, 1), F32)
    c16 = jax.ShapeDtypeStruct((N_STATE, SSM_GROUP), F32)
    return pl.pallas_call(_s5_disc_body, out_shape=[c1, c1, c16, c16], name="s5_discretize")(
        col(lambda_re), col(lambda_im), col(ldt), mat(b_re), mat(b_im))


def _s5_body(x_ref, win_ref, are_ref, aim_ref, bd_ref, cd_ref, dsk_ref, wglu_ref, wout_ref, g_ref, b_ref,
             o_ref, xs_ref, hre_ref, him_ref, sre_ref, sim_ref, *, lt):
    nb = S5_BATCH
    nlb = D_MODEL // LANES
    for c in range(nlb):
        for b in range(nb):
            xs_ref[c, pl.ds(b, lt, stride=nb), :] = x_ref[b, :, c * LANES:(c + 1) * LANES]
    x = jnp.concatenate([xs_ref[c] for c in range(nlb)], axis=1)
    u = _dot(x.astype(BF), win_ref[...])
    ub = u.astype(BF)
    for k in range(S5_SLABS):
        bu = _dot(ub[:, k * SLAB_CH:(k + 1) * SLAB_CH], bd_ref[k])
        hre_ref[:, k * SLAB_ST:(k + 1) * SLAB_ST] = bu[:, :SLAB_ST]
        him_ref[:, k * SLAB_ST:(k + 1) * SLAB_ST] = bu[:, SLAB_ST:]

    @pl.when(pl.program_id(1) == 0)
    def _():
        sre_ref[...] = jnp.zeros_like(sre_ref)
        sim_ref[...] = jnp.zeros_like(sim_ref)

    for c in range(N_STATE // SCAN_LANES):
        ls = slice(c * SCAN_LANES, (c + 1) * SCAN_LANES)
        a_re = jnp.broadcast_to(are_ref[:, ls], (nb, SCAN_LANES))
        a_im = jnp.broadcast_to(aim_ref[:, ls], (nb, SCAN_LANES))

        def steps(tb, state, ls=ls, a_re=a_re, a_im=a_im):
            s_re, s_im = state
            for uu in range(SCAN_UNROLL):
                r0 = pl.multiple_of((tb * SCAN_UNROLL + uu) * nb, nb)
                n_re = a_re * s_re - a_im * s_im + hre_ref[pl.ds(r0, nb), ls]
                n_im = a_re * s_im + a_im * s_re + him_ref[pl.ds(r0, nb), ls]
                hre_ref[pl.ds(r0, nb), ls] = n_re
                him_ref[pl.ds(r0, nb), ls] = n_im
                s_re, s_im = n_re, n_im
            return s_re, s_im

        s_re, s_im = lax.fori_loop(0, lt // SCAN_UNROLL, steps, (sre_ref[:, ls], sim_ref[:, ls]))
        sre_ref[:, ls] = s_re
        sim_ref[:, ls] = s_im

    ys = []
    for k in range(S5_SLABS):
        hk = jnp.concatenate([hre_ref[:, k * SLAB_ST:(k + 1) * SLAB_ST].astype(BF),
                              him_ref[:, k * SLAB_ST:(k + 1) * SLAB_ST].astype(BF)], axis=1)
        ys.append(_dot(hk, cd_ref[k]))
    y = jax.nn.gelu(jnp.concatenate(ys, axis=1) + dsk_ref[...] * u)
    y = y * jax.nn.sigmoid(_dot(y.astype(BF), wglu_ref[...]))
    mixed = _dot(y.astype(BF), wout_ref[...])
    res = _layer_norm(DN_ALPHA * x + mixed, g_ref[...], b_ref[...])
    for c in range(nlb):
        xs_ref[c] = res[:, c * LANES:(c + 1) * LANES]
    for c in range(nlb):
        for b in range(nb):
            o_ref[b, :, c * LANES:(c + 1) * LANES] = xs_ref[c, pl.ds(b, lt, stride=nb), :]


def _mixer1(x3, w_in, lambda_re, lambda_im, b_re, b_im, c_re, c_im, d_skip, log_dt, w_glu, w_out,
            ln_g, ln_b):
    B, S, _ = x3.shape
    lt = S5_STEPS
    rows = S5_BATCH * lt
    a_re, a_im, bb_re, bb_im = _s5_discretize(lambda_re, lambda_im, log_dt, b_re, b_im)
    gps = SSM_GROUPS // S5_SLABS
    eye = jnp.eye(gps, dtype=F32)

    def bdiag(bb):
        b4 = bb.reshape(S5_SLABS, gps, SSM_STATE, SSM_GROUP)
        return jnp.einsum('kgph,gf->kghfp', b4, eye).reshape(S5_SLABS, SLAB_CH, SLAB_ST)

    def cdiag(cc):
        c4 = cc.reshape(S5_SLABS, gps, SSM_GROUP, SSM_STATE)
        return jnp.einsum('kghp,gf->kfpgh', c4, eye).reshape(S5_SLABS, SLAB_ST, SLAB_CH)

    bd = jnp.concatenate([bdiag(bb_re), bdiag(bb_im)], axis=2).astype(BF)
    cd = jnp.concatenate([cdiag(c_re), -cdiag(c_im)], axis=1).astype(BF)
    c2 = lambda shp: pl.BlockSpec(shp, lambda bi, ti: (0,) * len(shp))
    return pl.pallas_call(
        functools.partial(_s5_body, lt=lt),
        grid=(B // S5_BATCH, S // lt),
        in_specs=[
            pl.BlockSpec((S5_BATCH, lt, D_MODEL), lambda bi, ti: (bi, ti, 0)),
            c2((D_MODEL, D_MODEL)), c2((1, N_STATE)), c2((1, N_STATE)),
            c2(bd.shape), c2(cd.shape), c2((1, D_MODEL)),
            c2((D_MODEL, D_MODEL)), c2((D_MODEL, D_MODEL)), c2((1, D_MODEL)), c2((1, D_MODEL)),
        ],
        out_specs=pl.BlockSpec((S5_BATCH, lt, D_MODEL), lambda bi, ti: (bi, ti, 0)),
        out_shape=jax.ShapeDtypeStruct((B, S, D_MODEL), F32),
        scratch_shapes=[
            pltpu.VMEM((D_MODEL // LANES, rows, LANES), F32),
            pltpu.VMEM((rows, N_STATE), F32),
            pltpu.VMEM((rows, N_STATE), F32),
            pltpu.VMEM((S5_BATCH, N_STATE), F32),
            pltpu.VMEM((S5_BATCH, N_STATE), F32),
        ],
        compiler_params=_cparams(("parallel", "arbitrary")),
        name="s5_mixer",
    )(x3, w_in.astype(BF), a_re.reshape(1, N_STATE), a_im.reshape(1, N_STATE), bd, cd,
      d_skip.reshape(1, D_MODEL), w_glu.astype(BF), w_out.astype(BF), ln_g.reshape(1, -1), ln_b.reshape(1, -1))


def _moe_block(h2, router, router_bias, w_gate, w_up, w_down, sh_gate, sh_up, sh_down, ln_g, ln_b):
    pos, gate, meta = _router(h2, router.T, router_bias.reshape(-1, 1))
    off = meta[:, 0, :N_EXPERTS].reshape(-1)
    cnt = meta[:, 1, :N_EXPERTS].reshape(-1)
    wgu = jnp.concatenate([w_gate, w_up], axis=-1).astype(BF)
    wsgu = jnp.concatenate([sh_gate, sh_up], axis=-1).astype(BF)
    return _moe_experts(h2, off, cnt, pos, gate.T, wgu, w_down.astype(BF), wsgu, sh_down.astype(BF),
                        ln_g.reshape(1, -1), ln_b.reshape(1, -1))


def kernel(x, l0_w_in, l0_cmp_pos_k, l0_cmp_pos_v, l0_cmp_k_w1, l0_cmp_k_w2, l0_cmp_v_w1, l0_cmp_v_w2, l0_w_out, l0_ln1_g, l0_ln1_b, l0_router, l0_router_bias, l0_w_gate, l0_w_up, l0_w_down, l0_sh_gate, l0_sh_up, l0_sh_down, l0_ln2_g, l0_ln2_b, l1_w_in, l1_lambda_re, l1_lambda_im, l1_b_re, l1_b_im, l1_c_re, l1_c_im, l1_d, l1_log_dt, l1_w_glu, l1_w_out, l1_ln1_g, l1_ln1_b, l1_router, l1_router_bias, l1_w_gate, l1_w_up, l1_w_down, l1_sh_gate, l1_sh_up, l1_sh_down, l1_ln2_g, l1_ln2_b):
    B, S, D = x.shape
    assert D == D_MODEL and S % 512 == 0 and B % S5_BATCH == 0 and (B * S) % MOE_TILE == 0
    T = B * S
    h = _mixer0(x.reshape(T, D), B, S, l0_w_in, l0_cmp_pos_k, l0_cmp_pos_v, l0_cmp_k_w1, l0_cmp_k_w2,
                l0_cmp_v_w1, l0_cmp_v_w2, l0_w_out, l0_ln1_g, l0_ln1_b)
    h = _moe_block(h, l0_router, l0_router_bias, l0_w_gate, l0_w_up, l0_w_down, l0_sh_gate, l0_sh_up,
                   l0_sh_down, l0_ln2_g, l0_ln2_b)
    h = _mixer1(h.reshape(B, S, D), l1_w_in, l1_lambda_re, l1_lambda_im, l1_b_re, l1_b_im, l1_c_re, l1_c_im,
                l1_d, l1_log_dt, l1_w_glu, l1_w_out, l1_ln1_g, l1_ln1_b)
    h = _moe_block(h.reshape(T, D), l1_router, l1_router_bias, l1_w_gate, l1_w_up, l1_w_down, l1_sh_gate,
                   l1_sh_up, l1_sh_down, l1_ln2_g, l1_ln2_b)
    return h.reshape(B, S, D)
```

```python
import functools
import math

import numpy as np
import jax
import jax.numpy as jnp
from jax import lax
from jax.experimental import pallas as pl
from jax.experimental.pallas import tpu as pltpu

F32 = jnp.float32
BF = jnp.bfloat16

D_MODEL = 1024
DEPTH = 2
HEAD_DIM = 64
LANES = 128
SB_HEADS = 8
SB_WIDTH = SB_HEADS * HEAD_DIM
NSA_HEADS = 8
NSA_KV_GROUPS = 2
NSA_REP = NSA_HEADS // NSA_KV_GROUPS
NSA_WIDTH = NSA_HEADS * HEAD_DIM
NSA_N_BRANCH = 3
CMP_LEN = 32
CMP_STRIDE = 16
CMP_HIDDEN = 256
SLC_LEN = 64
SLC_TOPN = 8
SLC_LOCAL = 2
SLC_FORCE_BONUS = 1e4
WINDOW = 512
ROPE_THETA = 500000.0
ROPE_DIM = HEAD_DIM // 4
SSM_GROUP = 16
SSM_GROUPS = D_MODEL // SSM_GROUP
SSM_STATE = 64
N_EXPERTS = 64
N_EXPERT_GROUPS = 8
TOPK_GROUPS = 4
TOP_K = 6
EXPERT_HIDDEN = 256
SHARED_HIDDEN = 256
ROUTED_SCALE = 2.5
DN_ALPHA = (2 * DEPTH) ** 0.25
LN_EPS = 1e-5
NEG_INF = -1e30
ATT_SCALE = HEAD_DIM ** -0.5
SB_UNDERFLOW = -104.0
SB_ROWS = 128

VMEM_LIMIT = 56 * 1024 * 1024


def _cparams(sem):
    return pltpu.CompilerParams(dimension_semantics=sem, vmem_limit_bytes=VMEM_LIMIT)


def _dot(a, b):
    return jnp.dot(a, b, preferred_element_type=F32)


def _dot_nt(a, b):
    return lax.dot_general(a, b, (((1,), (1,)), ((), ())), preferred_element_type=F32)


def _split2(x):
    hi = x.astype(BF)
    lo = (x - hi.astype(F32)).astype(BF)
    return hi, lo


def _layer_norm(h, g, b):
    mu = jnp.mean(h, axis=-1, keepdims=True)
    d = h - mu
    var = jnp.mean(d * d, axis=-1, keepdims=True)
    return d * lax.rsqrt(var + LN_EPS) * g + b


N_PLAIN = 3 * SB_WIDTH + 3 * LANES + LANES
N_ROPE = NSA_WIDTH + 3 * LANES


def _proj0_body(x_ref, wp_ref, wr_ref, wrr_ref, cos_ref, sin_ref,
                qkva_ref, vals_ref, gate_ref, qb_ref, kr_ref):
    xb = x_ref[...].astype(BF)
    a0, a1, a2 = 3 * SB_WIDTH, 3 * SB_WIDTH + 3 * LANES, N_PLAIN
    qkva_ref[...] = _dot(xb, wp_ref[:, 0:a0]).astype(BF)
    vals_ref[...] = _dot(xb, wp_ref[:, a0:a1]).astype(BF)
    gate_ref[...] = jax.nn.sigmoid(_dot(xb, wp_ref[:, a1:a2]))
    y = _dot(xb, wr_ref[...]) * cos_ref[...] + _dot(xb, wrr_ref[...]) * sin_ref[...]
    qb_ref[...] = y[:, 0:NSA_WIDTH].astype(BF)
    kr_ref[...] = y[:, NSA_WIDTH:N_ROPE].astype(BF)


def _proj0(x2, wp, wr, wrr, cos_t, sin_t, seq):
    T = x2.shape[0]
    tm = 512
    nseq = seq // tm
    row = lambda i: (i, 0)
    full = lambda i: (0, 0)
    tab = lambda i: (i % nseq, 0)
    return pl.pallas_call(
        _proj0_body,
        grid=(T // tm,),
        in_specs=[
            pl.BlockSpec((tm, D_MODEL), row),
            pl.BlockSpec((D_MODEL, N_PLAIN), full),
            pl.BlockSpec((D_MODEL, N_ROPE), full),
            pl.BlockSpec((D_MODEL, N_ROPE), full),
            pl.BlockSpec((tm, N_ROPE), tab),
            pl.BlockSpec((tm, N_ROPE), tab),
        ],
        out_specs=[
            pl.BlockSpec((tm, 3 * SB_WIDTH), row),
            pl.BlockSpec((tm, 3 * LANES), row),
            pl.BlockSpec((tm, LANES), row),
            pl.BlockSpec((tm, NSA_WIDTH), row),
            pl.BlockSpec((tm, 3 * LANES), row),
        ],
        out_shape=[
            jax.ShapeDtypeStruct((T, 3 * SB_WIDTH), BF),
            jax.ShapeDtypeStruct((T, 3 * LANES), BF),
            jax.ShapeDtypeStruct((T, LANES), F32),
            jax.ShapeDtypeStruct((T, NSA_WIDTH), BF),
            jax.ShapeDtypeStruct((T, 3 * LANES), BF),
        ],
        compiler_params=_cparams(("parallel",)),
        name="proj0",
    )(x2, wp, wr, wrr, cos_t, sin_t)


def _sb_body(q_ref, k_ref, v_ref, o_ref, *, tq):
    i = pl.program_id(2)
    q = q_ref[0]
    lane = lax.broadcasted_iota(jnp.int32, (1, LANES), 1)
    row = lax.broadcasted_iota(jnp.int32, (tq, tq), 0)
    col = lax.broadcasted_iota(jnp.int32, (tq, tq), 1)
    tri = jnp.where(row > col, 1.0, 0.0).astype(BF)
    diag_causal = col < row

    hms = [(lane // HEAD_DIM) == hh for hh in range(2)]
    nchunk = tq // SB_ROWS
    chains = [(hh, rc) for hh in range(2) for rc in range(nchunk)]
    qcs = [jnp.where(hms[hh], q[rc * SB_ROWS:(rc + 1) * SB_ROWS], jnp.zeros((SB_ROWS, LANES), BF))
           for hh, rc in chains]

    def blocks(j, state, masks):
        off = pl.multiple_of(j * tq, tq)
        k = k_ref[0, pl.ds(off, tq), :]
        v = v_ref[0, pl.ds(off, tq), :]
        zs = [_dot_nt(qc, k) for qc in qcs]
        mids = []
        for z, mask in zip(zs, masks):
            soft = jnp.log(1.0 + jnp.exp(-jnp.abs(z)))
            log_1m = -(jnp.maximum(z, 0.0) + soft)
            log_sig = jnp.minimum(z, 0.0) - soft
            if mask is not None:
                log_1m = jnp.where(mask, log_1m, 0.0)
            mids.append((log_1m, log_sig) + _split2(log_1m))
        sufs = [_dot(hi, tri) + _dot(lo, tri) for _, _, hi, lo in mids]
        ws = []
        for n, ((log_1m, log_sig, _, _), suffix, mask) in enumerate(zip(mids, sufs, masks)):
            w = jnp.exp(log_sig + suffix + state[2 * n])
            if mask is not None:
                w = jnp.where(mask, w, 0.0)
            ws.append(w.astype(BF))
        new = []
        for n, w in enumerate(ws):
            new.append(state[2 * n] + jnp.sum(mids[n][0], axis=-1, keepdims=True))
            new.append(state[2 * n + 1] + _dot(w, v))
        return new

    zc, za = jnp.zeros((SB_ROWS, 1), F32), jnp.zeros((SB_ROWS, LANES), F32)
    state = blocks(i, [zc, za] * len(chains),
                   [diag_causal[rc * SB_ROWS:(rc + 1) * SB_ROWS] for _, rc in chains])

    def live(state):
        top = jnp.max(state[0])
        for c in state[2::2]:
            top = jnp.maximum(top, jnp.max(c))
        return top > SB_UNDERFLOW

    def cond(st):
        return jnp.logical_and(st[0] >= 0, st[1])

    def body(st):
        new = blocks(st[0], st[2], [None] * len(chains))
        return st[0] - 1, live(new), tuple(new)

    _, _, state = lax.while_loop(cond, body, (i - 1, live(state), tuple(state)))
    accs = [jnp.concatenate([state[2 * (hh * nchunk + rc) + 1] for rc in range(nchunk)], axis=0)
            for hh in range(2)]
    o_ref[0] = jnp.where(hms[0], accs[0], accs[1]).astype(BF)


def _sb_attention(qkva3):
    B, S, _ = qkva3.shape
    tq = 256
    npair = SB_WIDTH // LANES
    return pl.pallas_call(
        functools.partial(_sb_body, tq=tq),
        grid=(B, npair, S // tq),
        in_specs=[
            pl.BlockSpec((1, tq, LANES), lambda b, p, i: (b, i, p)),
            pl.BlockSpec((1, S, LANES), lambda b, p, i: (b, 0, npair + p)),
            pl.BlockSpec((1, S, LANES), lambda b, p, i: (b, 0, 2 * npair + p)),
        ],
        out_specs=pl.BlockSpec((1, tq, LANES), lambda b, p, i: (b, i, p)),
        out_shape=jax.ShapeDtypeStruct((B, S, SB_WIDTH), BF),
        compiler_params=_cparams(("parallel", "parallel", "arbitrary")),
        name="sb_attention",
    )(qkva3, qkva3, qkva3)


def _cmp_body(ak_ref, av_ref, posk_ref, posv_ref, w1k_ref, w1kt_ref, w1kb_ref, w2k_ref,
              w1v_ref, w1vt_ref, w1vb_ref, w2v_ref, kc_ref, vc_ref):
    def one(a_ref, pos_ref, w1_ref, w1t_ref, w1b_ref, w2_ref, o_ref):
        a = a_ref[0]
        n = a.shape[0]
        bias = _dot(pos_ref[...], w1_ref[...])[0:1]
        out = jnp.zeros((n, LANES), F32)
        for g in range(NSA_KV_GROUPS):
            top = _dot(a, w1t_ref[g])
            bot = _dot(a, w1b_ref[g])
            h = top + pltpu.roll(bot, n - 1, 0) + bias
            out = out + _dot(jax.nn.gelu(h).astype(BF), w2_ref[g])
        o_ref[0] = out.astype(BF)

    one(ak_ref, posk_ref, w1k_ref, w1kt_ref, w1kb_ref, w2k_ref, kc_ref)
    one(av_ref, posv_ref, w1v_ref, w1vt_ref, w1vb_ref, w2v_ref, vc_ref)


def _compress(ak, av, posk, posv, wk, wv):
    B, n, width = ak.shape
    blk = pl.BlockSpec((1, n, width), lambda b: (b, 0, 0))
    c2 = lambda shp: pl.BlockSpec(shp, lambda b: (0, 0))
    c3 = lambda shp: pl.BlockSpec(shp, lambda b: (0, 0, 0))
    wspecs = [c2((CMP_LEN * HEAD_DIM, CMP_HIDDEN)), c3((2, width, CMP_HIDDEN)),
              c3((2, width, CMP_HIDDEN)), c3((2, CMP_HIDDEN, LANES))]
    out = pl.BlockSpec((1, n, LANES), lambda b: (b, 0, 0))
    return pl.pallas_call(
        _cmp_body,
        grid=(B,),
        in_specs=[blk, blk, c2((8, CMP_LEN * HEAD_DIM)), c2((8, CMP_LEN * HEAD_DIM))] + wspecs + wspecs,
        out_specs=[out, out],
        out_shape=[jax.ShapeDtypeStruct((B, n, LANES), BF)] * 2,
        compiler_params=_cparams(("parallel",)),
        name="nsa_compress",
    )(ak, av, posk, posv, *wk, *wv)


def _nsa_body(q_ref, kc_ref, vc_ref, ks_ref, vs_ref, kw_ref, vw_ref, g_ref, ovl_ref, et_ref,
              gexp_ref, o_ref, *, tq, tk, nblk):
    i = pl.program_id(1)
    R = NSA_REP
    t0 = i * tq
    lane = lax.broadcasted_iota(jnp.int32, (1, LANES), 1)
    q = q_ref[0]
    ghi, glo = _split2(g_ref[0])
    gx = _dot(ghi, gexp_ref[...]) + _dot(glo, gexp_ref[...])
    t_col = t0 + lax.broadcasted_iota(jnp.int32, (tq, 1), 0)
    t_row = t0 + lax.broadcasted_iota(jnp.int32, (1, tq), 1)
    n_idx = lax.broadcasted_iota(jnp.int32, (1, LANES), 1)
    cmp_bias = jnp.where((CMP_STRIDE * n_idx + (CMP_LEN - 1)) <= t_col, 0.0, NEG_INF)
    has_cmp = jnp.where(t_col >= CMP_LEN - 1, 1.0, 0.0)
    nkb = (t0 + tq + tk - 1) // tk
    key_last = (nkb - 1) * tk + lax.broadcasted_iota(jnp.int32, (1, tk), 1)
    diag_bias = jnp.where(key_last <= t_col, 0.0, NEG_INF)
    wkeys = WINDOW + tq
    woff = pl.multiple_of(jnp.maximum(i - WINDOW // tq, 0) * tq, tq)
    kp = woff + lax.broadcasted_iota(jnp.int32, (1, wkeys), 1)
    win_bias = jnp.where(kp <= t_col, jnp.where(kp > t_col - WINDOW, 0.0, NEG_INF), NEG_INF)
    blk = lax.broadcasted_iota(jnp.int32, (nblk, 1), 0)
    cur = t_row // SLC_LEN
    valid = blk <= cur
    forced = (blk == 0) | (valid & (blk > cur - SLC_LOCAL))
    ovl = ovl_ref[...]
    k_win = kw_ref[0, pl.ds(woff, wkeys), :]
    v_win = vw_ref[0, pl.ds(woff, wkeys), :]
    q_heads, q_aug, o_cmp = [], [], []
    for gi in range(NSA_KV_GROUPS):
        gm = (lane // HEAD_DIM) == gi
        qrs = [jnp.where(gm, q[:, LANES * r:LANES * (r + 1)], jnp.zeros((tq, LANES), BF))
               for r in range(R)]
        psum = jnp.zeros((tq, LANES), F32)
        for qr in qrs:
            s = _dot_nt(qr, kc_ref[0]) + cmp_bias
            e = jnp.exp(s - jnp.max(s, axis=-1, keepdims=True))
            pc = e / jnp.sum(e, axis=-1, keepdims=True) * has_cmp
            o_cmp.append(_dot(pc.astype(BF), vc_ref[0]))
            psum = psum + pc
        p1 = psum.astype(BF)
        r1 = psum - p1.astype(F32)
        p2 = r1.astype(BF)
        p3 = (r1 - p2.astype(F32)).astype(BF)
        imp = _dot_nt(ovl, p1) + _dot_nt(ovl, p2) + _dot_nt(ovl, p3)
        imp = jnp.where(forced, imp + SLC_FORCE_BONUS, jnp.where(valid, imp, NEG_INF))
        sel = jnp.zeros((nblk, tq), F32)
        for _ in range(min(SLC_TOPN, nblk)):
            mx = jnp.max(imp, axis=0, keepdims=True)
            idx = jnp.min(jnp.where(imp == mx, blk, nblk), axis=0, keepdims=True)
            hit = blk == idx
            sel = jnp.where(hit, 1.0, sel)
            imp = jnp.where(hit, -jnp.inf, imp)
        sel_t = jnp.concatenate([sel, jnp.zeros((LANES - nblk, tq), F32)], axis=0).T
        sel_bias = ((sel_t - 1.0) * -NEG_INF).astype(BF)
        q_heads.extend(qrs)
        q_aug.extend(jnp.concatenate([qr, sel_bias], axis=1) for qr in qrs)

    def sel_step(j, state, bias):
        off = pl.multiple_of(j * tk, tk)
        ka = jnp.concatenate([ks_ref[0, pl.ds(off, tk), :], et_ref[pl.ds(off, tk), :]], axis=1)
        v = vs_ref[0, pl.ds(off, tk), :]
        scores = [_dot_nt(qa, ka) for qa in q_aug]
        if bias is not None:
            scores = [s + bias for s in scores]
        parts = []
        for s, (m, l, acc) in zip(scores, state):
            m_new = jnp.maximum(m, jnp.max(s, axis=-1, keepdims=True))
            alpha = jnp.exp(m - m_new)
            p = jnp.exp(s - m_new)
            parts.append((m_new, alpha * l + jnp.sum(p, axis=-1, keepdims=True), alpha, p.astype(BF)))
        return tuple((m_new, l, alpha * st[2] + _dot(p, v)) for (m_new, l, alpha, p), st in zip(parts, state))

    init = (jnp.full((tq, 1), NEG_INF, F32), jnp.zeros((tq, 1), F32), jnp.zeros((tq, LANES), F32))
    state = lax.fori_loop(0, nkb - 1, lambda j, st: sel_step(j, st, None), (init,) * len(q_aug))
    state = sel_step(nkb - 1, state, diag_bias)
    o_sel = [acc / l for _, l, acc in state]

    scores = [_dot_nt(qr, k_win) + win_bias for qr in q_heads]
    parts = []
    for s in scores:
        p = jnp.exp(s - jnp.max(s, axis=-1, keepdims=True))
        parts.append((p.astype(BF), jnp.sum(p, axis=-1, keepdims=True)))
    o_win = [_dot(p, v_win) / l for p, l in parts]

    g0 = (lane // HEAD_DIM) == 0
    for r in range(R):
        out = jnp.zeros((tq, LANES), F32)
        for br, o_br in enumerate((o_cmp, o_sel, o_win)):
            c = r * NSA_N_BRANCH + br
            out = out + gx[:, c * LANES:(c + 1) * LANES] * jnp.where(g0, o_br[r], o_br[R + r])
        o_ref[0, :, r * LANES:(r + 1) * LANES] = out.astype(BF)


def _nsa_attention(qb3, kc, vc, kr3, vals3, gate3, ovl, expand, gexp):
    B, S, _ = qb3.shape
    tq, tk = 128, 256
    n = kc.shape[1]
    nblk = ovl.shape[0]
    assert S >= WINDOW + tq and S % tk == 0 and nblk % 8 == 0
    tok = lambda c: pl.BlockSpec((1, S, LANES), lambda b, i, c=c: (b, 0, c))
    return pl.pallas_call(
        functools.partial(_nsa_body, tq=tq, tk=tk, nblk=nblk),
        grid=(B, S // tq),
        in_specs=[
            pl.BlockSpec((1, tq, NSA_WIDTH), lambda b, i: (b, i, 0)),
            pl.BlockSpec((1, n, LANES), lambda b, i: (b, 0, 0)),
            pl.BlockSpec((1, n, LANES), lambda b, i: (b, 0, 0)),
            tok(1), tok(1), tok(2), tok(2),
            pl.BlockSpec((1, tq, LANES), lambda b, i: (b, i, 0)),
            pl.BlockSpec((nblk, LANES), lambda b, i: (0, 0)),
            pl.BlockSpec((S, LANES), lambda b, i: (0, 0)),
            pl.BlockSpec((LANES, NSA_REP * NSA_N_BRANCH * LANES), lambda b, i: (0, 0)),
        ],
        out_specs=pl.BlockSpec((1, tq, NSA_WIDTH), lambda b, i: (b, i, 0)),
        out_shape=jax.ShapeDtypeStruct((B, S, NSA_WIDTH), BF),
        compiler_params=_cparams(("parallel", "arbitrary")),
        name="nsa_attention",
    )(qb3, kc, vc, kr3, vals3, kr3, vals3, gate3, ovl, expand, gexp)


def _outproj_ln_body(oa_ref, ob_ref, x_ref, wa_ref, wb_ref, g_ref, b_ref, o_ref):
    y = _dot(oa_ref[...], wa_ref[...]) + _dot(ob_ref[...], wb_ref[...])
    o_ref[...] = _layer_norm(DN_ALPHA * x_ref[...] + y, g_ref[...], b_ref[...])


def _outproj_ln(oa, ob, x2, wa, wb, g, b):
    T = x2.shape[0]
    tm = 512
    row = lambda i: (i, 0)
    full = lambda i: (0, 0)
    return pl.pallas_call(
        _outproj_ln_body,
        grid=(T // tm,),
        in_specs=[
            pl.BlockSpec((tm, oa.shape[1]), row),
            pl.BlockSpec((tm, ob.shape[1]), row),
            pl.BlockSpec((tm, D_MODEL), row),
            pl.BlockSpec(wa.shape, full),
            pl.BlockSpec(wb.shape, full),
            pl.BlockSpec((1, D_MODEL), full),
            pl.BlockSpec((1, D_MODEL), full),
        ],
        out_specs=pl.BlockSpec((tm, D_MODEL), row),
        out_shape=jax.ShapeDtypeStruct((T, D_MODEL), F32),
        compiler_params=_cparams(("parallel",)),
        name="outproj_ln",
    )(oa, ob, x2, wa, wb, g, b)


def _rope_tables(seq):
    inv = jnp.power(ROPE_THETA, -jnp.arange(0, ROPE_DIM, 2, dtype=F32) / ROPE_DIM)
    ang = jnp.arange(seq, dtype=F32)[:, None] * inv[None, :]
    half = ROPE_DIM // 2
    rest = HEAD_DIM - ROPE_DIM
    cos_h = jnp.concatenate([jnp.cos(ang), jnp.cos(ang), jnp.ones((seq, rest), F32)], axis=1)
    sin_h = jnp.concatenate([jnp.sin(ang), jnp.sin(ang), jnp.zeros((seq, rest), F32)], axis=1)
    reps = N_ROPE // HEAD_DIM
    del half
    return jnp.tile(cos_h, (1, reps)), jnp.tile(sin_h, (1, reps))


def _rot_cols(w):
    k, n = w.shape
    w3 = w.reshape(k, n // HEAD_DIM, HEAD_DIM)
    half = ROPE_DIM // 2
    rot = jnp.concatenate([-w3[..., half:ROPE_DIM], w3[..., :half],
                           jnp.zeros_like(w3[..., ROPE_DIM:])], axis=-1)
    return rot.reshape(k, n)


def _nsa_head_perm():
    cols = []
    for r in range(NSA_REP):
        for g in range(NSA_KV_GROUPS):
            h = g * NSA_REP + r
            cols.extend(range(h * HEAD_DIM, (h + 1) * HEAD_DIM))
    return np.asarray(cols)


def _mixer0(x2, batch, seq, w_in, cmp_pos_k, cmp_pos_v, cmp_k_w1, cmp_k_w2, cmp_v_w1, cmp_v_w2,
            w_out, ln_g, ln_b):
    T = x2.shape[0]
    sizes = [SB_WIDTH] * 3 + [NSA_WIDTH] + [NSA_KV_GROUPS * HEAD_DIM] * 6 + [NSA_HEADS * NSA_N_BRANCH]
    offs = np.concatenate([[0], np.cumsum(sizes)])
    col = lambda j: w_in[:, offs[j]:offs[j + 1]]
    perm = _nsa_head_perm()
    ngate = sizes[-1]
    wp = jnp.concatenate([col(0) * ATT_SCALE, col(1), col(2), col(5), col(7), col(9),
                          jnp.pad(col(10), ((0, 0), (0, LANES - ngate)))], axis=1)
    wr = jnp.concatenate([col(3)[:, perm] * ATT_SCALE, col(4), col(6), col(8)], axis=1)
    wrr = _rot_cols(wr)
    cos_t, sin_t = _rope_tables(seq)
    qkva, vals, gates, qb, kr = _proj0(x2, wp.astype(BF), wr.astype(BF), wrr.astype(BF), cos_t, sin_t, seq)

    r3 = lambda a: a.reshape(batch, seq, a.shape[-1])
    o_a = _sb_attention(r3(qkva))

    ncmp = seq // CMP_STRIDE
    ak = kr[:, 0:LANES].reshape(batch, ncmp, CMP_STRIDE * LANES)
    av = vals[:, 0:LANES].reshape(batch, ncmp, CMP_STRIDE * LANES)

    def cmp_weights(w1, w2):
        w1r = w1.reshape(2, CMP_STRIDE, HEAD_DIM, CMP_HIDDEN)
        tops, bots, w2s = [], [], []
        for g in range(NSA_KV_GROUPS):
            ext = jnp.zeros((2, CMP_STRIDE, NSA_KV_GROUPS, HEAD_DIM, CMP_HIDDEN), F32).at[:, :, g].set(w1r)
            ext = ext.reshape(2, CMP_STRIDE * LANES, CMP_HIDDEN)
            tops.append(ext[0])
            bots.append(ext[1])
            w2s.append(jnp.zeros((CMP_HIDDEN, LANES), F32).at[:, g * HEAD_DIM:(g + 1) * HEAD_DIM].set(w2))
        return (w1.astype(BF), jnp.stack(tops).astype(BF), jnp.stack(bots).astype(BF),
                jnp.stack(w2s).astype(BF))

    posk = jnp.broadcast_to(cmp_pos_k.reshape(1, -1), (8, CMP_LEN * HEAD_DIM)).astype(BF)
    posv = jnp.broadcast_to(cmp_pos_v.reshape(1, -1), (8, CMP_LEN * HEAD_DIM)).astype(BF)
    kc, vc = _compress(ak, av, posk, posv, cmp_weights(cmp_k_w1, cmp_k_w2), cmp_weights(cmp_v_w1, cmp_v_w2))
    assert ncmp <= LANES
    if ncmp < LANES:
        kc = jnp.pad(kc, ((0, 0), (0, LANES - ncmp), (0, 0)))
        vc = jnp.pad(vc, ((0, 0), (0, LANES - ncmp), (0, 0)))

    n_slc = seq // SLC_LEN
    cmp_start = np.arange(ncmp) * CMP_STRIDE
    slc_start = np.arange(n_slc) * SLC_LEN
    ovl = ((cmp_start[None, :] <= slc_start[:, None] + SLC_LEN - 1)
           & (cmp_start[None, :] + CMP_LEN - 1 >= slc_start[:, None])).astype(np.float32)
    ovl = np.pad(ovl, ((0, 0), (0, LANES - ncmp))) if ncmp < LANES else ovl
    expand = (np.arange(seq)[:, None] // SLC_LEN == np.arange(LANES)[None, :]).astype(np.float32)
    gexp = np.zeros((LANES, NSA_REP * NSA_N_BRANCH * LANES), np.float32)
    for r in range(NSA_REP):
        for br in range(NSA_N_BRANCH):
            c = r * NSA_N_BRANCH + br
            for g in range(NSA_KV_GROUPS):
                src = (g * NSA_REP + r) * NSA_N_BRANCH + br
                gexp[src, c * LANES + g * HEAD_DIM:c * LANES + (g + 1) * HEAD_DIM] = 1.0
    o_b = _nsa_attention(r3(qb), kc, vc, r3(kr), r3(vals), r3(gates), jnp.asarray(ovl, BF),
                         jnp.asarray(expand, BF), jnp.asarray(gexp, BF))

    wa = w_out[:SB_WIDTH].astype(BF)
    wb = w_out[SB_WIDTH:][perm].astype(BF)
    return _outproj_ln(o_a.reshape(T, SB_WIDTH), o_b.reshape(T, NSA_WIDTH), x2, wa, wb,
                       ln_g.reshape(1, -1), ln_b.reshape(1, -1))


MOE_TILE = 1024
ROW_ALIGN = 8
RANK_CHUNK = 256


def _top_rows(vals, ids, n_ids, count):
    hits = []
    for _ in range(count):
        mx = jnp.max(vals, axis=0, keepdims=True)
        idx = jnp.min(jnp.where(vals == mx, ids, n_ids), axis=0, keepdims=True)
        hit = ids == idx
        hits.append(hit)
        vals = jnp.where(hit, -jnp.inf, vals)
    return hits


def _router_body(h_ref, rt_ref, rb_ref, pos_ref, gate_ref, meta_ref, *, tm):
    E, NG = N_EXPERTS, N_EXPERT_GROUPS
    per = E // NG
    hh, hl = _split2(h_ref[...])
    rh, rl = _split2(rt_ref[...])
    logits = _dot_nt(rh, hh) + _dot_nt(rh, hl) + _dot_nt(rl, hh)
    scores = jax.nn.sigmoid(logits)
    biased = scores + rb_ref[...]
    i8 = lax.broadcasted_iota(jnp.int32, (per, tm), 0)
    gs = []
    for g in range(NG):
        v = biased[g * per:(g + 1) * per]
        m1 = jnp.max(v, axis=0, keepdims=True)
        a1 = jnp.min(jnp.where(v == m1, i8, per), axis=0, keepdims=True)
        m2 = jnp.max(jnp.where(i8 == a1, -jnp.inf, v), axis=0, keepdims=True)
        gs.append(m1 + m2)
    gs = jnp.concatenate(gs, axis=0)
    gi = lax.broadcasted_iota(jnp.int32, (NG, tm), 0)
    ghits = _top_rows(gs, gi, NG, TOPK_GROUPS)
    gkeep = jnp.zeros((NG, tm), F32)
    for hit in ghits:
        gkeep = jnp.where(hit, 1.0, gkeep)
    ekeep = jnp.concatenate([jnp.broadcast_to(gkeep[g:g + 1], (per, tm)) for g in range(NG)], axis=0)
    ei = lax.broadcasted_iota(jnp.int32, (E, tm), 0)
    hits = _top_rows(jnp.where(ekeep > 0.5, biased, -jnp.inf), ei, E, TOP_K)
    gates = [jnp.sum(jnp.where(hit, scores, 0.0), axis=0, keepdims=True) for hit in hits]
    gsum = gates[0]
    for gk in gates[1:]:
        gsum = gsum + gk
    gates = [gk / gsum * ROUTED_SCALE for gk in gates]

    member = jnp.zeros((E, tm), F32)
    for hit in hits:
        member = jnp.where(hit, 1.0, member)
    cnt_col = jnp.sum(member, axis=1, keepdims=True)
    pad_col = jnp.floor((cnt_col + (ROW_ALIGN - 1)) * (1.0 / ROW_ALIGN)) * ROW_ALIGN
    sub_e = lax.broadcasted_iota(jnp.int32, (E, LANES), 0)
    lane_e = lax.broadcasted_iota(jnp.int32, (E, LANES), 1)
    cnt_row = jnp.sum(jnp.where(sub_e == lane_e, cnt_col, 0.0), axis=0, keepdims=True)
    pad_row = jnp.sum(jnp.where(sub_e == lane_e, pad_col, 0.0), axis=0, keepdims=True)
    off_row = jnp.sum(jnp.where(sub_e < lane_e, pad_col, 0.0), axis=0, keepdims=True)
    off_col = jnp.sum(jnp.where(lane_e < sub_e, pad_row, 0.0), axis=1, keepdims=True)
    r_i = lax.broadcasted_iota(jnp.int32, (RANK_CHUNK, RANK_CHUNK), 0)
    c_i = lax.broadcasted_iota(jnp.int32, (RANK_CHUNK, RANK_CHUNK), 1)
    before = jnp.where(r_i < c_i, 1.0, 0.0).astype(BF)
    running = off_col
    ranks = []
    for c in range(tm // RANK_CHUNK):
        mc = member[:, c * RANK_CHUNK:(c + 1) * RANK_CHUNK]
        ranks.append(_dot(mc.astype(BF), before) + running)
        running = running + jnp.sum(mc, axis=1, keepdims=True)
    slot = jnp.concatenate(ranks, axis=1)
    pos = [jnp.sum(jnp.where(hit, slot, 0.0), axis=0, keepdims=True) for hit in hits]
    zrow = jnp.zeros((1, tm), F32)
    pos_ref[...] = jnp.concatenate(pos + [zrow, zrow], axis=0).astype(jnp.int32)
    gate_ref[...] = jnp.concatenate(gates + [zrow, zrow], axis=0)
    z128 = jnp.zeros((1, LANES), F32)
    meta_ref[0] = jnp.concatenate([off_row, cnt_row] + [z128] * 6, axis=0).astype(jnp.int32)


def _router(h2, router_t, bias_col):
    T = h2.shape[0]
    tm = MOE_TILE
    nt = T // tm
    return pl.pallas_call(
        functools.partial(_router_body, tm=tm),
        grid=(nt,),
        in_specs=[
            pl.BlockSpec((tm, D_MODEL), lambda i: (i, 0)),
            pl.BlockSpec((N_EXPERTS, D_MODEL), lambda i: (0, 0)),
            pl.BlockSpec((N_EXPERTS, 1), lambda i: (0, 0)),
        ],
        out_specs=[
            pl.BlockSpec((8, tm), lambda i: (0, i)),
            pl.BlockSpec((8, tm), lambda i: (0, i)),
            pl.BlockSpec((1, 8, LANES), lambda i: (i, 0, 0)),
        ],
        out_shape=[
            jax.ShapeDtypeStruct((8, T), jnp.int32),
            jax.ShapeDtypeStruct((8, T), F32),
            jax.ShapeDtypeStruct((nt, 8, LANES), jnp.int32),
        ],
        compiler_params=_cparams(("parallel",)),
        name="moe_router",
    )(h2, router_t, bias_col)


EXPERTS_PER_STEP = 4
EXPERT_CHUNK = 128
COMBINE_SUB = 256
HALF = D_MODEL // 2
NWORD = HALF // LANES
HI_MASK = 0xFFFF0000


def _pack_pairs(a, b):
    lo = pltpu.bitcast(a.astype(BF).astype(F32), jnp.uint32)
    hi = pltpu.bitcast(b.astype(BF).astype(F32), jnp.uint32)
    return (lo >> 16) | (hi & jnp.uint32(HI_MASK))


def _unpack_lo(w):
    return pltpu.bitcast(w << 16, F32)


def _unpack_hi(w):
    return pltpu.bitcast(w & jnp.uint32(HI_MASK), F32)


def _swiglu(xb, wgu, wd, hidden):
    gu = _dot(xb, wgu)
    a = jax.nn.silu(gu[:, :hidden]) * gu[:, hidden:]
    return _dot(a.astype(BF), wd)


def _moe_body(off_ref, cnt_ref, pos_ref, x_ref, gcol_ref, wgu_ref, wd_ref, wsgu_ref, wsd_ref,
              g_ref, b_ref, o_ref, src_ref, xs_ref, z_ref, *, tm, eb, ch, sub, unroll):
    i = pl.program_id(0)
    j = pl.program_id(1)

    @pl.when(j == 0)
    def _dispatch():
        x = x_ref[...]
        for c in range(NWORD):
            src_ref[c] = _pack_pairs(x[:, c * LANES:(c + 1) * LANES],
                                     x[:, HALF + c * LANES:HALF + (c + 1) * LANES])
        xs_ref[...] = jnp.zeros_like(xs_ref)

        def tok(tb, carry):
            for u in range(unroll):
                t = tb * unroll + u
                for k in range(TOP_K):
                    p = pos_ref[k, t]
                    for c in range(NWORD):
                        xs_ref[c, pl.ds(p, 1), :] = src_ref[c, pl.ds(t, 1), :]
            return carry

        lax.fori_loop(0, tm // unroll, tok, 0)

    def chunks_in(els, offs, c):
        r0s = [pl.multiple_of(off + c * ch, ROW_ALIGN) for off in offs]
        words = [[xs_ref[cc, pl.ds(r0, ch), :] for cc in range(NWORD)] for r0 in r0s]
        xbs = [jnp.concatenate([_unpack_lo(w).astype(BF) for w in ws]
                               + [_unpack_hi(w).astype(BF) for w in ws], axis=1) for ws in words]
        gus = [_dot(xb, wgu_ref[el]) for xb, el in zip(xbs, els)]
        acts = [(jax.nn.silu(gu[:, :EXPERT_HIDDEN]) * gu[:, EXPERT_HIDDEN:]).astype(BF) for gu in gus]
        ys = [_dot(a, wd_ref[el]) for a, el in zip(acts, els)]
        return list(zip(r0s, words, ys))

    def chunk_out(r0, words, y, c, cnt):
        keep = (c * ch + lax.broadcasted_iota(jnp.int32, (ch, 1), 0)) < cnt
        for cc in range(NWORD):
            packed = _pack_pairs(y[:, cc * LANES:(cc + 1) * LANES],
                                 y[:, HALF + cc * LANES:HALF + (cc + 1) * LANES])
            xs_ref[cc, pl.ds(r0, ch), :] = jnp.where(keep, packed, words[cc])

    offs = [off_ref[i * N_EXPERTS + j * eb + el] for el in range(eb)]
    cnts = [cnt_ref[i * N_EXPERTS + j * eb + el] for el in range(eb)]
    firsts = chunks_in(list(range(eb)), offs, 0)
    for el in range(eb):
        chunk_out(*firsts[el], 0, cnts[el])
    for el in range(eb):
        def chunk(c, carry, el=el):
            chunk_out(*chunks_in([el], [offs[el]], c)[0], c, cnts[el])
            return carry

        lax.fori_loop(1, (cnts[el] + ch - 1) // ch, chunk, 0)

    @pl.when(j == pl.num_programs(1) - 1)
    def _combine():
        for sb in range(tm // sub):
            def tok(tb, carry, sb=sb):
                for u in range(unroll):
                    tl = tb * unroll + u
                    for k in range(TOP_K):
                        p = pos_ref[k, sb * sub + tl]
                        for c in range(NWORD):
                            z_ref[k, c, pl.ds(tl, 1), :] = xs_ref[c, pl.ds(p, 1), :]
                return carry

            lax.fori_loop(0, sub // unroll, tok, 0)
            rows = slice(sb * sub, (sb + 1) * sub)
            gcol = gcol_ref[rows, :]
            lo = [jnp.zeros((sub, LANES), F32) for _ in range(NWORD)]
            hi = [jnp.zeros((sub, LANES), F32) for _ in range(NWORD)]
            for k in range(TOP_K):
                gk = gcol[:, k:k + 1]
                for c in range(NWORD):
                    w = z_ref[k, c]
                    lo[c] = lo[c] + gk * _unpack_lo(w)
                    hi[c] = hi[c] + gk * _unpack_hi(w)
            routed = jnp.concatenate(lo + hi, axis=1)
            x = x_ref[rows, :]
            shared = _swiglu(x.astype(BF), wsgu_ref[...], wsd_ref[...], SHARED_HIDDEN)
            o_ref[rows, :] = _layer_norm(DN_ALPHA * x + routed + shared, g_ref[...], b_ref[...])


def _moe_experts(h2, off, cnt, pos, gcol, wgu, wd, wsgu, wsd, g, b):
    T = h2.shape[0]
    tm, eb, ch, sub = MOE_TILE, EXPERTS_PER_STEP, EXPERT_CHUNK, COMBINE_SUB
    nt = T // tm
    rows = TOP_K * tm + N_EXPERTS * ROW_ALIGN + ch
    hidden2 = wgu.shape[-1]
    return pl.pallas_call(
        functools.partial(_moe_body, tm=tm, eb=eb, ch=ch, sub=sub, unroll=4),
        grid_spec=pltpu.PrefetchScalarGridSpec(
            num_scalar_prefetch=2,
            grid=(nt, N_EXPERTS // eb),
            in_specs=[
                pl.BlockSpec((8, tm), lambda i, j, *_: (0, i), memory_space=pltpu.SMEM),
                pl.BlockSpec((tm, D_MODEL), lambda i, j, *_: (i, 0)),
                pl.BlockSpec((tm, 8), lambda i, j, *_: (i, 0)),
                pl.BlockSpec((eb, D_MODEL, hidden2), lambda i, j, *_: (j, 0, 0)),
                pl.BlockSpec((eb, hidden2 // 2, D_MODEL), lambda i, j, *_: (j, 0, 0)),
                pl.BlockSpec(wsgu.shape, lambda i, j, *_: (0, 0)),
                pl.BlockSpec(wsd.shape, lambda i, j, *_: (0, 0)),
                pl.BlockSpec((1, D_MODEL), lambda i, j, *_: (0, 0)),
                pl.BlockSpec((1, D_MODEL), lambda i, j, *_: (0, 0)),
            ],
            out_specs=pl.BlockSpec((tm, D_MODEL), lambda i, j, *_: (i, 0)),
            scratch_shapes=[
                pltpu.VMEM((NWORD, tm, LANES), jnp.uint32),
                pltpu.VMEM((NWORD, rows, LANES), jnp.uint32),
                pltpu.VMEM((TOP_K, NWORD, sub, LANES), jnp.uint32),
            ],
        ),
        out_shape=jax.ShapeDtypeStruct((T, D_MODEL), F32),
        compiler_params=_cparams(("parallel", "arbitrary")),
        name="moe_experts",
    )(off, cnt, pos, h2, gcol, wgu, wd, wsgu, wsd, g, b)


S5_BATCH = 8
S5_STEPS = 32
S5_SLABS = 4
SLAB_CH = D_MODEL // S5_SLABS
SLAB_ST = SSM_GROUPS * SSM_STATE // S5_SLABS
N_STATE = SSM_GROUPS * SSM_STATE
SCAN_LANES = 512
SCAN_UNROLL = 8


def _s5_disc_body(lre_ref, lim_ref, ldt_ref, bre_ref, bim_ref, are_ref, aim_ref, bbre_ref, bbim_ref):
    lre, lim = lre_ref[...], lim_ref[...]
    step = jnp.exp(ldt_ref[...])
    mag = jnp.exp(lre * step)
    a_re = mag * jnp.cos(lim * step)
    a_im = mag * jnp.sin(lim * step)
    den = lre * lre + lim * lim
    zoh_re = ((a_re - 1.0) * lre + a_im * lim) / den
    zoh_im = (a_im * lre - (a_re - 1.0) * lim) / den
    are_ref[...] = a_re
    aim_ref[...] = a_im
    bbre_ref[...] = zoh_re * bre_ref[...] - zoh_im * bim_ref[...]
    bbim_ref[...] = zoh_re * bim_ref[...] + zoh_im * bre_ref[...]


def _s5_discretize(lambda_re, lambda_im, log_dt, b_re, b_im):
    col = lambda a: a.reshape(N_STATE, 1)
    ldt = jnp.broadcast_to(log_dt[:, None], (SSM_GROUPS, SSM_STATE))
    mat = lambda a: a.reshape(N_STATE, SSM_GROUP)
    c1 = jax.ShapeDtypeStruct((N_STATE, 1), F32)
    c16 = jax.ShapeDtypeStruct((N_STATE, SSM_GROUP), F32)
    return pl.pallas_call(_s5_disc_body, out_shape=[c1, c1, c16, c16], name="s5_discretize")(
        col(lambda_re), col(lambda_im), col(ldt), mat(b_re), mat(b_im))


def _s5_body(x_ref, win_ref, are_ref, aim_ref, bd_ref, cd_ref, dsk_ref, wglu_ref, wout_ref, g_ref, b_ref,
             o_ref, xs_ref, hre_ref, him_ref, sre_ref, sim_ref, *, lt):
    nb = S5_BATCH
    nlb = D_MODEL // LANES
    for c in range(nlb):
        for b in range(nb):
            xs_ref[c, pl.ds(b, lt, stride=nb), :] = x_ref[b, :, c * LANES:(c + 1) * LANES]
    x = jnp.concatenate([xs_ref[c] for c in range(nlb)], axis=1)
    u = _dot(x.astype(BF), win_ref[...])
    ub = u.astype(BF)
    for k in range(S5_SLABS):
        bu = _dot(ub[:, k * SLAB_CH:(k + 1) * SLAB_CH], bd_ref[k])
        hre_ref[:, k * SLAB_ST:(k + 1) * SLAB_ST] = bu[:, :SLAB_ST]
        him_ref[:, k * SLAB_ST:(k + 1) * SLAB_ST] = bu[:, SLAB_ST:]

    @pl.when(pl.program_id(1) == 0)
    def _():
        sre_ref[...] = jnp.zeros_like(sre_ref)
        sim_ref[...] = jnp.zeros_like(sim_ref)

    for c in range(N_STATE // SCAN_LANES):
        ls = slice(c * SCAN_LANES, (c + 1) * SCAN_LANES)
        a_re = jnp.broadcast_to(are_ref[:, ls], (nb, SCAN_LANES))
        a_im = jnp.broadcast_to(aim_ref[:, ls], (nb, SCAN_LANES))

        def steps(tb, state, ls=ls, a_re=a_re, a_im=a_im):
            s_re, s_im = state
            for uu in range(SCAN_UNROLL):
                r0 = pl.multiple_of((tb * SCAN_UNROLL + uu) * nb, nb)
                n_re = a_re * s_re - a_im * s_im + hre_ref[pl.ds(r0, nb), ls]
                n_im = a_re * s_im + a_im * s_re + him_ref[pl.ds(r0, nb), ls]
                hre_ref[pl.ds(r0, nb), ls] = n_re
                him_ref[pl.ds(r0, nb), ls] = n_im
                s_re, s_im = n_re, n_im
            return s_re, s_im

        s_re, s_im = lax.fori_loop(0, lt // SCAN_UNROLL, steps, (sre_ref[:, ls], sim_ref[:, ls]))
        sre_ref[:, ls] = s_re
        sim_ref[:, ls] = s_im

    ys = []
    for k in range(S5_SLABS):
        hk = jnp.concatenate([hre_ref[:, k * SLAB_ST:(k + 1) * SLAB_ST].astype(BF),
                              him_ref[:, k * SLAB_ST:(k + 1) * SLAB_ST].astype(BF)], axis=1)
        ys.append(_dot(hk, cd_ref[k]))
    y = jax.nn.gelu(jnp.concatenate(ys, axis=1) + dsk_ref[...] * u)
    y = y * jax.nn.sigmoid(_dot(y.astype(BF), wglu_ref[...]))
    mixed = _dot(y.astype(BF), wout_ref[...])
    res = _layer_norm(DN_ALPHA * x + mixed, g_ref[...], b_ref[...])
    for c in range(nlb):
        xs_ref[c] = res[:, c * LANES:(c + 1) * LANES]
    for c in range(nlb):
        for b in range(nb):
            o_ref[b, :, c * LANES:(c + 1) * LANES] = xs_ref[c, pl.ds(b, lt, stride=nb), :]


def _mixer1(x3, w_in, lambda_re, lambda_im, b_re, b_im, c_re, c_im, d_skip, log_dt, w_glu, w_out,
            ln_g, ln_b):
    B, S, _ = x3.shape
    lt = S5_STEPS
    rows = S5_BATCH * lt
    a_re, a_im, bb_re, bb_im = _s5_discretize(lambda_re, lambda_im, log_dt, b_re, b_im)
    gps = SSM_GROUPS // S5_SLABS
    eye = jnp.eye(gps, dtype=F32)

    def bdiag(bb):
        b4 = bb.reshape(S5_SLABS, gps, SSM_STATE, SSM_GROUP)
        return jnp.einsum('kgph,gf->kghfp', b4, eye).reshape(S5_SLABS, SLAB_CH, SLAB_ST)

    def cdiag(cc):
        c4 = cc.reshape(S5_SLABS, gps, SSM_GROUP, SSM_STATE)
        return jnp.einsum('kghp,gf->kfpgh', c4, eye).reshape(S5_SLABS, SLAB_ST, SLAB_CH)

    bd = jnp.concatenate([bdiag(bb_re), bdiag(bb_im)], axis=2).astype(BF)
    cd = jnp.concatenate([cdiag(c_re), -cdiag(c_im)], axis=1).astype(BF)
    c2 = lambda shp: pl.BlockSpec(shp, lambda bi, ti: (0,) * len(shp))
    return pl.pallas_call(
        functools.partial(_s5_body, lt=lt),
        grid=(B // S5_BATCH, S // lt),
        in_specs=[
            pl.BlockSpec((S5_BATCH, lt, D_MODEL), lambda bi, ti: (bi, ti, 0)),
            c2((D_MODEL, D_MODEL)), c2((1, N_STATE)), c2((1, N_STATE)),
            c2(bd.shape), c2(cd.shape), c2((1, D_MODEL)),
            c2((D_MODEL, D_MODEL)), c2((D_MODEL, D_MODEL)), c2((1, D_MODEL)), c2((1, D_MODEL)),
        ],
        out_specs=pl.BlockSpec((S5_BATCH, lt, D_MODEL), lambda bi, ti: (bi, ti, 0)),
        out_shape=jax.ShapeDtypeStruct((B, S, D_MODEL), F32),
        scratch_shapes=[
            pltpu.VMEM((D_MODEL // LANES, rows, LANES), F32),
            pltpu.VMEM((rows, N_STATE), F32),
            pltpu.VMEM((rows, N_STATE), F32),
            pltpu.VMEM((S5_BATCH, N_STATE), F32),
            pltpu.VMEM((S5_BATCH, N_STATE), F32),
        ],
        compiler_params=_cparams(("parallel", "arbitrary")),
        name="s5_mixer",
    )(x3, w_in.astype(BF), a_re.reshape(1, N_STATE), a_im.reshape(1, N_STATE), bd, cd,
      d_skip.reshape(1, D_MODEL), w_glu.astype(BF), w_out.astype(BF), ln_g.reshape(1, -1), ln_b.reshape(1, -1))


def _moe_block(h2, router, router_bias, w_gate, w_up, w_down, sh_gate, sh_up, sh_down, ln_g, ln_b):
    pos, gate, meta = _router(h2, router.T, router_bias.reshape(-1, 1))
    off = meta[:, 0, :N_EXPERTS].reshape(-1)
    cnt = meta[:, 1, :N_EXPERTS].reshape(-1)
    wgu = jnp.concatenate([w_gate, w_up], axis=-1).astype(BF)
    wsgu = jnp.concatenate([sh_gate, sh_up], axis=-1).astype(BF)
    return _moe_experts(h2, off, cnt, pos, gate.T, wgu, w_down.astype(BF), wsgu, sh_down.astype(BF),
                        ln_g.reshape(1, -1), ln_b.reshape(1, -1))


def kernel(x, l0_w_in, l0_cmp_pos_k, l0_cmp_pos_v, l0_cmp_k_w1, l0_cmp_k_w2, l0_cmp_v_w1, l0_cmp_v_w2, l0_w_out, l0_ln1_g, l0_ln1_b, l0_router, l0_router_bias, l0_w_gate, l0_w_up, l0_w_down, l0_sh_gate, l0_sh_up, l0_sh_down, l0_ln2_g, l0_ln2_b, l1_w_in, l1_lambda_re, l1_lambda_im, l1_b_re, l1_b_im, l1_c_re, l1_c_im, l1_d, l1_log_dt, l1_w_glu, l1_w_out, l1_ln1_g, l1_ln1_b, l1_router, l1_router_bias, l1_w_gate, l1_w_up, l1_w_down, l1_sh_gate, l1_sh_up, l1_sh_down, l1_ln2_g, l1_ln2_b):
    B, S, D = x.shape
    assert D == D_MODEL and S % 512 == 0 and B % S5_BATCH == 0 and (B * S) % MOE_TILE == 0
    T = B * S
    h = _mixer0(x.reshape(T, D), B, S, l0_w_in, l0_cmp_pos_k, l0_cmp_pos_v, l0_cmp_k_w1, l0_cmp_k_w2,
                l0_cmp_v_w1, l0_cmp_v_w2, l0_w_out, l0_ln1_g, l0_ln1_b)
    h = _moe_block(h, l0_router, l0_router_bias, l0_w_gate, l0_w_up, l0_w_down, l0_sh_gate, l0_sh_up,
                   l0_sh_down, l0_ln2_g, l0_ln2_b)
    h = _mixer1(h.reshape(B, S, D), l1_w_in, l1_lambda_re, l1_lambda_im, l1_b_re, l1_b_im, l1_c_re, l1_c_im,
                l1_d, l1_log_dt, l1_w_glu, l1_w_out, l1_ln1_g, l1_ln1_b)
    h = _moe_block(h.reshape(T, D), l1_router, l1_router_bias, l1_w_gate, l1_w_up, l1_w_down, l1_sh_gate,
                   l1_sh_up, l1_sh_down, l1_ln2_g, l1_ln2_b)
    return h.reshape(B, S, D)
```

```python
import functools
import math

import numpy as np
import jax
import jax.numpy as jnp
from jax import lax
from jax.experimental import pallas as pl
from jax.experimental.pallas import tpu as pltpu

F32 = jnp.float32
BF = jnp.bfloat16

D_MODEL = 1024
DEPTH = 2
HEAD_DIM = 64
LANES = 128
SB_HEADS = 8
SB_WIDTH = SB_HEADS * HEAD_DIM
NSA_HEADS = 8
NSA_KV_GROUPS = 2
NSA_REP = NSA_HEADS // NSA_KV_GROUPS
NSA_WIDTH = NSA_HEADS * HEAD_DIM
NSA_N_BRANCH = 3
CMP_LEN = 32
CMP_STRIDE = 16
CMP_HIDDEN = 256
SLC_LEN = 64
SLC_TOPN = 8
SLC_LOCAL = 2
SLC_FORCE_BONUS = 1e4
WINDOW = 512
ROPE_THETA = 500000.0
ROPE_DIM = HEAD_DIM // 4
SSM_GROUP = 16
SSM_GROUPS = D_MODEL // SSM_GROUP
SSM_STATE = 64
N_EXPERTS = 64
N_EXPERT_GROUPS = 8
TOPK_GROUPS = 4
TOP_K = 6
EXPERT_HIDDEN = 256
SHARED_HIDDEN = 256
ROUTED_SCALE = 2.5
DN_ALPHA = (2 * DEPTH) ** 0.25
LN_EPS = 1e-5
NEG_INF = -1e30
ATT_SCALE = HEAD_DIM ** -0.5
SB_UNDERFLOW = -104.0
SB_ROWS = 128

V7X_VMEM_BYTES = 64 * 1024 * 1024
VMEM_LIMIT = V7X_VMEM_BYTES - 8 * 1024 * 1024
MOE_VMEM_LIMIT = V7X_VMEM_BYTES - 4 * 1024 * 1024


def _cparams(sem, vmem=VMEM_LIMIT):
    return pltpu.CompilerParams(dimension_semantics=sem, vmem_limit_bytes=vmem)


def _dot(a, b):
    return jnp.dot(a, b, preferred_element_type=F32)


def _dot_nt(a, b):
    return lax.dot_general(a, b, (((1,), (1,)), ((), ())), preferred_element_type=F32)


def _split2(x):
    hi = x.astype(BF)
    lo = (x - hi.astype(F32)).astype(BF)
    return hi, lo


def _layer_norm(h, g, b):
    mu = jnp.mean(h, axis=-1, keepdims=True)
    d = h - mu
    var = jnp.mean(d * d, axis=-1, keepdims=True)
    return d * lax.rsqrt(var + LN_EPS) * g + b


HALF = D_MODEL // 2
NWORD = HALF // LANES
HI_MASK = 0xFFFF0000


def _pack_pairs(a, b):
    lo = pltpu.bitcast(a.astype(BF).astype(F32), jnp.uint32)
    hi = pltpu.bitcast(b.astype(BF).astype(F32), jnp.uint32)
    return (lo >> 16) | (hi & jnp.uint32(HI_MASK))


def _pack_row_words(h):
    return [_pack_pairs(h[:, c * LANES:(c + 1) * LANES], h[:, HALF + c * LANES:HALF + (c + 1) * LANES])
            for c in range(NWORD)]


def _store_row_words(ref, start, n, h):
    for c, w in enumerate(_pack_row_words(h)):
        ref[pl.ds(start * NWORD + c, n, stride=NWORD), :] = w


def _load_row_words(ref, start, n):
    return [ref[pl.ds(start * NWORD + c, n, stride=NWORD), :] for c in range(NWORD)]


def _unpack_lo(w):
    return pltpu.bitcast(w << 16, F32)


def _unpack_hi(w):
    return pltpu.bitcast(w & jnp.uint32(HI_MASK), F32)


N_PLAIN = 3 * SB_WIDTH + 3 * LANES + LANES
N_ROPE = NSA_WIDTH + 3 * LANES


def _proj0_body(x_ref, wp_ref, wr_ref, wrr_ref, cos_ref, sin_ref,
                qkva_ref, vals_ref, gate_ref, qb_ref, kr_ref):
    xb = x_ref[...].astype(BF)
    a0, a1, a2 = 3 * SB_WIDTH, 3 * SB_WIDTH + 3 * LANES, N_PLAIN
    qkva_ref[...] = _dot(xb, wp_ref[:, 0:a0]).astype(BF)
    vals_ref[...] = _dot(xb, wp_ref[:, a0:a1]).astype(BF)
    gate_ref[...] = jax.nn.sigmoid(_dot(xb, wp_ref[:, a1:a2]))
    y = _dot(xb, wr_ref[...]) * cos_ref[...] + _dot(xb, wrr_ref[...]) * sin_ref[...]
    qb_ref[...] = y[:, 0:NSA_WIDTH].astype(BF)
    kr_ref[...] = y[:, NSA_WIDTH:N_ROPE].astype(BF)


def _proj0(x2, wp, wr, wrr, cos_t, sin_t, seq):
    T = x2.shape[0]
    tm = 512
    nseq = seq // tm
    row = lambda i: (i, 0)
    full = lambda i: (0, 0)
    tab = lambda i: (i % nseq, 0)
    return pl.pallas_call(
        _proj0_body,
        grid=(T // tm,),
        in_specs=[
            pl.BlockSpec((tm, D_MODEL), row),
            pl.BlockSpec((D_MODEL, N_PLAIN), full),
            pl.BlockSpec((D_MODEL, N_ROPE), full),
            pl.BlockSpec((D_MODEL, N_ROPE), full),
            pl.BlockSpec((tm, N_ROPE), tab),
            pl.BlockSpec((tm, N_ROPE), tab),
        ],
        out_specs=[
            pl.BlockSpec((tm, 3 * SB_WIDTH), row),
            pl.BlockSpec((tm, 3 * LANES), row),
            pl.BlockSpec((tm, LANES), row),
            pl.BlockSpec((tm, NSA_WIDTH), row),
            pl.BlockSpec((tm, 3 * LANES), row),
        ],
        out_shape=[
            jax.ShapeDtypeStruct((T, 3 * SB_WIDTH), BF),
            jax.ShapeDtypeStruct((T, 3 * LANES), BF),
            jax.ShapeDtypeStruct((T, LANES), F32),
            jax.ShapeDtypeStruct((T, NSA_WIDTH), BF),
            jax.ShapeDtypeStruct((T, 3 * LANES), BF),
        ],
        compiler_params=_cparams(("parallel",)),
        name="proj0",
    )(x2, wp, wr, wrr, cos_t, sin_t)


def _sb_body(q_ref, k_ref, v_ref, o_ref, *, tq):
    i = pl.program_id(2)
    q = q_ref[0]
    lane = lax.broadcasted_iota(jnp.int32, (1, LANES), 1)
    row = lax.broadcasted_iota(jnp.int32, (tq, tq), 0)
    col = lax.broadcasted_iota(jnp.int32, (tq, tq), 1)
    tri = jnp.where(row > col, 1.0, 0.0).astype(BF)
    diag_causal = col < row

    hms = [(lane // HEAD_DIM) == hh for hh in range(2)]
    nchunk = tq // SB_ROWS
    chains = [(hh, rc) for hh in range(2) for rc in range(nchunk)]
    qcs = [jnp.where(hms[hh], q[rc * SB_ROWS:(rc + 1) * SB_ROWS], jnp.zeros((SB_ROWS, LANES), BF))
           for hh, rc in chains]

    def blocks(j, state, masks):
        off = pl.multiple_of(j * tq, tq)
        k = k_ref[0, pl.ds(off, tq), :]
        v = v_ref[0, pl.ds(off, tq), :]
        zs = [_dot_nt(qc, k) for qc in qcs]
        mids = []
        for z, mask in zip(zs, masks):
            soft = jnp.log(1.0 + jnp.exp(-jnp.abs(z)))
            log_1m = -(jnp.maximum(z, 0.0) + soft)
            log_sig = jnp.minimum(z, 0.0) - soft
            if mask is not None:
                log_1m = jnp.where(mask, log_1m, 0.0)
            mids.append((log_1m, log_sig) + _split2(log_1m))
        sufs = [_dot(hi, tri) + _dot(lo, tri) for _, _, hi, lo in mids]
        ws = []
        for n, ((log_1m, log_sig, _, _), suffix, mask) in enumerate(zip(mids, sufs, masks)):
            w = jnp.exp(log_sig + suffix + state[2 * n])
            if mask is not None:
                w = jnp.where(mask, w, 0.0)
            ws.append(w.astype(BF))
        new = []
        for n, w in enumerate(ws):
            new.append(state[2 * n] + jnp.sum(mids[n][0], axis=-1, keepdims=True))
            new.append(state[2 * n + 1] + _dot(w, v))
        return new

    zc, za = jnp.zeros((SB_ROWS, 1), F32), jnp.zeros((SB_ROWS, LANES), F32)
    state = blocks(i, [zc, za] * len(chains),
                   [diag_causal[rc * SB_ROWS:(rc + 1) * SB_ROWS] for _, rc in chains])

    def live(state):
        top = jnp.max(state[0])
        for c in state[2::2]:
            top = jnp.maximum(top, jnp.max(c))
        return top > SB_UNDERFLOW

    def cond(st):
        return jnp.logical_and(st[0] >= 0, st[1])

    def body(st):
        new = blocks(st[0], st[2], [None] * len(chains))
        return st[0] - 1, live(new), tuple(new)

    _, _, state = lax.while_loop(cond, body, (i - 1, live(state), tuple(state)))
    accs = [jnp.concatenate([state[2 * (hh * nchunk + rc) + 1] for rc in range(nchunk)], axis=0)
            for hh in range(2)]
    o_ref[0] = jnp.where(hms[0], accs[0], accs[1]).astype(BF)


def _sb_attention(qkva3):
    B, S, _ = qkva3.shape
    tq = 256
    npair = SB_WIDTH // LANES
    return pl.pallas_call(
        functools.partial(_sb_body, tq=tq),
        grid=(B, npair, S // tq),
        in_specs=[
            pl.BlockSpec((1, tq, LANES), lambda b, p, i: (b, i, p)),
            pl.BlockSpec((1, S, LANES), lambda b, p, i: (b, 0, npair + p)),
            pl.BlockSpec((1, S, LANES), lambda b, p, i: (b, 0, 2 * npair + p)),
        ],
        out_specs=pl.BlockSpec((1, tq, LANES), lambda b, p, i: (b, i, p)),
        out_shape=jax.ShapeDtypeStruct((B, S, SB_WIDTH), BF),
        compiler_params=_cparams(("parallel", "parallel", "arbitrary")),
        name="sb_attention",
    )(qkva3, qkva3, qkva3)


def _cmp_body(ak_ref, av_ref, posk_ref, posv_ref, w1k_ref, w1kt_ref, w1kb_ref, w2k_ref,
              w1v_ref, w1vt_ref, w1vb_ref, w2v_ref, kc_ref, vc_ref):
    def one(a_ref, pos_ref, w1_ref, w1t_ref, w1b_ref, w2_ref, o_ref):
        a = a_ref[0]
        n = a.shape[0]
        bias = _dot(pos_ref[...], w1_ref[...])[0:1]
        out = jnp.zeros((n, LANES), F32)
        for g in range(NSA_KV_GROUPS):
            top = _dot(a, w1t_ref[g])
            bot = _dot(a, w1b_ref[g])
            h = top + pltpu.roll(bot, n - 1, 0) + bias
            out = out + _dot(jax.nn.gelu(h).astype(BF), w2_ref[g])
        o_ref[0] = out.astype(BF)

    one(ak_ref, posk_ref, w1k_ref, w1kt_ref, w1kb_ref, w2k_ref, kc_ref)
    one(av_ref, posv_ref, w1v_ref, w1vt_ref, w1vb_ref, w2v_ref, vc_ref)


def _compress(ak, av, posk, posv, wk, wv):
    B, n, width = ak.shape
    blk = pl.BlockSpec((1, n, width), lambda b: (b, 0, 0))
    c2 = lambda shp: pl.BlockSpec(shp, lambda b: (0, 0))
    c3 = lambda shp: pl.BlockSpec(shp, lambda b: (0, 0, 0))
    wspecs = [c2((CMP_LEN * HEAD_DIM, CMP_HIDDEN)), c3((2, width, CMP_HIDDEN)),
              c3((2, width, CMP_HIDDEN)), c3((2, CMP_HIDDEN, LANES))]
    out = pl.BlockSpec((1, n, LANES), lambda b: (b, 0, 0))
    return pl.pallas_call(
        _cmp_body,
        grid=(B,),
        in_specs=[blk, blk, c2((8, CMP_LEN * HEAD_DIM)), c2((8, CMP_LEN * HEAD_DIM))] + wspecs + wspecs,
        out_specs=[out, out],
        out_shape=[jax.ShapeDtypeStruct((B, n, LANES), BF)] * 2,
        compiler_params=_cparams(("parallel",)),
        name="nsa_compress",
    )(ak, av, posk, posv, *wk, *wv)


def _nsa_body(q_ref, kc_ref, vc_ref, ks_ref, vs_ref, kw_ref, vw_ref, g_ref, ovl_ref, et_ref,
              gexp_ref, o_ref, *, tq, tk, nblk):
    i = pl.program_id(1)
    R = NSA_REP
    t0 = i * tq
    lane = lax.broadcasted_iota(jnp.int32, (1, LANES), 1)
    q = q_ref[0]
    ghi, glo = _split2(g_ref[0])
    gx = _dot(ghi, gexp_ref[...]) + _dot(glo, gexp_ref[...])
    t_col = t0 + lax.broadcasted_iota(jnp.int32, (tq, 1), 0)
    t_row = t0 + lax.broadcasted_iota(jnp.int32, (1, tq), 1)
    n_idx = lax.broadcasted_iota(jnp.int32, (1, LANES), 1)
    cmp_bias = jnp.where((CMP_STRIDE * n_idx + (CMP_LEN - 1)) <= t_col, 0.0, NEG_INF)
    has_cmp = jnp.where(t_col >= CMP_LEN - 1, 1.0, 0.0)
    nkb = (t0 + tq + tk - 1) // tk
    key_last = (nkb - 1) * tk + lax.broadcasted_iota(jnp.int32, (1, tk), 1)
    diag_bias = jnp.where(key_last <= t_col, 0.0, NEG_INF)
    wkeys = WINDOW + tq
    woff = pl.multiple_of(jnp.maximum(i - WINDOW // tq, 0) * tq, tq)
    kp = woff + lax.broadcasted_iota(jnp.int32, (1, wkeys), 1)
    win_bias = jnp.where(kp <= t_col, jnp.where(kp > t_col - WINDOW, 0.0, NEG_INF), NEG_INF)
    blk = lax.broadcasted_iota(jnp.int32, (nblk, 1), 0)
    cur = t_row // SLC_LEN
    valid = blk <= cur
    forced = (blk == 0) | (valid & (blk > cur - SLC_LOCAL))
    ovl = ovl_ref[...]
    k_win = kw_ref[0, pl.ds(woff, wkeys), :]
    v_win = vw_ref[0, pl.ds(woff, wkeys), :]
    q_heads, q_aug, o_cmp = [], [], []
    for gi in range(NSA_KV_GROUPS):
        gm = (lane // HEAD_DIM) == gi
        qrs = [jnp.where(gm, q[:, LANES * r:LANES * (r + 1)], jnp.zeros((tq, LANES), BF))
               for r in range(R)]
        psum = jnp.zeros((tq, LANES), F32)
        for qr in qrs:
            s = _dot_nt(qr, kc_ref[0]) + cmp_bias
            e = jnp.exp(s - jnp.max(s, axis=-1, keepdims=True))
            pc = e / jnp.sum(e, axis=-1, keepdims=True) * has_cmp
            o_cmp.append(_dot(pc.astype(BF), vc_ref[0]))
            psum = psum + pc
        p1 = psum.astype(BF)
        r1 = psum - p1.astype(F32)
        p2 = r1.astype(BF)
        p3 = (r1 - p2.astype(F32)).astype(BF)
        imp = _dot_nt(ovl, p1) + _dot_nt(ovl, p2) + _dot_nt(ovl, p3)
        imp = jnp.where(forced, imp + SLC_FORCE_BONUS, jnp.where(valid, imp, NEG_INF))
        sel = jnp.zeros((nblk, tq), F32)
        for _ in range(min(SLC_TOPN, nblk)):
            mx = jnp.max(imp, axis=0, keepdims=True)
            idx = jnp.min(jnp.where(imp == mx, blk, nblk), axis=0, keepdims=True)
            hit = blk == idx
            sel = jnp.where(hit, 1.0, sel)
            imp = jnp.where(hit, -jnp.inf, imp)
        sel_t = jnp.concatenate([sel, jnp.zeros((LANES - nblk, tq), F32)], axis=0).T
        sel_bias = ((sel_t - 1.0) * -NEG_INF).astype(BF)
        q_heads.extend(qrs)
        q_aug.extend(jnp.concatenate([qr, sel_bias], axis=1) for qr in qrs)

    def sel_step(j, state, bias):
        off = pl.multiple_of(j * tk, tk)
        ka = jnp.concatenate([ks_ref[0, pl.ds(off, tk), :], et_ref[pl.ds(off, tk), :]], axis=1)
        v = vs_ref[0, pl.ds(off, tk), :]
        scores = [_dot_nt(qa, ka) for qa in q_aug]
        if bias is not None:
            scores = [s + bias for s in scores]
        parts = []
        for s, (m, l, acc) in zip(scores, state):
            m_new = jnp.maximum(m, jnp.max(s, axis=-1, keepdims=True))
            alpha = jnp.exp(m - m_new)
            p = jnp.exp(s - m_new)
            parts.append((m_new, alpha * l + jnp.sum(p, axis=-1, keepdims=True), alpha, p.astype(BF)))
        return tuple((m_new, l, alpha * st[2] + _dot(p, v)) for (m_new, l, alpha, p), st in zip(parts, state))

    init = (jnp.full((tq, 1), NEG_INF, F32), jnp.zeros((tq, 1), F32), jnp.zeros((tq, LANES), F32))
    state = lax.fori_loop(0, nkb - 1, lambda j, st: sel_step(j, st, None), (init,) * len(q_aug))
    state = sel_step(nkb - 1, state, diag_bias)
    o_sel = [acc / l for _, l, acc in state]

    scores = [_dot_nt(qr, k_win) + win_bias for qr in q_heads]
    parts = []
    for s in scores:
        p = jnp.exp(s - jnp.max(s, axis=-1, keepdims=True))
        parts.append((p.astype(BF), jnp.sum(p, axis=-1, keepdims=True)))
    o_win = [_dot(p, v_win) / l for p, l in parts]

    g0 = (lane // HEAD_DIM) == 0
    for r in range(R):
        out = jnp.zeros((tq, LANES), F32)
        for br, o_br in enumerate((o_cmp, o_sel, o_win)):
            c = r * NSA_N_BRANCH + br
            out = out + gx[:, c * LANES:(c + 1) * LANES] * jnp.where(g0, o_br[r], o_br[R + r])
        o_ref[0, :, r * LANES:(r + 1) * LANES] = out.astype(BF)


def _nsa_attention(qb3, kc, vc, kr3, vals3, gate3, ovl, expand, gexp):
    B, S, _ = qb3.shape
    tq, tk = 128, 256
    n = kc.shape[1]
    nblk = ovl.shape[0]
    assert S >= WINDOW + tq and S % tk == 0 and nblk % 8 == 0
    tok = lambda c: pl.BlockSpec((1, S, LANES), lambda b, i, c=c: (b, 0, c))
    return pl.pallas_call(
        functools.partial(_nsa_body, tq=tq, tk=tk, nblk=nblk),
        grid=(B, S // tq),
        in_specs=[
            pl.BlockSpec((1, tq, NSA_WIDTH), lambda b, i: (b, i, 0)),
            pl.BlockSpec((1, n, LANES), lambda b, i: (b, 0, 0)),
            pl.BlockSpec((1, n, LANES), lambda b, i: (b, 0, 0)),
            tok(1), tok(1), tok(2), tok(2),
            pl.BlockSpec((1, tq, LANES), lambda b, i: (b, i, 0)),
            pl.BlockSpec((nblk, LANES), lambda b, i: (0, 0)),
            pl.BlockSpec((S, LANES), lambda b, i: (0, 0)),
            pl.BlockSpec((LANES, NSA_REP * NSA_N_BRANCH * LANES), lambda b, i: (0, 0)),
        ],
        out_specs=pl.BlockSpec((1, tq, NSA_WIDTH), lambda b, i: (b, i, 0)),
        out_shape=jax.ShapeDtypeStruct((B, S, NSA_WIDTH), BF),
        compiler_params=_cparams(("parallel", "arbitrary")),
        name="nsa_attention",
    )(qb3, kc, vc, kr3, vals3, kr3, vals3, gate3, ovl, expand, gexp)


def _outproj_ln_body(oa_ref, ob_ref, x_ref, wa_ref, wb_ref, g_ref, b_ref, o_ref, p_ref):
    y = _dot(oa_ref[...], wa_ref[...]) + _dot(ob_ref[...], wb_ref[...])
    res = _layer_norm(DN_ALPHA * x_ref[...] + y, g_ref[...], b_ref[...])
    o_ref[...] = res
    _store_row_words(p_ref, 0, res.shape[0], res)


def _outproj_ln(oa, ob, x2, wa, wb, g, b):
    T = x2.shape[0]
    tm = 512
    row = lambda i: (i, 0)
    full = lambda i: (0, 0)
    return pl.pallas_call(
        _outproj_ln_body,
        grid=(T // tm,),
        in_specs=[
            pl.BlockSpec((tm, oa.shape[1]), row),
            pl.BlockSpec((tm, ob.shape[1]), row),
            pl.BlockSpec((tm, D_MODEL), row),
            pl.BlockSpec(wa.shape, full),
            pl.BlockSpec(wb.shape, full),
            pl.BlockSpec((1, D_MODEL), full),
            pl.BlockSpec((1, D_MODEL), full),
        ],
        out_specs=[pl.BlockSpec((tm, D_MODEL), row),
                   pl.BlockSpec((tm * NWORD, LANES), row)],
        out_shape=[jax.ShapeDtypeStruct((T, D_MODEL), F32),
                   jax.ShapeDtypeStruct((T * NWORD, LANES), jnp.uint32)],
        compiler_params=_cparams(("parallel",)),
        name="outproj_ln",
    )(oa, ob, x2, wa, wb, g, b)


def _rope_tables(seq):
    inv = jnp.power(ROPE_THETA, -jnp.arange(0, ROPE_DIM, 2, dtype=F32) / ROPE_DIM)
    ang = jnp.arange(seq, dtype=F32)[:, None] * inv[None, :]
    half = ROPE_DIM // 2
    rest = HEAD_DIM - ROPE_DIM
    cos_h = jnp.concatenate([jnp.cos(ang), jnp.cos(ang), jnp.ones((seq, rest), F32)], axis=1)
    sin_h = jnp.concatenate([jnp.sin(ang), jnp.sin(ang), jnp.zeros((seq, rest), F32)], axis=1)
    reps = N_ROPE // HEAD_DIM
    del half
    return jnp.tile(cos_h, (1, reps)), jnp.tile(sin_h, (1, reps))


def _rot_cols(w):
    k, n = w.shape
    w3 = w.reshape(k, n // HEAD_DIM, HEAD_DIM)
    half = ROPE_DIM // 2
    rot = jnp.concatenate([-w3[..., half:ROPE_DIM], w3[..., :half],
                           jnp.zeros_like(w3[..., ROPE_DIM:])], axis=-1)
    return rot.reshape(k, n)


def _nsa_head_perm():
    cols = []
    for r in range(NSA_REP):
        for g in range(NSA_KV_GROUPS):
            h = g * NSA_REP + r
            cols.extend(range(h * HEAD_DIM, (h + 1) * HEAD_DIM))
    return np.asarray(cols)


def _mixer0(x2, batch, seq, w_in, cmp_pos_k, cmp_pos_v, cmp_k_w1, cmp_k_w2, cmp_v_w1, cmp_v_w2,
            w_out, ln_g, ln_b):
    T = x2.shape[0]
    sizes = [SB_WIDTH] * 3 + [NSA_WIDTH] + [NSA_KV_GROUPS * HEAD_DIM] * 6 + [NSA_HEADS * NSA_N_BRANCH]
    offs = np.concatenate([[0], np.cumsum(sizes)])
    col = lambda j: w_in[:, offs[j]:offs[j + 1]]
    perm = _nsa_head_perm()
    ngate = sizes[-1]
    wp = jnp.concatenate([col(0) * ATT_SCALE, col(1), col(2), col(5), col(7), col(9),
                          jnp.pad(col(10), ((0, 0), (0, LANES - ngate)))], axis=1)
    wr = jnp.concatenate([col(3)[:, perm] * ATT_SCALE, col(4), col(6), col(8)], axis=1)
    wrr = _rot_cols(wr)
    cos_t, sin_t = _rope_tables(seq)
    qkva, vals, gates, qb, kr = _proj0(x2, wp.astype(BF), wr.astype(BF), wrr.astype(BF), cos_t, sin_t, seq)

    r3 = lambda a: a.reshape(batch, seq, a.shape[-1])
    o_a = _sb_attention(r3(qkva))

    ncmp = seq // CMP_STRIDE
    ak = kr[:, 0:LANES].reshape(batch, ncmp, CMP_STRIDE * LANES)
    av = vals[:, 0:LANES].reshape(batch, ncmp, CMP_STRIDE * LANES)

    def cmp_weights(w1, w2):
        w1r = w1.reshape(2, CMP_STRIDE, HEAD_DIM, CMP_HIDDEN)
        tops, bots, w2s = [], [], []
        for g in range(NSA_KV_GROUPS):
            ext = jnp.zeros((2, CMP_STRIDE, NSA_KV_GROUPS, HEAD_DIM, CMP_HIDDEN), F32).at[:, :, g].set(w1r)
            ext = ext.reshape(2, CMP_STRIDE * LANES, CMP_HIDDEN)
            tops.append(ext[0])
            bots.append(ext[1])
            w2s.append(jnp.zeros((CMP_HIDDEN, LANES), F32).at[:, g * HEAD_DIM:(g + 1) * HEAD_DIM].set(w2))
        return (w1.astype(BF), jnp.stack(tops).astype(BF), jnp.stack(bots).astype(BF),
                jnp.stack(w2s).astype(BF))

    posk = jnp.broadcast_to(cmp_pos_k.reshape(1, -1), (8, CMP_LEN * HEAD_DIM)).astype(BF)
    posv = jnp.broadcast_to(cmp_pos_v.reshape(1, -1), (8, CMP_LEN * HEAD_DIM)).astype(BF)
    kc, vc = _compress(ak, av, posk, posv, cmp_weights(cmp_k_w1, cmp_k_w2), cmp_weights(cmp_v_w1, cmp_v_w2))
    assert ncmp <= LANES
    if ncmp < LANES:
        kc = jnp.pad(kc, ((0, 0), (0, LANES - ncmp), (0, 0)))
        vc = jnp.pad(vc, ((0, 0), (0, LANES - ncmp), (0, 0)))

    n_slc = seq // SLC_LEN
    cmp_start = np.arange(ncmp) * CMP_STRIDE
    slc_start = np.arange(n_slc) * SLC_LEN
    ovl = ((cmp_start[None, :] <= slc_start[:, None] + SLC_LEN - 1)
           & (cmp_start[None, :] + CMP_LEN - 1 >= slc_start[:, None])).astype(np.float32)
    ovl = np.pad(ovl, ((0, 0), (0, LANES - ncmp))) if ncmp < LANES else ovl
    expand = (np.arange(seq)[:, None] // SLC_LEN == np.arange(LANES)[None, :]).astype(np.float32)
    gexp = np.zeros((LANES, NSA_REP * NSA_N_BRANCH * LANES), np.float32)
    for r in range(NSA_REP):
        for br in range(NSA_N_BRANCH):
            c = r * NSA_N_BRANCH + br
            for g in range(NSA_KV_GROUPS):
                src = (g * NSA_REP + r) * NSA_N_BRANCH + br
                gexp[src, c * LANES + g * HEAD_DIM:c * LANES + (g + 1) * HEAD_DIM] = 1.0
    o_b = _nsa_attention(r3(qb), kc, vc, r3(kr), r3(vals), r3(gates), jnp.asarray(ovl, BF),
                         jnp.asarray(expand, BF), jnp.asarray(gexp, BF))

    wa = w_out[:SB_WIDTH].astype(BF)
    wb = w_out[SB_WIDTH:][perm].astype(BF)
    return _outproj_ln(o_a.reshape(T, SB_WIDTH), o_b.reshape(T, NSA_WIDTH), x2, wa, wb,
                       ln_g.reshape(1, -1), ln_b.reshape(1, -1))


MOE_TILE = 2048
ROW_ALIGN = 8
RANK_CHUNK = 256


def _top_rows(vals, ids, n_ids, count):
    hits = []
    for _ in range(count):
        mx = jnp.max(vals, axis=0, keepdims=True)
        idx = jnp.min(jnp.where(vals == mx, ids, n_ids), axis=0, keepdims=True)
        hit = ids == idx
        hits.append(hit)
        vals = jnp.where(hit, -jnp.inf, vals)
    return hits


def _router_body(h_ref, rt_ref, rb_ref, pos_ref, gate_ref, meta_ref, *, tm):
    E, NG = N_EXPERTS, N_EXPERT_GROUPS
    per = E // NG
    hh, hl = _split2(h_ref[...])
    rh, rl = _split2(rt_ref[...])
    logits = _dot_nt(rh, hh) + _dot_nt(rh, hl) + _dot_nt(rl, hh)
    scores = jax.nn.sigmoid(logits)
    biased = scores + rb_ref[...]
    i8 = lax.broadcasted_iota(jnp.int32, (per, tm), 0)
    gs = []
    for g in range(NG):
        v = biased[g * per:(g + 1) * per]
        m1 = jnp.max(v, axis=0, keepdims=True)
        a1 = jnp.min(jnp.where(v == m1, i8, per), axis=0, keepdims=True)
        m2 = jnp.max(jnp.where(i8 == a1, -jnp.inf, v), axis=0, keepdims=True)
        gs.append(m1 + m2)
    gs = jnp.concatenate(gs, axis=0)
    gi = lax.broadcasted_iota(jnp.int32, (NG, tm), 0)
    ghits = _top_rows(gs, gi, NG, TOPK_GROUPS)
    gkeep = jnp.zeros((NG, tm), F32)
    for hit in ghits:
        gkeep = jnp.where(hit, 1.0, gkeep)
    ekeep = jnp.concatenate([jnp.broadcast_to(gkeep[g:g + 1], (per, tm)) for g in range(NG)], axis=0)
    ei = lax.broadcasted_iota(jnp.int32, (E, tm), 0)
    hits = _top_rows(jnp.where(ekeep > 0.5, biased, -jnp.inf), ei, E, TOP_K)
    gates = [jnp.sum(jnp.where(hit, scores, 0.0), axis=0, keepdims=True) for hit in hits]
    gsum = gates[0]
    for gk in gates[1:]:
        gsum = gsum + gk
    gates = [gk / gsum * ROUTED_SCALE for gk in gates]

    member = jnp.zeros((E, tm), F32)
    for hit in hits:
        member = jnp.where(hit, 1.0, member)
    cnt_col = jnp.sum(member, axis=1, keepdims=True)
    pad_col = jnp.floor((cnt_col + (ROW_ALIGN - 1)) * (1.0 / ROW_ALIGN)) * ROW_ALIGN
    sub_e = lax.broadcasted_iota(jnp.int32, (E, LANES), 0)
    lane_e = lax.broadcasted_iota(jnp.int32, (E, LANES), 1)
    cnt_row = jnp.sum(jnp.where(sub_e == lane_e, cnt_col, 0.0), axis=0, keepdims=True)
    pad_row = jnp.sum(jnp.where(sub_e == lane_e, pad_col, 0.0), axis=0, keepdims=True)
    off_row = jnp.sum(jnp.where(sub_e < lane_e, pad_col, 0.0), axis=0, keepdims=True)
    off_col = jnp.sum(jnp.where(lane_e < sub_e, pad_row, 0.0), axis=1, keepdims=True)
    r_i = lax.broadcasted_iota(jnp.int32, (RANK_CHUNK, RANK_CHUNK), 0)
    c_i = lax.broadcasted_iota(jnp.int32, (RANK_CHUNK, RANK_CHUNK), 1)
    before = jnp.where(r_i < c_i, 1.0, 0.0).astype(BF)
    running = off_col
    ranks = []
    for c in range(tm // RANK_CHUNK):
        mc = member[:, c * RANK_CHUNK:(c + 1) * RANK_CHUNK]
        ranks.append(_dot(mc.astype(BF), before) + running)
        running = running + jnp.sum(mc, axis=1, keepdims=True)
    slot = jnp.concatenate(ranks, axis=1)
    pos = [jnp.sum(jnp.where(hit, slot, 0.0), axis=0, keepdims=True) for hit in hits]
    zrow = jnp.zeros((1, tm), F32)
    pos_ref[...] = jnp.concatenate(pos + [zrow, zrow], axis=0).astype(jnp.int32)
    gate_ref[...] = jnp.concatenate(gates + [zrow, zrow], axis=0)
    z128 = jnp.zeros((1, LANES), F32)
    meta_ref[0] = jnp.concatenate([off_row, cnt_row] + [z128] * 6, axis=0).astype(jnp.int32)


def _router(h2, router_t, bias_col):
    T = h2.shape[0]
    tm = MOE_TILE
    nt = T // tm
    return pl.pallas_call(
        functools.partial(_router_body, tm=tm),
        grid=(nt,),
        in_specs=[
            pl.BlockSpec((tm, D_MODEL), lambda i: (i, 0)),
            pl.BlockSpec((N_EXPERTS, D_MODEL), lambda i: (0, 0)),
            pl.BlockSpec((N_EXPERTS, 1), lambda i: (0, 0)),
        ],
        out_specs=[
            pl.BlockSpec((8, tm), lambda i: (0, i)),
            pl.BlockSpec((8, tm), lambda i: (0, i)),
            pl.BlockSpec((1, 8, LANES), lambda i: (i, 0, 0)),
        ],
        out_shape=[
            jax.ShapeDtypeStruct((8, T), jnp.int32),
            jax.ShapeDtypeStruct((8, T), F32),
            jax.ShapeDtypeStruct((nt, 8, LANES), jnp.int32),
        ],
        compiler_params=_cparams(("parallel",)),
        name="moe_router",
    )(h2, router_t, bias_col)


EXPERTS_PER_STEP = 4
EXPERT_CHUNK = 256
COMBINE_SUB = 256
POS_STRIDE = 8


def _swiglu(xb, wgu, wd, hidden):
    gu = _dot(xb, wgu)
    a = jax.nn.silu(gu[:, :hidden]) * gu[:, hidden:]
    return _dot(a.astype(BF), wd)


def _moe_body(off_ref, cnt_ref, pos_ref, src_ref, x_ref, gcol_ref, wgu_ref, wd_ref, wsgu_ref, wsd_ref,
              g_ref, b_ref, o_ref, xs_ref, z_ref, *, tm, eb, ch, sub, unroll):
    i = pl.program_id(0)
    j = pl.program_id(1)
    nj = N_EXPERTS // eb

    @pl.when(j == 0)
    def _dispatch():
        xs_ref[...] = jnp.zeros_like(xs_ref)

        def tok(tb, carry):
            for u in range(unroll):
                t = tb * unroll + u
                slab = src_ref[pl.ds(pl.multiple_of(t * NWORD, NWORD), NWORD), :]
                for k in range(TOP_K):
                    p = pl.multiple_of(pos_ref[t * POS_STRIDE + k], NWORD)
                    xs_ref[pl.ds(p, NWORD), :] = slab
            return carry

        lax.fori_loop(0, tm // unroll, tok, 0)

    def chunks_in(els, offs, c):
        r0s = [pl.multiple_of(off + c * ch, ROW_ALIGN) for off in offs]
        words = [_load_row_words(xs_ref, r0, ch) for r0 in r0s]
        xbs = [jnp.concatenate([_unpack_lo(w).astype(BF) for w in ws]
                               + [_unpack_hi(w).astype(BF) for w in ws], axis=1) for ws in words]
        gus = [_dot(xb, wgu_ref[el]) for xb, el in zip(xbs, els)]
        acts = [(jax.nn.silu(gu[:, :EXPERT_HIDDEN]) * gu[:, EXPERT_HIDDEN:]).astype(BF) for gu in gus]
        ys = [_dot(a, wd_ref[el]) for a, el in zip(acts, els)]
        return list(zip(r0s, words, ys))

    def chunk_out(r0, words, y, c, cnt):
        keep = (c * ch + lax.broadcasted_iota(jnp.int32, (ch, 1), 0)) < cnt
        for cc, packed in enumerate(_pack_row_words(y)):
            xs_ref[pl.ds(r0 * NWORD + cc, ch, stride=NWORD), :] = jnp.where(keep, packed, words[cc])

    @pl.when(j < nj)
    def _experts():
        offs = [off_ref[i * N_EXPERTS + j * eb + el] for el in range(eb)]
        cnts = [cnt_ref[i * N_EXPERTS + j * eb + el] for el in range(eb)]
        firsts = chunks_in(list(range(eb)), offs, 0)
        for el in range(eb):
            chunk_out(*firsts[el], 0, cnts[el])
        for el in range(eb):
            def chunk(c, carry, el=el):
                chunk_out(*chunks_in([el], [offs[el]], c)[0], c, cnts[el])
                return carry

            lax.fori_loop(1, (cnts[el] + ch - 1) // ch, chunk, 0)

    @pl.when(j >= nj)
    def _combine():
        base = (j - nj) * sub

        def tok(tb, carry):
            for u in range(unroll):
                tl = tb * unroll + u
                dst = pl.multiple_of(tl * NWORD, NWORD)
                for k in range(TOP_K):
                    p = pl.multiple_of(pos_ref[(base + tl) * POS_STRIDE + k], NWORD)
                    z_ref[k, pl.ds(dst, NWORD), :] = xs_ref[pl.ds(p, NWORD), :]
            return carry

        lax.fori_loop(0, sub // unroll, tok, 0)
        gcol = gcol_ref[...]
        lo = [jnp.zeros((sub, LANES), F32) for _ in range(NWORD)]
        hi = [jnp.zeros((sub, LANES), F32) for _ in range(NWORD)]
        for k in range(TOP_K):
            gk = gcol[:, k:k + 1]
            for c, w in enumerate(_load_row_words(z_ref.at[k], 0, sub)):
                lo[c] = lo[c] + gk * _unpack_lo(w)
                hi[c] = hi[c] + gk * _unpack_hi(w)
        routed = jnp.concatenate(lo + hi, axis=1)
        x = x_ref[...]
        shared = _swiglu(x.astype(BF), wsgu_ref[...], wsd_ref[...], SHARED_HIDDEN)
        o_ref[...] = _layer_norm(DN_ALPHA * x + routed + shared, g_ref[...], b_ref[...])


def _moe_experts(h2, packed, off, cnt, pos, gcol, wgu, wd, wsgu, wsd, g, b):
    T = h2.shape[0]
    tm, eb, ch, sub = MOE_TILE, EXPERTS_PER_STEP, EXPERT_CHUNK, COMBINE_SUB
    nt, nj, nsub = T // tm, N_EXPERTS // eb, tm // sub
    rows = TOP_K * tm + N_EXPERTS * ROW_ALIGN + ch
    hidden2 = wgu.shape[-1]
    wblk = lambda i, j, *_: (jnp.minimum(j, nj - 1), 0, 0)
    sub_i = lambda i, j: i * nsub + jnp.clip(j - nj, 0, nsub - 1)
    once = pl.Buffered(1)
    return pl.pallas_call(
        functools.partial(_moe_body, tm=tm, eb=eb, ch=ch, sub=sub, unroll=4),
        grid_spec=pltpu.PrefetchScalarGridSpec(
            num_scalar_prefetch=2,
            grid=(nt, nj + nsub),
            in_specs=[
                pl.BlockSpec((tm * POS_STRIDE,), lambda i, j, *_: (i,), memory_space=pltpu.SMEM),
                pl.BlockSpec((tm * NWORD, LANES), lambda i, j, *_: (i, 0), pipeline_mode=once),
                pl.BlockSpec((sub, D_MODEL), lambda i, j, *_: (sub_i(i, j), 0)),
                pl.BlockSpec((sub, 8), lambda i, j, *_: (sub_i(i, j), 0)),
                pl.BlockSpec((eb, D_MODEL, hidden2), wblk),
                pl.BlockSpec((eb, hidden2 // 2, D_MODEL), wblk),
                pl.BlockSpec(wsgu.shape, lambda i, j, *_: (0, 0), pipeline_mode=once),
                pl.BlockSpec(wsd.shape, lambda i, j, *_: (0, 0), pipeline_mode=once),
                pl.BlockSpec((1, D_MODEL), lambda i, j, *_: (0, 0)),
                pl.BlockSpec((1, D_MODEL), lambda i, j, *_: (0, 0)),
            ],
            out_specs=pl.BlockSpec((sub, D_MODEL), lambda i, j, *_: (sub_i(i, j), 0)),
            scratch_shapes=[
                pltpu.VMEM((rows * NWORD, LANES), jnp.uint32),
                pltpu.VMEM((TOP_K, sub * NWORD, LANES), jnp.uint32),
            ],
        ),
        out_shape=jax.ShapeDtypeStruct((T, D_MODEL), F32),
        compiler_params=_cparams(("parallel", "arbitrary"), MOE_VMEM_LIMIT),
        name="moe_experts",
    )(off, cnt, pos, packed, h2, gcol, wgu, wd, wsgu, wsd, g, b)


S5_BATCH = 8
S5_STEPS = 32
S5_SLABS = 4
SLAB_CH = D_MODEL // S5_SLABS
SLAB_ST = SSM_GROUPS * SSM_STATE // S5_SLABS
N_STATE = SSM_GROUPS * SSM_STATE
SCAN_LANES = 512
SCAN_UNROLL = 8


def _s5_disc_body(lre_ref, lim_ref, ldt_ref, bre_ref, bim_ref, are_ref, aim_ref, bbre_ref, bbim_ref):
    lre, lim = lre_ref[...], lim_ref[...]
    step = jnp.exp(ldt_ref[...])
    mag = jnp.exp(lre * step)
    a_re = mag * jnp.cos(lim * step)
    a_im = mag * jnp.sin(lim * step)
    den = lre * lre + lim * lim
    zoh_re = ((a_re - 1.0) * lre + a_im * lim) / den
    zoh_im = (a_im * lre - (a_re - 1.0) * lim) / den
    are_ref[...] = a_re
    aim_ref[...] = a_im
    bbre_ref[...] = zoh_re * bre_ref[...] - zoh_im * bim_ref[...]
    bbim_ref[...] = zoh_re * bim_ref[...] + zoh_im * bre_ref[...]


def _s5_discretize(lambda_re, lambda_im, log_dt, b_re, b_im):
    col = lambda a: a.reshape(N_STATE, 1)
    ldt = jnp.broadcast_to(log_dt[:, None], (SSM_GROUPS, SSM_STATE))
    mat = lambda a: a.reshape(N_STATE, SSM_GROUP)
    c1 = jax.ShapeDtypeStruct((N_STATE, 1), F32)
    c16 = jax.ShapeDtypeStruct((N_STATE, SSM_GROUP), F32)
    return pl.pallas_call(_s5_disc_body, out_shape=[c1, c1, c16, c16], name="s5_discretize")(
        col(lambda_re), col(lambda_im), col(ldt), mat(b_re), mat(b_im))


def _s5_body(x_ref, win_ref, are_ref, aim_ref, bd_ref, cd_ref, dsk_ref, wglu_ref, wout_ref, g_ref, b_ref,
             o_ref, p_ref, xs_ref, pk_ref, hre_ref, him_ref, sre_ref, sim_ref, *, lt):
    nb = S5_BATCH
    nlb = D_MODEL // LANES
    for c in range(nlb):
        for b in range(nb):
            xs_ref[c, pl.ds(b, lt, stride=nb), :] = x_ref[b, :, c * LANES:(c + 1) * LANES]
    x = jnp.concatenate([xs_ref[c] for c in range(nlb)], axis=1)
    u = _dot(x.astype(BF), win_ref[...])
    ub = u.astype(BF)
    for k in range(S5_SLABS):
        bu = _dot(ub[:, k * SLAB_CH:(k + 1) * SLAB_CH], bd_ref[k])
        hre_ref[:, k * SLAB_ST:(k + 1) * SLAB_ST] = bu[:, :SLAB_ST]
        him_ref[:, k * SLAB_ST:(k + 1) * SLAB_ST] = bu[:, SLAB_ST:]

    @pl.when(pl.program_id(1) == 0)
    def _():
        sre_ref[...] = jnp.zeros_like(sre_ref)
        sim_ref[...] = jnp.zeros_like(sim_ref)

    for c in range(N_STATE // SCAN_LANES):
        ls = slice(c * SCAN_LANES, (c + 1) * SCAN_LANES)
        a_re = jnp.broadcast_to(are_ref[:, ls], (nb, SCAN_LANES))
        a_im = jnp.broadcast_to(aim_ref[:, ls], (nb, SCAN_LANES))

        def steps(tb, state, ls=ls, a_re=a_re, a_im=a_im):
            s_re, s_im = state
            for uu in range(SCAN_UNROLL):
                r0 = pl.multiple_of((tb * SCAN_UNROLL + uu) * nb, nb)
                n_re = a_re * s_re - a_im * s_im + hre_ref[pl.ds(r0, nb), ls]
                n_im = a_re * s_im + a_im * s_re + him_ref[pl.ds(r0, nb), ls]
                hre_ref[pl.ds(r0, nb), ls] = n_re
                him_ref[pl.ds(r0, nb), ls] = n_im
                s_re, s_im = n_re, n_im
            return s_re, s_im

        s_re, s_im = lax.fori_loop(0, lt // SCAN_UNROLL, steps, (sre_ref[:, ls], sim_ref[:, ls]))
        sre_ref[:, ls] = s_re
        sim_ref[:, ls] = s_im

    ys = []
    for k in range(S5_SLABS):
        hk = jnp.concatenate([hre_ref[:, k * SLAB_ST:(k + 1) * SLAB_ST].astype(BF),
                              him_ref[:, k * SLAB_ST:(k + 1) * SLAB_ST].astype(BF)], axis=1)
        ys.append(_dot(hk, cd_ref[k]))
    y = jax.nn.gelu(jnp.concatenate(ys, axis=1) + dsk_ref[...] * u)
    y = y * jax.nn.sigmoid(_dot(y.astype(BF), wglu_ref[...]))
    mixed = _dot(y.astype(BF), wout_ref[...])
    res = _layer_norm(DN_ALPHA * x + mixed, g_ref[...], b_ref[...])
    for c in range(nlb):
        xs_ref[c] = res[:, c * LANES:(c + 1) * LANES]
    for c, w in enumerate(_pack_row_words(res)):
        pk_ref[c] = w
    for c in range(nlb):
        for b in range(nb):
            o_ref[b, :, c * LANES:(c + 1) * LANES] = xs_ref[c, pl.ds(b, lt, stride=nb), :]
    for c in range(NWORD):
        for b in range(nb):
            p_ref[b, pl.ds(c, lt, stride=NWORD), :] = pk_ref[c, pl.ds(b, lt, stride=nb), :]


def _mixer1(x3, w_in, lambda_re, lambda_im, b_re, b_im, c_re, c_im, d_skip, log_dt, w_glu, w_out,
            ln_g, ln_b):
    B, S, _ = x3.shape
    lt = S5_STEPS
    rows = S5_BATCH * lt
    a_re, a_im, bb_re, bb_im = _s5_discretize(lambda_re, lambda_im, log_dt, b_re, b_im)
    gps = SSM_GROUPS // S5_SLABS
    eye = jnp.eye(gps, dtype=F32)

    def bdiag(bb):
        b4 = bb.reshape(S5_SLABS, gps, SSM_STATE, SSM_GROUP)
        return jnp.einsum('kgph,gf->kghfp', b4, eye).reshape(S5_SLABS, SLAB_CH, SLAB_ST)

    def cdiag(cc):
        c4 = cc.reshape(S5_SLABS, gps, SSM_GROUP, SSM_STATE)
        return jnp.einsum('kghp,gf->kfpgh', c4, eye).reshape(S5_SLABS, SLAB_ST, SLAB_CH)

    bd = jnp.concatenate([bdiag(bb_re), bdiag(bb_im)], axis=2).astype(BF)
    cd = jnp.concatenate([cdiag(c_re), -cdiag(c_im)], axis=1).astype(BF)
    c2 = lambda shp: pl.BlockSpec(shp, lambda bi, ti: (0,) * len(shp))
    return pl.pallas_call(
        functools.partial(_s5_body, lt=lt),
        grid=(B // S5_BATCH, S // lt),
        in_specs=[
            pl.BlockSpec((S5_BATCH, lt, D_MODEL), lambda bi, ti: (bi, ti, 0)),
            c2((D_MODEL, D_MODEL)), c2((1, N_STATE)), c2((1, N_STATE)),
            c2(bd.shape), c2(cd.shape), c2((1, D_MODEL)),
            c2((D_MODEL, D_MODEL)), c2((D_MODEL, D_MODEL)), c2((1, D_MODEL)), c2((1, D_MODEL)),
        ],
        out_specs=[pl.BlockSpec((S5_BATCH, lt, D_MODEL), lambda bi, ti: (bi, ti, 0)),
                   pl.BlockSpec((S5_BATCH, lt * NWORD, LANES), lambda bi, ti: (bi, ti, 0))],
        out_shape=[jax.ShapeDtypeStruct((B, S, D_MODEL), F32),
                   jax.ShapeDtypeStruct((B, S * NWORD, LANES), jnp.uint32)],
        scratch_shapes=[
            pltpu.VMEM((D_MODEL // LANES, rows, LANES), F32),
            pltpu.VMEM((NWORD, rows, LANES), jnp.uint32),
            pltpu.VMEM((rows, N_STATE), F32),
            pltpu.VMEM((rows, N_STATE), F32),
            pltpu.VMEM((S5_BATCH, N_STATE), F32),
            pltpu.VMEM((S5_BATCH, N_STATE), F32),
        ],
        compiler_params=_cparams(("parallel", "arbitrary")),
        name="s5_mixer",
    )(x3, w_in.astype(BF), a_re.reshape(1, N_STATE), a_im.reshape(1, N_STATE), bd, cd,
      d_skip.reshape(1, D_MODEL), w_glu.astype(BF), w_out.astype(BF), ln_g.reshape(1, -1), ln_b.reshape(1, -1))


def _moe_block(h2, packed, router, router_bias, w_gate, w_up, w_down, sh_gate, sh_up, sh_down, ln_g, ln_b):
    pos, gate, meta = _router(h2, router.T, router_bias.reshape(-1, 1))
    off = meta[:, 0, :N_EXPERTS].reshape(-1)
    cnt = meta[:, 1, :N_EXPERTS].reshape(-1)
    wgu = jnp.concatenate([w_gate, w_up], axis=-1).astype(BF)
    wsgu = jnp.concatenate([sh_gate, sh_up], axis=-1).astype(BF)
    pos = (pos.T * NWORD).reshape(-1)
    return _moe_experts(h2, packed, off, cnt, pos, gate.T, wgu, w_down.astype(BF), wsgu, sh_down.astype(BF),
                        ln_g.reshape(1, -1), ln_b.reshape(1, -1))


def kernel(x, l0_w_in, l0_cmp_pos_k, l0_cmp_pos_v, l0_cmp_k_w1, l0_cmp_k_w2, l0_cmp_v_w1, l0_cmp_v_w2, l0_w_out, l0_ln1_g, l0_ln1_b, l0_router, l0_router_bias, l0_w_gate, l0_w_up, l0_w_down, l0_sh_gate, l0_sh_up, l0_sh_down, l0_ln2_g, l0_ln2_b, l1_w_in, l1_lambda_re, l1_lambda_im, l1_b_re, l1_b_im, l1_c_re, l1_c_im, l1_d, l1_log_dt, l1_w_glu, l1_w_out, l1_ln1_g, l1_ln1_b, l1_router, l1_router_bias, l1_w_gate, l1_w_up, l1_w_down, l1_sh_gate, l1_sh_up, l1_sh_down, l1_ln2_g, l1_ln2_b):
    B, S, D = x.shape
    assert D == D_MODEL and S % 512 == 0 and B % S5_BATCH == 0 and (B * S) % MOE_TILE == 0
    T = B * S
    h, hp = _mixer0(x.reshape(T, D), B, S, l0_w_in, l0_cmp_pos_k, l0_cmp_pos_v, l0_cmp_k_w1, l0_cmp_k_w2,
                    l0_cmp_v_w1, l0_cmp_v_w2, l0_w_out, l0_ln1_g, l0_ln1_b)
    h = _moe_block(h, hp, l0_router, l0_router_bias, l0_w_gate, l0_w_up, l0_w_down, l0_sh_gate, l0_sh_up,
                   l0_sh_down, l0_ln2_g, l0_ln2_b)
    h, hp = _mixer1(h.reshape(B, S, D), l1_w_in, l1_lambda_re, l1_lambda_im, l1_b_re, l1_b_im, l1_c_re,
                    l1_c_im, l1_d, l1_log_dt, l1_w_glu, l1_w_out, l1_ln1_g, l1_ln1_b)
    h = _moe_block(h.reshape(T, D), hp.reshape(T * NWORD, LANES), l1_router, l1_router_bias, l1_w_gate, l1_w_up,
                   l1_w_down, l1_sh_gate, l1_sh_up, l1_sh_down, l1_ln2_g, l1_ln2_b)
    return h.reshape(B, S, D)
```

```python
import functools
import math

import numpy as np
import jax
import jax.numpy as jnp
from jax import lax
from jax.experimental import pallas as pl
from jax.experimental.pallas import tpu as pltpu

F32 = jnp.float32
BF = jnp.bfloat16

D_MODEL = 1024
DEPTH = 2
HEAD_DIM = 64
LANES = 128
SB_HEADS = 8
SB_WIDTH = SB_HEADS * HEAD_DIM
NSA_HEADS = 8
NSA_KV_GROUPS = 2
NSA_REP = NSA_HEADS // NSA_KV_GROUPS
NSA_WIDTH = NSA_HEADS * HEAD_DIM
NSA_N_BRANCH = 3
CMP_LEN = 32
CMP_STRIDE = 16
CMP_HIDDEN = 256
SLC_LEN = 64
SLC_TOPN = 8
SLC_LOCAL = 2
SLC_FORCE_BONUS = 1e4
WINDOW = 512
ROPE_THETA = 500000.0
ROPE_DIM = HEAD_DIM // 4
SSM_GROUP = 16
SSM_GROUPS = D_MODEL // SSM_GROUP
SSM_STATE = 64
N_EXPERTS = 64
N_EXPERT_GROUPS = 8
TOPK_GROUPS = 4
TOP_K = 6
EXPERT_HIDDEN = 256
SHARED_HIDDEN = 256
ROUTED_SCALE = 2.5
DN_ALPHA = (2 * DEPTH) ** 0.25
LN_EPS = 1e-5
NEG_INF = -1e30
ATT_SCALE = HEAD_DIM ** -0.5
SB_UNDERFLOW = -104.0
SB_ROWS = 128

V7X_VMEM_BYTES = 64 * 1024 * 1024
VMEM_LIMIT = V7X_VMEM_BYTES - 8 * 1024 * 1024
MOE_VMEM_LIMIT = V7X_VMEM_BYTES - 4 * 1024 * 1024


def _cparams(sem, vmem=VMEM_LIMIT):
    return pltpu.CompilerParams(dimension_semantics=sem, vmem_limit_bytes=vmem)


def _dot(a, b):
    return jnp.dot(a, b, preferred_element_type=F32)


def _dot_nt(a, b):
    return lax.dot_general(a, b, (((1,), (1,)), ((), ())), preferred_element_type=F32)


def _split2(x):
    hi = x.astype(BF)
    lo = (x - hi.astype(F32)).astype(BF)
    return hi, lo


def _layer_norm(h, g, b):
    mu = jnp.mean(h, axis=-1, keepdims=True)
    d = h - mu
    var = jnp.mean(d * d, axis=-1, keepdims=True)
    return d * lax.rsqrt(var + LN_EPS) * g + b


HALF = D_MODEL // 2
NWORD = HALF // LANES
HI_MASK = 0xFFFF0000


def _pack_pairs(a, b):
    lo = pltpu.bitcast(a.astype(BF).astype(F32), jnp.uint32)
    hi = pltpu.bitcast(b.astype(BF).astype(F32), jnp.uint32)
    return (lo >> 16) | (hi & jnp.uint32(HI_MASK))


def _pack_row_words(h):
    return [_pack_pairs(h[:, c * LANES:(c + 1) * LANES], h[:, HALF + c * LANES:HALF + (c + 1) * LANES])
            for c in range(NWORD)]


def _store_row_words(ref, start, n, h):
    for c, w in enumerate(_pack_row_words(h)):
        ref[pl.ds(start * NWORD + c, n, stride=NWORD), :] = w


def _load_row_words(ref, start, n):
    return [ref[pl.ds(start * NWORD + c, n, stride=NWORD), :] for c in range(NWORD)]


def _unpack_lo(w):
    return pltpu.bitcast(w << 16, F32)


def _unpack_hi(w):
    return pltpu.bitcast(w & jnp.uint32(HI_MASK), F32)


N_PLAIN = 3 * SB_WIDTH + 3 * LANES + LANES
N_ROPE = NSA_WIDTH + 3 * LANES


def _proj0_body(x_ref, wp_ref, wr_ref, wrr_ref, cos_ref, sin_ref,
                qkva_ref, vals_ref, gate_ref, qb_ref, kr_ref):
    xb = x_ref[...].astype(BF)
    a0, a1, a2 = 3 * SB_WIDTH, 3 * SB_WIDTH + 3 * LANES, N_PLAIN
    qkva_ref[...] = _dot(xb, wp_ref[:, 0:a0]).astype(BF)
    vals_ref[...] = _dot(xb, wp_ref[:, a0:a1]).astype(BF)
    gate_ref[...] = jax.nn.sigmoid(_dot(xb, wp_ref[:, a1:a2]))
    y = _dot(xb, wr_ref[...]) * cos_ref[...] + _dot(xb, wrr_ref[...]) * sin_ref[...]
    qb_ref[...] = y[:, 0:NSA_WIDTH].astype(BF)
    kr_ref[...] = y[:, NSA_WIDTH:N_ROPE].astype(BF)


def _proj0(x2, wp, wr, wrr, cos_t, sin_t, seq):
    T = x2.shape[0]
    tm = 512
    nseq = seq // tm
    row = lambda i: (i, 0)
    full = lambda i: (0, 0)
    tab = lambda i: (i % nseq, 0)
    return pl.pallas_call(
        _proj0_body,
        grid=(T // tm,),
        in_specs=[
            pl.BlockSpec((tm, D_MODEL), row),
            pl.BlockSpec((D_MODEL, N_PLAIN), full),
            pl.BlockSpec((D_MODEL, N_ROPE), full),
            pl.BlockSpec((D_MODEL, N_ROPE), full),
            pl.BlockSpec((tm, N_ROPE), tab),
            pl.BlockSpec((tm, N_ROPE), tab),
        ],
        out_specs=[
            pl.BlockSpec((tm, 3 * SB_WIDTH), row),
            pl.BlockSpec((tm, 3 * LANES), row),
            pl.BlockSpec((tm, LANES), row),
            pl.BlockSpec((tm, NSA_WIDTH), row),
            pl.BlockSpec((tm, 3 * LANES), row),
        ],
        out_shape=[
            jax.ShapeDtypeStruct((T, 3 * SB_WIDTH), BF),
            jax.ShapeDtypeStruct((T, 3 * LANES), BF),
            jax.ShapeDtypeStruct((T, LANES), F32),
            jax.ShapeDtypeStruct((T, NSA_WIDTH), BF),
            jax.ShapeDtypeStruct((T, 3 * LANES), BF),
        ],
        compiler_params=_cparams(("parallel",)),
        name="proj0",
    )(x2, wp, wr, wrr, cos_t, sin_t)


def _sb_body(q_ref, k_ref, v_ref, o_ref, *, tq):
    i = pl.program_id(2)
    q = q_ref[0]
    lane = lax.broadcasted_iota(jnp.int32, (1, LANES), 1)
    row = lax.broadcasted_iota(jnp.int32, (tq, tq), 0)
    col = lax.broadcasted_iota(jnp.int32, (tq, tq), 1)
    tri = jnp.where(row > col, 1.0, 0.0).astype(BF)
    diag_causal = col < row

    hms = [(lane // HEAD_DIM) == hh for hh in range(2)]
    nchunk = tq // SB_ROWS
    chains = [(hh, rc) for hh in range(2) for rc in range(nchunk)]
    qcs = [jnp.where(hms[hh], q[rc * SB_ROWS:(rc + 1) * SB_ROWS], jnp.zeros((SB_ROWS, LANES), BF))
           for hh, rc in chains]

    def blocks(j, state, masks):
        off = pl.multiple_of(j * tq, tq)
        k = k_ref[0, pl.ds(off, tq), :]
        v = v_ref[0, pl.ds(off, tq), :]
        zs = [_dot_nt(qc, k) for qc in qcs]
        mids = []
        for z, mask in zip(zs, masks):
            soft = jnp.log(1.0 + jnp.exp(-jnp.abs(z)))
            log_1m = -(jnp.maximum(z, 0.0) + soft)
            log_sig = jnp.minimum(z, 0.0) - soft
            if mask is not None:
                log_1m = jnp.where(mask, log_1m, 0.0)
            mids.append((log_1m, log_sig) + _split2(log_1m))
        sufs = [_dot(hi, tri) + _dot(lo, tri) for _, _, hi, lo in mids]
        ws = []
        for n, ((log_1m, log_sig, _, _), suffix, mask) in enumerate(zip(mids, sufs, masks)):
            w = jnp.exp(log_sig + suffix + state[2 * n])
            if mask is not None:
                w = jnp.where(mask, w, 0.0)
            ws.append(w.astype(BF))
        new = []
        for n, w in enumerate(ws):
            new.append(state[2 * n] + jnp.sum(mids[n][0], axis=-1, keepdims=True))
            new.append(state[2 * n + 1] + _dot(w, v))
        return new

    zc, za = jnp.zeros((SB_ROWS, 1), F32), jnp.zeros((SB_ROWS, LANES), F32)
    state = blocks(i, [zc, za] * len(chains),
                   [diag_causal[rc * SB_ROWS:(rc + 1) * SB_ROWS] for _, rc in chains])

    def live(state):
        top = jnp.max(state[0])
        for c in state[2::2]:
            top = jnp.maximum(top, jnp.max(c))
        return top > SB_UNDERFLOW

    def cond(st):
        return jnp.logical_and(st[0] >= 0, st[1])

    def body(st):
        new = blocks(st[0], st[2], [None] * len(chains))
        return st[0] - 1, live(new), tuple(new)

    _, _, state = lax.while_loop(cond, body, (i - 1, live(state), tuple(state)))
    accs = [jnp.concatenate([state[2 * (hh * nchunk + rc) + 1] for rc in range(nchunk)], axis=0)
            for hh in range(2)]
    o_ref[0] = jnp.where(hms[0], accs[0], accs[1]).astype(BF)


def _sb_attention(qkva3):
    B, S, _ = qkva3.shape
    tq = 256
    npair = SB_WIDTH // LANES
    return pl.pallas_call(
        functools.partial(_sb_body, tq=tq),
        grid=(B, npair, S // tq),
        in_specs=[
            pl.BlockSpec((1, tq, LANES), lambda b, p, i: (b, i, p)),
            pl.BlockSpec((1, S, LANES), lambda b, p, i: (b, 0, npair + p)),
            pl.BlockSpec((1, S, LANES), lambda b, p, i: (b, 0, 2 * npair + p)),
        ],
        out_specs=pl.BlockSpec((1, tq, LANES), lambda b, p, i: (b, i, p)),
        out_shape=jax.ShapeDtypeStruct((B, S, SB_WIDTH), BF),
        compiler_params=_cparams(("parallel", "parallel", "arbitrary")),
        name="sb_attention",
    )(qkva3, qkva3, qkva3)


def _cmp_body(ak_ref, av_ref, posk_ref, posv_ref, w1k_ref, w1kt_ref, w1kb_ref, w2k_ref,
              w1v_ref, w1vt_ref, w1vb_ref, w2v_ref, kc_ref, vc_ref):
    def one(a_ref, pos_ref, w1_ref, w1t_ref, w1b_ref, w2_ref, o_ref):
        a = a_ref[0]
        n = a.shape[0]
        bias = _dot(pos_ref[...], w1_ref[...])[0:1]
        out = jnp.zeros((n, LANES), F32)
        for g in range(NSA_KV_GROUPS):
            top = _dot(a, w1t_ref[g])
            bot = _dot(a, w1b_ref[g])
            h = top + pltpu.roll(bot, n - 1, 0) + bias
            out = out + _dot(jax.nn.gelu(h).astype(BF), w2_ref[g])
        o_ref[0] = out.astype(BF)

    one(ak_ref, posk_ref, w1k_ref, w1kt_ref, w1kb_ref, w2k_ref, kc_ref)
    one(av_ref, posv_ref, w1v_ref, w1vt_ref, w1vb_ref, w2v_ref, vc_ref)


def _compress(ak, av, posk, posv, wk, wv):
    B, n, width = ak.shape
    blk = pl.BlockSpec((1, n, width), lambda b: (b, 0, 0))
    c2 = lambda shp: pl.BlockSpec(shp, lambda b: (0, 0))
    c3 = lambda shp: pl.BlockSpec(shp, lambda b: (0, 0, 0))
    wspecs = [c2((CMP_LEN * HEAD_DIM, CMP_HIDDEN)), c3((2, width, CMP_HIDDEN)),
              c3((2, width, CMP_HIDDEN)), c3((2, CMP_HIDDEN, LANES))]
    out = pl.BlockSpec((1, n, LANES), lambda b: (b, 0, 0))
    return pl.pallas_call(
        _cmp_body,
        grid=(B,),
        in_specs=[blk, blk, c2((8, CMP_LEN * HEAD_DIM)), c2((8, CMP_LEN * HEAD_DIM))] + wspecs + wspecs,
        out_specs=[out, out],
        out_shape=[jax.ShapeDtypeStruct((B, n, LANES), BF)] * 2,
        compiler_params=_cparams(("parallel",)),
        name="nsa_compress",
    )(ak, av, posk, posv, *wk, *wv)


def _nsa_body(q_ref, kc_ref, vct_ref, ks_ref, vst_ref, kw_ref, vwt_ref, gt_ref, ovl_ref, et_ref,
              o_ref, *, tq, tk, nblk):
    i = pl.program_id(1)
    R = NSA_REP
    nchain = NSA_KV_GROUPS * R
    t0 = i * tq
    lane = lax.broadcasted_iota(jnp.int32, (1, LANES), 1)
    q = q_ref[0]
    t_row = t0 + lax.broadcasted_iota(jnp.int32, (1, tq), 1)
    tile = lambda a: jnp.concatenate([a] * nchain, axis=1)
    n_col = lax.broadcasted_iota(jnp.int32, (LANES, 1), 0)
    cmp_bias = tile(jnp.where((CMP_STRIDE * n_col + (CMP_LEN - 1)) <= t_row, 0.0, NEG_INF))
    has_cmp = tile(jnp.where(t_row >= CMP_LEN - 1, 1.0, 0.0))
    nkb = (t0 + tq + tk - 1) // tk
    key_last = (nkb - 1) * tk + lax.broadcasted_iota(jnp.int32, (tk, 1), 0)
    diag_bias = tile(jnp.where(key_last <= t_row, 0.0, NEG_INF))
    wkeys = WINDOW + tq
    woff = pl.multiple_of(jnp.maximum(i - WINDOW // tq, 0) * tq, tq)
    kp = woff + lax.broadcasted_iota(jnp.int32, (wkeys, 1), 0)
    win_bias = tile(jnp.where(kp <= t_row, jnp.where(kp > t_row - WINDOW, 0.0, NEG_INF), NEG_INF))
    blk = lax.broadcasted_iota(jnp.int32, (nblk, 1), 0)
    cur = t_row // SLC_LEN
    valid = blk <= cur
    forced = (blk == 0) | (valid & (blk > cur - SLC_LOCAL))
    ovl = ovl_ref[...]

    qrs = [jnp.where((lane // HEAD_DIM) == gi, q[:, LANES * r:LANES * (r + 1)], jnp.zeros((tq, LANES), BF))
           for gi in range(NSA_KV_GROUPS) for r in range(R)]
    q_all = jnp.concatenate(qrs, axis=0)

    s = _dot_nt(kc_ref[0], q_all) + cmp_bias
    e = jnp.exp(s - jnp.max(s, axis=0, keepdims=True))
    pc = e / jnp.sum(e, axis=0, keepdims=True) * has_cmp
    o_cmp = _dot(vct_ref[0], pc.astype(BF))

    q_aug = []
    for gi in range(NSA_KV_GROUPS):
        psum = jnp.zeros((LANES, tq), F32)
        for r in range(R):
            c = gi * R + r
            psum = psum + pc[:, c * tq:(c + 1) * tq]
        p1 = psum.astype(BF)
        r1 = psum - p1.astype(F32)
        p2 = r1.astype(BF)
        p3 = (r1 - p2.astype(F32)).astype(BF)
        imp = _dot(ovl, p1) + _dot(ovl, p2) + _dot(ovl, p3)
        imp = jnp.where(forced, imp + SLC_FORCE_BONUS, jnp.where(valid, imp, NEG_INF))
        sel = jnp.zeros((nblk, tq), F32)
        for _ in range(min(SLC_TOPN, nblk)):
            mx = jnp.max(imp, axis=0, keepdims=True)
            idx = jnp.min(jnp.where(imp == mx, blk, nblk), axis=0, keepdims=True)
            hit = blk == idx
            sel = jnp.where(hit, 1.0, sel)
            imp = jnp.where(hit, -jnp.inf, imp)
        sel_t = jnp.concatenate([sel, jnp.zeros((LANES - nblk, tq), F32)], axis=0).T
        sel_bias = ((sel_t - 1.0) * -NEG_INF).astype(BF)
        q_aug.extend(jnp.concatenate([qrs[gi * R + r], sel_bias], axis=1) for r in range(R))
    qa_all = jnp.concatenate(q_aug, axis=0)

    def sel_step(j, state, bias):
        m, l, acc = state
        off = pl.multiple_of(j * tk, tk)
        ka = jnp.concatenate([ks_ref[0, pl.ds(off, tk), :], et_ref[pl.ds(off, tk), :]], axis=1)
        s = _dot_nt(ka, qa_all)
        if bias is not None:
            s = s + bias
        m_new = jnp.maximum(m, jnp.max(s, axis=0, keepdims=True))
        alpha = jnp.exp(m - m_new)
        p = jnp.exp(s - m_new)
        l = alpha * l + jnp.sum(p, axis=0, keepdims=True)
        acc = alpha * acc + _dot(vst_ref[0, :, pl.ds(off, tk)], p.astype(BF))
        return m_new, l, acc

    width = nchain * tq
    init = (jnp.full((1, width), NEG_INF, F32), jnp.zeros((1, width), F32), jnp.zeros((LANES, width), F32))
    state = lax.fori_loop(0, nkb - 1, lambda j, st: sel_step(j, st, None), init)
    _, l_s, acc_s = sel_step(nkb - 1, state, diag_bias)
    o_sel = acc_s / l_s

    s = _dot_nt(kw_ref[0, pl.ds(woff, wkeys), :], q_all) + win_bias
    p = jnp.exp(s - jnp.max(s, axis=0, keepdims=True))
    o_win = _dot(vwt_ref[0, :, pl.ds(woff, wkeys)], p.astype(BF)) / jnp.sum(p, axis=0, keepdims=True)

    gt = gt_ref[0]
    sub = lax.broadcasted_iota(jnp.int32, (LANES, 1), 0)
    g0 = (sub // HEAD_DIM) == 0
    for r in range(R):
        out = jnp.zeros((LANES, tq), F32)
        for br, o_br in enumerate((o_cmp, o_sel, o_win)):
            rows = [(gi * R + r) * NSA_N_BRANCH + br for gi in range(NSA_KV_GROUPS)]
            gate = jnp.where(g0, gt[rows[0]:rows[0] + 1], gt[rows[1]:rows[1] + 1])
            both = jnp.where(g0, o_br[:, r * tq:(r + 1) * tq], o_br[:, (R + r) * tq:(R + r + 1) * tq])
            out = out + gate * both
        o_ref[0, r * LANES:(r + 1) * LANES, :] = out.astype(BF)


def _nsa_attention(qb3, kc, vct, kr3, vst, vwt, gt, ovl, expand):
    B, S, _ = qb3.shape
    tq, tk = 256, 512
    n = kc.shape[1]
    nblk = ovl.shape[0]
    ngate = gt.shape[1]
    assert S >= WINDOW + tq and S % tk == 0 and nblk % 8 == 0 and n == LANES
    tok = lambda c: pl.BlockSpec((1, S, LANES), lambda b, i, c=c: (b, 0, c))
    tr = pl.BlockSpec((1, LANES, S), lambda b, i: (b, 0, 0))
    return pl.pallas_call(
        functools.partial(_nsa_body, tq=tq, tk=tk, nblk=nblk),
        grid=(B, S // tq),
        in_specs=[
            pl.BlockSpec((1, tq, NSA_WIDTH), lambda b, i: (b, i, 0)),
            pl.BlockSpec((1, n, LANES), lambda b, i: (b, 0, 0)),
            pl.BlockSpec((1, LANES, n), lambda b, i: (b, 0, 0)),
            tok(1), tr, tok(2), tr,
            pl.BlockSpec((1, ngate, tq), lambda b, i: (b, 0, i)),
            pl.BlockSpec((nblk, LANES), lambda b, i: (0, 0)),
            pl.BlockSpec((S, LANES), lambda b, i: (0, 0)),
        ],
        out_specs=pl.BlockSpec((1, NSA_WIDTH, tq), lambda b, i: (b, 0, i)),
        out_shape=jax.ShapeDtypeStruct((B, NSA_WIDTH, S), BF),
        compiler_params=_cparams(("parallel", "arbitrary")),
        name="nsa_attention",
    )(qb3, kc, vct, kr3, vst, kr3, vwt, gt, ovl, expand)


def _outproj_ln_body(oa_ref, obt_ref, x_ref, wa_ref, wb_ref, g_ref, b_ref, o_ref, p_ref):
    yb = lax.dot_general(obt_ref[0], wb_ref[...], (((0,), (0,)), ((), ())), preferred_element_type=F32)
    y = _dot(oa_ref[...], wa_ref[...]) + yb
    res = _layer_norm(DN_ALPHA * x_ref[...] + y, g_ref[...], b_ref[...])
    o_ref[...] = res
    _store_row_words(p_ref, 0, res.shape[0], res)


def _outproj_ln(oa, obt, x2, wa, wb, g, b):
    T = x2.shape[0]
    tm = 512
    nseq = obt.shape[2] // tm
    row = lambda i: (i, 0)
    full = lambda i: (0, 0)
    return pl.pallas_call(
        _outproj_ln_body,
        grid=(T // tm,),
        in_specs=[
            pl.BlockSpec((tm, oa.shape[1]), row),
            pl.BlockSpec((1, obt.shape[1], tm), lambda i: (i // nseq, 0, i % nseq)),
            pl.BlockSpec((tm, D_MODEL), row),
            pl.BlockSpec(wa.shape, full),
            pl.BlockSpec(wb.shape, full),
            pl.BlockSpec((1, D_MODEL), full),
            pl.BlockSpec((1, D_MODEL), full),
        ],
        out_specs=[pl.BlockSpec((tm, D_MODEL), row),
                   pl.BlockSpec((tm * NWORD, LANES), row)],
        out_shape=[jax.ShapeDtypeStruct((T, D_MODEL), F32),
                   jax.ShapeDtypeStruct((T * NWORD, LANES), jnp.uint32)],
        compiler_params=_cparams(("parallel",)),
        name="outproj_ln",
    )(oa, obt, x2, wa, wb, g, b)


def _rope_tables(seq):
    inv = jnp.power(ROPE_THETA, -jnp.arange(0, ROPE_DIM, 2, dtype=F32) / ROPE_DIM)
    ang = jnp.arange(seq, dtype=F32)[:, None] * inv[None, :]
    half = ROPE_DIM // 2
    rest = HEAD_DIM - ROPE_DIM
    cos_h = jnp.concatenate([jnp.cos(ang), jnp.cos(ang), jnp.ones((seq, rest), F32)], axis=1)
    sin_h = jnp.concatenate([jnp.sin(ang), jnp.sin(ang), jnp.zeros((seq, rest), F32)], axis=1)
    reps = N_ROPE // HEAD_DIM
    del half
    return jnp.tile(cos_h, (1, reps)), jnp.tile(sin_h, (1, reps))


def _rot_cols(w):
    k, n = w.shape
    w3 = w.reshape(k, n // HEAD_DIM, HEAD_DIM)
    half = ROPE_DIM // 2
    rot = jnp.concatenate([-w3[..., half:ROPE_DIM], w3[..., :half],
                           jnp.zeros_like(w3[..., ROPE_DIM:])], axis=-1)
    return rot.reshape(k, n)


def _nsa_head_perm():
    cols = []
    for r in range(NSA_REP):
        for g in range(NSA_KV_GROUPS):
            h = g * NSA_REP + r
            cols.extend(range(h * HEAD_DIM, (h + 1) * HEAD_DIM))
    return np.asarray(cols)


def _mixer0(x2, batch, seq, w_in, cmp_pos_k, cmp_pos_v, cmp_k_w1, cmp_k_w2, cmp_v_w1, cmp_v_w2,
            w_out, ln_g, ln_b):
    T = x2.shape[0]
    sizes = [SB_WIDTH] * 3 + [NSA_WIDTH] + [NSA_KV_GROUPS * HEAD_DIM] * 6 + [NSA_HEADS * NSA_N_BRANCH]
    offs = np.concatenate([[0], np.cumsum(sizes)])
    col = lambda j: w_in[:, offs[j]:offs[j + 1]]
    perm = _nsa_head_perm()
    ngate = sizes[-1]
    wp = jnp.concatenate([col(0) * ATT_SCALE, col(1), col(2), col(5), col(7), col(9),
                          jnp.pad(col(10), ((0, 0), (0, LANES - ngate)))], axis=1)
    wr = jnp.concatenate([col(3)[:, perm] * ATT_SCALE, col(4), col(6), col(8)], axis=1)
    wrr = _rot_cols(wr)
    cos_t, sin_t = _rope_tables(seq)
    qkva, vals, gates, qb, kr = _proj0(x2, wp.astype(BF), wr.astype(BF), wrr.astype(BF), cos_t, sin_t, seq)

    r3 = lambda a: a.reshape(batch, seq, a.shape[-1])
    o_a = _sb_attention(r3(qkva))

    ncmp = seq // CMP_STRIDE
    ak = kr[:, 0:LANES].reshape(batch, ncmp, CMP_STRIDE * LANES)
    av = vals[:, 0:LANES].reshape(batch, ncmp, CMP_STRIDE * LANES)

    def cmp_weights(w1, w2):
        w1r = w1.reshape(2, CMP_STRIDE, HEAD_DIM, CMP_HIDDEN)
        tops, bots, w2s = [], [], []
        for g in range(NSA_KV_GROUPS):
            ext = jnp.zeros((2, CMP_STRIDE, NSA_KV_GROUPS, HEAD_DIM, CMP_HIDDEN), F32).at[:, :, g].set(w1r)
            ext = ext.reshape(2, CMP_STRIDE * LANES, CMP_HIDDEN)
            tops.append(ext[0])
            bots.append(ext[1])
            w2s.append(jnp.zeros((CMP_HIDDEN, LANES), F32).at[:, g * HEAD_DIM:(g + 1) * HEAD_DIM].set(w2))
        return (w1.astype(BF), jnp.stack(tops).astype(BF), jnp.stack(bots).astype(BF),
                jnp.stack(w2s).astype(BF))

    posk = jnp.broadcast_to(cmp_pos_k.reshape(1, -1), (8, CMP_LEN * HEAD_DIM)).astype(BF)
    posv = jnp.broadcast_to(cmp_pos_v.reshape(1, -1), (8, CMP_LEN * HEAD_DIM)).astype(BF)
    kc, vc = _compress(ak, av, posk, posv, cmp_weights(cmp_k_w1, cmp_k_w2), cmp_weights(cmp_v_w1, cmp_v_w2))
    assert ncmp <= LANES
    if ncmp < LANES:
        kc = jnp.pad(kc, ((0, 0), (0, LANES - ncmp), (0, 0)))
        vc = jnp.pad(vc, ((0, 0), (0, LANES - ncmp), (0, 0)))

    n_slc = seq // SLC_LEN
    cmp_start = np.arange(ncmp) * CMP_STRIDE
    slc_start = np.arange(n_slc) * SLC_LEN
    ovl = ((cmp_start[None, :] <= slc_start[:, None] + SLC_LEN - 1)
           & (cmp_start[None, :] + CMP_LEN - 1 >= slc_start[:, None])).astype(np.float32)
    ovl = np.pad(ovl, ((0, 0), (0, LANES - ncmp))) if ncmp < LANES else ovl
    expand = (np.arange(seq)[:, None] // SLC_LEN == np.arange(LANES)[None, :]).astype(np.float32)
    vals_t = jnp.swapaxes(r3(vals), 1, 2)
    gt = jnp.swapaxes(r3(gates)[:, :, :32], 1, 2)
    o_bt = _nsa_attention(r3(qb), kc, jnp.swapaxes(vc, 1, 2), r3(kr), vals_t[:, LANES:2 * LANES],
                          vals_t[:, 2 * LANES:], gt, jnp.asarray(ovl, BF), jnp.asarray(expand, BF))

    wa = w_out[:SB_WIDTH].astype(BF)
    wb = w_out[SB_WIDTH:][perm].astype(BF)
    return _outproj_ln(o_a.reshape(T, SB_WIDTH), o_bt, x2, wa, wb,
                       ln_g.reshape(1, -1), ln_b.reshape(1, -1))


MOE_TILE = 2048
ROW_ALIGN = 8
RANK_CHUNK = 256


def _top_rows(vals, ids, n_ids, count):
    hits = []
    for _ in range(count):
        mx = jnp.max(vals, axis=0, keepdims=True)
        idx = jnp.min(jnp.where(vals == mx, ids, n_ids), axis=0, keepdims=True)
        hit = ids == idx
        hits.append(hit)
        vals = jnp.where(hit, -jnp.inf, vals)
    return hits


def _router_body(h_ref, rt_ref, rb_ref, pos_ref, gate_ref, meta_ref, *, tm):
    E, NG = N_EXPERTS, N_EXPERT_GROUPS
    per = E // NG
    hh, hl = _split2(h_ref[...])
    rh, rl = _split2(rt_ref[...])
    logits = _dot_nt(rh, hh) + _dot_nt(rh, hl) + _dot_nt(rl, hh)
    scores = jax.nn.sigmoid(logits)
    biased = scores + rb_ref[...]
    i8 = lax.broadcasted_iota(jnp.int32, (per, tm), 0)
    gs = []
    for g in range(NG):
        v = biased[g * per:(g + 1) * per]
        m1 = jnp.max(v, axis=0, keepdims=True)
        a1 = jnp.min(jnp.where(v == m1, i8, per), axis=0, keepdims=True)
        m2 = jnp.max(jnp.where(i8 == a1, -jnp.inf, v), axis=0, keepdims=True)
        gs.append(m1 + m2)
    gs = jnp.concatenate(gs, axis=0)
    gi = lax.broadcasted_iota(jnp.int32, (NG, tm), 0)
    ghits = _top_rows(gs, gi, NG, TOPK_GROUPS)
    gkeep = jnp.zeros((NG, tm), F32)
    for hit in ghits:
        gkeep = jnp.where(hit, 1.0, gkeep)
    ekeep = jnp.concatenate([jnp.broadcast_to(gkeep[g:g + 1], (per, tm)) for g in range(NG)], axis=0)
    ei = lax.broadcasted_iota(jnp.int32, (E, tm), 0)
    hits = _top_rows(jnp.where(ekeep > 0.5, biased, -jnp.inf), ei, E, TOP_K)
    gates = [jnp.sum(jnp.where(hit, scores, 0.0), axis=0, keepdims=True) for hit in hits]
    gsum = gates[0]
    for gk in gates[1:]:
        gsum = gsum + gk
    gates = [gk / gsum * ROUTED_SCALE for gk in gates]

    member = jnp.zeros((E, tm), F32)
    for hit in hits:
        member = jnp.where(hit, 1.0, member)
    cnt_col = jnp.sum(member, axis=1, keepdims=True)
    pad_col = jnp.floor((cnt_col + (ROW_ALIGN - 1)) * (1.0 / ROW_ALIGN)) * ROW_ALIGN
    sub_e = lax.broadcasted_iota(jnp.int32, (E, LANES), 0)
    lane_e = lax.broadcasted_iota(jnp.int32, (E, LANES), 1)
    cnt_row = jnp.sum(jnp.where(sub_e == lane_e, cnt_col, 0.0), axis=0, keepdims=True)
    pad_row = jnp.sum(jnp.where(sub_e == lane_e, pad_col, 0.0), axis=0, keepdims=True)
    off_row = jnp.sum(jnp.where(sub_e < lane_e, pad_col, 0.0), axis=0, keepdims=True)
    off_col = jnp.sum(jnp.where(lane_e < sub_e, pad_row, 0.0), axis=1, keepdims=True)
    r_i = lax.broadcasted_iota(jnp.int32, (RANK_CHUNK, RANK_CHUNK), 0)
    c_i = lax.broadcasted_iota(jnp.int32, (RANK_CHUNK, RANK_CHUNK), 1)
    before = jnp.where(r_i < c_i, 1.0, 0.0).astype(BF)
    running = off_col
    ranks = []
    for c in range(tm // RANK_CHUNK):
        mc = member[:, c * RANK_CHUNK:(c + 1) * RANK_CHUNK]
        ranks.append(_dot(mc.astype(BF), before) + running)
        running = running + jnp.sum(mc, axis=1, keepdims=True)
    slot = jnp.concatenate(ranks, axis=1)
    pos = [jnp.sum(jnp.where(hit, slot, 0.0), axis=0, keepdims=True) for hit in hits]
    zrow = jnp.zeros((1, tm), F32)
    pos_ref[...] = jnp.concatenate(pos + [zrow, zrow], axis=0).astype(jnp.int32)
    gate_ref[...] = jnp.concatenate(gates + [zrow, zrow], axis=0)
    z128 = jnp.zeros((1, LANES), F32)
    meta_ref[0] = jnp.concatenate([off_row, cnt_row] + [z128] * 6, axis=0).astype(jnp.int32)


def _router(h2, router_t, bias_col):
    T = h2.shape[0]
    tm = MOE_TILE
    nt = T // tm
    return pl.pallas_call(
        functools.partial(_router_body, tm=tm),
        grid=(nt,),
        in_specs=[
            pl.BlockSpec((tm, D_MODEL), lambda i: (i, 0)),
            pl.BlockSpec((N_EXPERTS, D_MODEL), lambda i: (0, 0)),
            pl.BlockSpec((N_EXPERTS, 1), lambda i: (0, 0)),
        ],
        out_specs=[
            pl.BlockSpec((8, tm), lambda i: (0, i)),
            pl.BlockSpec((8, tm), lambda i: (0, i)),
            pl.BlockSpec((1, 8, LANES), lambda i: (i, 0, 0)),
        ],
        out_shape=[
            jax.ShapeDtypeStruct((8, T), jnp.int32),
            jax.ShapeDtypeStruct((8, T), F32),
            jax.ShapeDtypeStruct((nt, 8, LANES), jnp.int32),
        ],
        compiler_params=_cparams(("parallel",)),
        name="moe_router",
    )(h2, router_t, bias_col)


EXPERTS_PER_STEP = 4
EXPERT_CHUNK = 256
COMBINE_SUB = 256
POS_STRIDE = 8


def _swiglu(xb, wgu, wd, hidden):
    gu = _dot(xb, wgu)
    a = jax.nn.silu(gu[:, :hidden]) * gu[:, hidden:]
    return _dot(a.astype(BF), wd)


def _moe_body(off_ref, cnt_ref, pos_ref, src_ref, x_ref, gcol_ref, wgu_ref, wd_ref, wsgu_ref, wsd_ref,
              g_ref, b_ref, o_ref, xs_ref, z_ref, *, tm, eb, ch, sub, unroll):
    i = pl.program_id(0)
    j = pl.program_id(1)
    nj = N_EXPERTS // eb

    @pl.when(j == 0)
    def _dispatch():
        xs_ref[...] = jnp.zeros_like(xs_ref)

        def tok(tb, carry):
            for u in range(unroll):
                t = tb * unroll + u
                slab = src_ref[pl.ds(pl.multiple_of(t * NWORD, NWORD), NWORD), :]
                for k in range(TOP_K):
                    p = pl.multiple_of(pos_ref[t * POS_STRIDE + k], NWORD)
                    xs_ref[pl.ds(p, NWORD), :] = slab
            return carry

        lax.fori_loop(0, tm // unroll, tok, 0)

    def chunks_in(els, offs, c):
        r0s = [pl.multiple_of(off + c * ch, ROW_ALIGN) for off in offs]
        words = [_load_row_words(xs_ref, r0, ch) for r0 in r0s]
        xbs = [jnp.concatenate([_unpack_lo(w).astype(BF) for w in ws]
                               + [_unpack_hi(w).astype(BF) for w in ws], axis=1) for ws in words]
        gus = [_dot(xb, wgu_ref[el]) for xb, el in zip(xbs, els)]
        acts = [(jax.nn.silu(gu[:, :EXPERT_HIDDEN]) * gu[:, EXPERT_HIDDEN:]).astype(BF) for gu in gus]
        ys = [_dot(a, wd_ref[el]) for a, el in zip(acts, els)]
        return list(zip(r0s, words, ys))

    def chunk_out(r0, words, y, c, cnt):
        keep = (c * ch + lax.broadcasted_iota(jnp.int32, (ch, 1), 0)) < cnt
        for cc, packed in enumerate(_pack_row_words(y)):
            xs_ref[pl.ds(r0 * NWORD + cc, ch, stride=NWORD), :] = jnp.where(keep, packed, words[cc])

    @pl.when(j < nj)
    def _experts():
        offs = [off_ref[i * N_EXPERTS + j * eb + el] for el in range(eb)]
        cnts = [cnt_ref[i * N_EXPERTS + j * eb + el] for el in range(eb)]
        firsts = chunks_in(list(range(eb)), offs, 0)
        for el in range(eb):
            chunk_out(*firsts[el], 0, cnts[el])
        for el in range(eb):
            def chunk(c, carry, el=el):
                chunk_out(*chunks_in([el], [offs[el]], c)[0], c, cnts[el])
                return carry

            lax.fori_loop(1, (cnts[el] + ch - 1) // ch, chunk, 0)

    @pl.when(j >= nj)
    def _combine():
        base = (j - nj) * sub

        def tok(tb, carry):
            for u in range(unroll):
                tl = tb * unroll + u
                dst = pl.multiple_of(tl * NWORD, NWORD)
                for k in range(TOP_K):
                    p = pl.multiple_of(pos_ref[(base + tl) * POS_STRIDE + k], NWORD)
                    z_ref[k, pl.ds(dst, NWORD), :] = xs_ref[pl.ds(p, NWORD), :]
            return carry

        lax.fori_loop(0, sub // unroll, tok, 0)
        gcol = gcol_ref[...]
        lo = [jnp.zeros((sub, LANES), F32) for _ in range(NWORD)]
        hi = [jnp.zeros((sub, LANES), F32) for _ in range(NWORD)]
        for k in range(TOP_K):
            gk = gcol[:, k:k + 1]
            for c, w in enumerate(_load_row_words(z_ref.at[k], 0, sub)):
                lo[c] = lo[c] + gk * _unpack_lo(w)
                hi[c] = hi[c] + gk * _unpack_hi(w)
        routed = jnp.concatenate(lo + hi, axis=1)
        x = x_ref[...]
        shared = _swiglu(x.astype(BF), wsgu_ref[...], wsd_ref[...], SHARED_HIDDEN)
        o_ref[...] = _layer_norm(DN_ALPHA * x + routed + shared, g_ref[...], b_ref[...])


def _moe_experts(h2, packed, off, cnt, pos, gcol, wgu, wd, wsgu, wsd, g, b):
    T = h2.shape[0]
    tm, eb, ch, sub = MOE_TILE, EXPERTS_PER_STEP, EXPERT_CHUNK, COMBINE_SUB
    nt, nj, nsub = T // tm, N_EXPERTS // eb, tm // sub
    rows = TOP_K * tm + N_EXPERTS * ROW_ALIGN + ch
    hidden2 = wgu.shape[-1]
    wblk = lambda i, j, *_: (jnp.minimum(j, nj - 1), 0, 0)
    sub_i = lambda i, j: i * nsub + jnp.clip(j - nj, 0, nsub - 1)
    once = pl.Buffered(1)
    return pl.pallas_call(
        functools.partial(_moe_body, tm=tm, eb=eb, ch=ch, sub=sub, unroll=4),
        grid_spec=pltpu.PrefetchScalarGridSpec(
            num_scalar_prefetch=2,
            grid=(nt, nj + nsub),
            in_specs=[
                pl.BlockSpec((tm * POS_STRIDE,), lambda i, j, *_: (i,), memory_space=pltpu.SMEM),
                pl.BlockSpec((tm * NWORD, LANES), lambda i, j, *_: (i, 0), pipeline_mode=once),
                pl.BlockSpec((sub, D_MODEL), lambda i, j, *_: (sub_i(i, j), 0)),
                pl.BlockSpec((sub, 8), lambda i, j, *_: (sub_i(i, j), 0)),
                pl.BlockSpec((eb, D_MODEL, hidden2), wblk),
                pl.BlockSpec((eb, hidden2 // 2, D_MODEL), wblk),
                pl.BlockSpec(wsgu.shape, lambda i, j, *_: (0, 0), pipeline_mode=once),
                pl.BlockSpec(wsd.shape, lambda i, j, *_: (0, 0), pipeline_mode=once),
                pl.BlockSpec((1, D_MODEL), lambda i, j, *_: (0, 0)),
                pl.BlockSpec((1, D_MODEL), lambda i, j, *_: (0, 0)),
            ],
            out_specs=pl.BlockSpec((sub, D_MODEL), lambda i, j, *_: (sub_i(i, j), 0)),
            scratch_shapes=[
                pltpu.VMEM((rows * NWORD, LANES), jnp.uint32),
                pltpu.VMEM((TOP_K, sub * NWORD, LANES), jnp.uint32),
            ],
        ),
        out_shape=jax.ShapeDtypeStruct((T, D_MODEL), F32),
        compiler_params=_cparams(("parallel", "arbitrary"), MOE_VMEM_LIMIT),
        name="moe_experts",
    )(off, cnt, pos, packed, h2, gcol, wgu, wd, wsgu, wsd, g, b)


S5_BATCH = 8
S5_STEPS = 32
S5_SLABS = 4
SLAB_CH = D_MODEL // S5_SLABS
SLAB_ST = SSM_GROUPS * SSM_STATE // S5_SLABS
N_STATE = SSM_GROUPS * SSM_STATE
SCAN_LANES = 512
SCAN_UNROLL = 8


def _s5_disc_body(lre_ref, lim_ref, ldt_ref, bre_ref, bim_ref, are_ref, aim_ref, bbre_ref, bbim_ref):
    lre, lim = lre_ref[...], lim_ref[...]
    step = jnp.exp(ldt_ref[...])
    mag = jnp.exp(lre * step)
    a_re = mag * jnp.cos(lim * step)
    a_im = mag * jnp.sin(lim * step)
    den = lre * lre + lim * lim
    zoh_re = ((a_re - 1.0) * lre + a_im * lim) / den
    zoh_im = (a_im * lre - (a_re - 1.0) * lim) / den
    are_ref[...] = a_re
    aim_ref[...] = a_im
    bbre_ref[...] = zoh_re * bre_ref[...] - zoh_im * bim_ref[...]
    bbim_ref[...] = zoh_re * bim_ref[...] + zoh_im * bre_ref[...]


def _s5_discretize(lambda_re, lambda_im, log_dt, b_re, b_im):
    col = lambda a: a.reshape(N_STATE, 1)
    ldt = jnp.broadcast_to(log_dt[:, None], (SSM_GROUPS, SSM_STATE))
    mat = lambda a: a.reshape(N_STATE, SSM_GROUP)
    c1 = jax.ShapeDtypeStruct((N_STATE, 1), F32)
    c16 = jax.ShapeDtypeStruct((N_STATE, SSM_GROUP), F32)
    return pl.pallas_call(_s5_disc_body, out_shape=[c1, c1, c16, c16], name="s5_discretize")(
        col(lambda_re), col(lambda_im), col(ldt), mat(b_re), mat(b_im))


def _s5_body(x_ref, win_ref, are_ref, aim_ref, bd_ref, cd_ref, dsk_ref, wglu_ref, wout_ref, g_ref, b_ref,
             o_ref, p_ref, xs_ref, pk_ref, hre_ref, him_ref, sre_ref, sim_ref, *, lt):
    nb = S5_BATCH
    nlb = D_MODEL // LANES
    for c in range(nlb):
        for b in range(nb):
            xs_ref[c, pl.ds(b, lt, stride=nb), :] = x_ref[b, :, c * LANES:(c + 1) * LANES]
    x = jnp.concatenate([xs_ref[c] for c in range(nlb)], axis=1)
    u = _dot(x.astype(BF), win_ref[...])
    ub = u.astype(BF)
    for k in range(S5_SLABS):
        bu = _dot(ub[:, k * SLAB_CH:(k + 1) * SLAB_CH], bd_ref[k])
        hre_ref[:, k * SLAB_ST:(k + 1) * SLAB_ST] = bu[:, :SLAB_ST]
        him_ref[:, k * SLAB_ST:(k + 1) * SLAB_ST] = bu[:, SLAB_ST:]

    @pl.when(pl.program_id(1) == 0)
    def _():
        sre_ref[...] = jnp.zeros_like(sre_ref)
        sim_ref[...] = jnp.zeros_like(sim_ref)

    for c in range(N_STATE // SCAN_LANES):
        ls = slice(c * SCAN_LANES, (c + 1) * SCAN_LANES)
        a_re = jnp.broadcast_to(are_ref[:, ls], (nb, SCAN_LANES))
        a_im = jnp.broadcast_to(aim_ref[:, ls], (nb, SCAN_LANES))

        def steps(tb, state, ls=ls, a_re=a_re, a_im=a_im):
            s_re, s_im = state
            for uu in range(SCAN_UNROLL):
                r0 = pl.multiple_of((tb * SCAN_UNROLL + uu) * nb, nb)
                n_re = a_re * s_re - a_im * s_im + hre_ref[pl.ds(r0, nb), ls]
                n_im = a_re * s_im + a_im * s_re + him_ref[pl.ds(r0, nb), ls]
                hre_ref[pl.ds(r0, nb), ls] = n_re
                him_ref[pl.ds(r0, nb), ls] = n_im
                s_re, s_im = n_re, n_im
            return s_re, s_im

        s_re, s_im = lax.fori_loop(0, lt // SCAN_UNROLL, steps, (sre_ref[:, ls], sim_ref[:, ls]))
        sre_ref[:, ls] = s_re
        sim_ref[:, ls] = s_im

    ys = []
    for k in range(S5_SLABS):
        hk = jnp.concatenate([hre_ref[:, k * SLAB_ST:(k + 1) * SLAB_ST].astype(BF),
                              him_ref[:, k * SLAB_ST:(k + 1) * SLAB_ST].astype(BF)], axis=1)
        ys.append(_dot(hk, cd_ref[k]))
    y = jax.nn.gelu(jnp.concatenate(ys, axis=1) + dsk_ref[...] * u)
    y = y * jax.nn.sigmoid(_dot(y.astype(BF), wglu_ref[...]))
    mixed = _dot(y.astype(BF), wout_ref[...])
    res = _layer_norm(DN_ALPHA * x + mixed, g_ref[...], b_ref[...])
    for c in range(nlb):
        xs_ref[c] = res[:, c * LANES:(c + 1) * LANES]
    for c, w in enumerate(_pack_row_words(res)):
        pk_ref[c] = w
    for c in range(nlb):
        for b in range(nb):
            o_ref[b, :, c * LANES:(c + 1) * LANES] = xs_ref[c, pl.ds(b, lt, stride=nb), :]
    for c in range(NWORD):
        for b in range(nb):
            p_ref[b, pl.ds(c, lt, stride=NWORD), :] = pk_ref[c, pl.ds(b, lt, stride=nb), :]


def _mixer1(x3, w_in, lambda_re, lambda_im, b_re, b_im, c_re, c_im, d_skip, log_dt, w_glu, w_out,
            ln_g, ln_b):
    B, S, _ = x3.shape
    lt = S5_STEPS
    rows = S5_BATCH * lt
    a_re, a_im, bb_re, bb_im = _s5_discretize(lambda_re, lambda_im, log_dt, b_re, b_im)
    gps = SSM_GROUPS // S5_SLABS
    eye = jnp.eye(gps, dtype=F32)

    def bdiag(bb):
        b4 = bb.reshape(S5_SLABS, gps, SSM_STATE, SSM_GROUP)
        return jnp.einsum('kgph,gf->kghfp', b4, eye).reshape(S5_SLABS, SLAB_CH, SLAB_ST)

    def cdiag(cc):
        c4 = cc.reshape(S5_SLABS, gps, SSM_GROUP, SSM_STATE)
        return jnp.einsum('kghp,gf->kfpgh', c4, eye).reshape(S5_SLABS, SLAB_ST, SLAB_CH)

    bd = jnp.concatenate([bdiag(bb_re), bdiag(bb_im)], axis=2).astype(BF)
    cd = jnp.concatenate([cdiag(c_re), -cdiag(c_im)], axis=1).astype(BF)
    c2 = lambda shp: pl.BlockSpec(shp, lambda bi, ti: (0,) * len(shp))
    return pl.pallas_call(
        functools.partial(_s5_body, lt=lt),
        grid=(B // S5_BATCH, S // lt),
        in_specs=[
            pl.BlockSpec((S5_BATCH, lt, D_MODEL), lambda bi, ti: (bi, ti, 0)),
            c2((D_MODEL, D_MODEL)), c2((1, N_STATE)), c2((1, N_STATE)),
            c2(bd.shape), c2(cd.shape), c2((1, D_MODEL)),
            c2((D_MODEL, D_MODEL)), c2((D_MODEL, D_MODEL)), c2((1, D_MODEL)), c2((1, D_MODEL)),
        ],
        out_specs=[pl.BlockSpec((S5_BATCH, lt, D_MODEL), lambda bi, ti: (bi, ti, 0)),
                   pl.BlockSpec((S5_BATCH, lt * NWORD, LANES), lambda bi, ti: (bi, ti, 0))],
        out_shape=[jax.ShapeDtypeStruct((B, S, D_MODEL), F32),
                   jax.ShapeDtypeStruct((B, S * NWORD, LANES), jnp.uint32)],
        scratch_shapes=[
            pltpu.VMEM((D_MODEL // LANES, rows, LANES), F32),
            pltpu.VMEM((NWORD, rows, LANES), jnp.uint32),
            pltpu.VMEM((rows, N_STATE), F32),
            pltpu.VMEM((rows, N_STATE), F32),
            pltpu.VMEM((S5_BATCH, N_STATE), F32),
            pltpu.VMEM((S5_BATCH, N_STATE), F32),
        ],
        compiler_params=_cparams(("parallel", "arbitrary")),
        name="s5_mixer",
    )(x3, w_in.astype(BF), a_re.reshape(1, N_STATE), a_im.reshape(1, N_STATE), bd, cd,
      d_skip.reshape(1, D_MODEL), w_glu.astype(BF), w_out.astype(BF), ln_g.reshape(1, -1), ln_b.reshape(1, -1))


def _moe_block(h2, packed, router, router_bias, w_gate, w_up, w_down, sh_gate, sh_up, sh_down, ln_g, ln_b):
    pos, gate, meta = _router(h2, router.T, router_bias.reshape(-1, 1))
    off = meta[:, 0, :N_EXPERTS].reshape(-1)
    cnt = meta[:, 1, :N_EXPERTS].reshape(-1)
    wgu = jnp.concatenate([w_gate, w_up], axis=-1).astype(BF)
    wsgu = jnp.concatenate([sh_gate, sh_up], axis=-1).astype(BF)
    pos = (pos.T * NWORD).reshape(-1)
    return _moe_experts(h2, packed, off, cnt, pos, gate.T, wgu, w_down.astype(BF), wsgu, sh_down.astype(BF),
                        ln_g.reshape(1, -1), ln_b.reshape(1, -1))


def kernel(x, l0_w_in, l0_cmp_pos_k, l0_cmp_pos_v, l0_cmp_k_w1, l0_cmp_k_w2, l0_cmp_v_w1, l0_cmp_v_w2, l0_w_out, l0_ln1_g, l0_ln1_b, l0_router, l0_router_bias, l0_w_gate, l0_w_up, l0_w_down, l0_sh_gate, l0_sh_up, l0_sh_down, l0_ln2_g, l0_ln2_b, l1_w_in, l1_lambda_re, l1_lambda_im, l1_b_re, l1_b_im, l1_c_re, l1_c_im, l1_d, l1_log_dt, l1_w_glu, l1_w_out, l1_ln1_g, l1_ln1_b, l1_router, l1_router_bias, l1_w_gate, l1_w_up, l1_w_down, l1_sh_gate, l1_sh_up, l1_sh_down, l1_ln2_g, l1_ln2_b):
    B, S, D = x.shape
    assert D == D_MODEL and S % 512 == 0 and B % S5_BATCH == 0 and (B * S) % MOE_TILE == 0
    T = B * S
    h, hp = _mixer0(x.reshape(T, D), B, S, l0_w_in, l0_cmp_pos_k, l0_cmp_pos_v, l0_cmp_k_w1, l0_cmp_k_w2,
                    l0_cmp_v_w1, l0_cmp_v_w2, l0_w_out, l0_ln1_g, l0_ln1_b)
    h = _moe_block(h, hp, l0_router, l0_router_bias, l0_w_gate, l0_w_up, l0_w_down, l0_sh_gate, l0_sh_up,
                   l0_sh_down, l0_ln2_g, l0_ln2_b)
    h, hp = _mixer1(h.reshape(B, S, D), l1_w_in, l1_lambda_re, l1_lambda_im, l1_b_re, l1_b_im, l1_c_re,
                    l1_c_im, l1_d, l1_log_dt, l1_w_glu, l1_w_out, l1_ln1_g, l1_ln1_b)
    h = _moe_block(h.reshape(T, D), hp.reshape(T * NWORD, LANES), l1_router, l1_router_bias, l1_w_gate, l1_w_up,
                   l1_w_down, l1_sh_gate, l1_sh_up, l1_sh_down, l1_ln2_g, l1_ln2_b)
    return h.reshape(B, S, D)
```

```python
import functools
import math

import numpy as np
import jax
import jax.numpy as jnp
from jax import lax
from jax.experimental import pallas as pl
from jax.experimental.pallas import tpu as pltpu

F32 = jnp.float32
BF = jnp.bfloat16

D_MODEL = 1024
DEPTH = 2
HEAD_DIM = 64
LANES = 128
SB_HEADS = 8
SB_WIDTH = SB_HEADS * HEAD_DIM
NSA_HEADS = 8
NSA_KV_GROUPS = 2
NSA_REP = NSA_HEADS // NSA_KV_GROUPS
NSA_WIDTH = NSA_HEADS * HEAD_DIM
NSA_N_BRANCH = 3
CMP_LEN = 32
CMP_STRIDE = 16
CMP_HIDDEN = 256
SLC_LEN = 64
SLC_TOPN = 8
SLC_LOCAL = 2
SLC_FORCE_BONUS = 1e4
WINDOW = 512
ROPE_THETA = 500000.0
ROPE_DIM = HEAD_DIM // 4
SSM_GROUP = 16
SSM_GROUPS = D_MODEL // SSM_GROUP
SSM_STATE = 64
N_EXPERTS = 64
N_EXPERT_GROUPS = 8
TOPK_GROUPS = 4
TOP_K = 6
EXPERT_HIDDEN = 256
SHARED_HIDDEN = 256
ROUTED_SCALE = 2.5
DN_ALPHA = (2 * DEPTH) ** 0.25
LN_EPS = 1e-5
NEG_INF = -1e30
ATT_SCALE = HEAD_DIM ** -0.5
SB_UNDERFLOW = -104.0
SB_ROWS = 128

V7X_VMEM_BYTES = 64 * 1024 * 1024
VMEM_LIMIT = V7X_VMEM_BYTES - 8 * 1024 * 1024
MOE_VMEM_LIMIT = V7X_VMEM_BYTES - 4 * 1024 * 1024


def _cparams(sem, vmem=VMEM_LIMIT):
    return pltpu.CompilerParams(dimension_semantics=sem, vmem_limit_bytes=vmem)


def _dot(a, b):
    return jnp.dot(a, b, preferred_element_type=F32)


def _dot_nt(a, b):
    return lax.dot_general(a, b, (((1,), (1,)), ((), ())), preferred_element_type=F32)


def _split2(x):
    hi = x.astype(BF)
    lo = (x - hi.astype(F32)).astype(BF)
    return hi, lo


def _layer_norm(h, g, b):
    mu = jnp.mean(h, axis=-1, keepdims=True)
    d = h - mu
    var = jnp.mean(d * d, axis=-1, keepdims=True)
    return d * lax.rsqrt(var + LN_EPS) * g + b


HALF = D_MODEL // 2
NWORD = HALF // LANES
HI_MASK = 0xFFFF0000


def _pack_pairs(a, b):
    lo = pltpu.bitcast(a.astype(BF).astype(F32), jnp.uint32)
    hi = pltpu.bitcast(b.astype(BF).astype(F32), jnp.uint32)
    return (lo >> 16) | (hi & jnp.uint32(HI_MASK))


def _pack_row_words(h):
    return [_pack_pairs(h[:, c * LANES:(c + 1) * LANES], h[:, HALF + c * LANES:HALF + (c + 1) * LANES])
            for c in range(NWORD)]


def _store_row_words(ref, start, n, h):
    for c, w in enumerate(_pack_row_words(h)):
        ref[pl.ds(start * NWORD + c, n, stride=NWORD), :] = w


def _load_row_words(ref, start, n):
    return [ref[pl.ds(start * NWORD + c, n, stride=NWORD), :] for c in range(NWORD)]


def _unpack_lo(w):
    return pltpu.bitcast(w << 16, F32)


def _unpack_hi(w):
    return pltpu.bitcast(w & jnp.uint32(HI_MASK), F32)


N_PLAIN = 3 * SB_WIDTH + 3 * LANES + LANES
N_ROPE = NSA_WIDTH + 3 * LANES


def _proj0_body(x_ref, wp_ref, wr_ref, wrr_ref, cos_ref, sin_ref,
                qkva_ref, vals_ref, gate_ref, qb_ref, kr_ref):
    xb = x_ref[...].astype(BF)
    a0, a1, a2 = 3 * SB_WIDTH, 3 * SB_WIDTH + 3 * LANES, N_PLAIN
    qkva_ref[...] = _dot(xb, wp_ref[:, 0:a0]).astype(BF)
    vals_ref[...] = _dot(xb, wp_ref[:, a0:a1]).astype(BF)
    gate_ref[...] = jax.nn.sigmoid(_dot(xb, wp_ref[:, a1:a2]))
    y = _dot(xb, wr_ref[...]) * cos_ref[...] + _dot(xb, wrr_ref[...]) * sin_ref[...]
    qb_ref[...] = y[:, 0:NSA_WIDTH].astype(BF)
    kr_ref[...] = y[:, NSA_WIDTH:N_ROPE].astype(BF)


def _proj0(x2, wp, wr, wrr, cos_t, sin_t, seq):
    T = x2.shape[0]
    tm = 512
    nseq = seq // tm
    row = lambda i: (i, 0)
    full = lambda i: (0, 0)
    tab = lambda i: (i % nseq, 0)
    return pl.pallas_call(
        _proj0_body,
        grid=(T // tm,),
        in_specs=[
            pl.BlockSpec((tm, D_MODEL), row),
            pl.BlockSpec((D_MODEL, N_PLAIN), full),
            pl.BlockSpec((D_MODEL, N_ROPE), full),
            pl.BlockSpec((D_MODEL, N_ROPE), full),
            pl.BlockSpec((tm, N_ROPE), tab),
            pl.BlockSpec((tm, N_ROPE), tab),
        ],
        out_specs=[
            pl.BlockSpec((tm, 3 * SB_WIDTH), row),
            pl.BlockSpec((tm, 3 * LANES), row),
            pl.BlockSpec((tm, LANES), row),
            pl.BlockSpec((tm, NSA_WIDTH), row),
            pl.BlockSpec((tm, 3 * LANES), row),
        ],
        out_shape=[
            jax.ShapeDtypeStruct((T, 3 * SB_WIDTH), BF),
            jax.ShapeDtypeStruct((T, 3 * LANES), BF),
            jax.ShapeDtypeStruct((T, LANES), F32),
            jax.ShapeDtypeStruct((T, NSA_WIDTH), BF),
            jax.ShapeDtypeStruct((T, 3 * LANES), BF),
        ],
        compiler_params=_cparams(("parallel",)),
        name="proj0",
    )(x2, wp, wr, wrr, cos_t, sin_t)


def _sb_body(q_ref, k_ref, v_ref, o_ref, *, tq):
    i = pl.program_id(2)
    q = q_ref[0]
    lane = lax.broadcasted_iota(jnp.int32, (1, LANES), 1)
    row = lax.broadcasted_iota(jnp.int32, (tq, tq), 0)
    col = lax.broadcasted_iota(jnp.int32, (tq, tq), 1)
    tri = jnp.where(row > col, 1.0, 0.0).astype(BF)
    diag_causal = col < row

    hms = [(lane // HEAD_DIM) == hh for hh in range(2)]
    nchunk = tq // SB_ROWS
    chains = [(hh, rc) for hh in range(2) for rc in range(nchunk)]
    qcs = [jnp.where(hms[hh], q[rc * SB_ROWS:(rc + 1) * SB_ROWS], jnp.zeros((SB_ROWS, LANES), BF))
           for hh, rc in chains]

    def blocks(j, state, masks):
        off = pl.multiple_of(j * tq, tq)
        k = k_ref[0, pl.ds(off, tq), :]
        v = v_ref[0, pl.ds(off, tq), :]
        zs = [_dot_nt(qc, k) for qc in qcs]
        mids = []
        for z, mask in zip(zs, masks):
            soft = jnp.log(1.0 + jnp.exp(-jnp.abs(z)))
            log_1m = -(jnp.maximum(z, 0.0) + soft)
            log_sig = jnp.minimum(z, 0.0) - soft
            if mask is not None:
                log_1m = jnp.where(mask, log_1m, 0.0)
            mids.append((log_1m, log_sig, log_1m.astype(BF)))
        sufs = [_dot(hi, tri) for _, _, hi in mids]
        ws = []
        for n, ((log_1m, log_sig, _), suffix, mask) in enumerate(zip(mids, sufs, masks)):
            w = jnp.exp(log_sig + suffix + state[2 * n])
            if mask is not None:
                w = jnp.where(mask, w, 0.0)
            ws.append(w.astype(BF))
        new = []
        for n, w in enumerate(ws):
            new.append(state[2 * n] + jnp.sum(mids[n][0], axis=-1, keepdims=True))
            new.append(state[2 * n + 1] + _dot(w, v))
        return new

    zc, za = jnp.zeros((SB_ROWS, 1), F32), jnp.zeros((SB_ROWS, LANES), F32)
    state = blocks(i, [zc, za] * len(chains),
                   [diag_causal[rc * SB_ROWS:(rc + 1) * SB_ROWS] for _, rc in chains])

    def live(state):
        top = jnp.max(state[0])
        for c in state[2::2]:
            top = jnp.maximum(top, jnp.max(c))
        return top > SB_UNDERFLOW

    def cond(st):
        return jnp.logical_and(st[0] >= 0, st[1])

    def body(st):
        new = blocks(st[0], st[2], [None] * len(chains))
        return st[0] - 1, live(new), tuple(new)

    _, _, state = lax.while_loop(cond, body, (i - 1, live(state), tuple(state)))
    accs = [jnp.concatenate([state[2 * (hh * nchunk + rc) + 1] for rc in range(nchunk)], axis=0)
            for hh in range(2)]
    o_ref[0] = jnp.where(hms[0], accs[0], accs[1]).astype(BF)


def _sb_attention(qkva3):
    B, S, _ = qkva3.shape
    tq = 256
    npair = SB_WIDTH // LANES
    return pl.pallas_call(
        functools.partial(_sb_body, tq=tq),
        grid=(B, npair, S // tq),
        in_specs=[
            pl.BlockSpec((1, tq, LANES), lambda b, p, i: (b, i, p)),
            pl.BlockSpec((1, S, LANES), lambda b, p, i: (b, 0, npair + p)),
            pl.BlockSpec((1, S, LANES), lambda b, p, i: (b, 0, 2 * npair + p)),
        ],
        out_specs=pl.BlockSpec((1, tq, LANES), lambda b, p, i: (b, i, p)),
        out_shape=jax.ShapeDtypeStruct((B, S, SB_WIDTH), BF),
        compiler_params=_cparams(("parallel", "parallel", "arbitrary")),
        name="sb_attention",
    )(qkva3, qkva3, qkva3)


def _cmp_body(ak_ref, av_ref, posk_ref, posv_ref, w1k_ref, w1kt_ref, w1kb_ref, w2k_ref,
              w1v_ref, w1vt_ref, w1vb_ref, w2v_ref, kc_ref, vc_ref):
    def one(a_ref, pos_ref, w1_ref, w1t_ref, w1b_ref, w2_ref, o_ref):
        a = a_ref[0]
        n = a.shape[0]
        bias = _dot(pos_ref[...], w1_ref[...])[0:1]
        out = jnp.zeros((n, LANES), F32)
        for g in range(NSA_KV_GROUPS):
            top = _dot(a, w1t_ref[g])
            bot = _dot(a, w1b_ref[g])
            h = top + pltpu.roll(bot, n - 1, 0) + bias
            out = out + _dot(jax.nn.gelu(h).astype(BF), w2_ref[g])
        o_ref[0] = out.astype(BF)

    one(ak_ref, posk_ref, w1k_ref, w1kt_ref, w1kb_ref, w2k_ref, kc_ref)
    one(av_ref, posv_ref, w1v_ref, w1vt_ref, w1vb_ref, w2v_ref, vc_ref)


def _compress(ak, av, posk, posv, wk, wv):
    B, n, width = ak.shape
    blk = pl.BlockSpec((1, n, width), lambda b: (b, 0, 0))
    c2 = lambda shp: pl.BlockSpec(shp, lambda b: (0, 0))
    c3 = lambda shp: pl.BlockSpec(shp, lambda b: (0, 0, 0))
    wspecs = [c2((CMP_LEN * HEAD_DIM, CMP_HIDDEN)), c3((2, width, CMP_HIDDEN)),
              c3((2, width, CMP_HIDDEN)), c3((2, CMP_HIDDEN, LANES))]
    out = pl.BlockSpec((1, n, LANES), lambda b: (b, 0, 0))
    return pl.pallas_call(
        _cmp_body,
        grid=(B,),
        in_specs=[blk, blk, c2((8, CMP_LEN * HEAD_DIM)), c2((8, CMP_LEN * HEAD_DIM))] + wspecs + wspecs,
        out_specs=[out, out],
        out_shape=[jax.ShapeDtypeStruct((B, n, LANES), BF)] * 2,
        compiler_params=_cparams(("parallel",)),
        name="nsa_compress",
    )(ak, av, posk, posv, *wk, *wv)


def _nsa_body(q_ref, kc_ref, vct_ref, ks_ref, vst_ref, kw_ref, vwt_ref, gt_ref, ovl_ref, et_ref,
              o_ref, *, tq, tk, nblk):
    i = pl.program_id(1)
    R = NSA_REP
    nchain = NSA_KV_GROUPS * R
    t0 = i * tq
    lane = lax.broadcasted_iota(jnp.int32, (1, LANES), 1)
    q = q_ref[0]
    t_row = t0 + lax.broadcasted_iota(jnp.int32, (1, tq), 1)
    tile = lambda a: jnp.concatenate([a] * nchain, axis=1)
    n_col = lax.broadcasted_iota(jnp.int32, (LANES, 1), 0)
    cmp_bias = tile(jnp.where((CMP_STRIDE * n_col + (CMP_LEN - 1)) <= t_row, 0.0, NEG_INF))
    has_cmp = tile(jnp.where(t_row >= CMP_LEN - 1, 1.0, 0.0))
    nkb = (t0 + tq + tk - 1) // tk
    key_last = (nkb - 1) * tk + lax.broadcasted_iota(jnp.int32, (tk, 1), 0)
    diag_bias = tile(jnp.where(key_last <= t_row, 0.0, NEG_INF))
    wkeys = WINDOW + tq
    woff = pl.multiple_of(jnp.maximum(i - WINDOW // tq, 0) * tq, tq)
    kp = woff + lax.broadcasted_iota(jnp.int32, (wkeys, 1), 0)
    win_bias = tile(jnp.where(kp <= t_row, jnp.where(kp > t_row - WINDOW, 0.0, NEG_INF), NEG_INF))
    blk = lax.broadcasted_iota(jnp.int32, (nblk, 1), 0)
    cur = t_row // SLC_LEN
    valid = blk <= cur
    forced = (blk == 0) | (valid & (blk > cur - SLC_LOCAL))
    ovl = ovl_ref[...]

    qrs = [jnp.where((lane // HEAD_DIM) == gi, q[:, LANES * r:LANES * (r + 1)], jnp.zeros((tq, LANES), BF))
           for gi in range(NSA_KV_GROUPS) for r in range(R)]
    q_all = jnp.concatenate(qrs, axis=0)

    s = _dot_nt(kc_ref[0], q_all) + cmp_bias
    e = jnp.exp(s - jnp.max(s, axis=0, keepdims=True))
    pc = e / jnp.sum(e, axis=0, keepdims=True) * has_cmp
    o_cmp = _dot(vct_ref[0], pc.astype(BF))

    q_aug = []
    for gi in range(NSA_KV_GROUPS):
        psum = jnp.zeros((LANES, tq), F32)
        for r in range(R):
            c = gi * R + r
            psum = psum + pc[:, c * tq:(c + 1) * tq]
        p1 = psum.astype(BF)
        r1 = psum - p1.astype(F32)
        p2 = r1.astype(BF)
        p3 = (r1 - p2.astype(F32)).astype(BF)
        imp = _dot(ovl, p1) + _dot(ovl, p2) + _dot(ovl, p3)
        imp = jnp.where(forced, imp + SLC_FORCE_BONUS, jnp.where(valid, imp, NEG_INF))
        sel = jnp.zeros((nblk, tq), F32)
        for _ in range(min(SLC_TOPN, nblk)):
            mx = jnp.max(imp, axis=0, keepdims=True)
            idx = jnp.min(jnp.where(imp == mx, blk, nblk), axis=0, keepdims=True)
            hit = blk == idx
            sel = jnp.where(hit, 1.0, sel)
            imp = jnp.where(hit, -jnp.inf, imp)
        sel_t = jnp.concatenate([sel, jnp.zeros((LANES - nblk, tq), F32)], axis=0).T
        sel_bias = ((sel_t - 1.0) * -NEG_INF).astype(BF)
        q_aug.extend(jnp.concatenate([qrs[gi * R + r], sel_bias], axis=1) for r in range(R))
    qa_all = jnp.concatenate(q_aug, axis=0)

    def sel_step(j, state, bias):
        m, l, acc = state
        off = pl.multiple_of(j * tk, tk)
        ka = jnp.concatenate([ks_ref[0, pl.ds(off, tk), :], et_ref[pl.ds(off, tk), :]], axis=1)
        s = _dot_nt(ka, qa_all)
        if bias is not None:
            s = s + bias
        m_new = jnp.maximum(m, jnp.max(s, axis=0, keepdims=True))
        alpha = jnp.exp(m - m_new)
        p = jnp.exp(s - m_new)
        l = alpha * l + jnp.sum(p, axis=0, keepdims=True)
        acc = alpha * acc + _dot(vst_ref[0, :, pl.ds(off, tk)], p.astype(BF))
        return m_new, l, acc

    width = nchain * tq
    init = (jnp.full((1, width), NEG_INF, F32), jnp.zeros((1, width), F32), jnp.zeros((LANES, width), F32))
    state = lax.fori_loop(0, nkb - 1, lambda j, st: sel_step(j, st, None), init)
    _, l_s, acc_s = sel_step(nkb - 1, state, diag_bias)
    o_sel = acc_s / l_s

    s = _dot_nt(kw_ref[0, pl.ds(woff, wkeys), :], q_all) + win_bias
    p = jnp.exp(s - jnp.max(s, axis=0, keepdims=True))
    o_win = _dot(vwt_ref[0, :, pl.ds(woff, wkeys)], p.astype(BF)) / jnp.sum(p, axis=0, keepdims=True)

    gt = gt_ref[0]
    sub = lax.broadcasted_iota(jnp.int32, (LANES, 1), 0)
    g0 = (sub // HEAD_DIM) == 0
    for r in range(R):
        out = jnp.zeros((LANES, tq), F32)
        for br, o_br in enumerate((o_cmp, o_sel, o_win)):
            rows = [(gi * R + r) * NSA_N_BRANCH + br for gi in range(NSA_KV_GROUPS)]
            gate = jnp.where(g0, gt[rows[0]:rows[0] + 1], gt[rows[1]:rows[1] + 1])
            both = jnp.where(g0, o_br[:, r * tq:(r + 1) * tq], o_br[:, (R + r) * tq:(R + r + 1) * tq])
            out = out + gate * both
        o_ref[0, r * LANES:(r + 1) * LANES, :] = out.astype(BF)


def _nsa_attention(qb3, kc, vct, kr3, vals_t, gt, ovl, expand):
    B, S, _ = qb3.shape
    tq, tk = 256, 512
    n = kc.shape[1]
    nblk = ovl.shape[0]
    ngate = gt.shape[1]
    assert S >= WINDOW + tq and S % tk == 0 and nblk % 8 == 0 and n == LANES
    tok = lambda c: pl.BlockSpec((1, S, LANES), lambda b, i, c=c: (b, 0, c))
    tr = lambda c: pl.BlockSpec((1, LANES, S), lambda b, i, c=c: (b, c, 0))
    return pl.pallas_call(
        functools.partial(_nsa_body, tq=tq, tk=tk, nblk=nblk),
        grid=(B, S // tq),
        in_specs=[
            pl.BlockSpec((1, tq, NSA_WIDTH), lambda b, i: (b, i, 0)),
            pl.BlockSpec((1, n, LANES), lambda b, i: (b, 0, 0)),
            pl.BlockSpec((1, LANES, n), lambda b, i: (b, 0, 0)),
            tok(1), tr(1), tok(2), tr(2),
            pl.BlockSpec((1, ngate, tq), lambda b, i: (b, 0, i)),
            pl.BlockSpec((nblk, LANES), lambda b, i: (0, 0)),
            pl.BlockSpec((S, LANES), lambda b, i: (0, 0)),
        ],
        out_specs=pl.BlockSpec((1, NSA_WIDTH, tq), lambda b, i: (b, 0, i)),
        out_shape=jax.ShapeDtypeStruct((B, NSA_WIDTH, S), BF),
        compiler_params=_cparams(("parallel", "arbitrary")),
        name="nsa_attention",
    )(qb3, kc, vct, kr3, vals_t, kr3, vals_t, gt, ovl, expand)


def _outproj_ln_body(oa_ref, obt_ref, x_ref, wa_ref, wb_ref, g_ref, b_ref, o_ref, p_ref):
    yb = lax.dot_general(obt_ref[0], wb_ref[...], (((0,), (0,)), ((), ())), preferred_element_type=F32)
    y = _dot(oa_ref[...], wa_ref[...]) + yb
    res = _layer_norm(DN_ALPHA * x_ref[...] + y, g_ref[...], b_ref[...])
    o_ref[...] = res
    _store_row_words(p_ref, 0, res.shape[0], res)


def _outproj_ln(oa, obt, x2, wa, wb, g, b):
    T = x2.shape[0]
    tm = 512
    nseq = obt.shape[2] // tm
    row = lambda i: (i, 0)
    full = lambda i: (0, 0)
    return pl.pallas_call(
        _outproj_ln_body,
        grid=(T // tm,),
        in_specs=[
            pl.BlockSpec((tm, oa.shape[1]), row),
            pl.BlockSpec((1, obt.shape[1], tm), lambda i: (i // nseq, 0, i % nseq)),
            pl.BlockSpec((tm, D_MODEL), row),
            pl.BlockSpec(wa.shape, full),
            pl.BlockSpec(wb.shape, full),
            pl.BlockSpec((1, D_MODEL), full),
            pl.BlockSpec((1, D_MODEL), full),
        ],
        out_specs=[pl.BlockSpec((tm, D_MODEL), row),
                   pl.BlockSpec((tm * NWORD, LANES), row)],
        out_shape=[jax.ShapeDtypeStruct((T, D_MODEL), F32),
                   jax.ShapeDtypeStruct((T * NWORD, LANES), jnp.uint32)],
        compiler_params=_cparams(("parallel",)),
        name="outproj_ln",
    )(oa, obt, x2, wa, wb, g, b)


def _rope_tables(seq):
    inv = jnp.power(ROPE_THETA, -jnp.arange(0, ROPE_DIM, 2, dtype=F32) / ROPE_DIM)
    ang = jnp.arange(seq, dtype=F32)[:, None] * inv[None, :]
    half = ROPE_DIM // 2
    rest = HEAD_DIM - ROPE_DIM
    cos_h = jnp.concatenate([jnp.cos(ang), jnp.cos(ang), jnp.ones((seq, rest), F32)], axis=1)
    sin_h = jnp.concatenate([jnp.sin(ang), jnp.sin(ang), jnp.zeros((seq, rest), F32)], axis=1)
    reps = N_ROPE // HEAD_DIM
    del half
    return jnp.tile(cos_h, (1, reps)), jnp.tile(sin_h, (1, reps))


def _rot_cols(w):
    k, n = w.shape
    w3 = w.reshape(k, n // HEAD_DIM, HEAD_DIM)
    half = ROPE_DIM // 2
    rot = jnp.concatenate([-w3[..., half:ROPE_DIM], w3[..., :half],
                           jnp.zeros_like(w3[..., ROPE_DIM:])], axis=-1)
    return rot.reshape(k, n)


def _nsa_head_perm():
    cols = []
    for r in range(NSA_REP):
        for g in range(NSA_KV_GROUPS):
            h = g * NSA_REP + r
            cols.extend(range(h * HEAD_DIM, (h + 1) * HEAD_DIM))
    return np.asarray(cols)


def _mixer0(x2, batch, seq, w_in, cmp_pos_k, cmp_pos_v, cmp_k_w1, cmp_k_w2, cmp_v_w1, cmp_v_w2,
            w_out, ln_g, ln_b):
    T = x2.shape[0]
    sizes = [SB_WIDTH] * 3 + [NSA_WIDTH] + [NSA_KV_GROUPS * HEAD_DIM] * 6 + [NSA_HEADS * NSA_N_BRANCH]
    offs = np.concatenate([[0], np.cumsum(sizes)])
    col = lambda j: w_in[:, offs[j]:offs[j + 1]]
    perm = _nsa_head_perm()
    ngate = sizes[-1]
    wp = jnp.concatenate([col(0) * ATT_SCALE, col(1), col(2), col(5), col(7), col(9),
                          jnp.pad(col(10), ((0, 0), (0, LANES - ngate)))], axis=1)
    wr = jnp.concatenate([col(3)[:, perm] * ATT_SCALE, col(4), col(6), col(8)], axis=1)
    wrr = _rot_cols(wr)
    cos_t, sin_t = _rope_tables(seq)
    qkva, vals, gates, qb, kr = _proj0(x2, wp.astype(BF), wr.astype(BF), wrr.astype(BF), cos_t, sin_t, seq)

    r3 = lambda a: a.reshape(batch, seq, a.shape[-1])
    o_a = _sb_attention(r3(qkva))

    ncmp = seq // CMP_STRIDE
    ak = kr[:, 0:LANES].reshape(batch, ncmp, CMP_STRIDE * LANES)
    av = vals[:, 0:LANES].reshape(batch, ncmp, CMP_STRIDE * LANES)

    def cmp_weights(w1, w2):
        w1r = w1.reshape(2, CMP_STRIDE, HEAD_DIM, CMP_HIDDEN)
        tops, bots, w2s = [], [], []
        for g in range(NSA_KV_GROUPS):
            ext = jnp.zeros((2, CMP_STRIDE, NSA_KV_GROUPS, HEAD_DIM, CMP_HIDDEN), F32).at[:, :, g].set(w1r)
            ext = ext.reshape(2, CMP_STRIDE * LANES, CMP_HIDDEN)
            tops.append(ext[0])
            bots.append(ext[1])
            w2s.append(jnp.zeros((CMP_HIDDEN, LANES), F32).at[:, g * HEAD_DIM:(g + 1) * HEAD_DIM].set(w2))
        return (w1.astype(BF), jnp.stack(tops).astype(BF), jnp.stack(bots).astype(BF),
                jnp.stack(w2s).astype(BF))

    posk = jnp.broadcast_to(cmp_pos_k.reshape(1, -1), (8, CMP_LEN * HEAD_DIM)).astype(BF)
    posv = jnp.broadcast_to(cmp_pos_v.reshape(1, -1), (8, CMP_LEN * HEAD_DIM)).astype(BF)
    kc, vc = _compress(ak, av, posk, posv, cmp_weights(cmp_k_w1, cmp_k_w2), cmp_weights(cmp_v_w1, cmp_v_w2))
    assert ncmp <= LANES
    if ncmp < LANES:
        kc = jnp.pad(kc, ((0, 0), (0, LANES - ncmp), (0, 0)))
        vc = jnp.pad(vc, ((0, 0), (0, LANES - ncmp), (0, 0)))

    n_slc = seq // SLC_LEN
    cmp_start = np.arange(ncmp) * CMP_STRIDE
    slc_start = np.arange(n_slc) * SLC_LEN
    ovl = ((cmp_start[None, :] <= slc_start[:, None] + SLC_LEN - 1)
           & (cmp_start[None, :] + CMP_LEN - 1 >= slc_start[:, None])).astype(np.float32)
    ovl = np.pad(ovl, ((0, 0), (0, LANES - ncmp))) if ncmp < LANES else ovl
    expand = (np.arange(seq)[:, None] // SLC_LEN == np.arange(LANES)[None, :]).astype(np.float32)
    vals_t = jnp.swapaxes(r3(vals), 1, 2)
    gt = jnp.swapaxes(r3(gates)[:, :, :32], 1, 2)
    o_bt = _nsa_attention(r3(qb), kc, jnp.swapaxes(vc, 1, 2), r3(kr), vals_t, gt,
                          jnp.asarray(ovl, BF), jnp.asarray(expand, BF))

    wa = w_out[:SB_WIDTH].astype(BF)
    wb = w_out[SB_WIDTH:][perm].astype(BF)
    return _outproj_ln(o_a.reshape(T, SB_WIDTH), o_bt, x2, wa, wb,
                       ln_g.reshape(1, -1), ln_b.reshape(1, -1))


MOE_TILE = 2048
ROW_ALIGN = 8
RANK_CHUNK = 256


def _top_rows(vals, ids, n_ids, count):
    hits = []
    for _ in range(count):
        mx = jnp.max(vals, axis=0, keepdims=True)
        idx = jnp.min(jnp.where(vals == mx, ids, n_ids), axis=0, keepdims=True)
        hit = ids == idx
        hits.append(hit)
        vals = jnp.where(hit, -jnp.inf, vals)
    return hits


def _router_body(h_ref, rt_ref, rb_ref, pos_ref, gate_ref, meta_ref, *, tm):
    E, NG = N_EXPERTS, N_EXPERT_GROUPS
    per = E // NG
    hh, hl = _split2(h_ref[...])
    rh, rl = _split2(rt_ref[...])
    logits = _dot_nt(rh, hh) + _dot_nt(rh, hl) + _dot_nt(rl, hh)
    scores = jax.nn.sigmoid(logits)
    biased = scores + rb_ref[...]
    i8 = lax.broadcasted_iota(jnp.int32, (per, tm), 0)
    gs = []
    for g in range(NG):
        v = biased[g * per:(g + 1) * per]
        m1 = jnp.max(v, axis=0, keepdims=True)
        a1 = jnp.min(jnp.where(v == m1, i8, per), axis=0, keepdims=True)
        m2 = jnp.max(jnp.where(i8 == a1, -jnp.inf, v), axis=0, keepdims=True)
        gs.append(m1 + m2)
    gs = jnp.concatenate(gs, axis=0)
    gi = lax.broadcasted_iota(jnp.int32, (NG, tm), 0)
    ghits = _top_rows(gs, gi, NG, TOPK_GROUPS)
    gkeep = jnp.zeros((NG, tm), F32)
    for hit in ghits:
        gkeep = jnp.where(hit, 1.0, gkeep)
    ekeep = jnp.concatenate([jnp.broadcast_to(gkeep[g:g + 1], (per, tm)) for g in range(NG)], axis=0)
    ei = lax.broadcasted_iota(jnp.int32, (E, tm), 0)
    hits = _top_rows(jnp.where(ekeep > 0.5, biased, -jnp.inf), ei, E, TOP_K)
    gates = [jnp.sum(jnp.where(hit, scores, 0.0), axis=0, keepdims=True) for hit in hits]
    gsum = gates[0]
    for gk in gates[1:]:
        gsum = gsum + gk
    gates = [gk / gsum * ROUTED_SCALE for gk in gates]

    member = jnp.zeros((E, tm), F32)
    for hit in hits:
        member = jnp.where(hit, 1.0, member)
    cnt_col = jnp.sum(member, axis=1, keepdims=True)
    pad_col = jnp.floor((cnt_col + (ROW_ALIGN - 1)) * (1.0 / ROW_ALIGN)) * ROW_ALIGN
    sub_e = lax.broadcasted_iota(jnp.int32, (E, LANES), 0)
    lane_e = lax.broadcasted_iota(jnp.int32, (E, LANES), 1)
    cnt_row = jnp.sum(jnp.where(sub_e == lane_e, cnt_col, 0.0), axis=0, keepdims=True)
    pad_row = jnp.sum(jnp.where(sub_e == lane_e, pad_col, 0.0), axis=0, keepdims=True)
    off_row = jnp.sum(jnp.where(sub_e < lane_e, pad_col, 0.0), axis=0, keepdims=True)
    off_col = jnp.sum(jnp.where(lane_e < sub_e, pad_row, 0.0), axis=1, keepdims=True)
    r_i = lax.broadcasted_iota(jnp.int32, (RANK_CHUNK, RANK_CHUNK), 0)
    c_i = lax.broadcasted_iota(jnp.int32, (RANK_CHUNK, RANK_CHUNK), 1)
    before = jnp.where(r_i < c_i, 1.0, 0.0).astype(BF)
    running = off_col
    ranks = []
    for c in range(tm // RANK_CHUNK):
        mc = member[:, c * RANK_CHUNK:(c + 1) * RANK_CHUNK]
        ranks.append(_dot(mc.astype(BF), before) + running)
        running = running + jnp.sum(mc, axis=1, keepdims=True)
    slot = jnp.concatenate(ranks, axis=1)
    pos = [jnp.sum(jnp.where(hit, slot, 0.0), axis=0, keepdims=True) for hit in hits]
    zrow = jnp.zeros((1, tm), F32)
    pos_ref[...] = jnp.concatenate(pos + [zrow, zrow], axis=0).astype(jnp.int32)
    gate_ref[...] = jnp.concatenate(gates + [zrow, zrow], axis=0)
    z128 = jnp.zeros((1, LANES), F32)
    meta_ref[0] = jnp.concatenate([off_row, cnt_row] + [z128] * 6, axis=0).astype(jnp.int32)


def _router(h2, router_t, bias_col):
    T = h2.shape[0]
    tm = MOE_TILE
    nt = T // tm
    return pl.pallas_call(
        functools.partial(_router_body, tm=tm),
        grid=(nt,),
        in_specs=[
            pl.BlockSpec((tm, D_MODEL), lambda i: (i, 0)),
            pl.BlockSpec((N_EXPERTS, D_MODEL), lambda i: (0, 0)),
            pl.BlockSpec((N_EXPERTS, 1), lambda i: (0, 0)),
        ],
        out_specs=[
            pl.BlockSpec((8, tm), lambda i: (0, i)),
            pl.BlockSpec((8, tm), lambda i: (0, i)),
            pl.BlockSpec((1, 8, LANES), lambda i: (i, 0, 0)),
        ],
        out_shape=[
            jax.ShapeDtypeStruct((8, T), jnp.int32),
            jax.ShapeDtypeStruct((8, T), F32),
            jax.ShapeDtypeStruct((nt, 8, LANES), jnp.int32),
        ],
        compiler_params=_cparams(("parallel",)),
        name="moe_router",
    )(h2, router_t, bias_col)


EXPERTS_PER_STEP = 4
EXPERT_CHUNK = 256
COMBINE_SUB = 256
POS_STRIDE = 8


def _swiglu(xb, wgu, wd, hidden):
    gu = _dot(xb, wgu)
    a = jax.nn.silu(gu[:, :hidden]) * gu[:, hidden:]
    return _dot(a.astype(BF), wd)


def _moe_body(off_ref, cnt_ref, pos_ref, src_ref, x_ref, gcol_ref, wgu_ref, wd_ref, wsgu_ref, wsd_ref,
              g_ref, b_ref, o_ref, xs_ref, z_ref, *, tm, eb, ch, sub, unroll):
    i = pl.program_id(0)
    j = pl.program_id(1)
    nj = N_EXPERTS // eb

    @pl.when(j == 0)
    def _dispatch():
        zeros = jnp.zeros((ROW_ALIGN * NWORD, LANES), jnp.uint32)

        def pad(e, carry):
            off, cnt = off_ref[i * N_EXPERTS + e], cnt_ref[i * N_EXPERTS + e]
            last = pl.multiple_of((off + cnt // ROW_ALIGN * ROW_ALIGN) * NWORD, ROW_ALIGN * NWORD)
            xs_ref[pl.ds(last, ROW_ALIGN * NWORD), :] = zeros
            return carry

        lax.fori_loop(0, N_EXPERTS, pad, 0)
        end = off_ref[i * N_EXPERTS + N_EXPERTS - 1] + cnt_ref[i * N_EXPERTS + N_EXPERTS - 1]
        end = pl.multiple_of((end + ROW_ALIGN - 1) // ROW_ALIGN * ROW_ALIGN * NWORD, ROW_ALIGN * NWORD)
        xs_ref[pl.ds(end, ch * NWORD), :] = jnp.zeros((ch * NWORD, LANES), jnp.uint32)

        def tok(tb, carry):
            for u in range(unroll):
                t = tb * unroll + u
                slab = src_ref[pl.ds(pl.multiple_of(t * NWORD, NWORD), NWORD), :]
                for k in range(TOP_K):
                    p = pl.multiple_of(pos_ref[t * POS_STRIDE + k], NWORD)
                    xs_ref[pl.ds(p, NWORD), :] = slab
            return carry

        lax.fori_loop(0, tm // unroll, tok, 0)

    def chunks_in(els, offs, c):
        r0s = [pl.multiple_of(off + c * ch, ROW_ALIGN) for off in offs]
        words = [_load_row_words(xs_ref, r0, ch) for r0 in r0s]
        xbs = [jnp.concatenate([_unpack_lo(w).astype(BF) for w in ws]
                               + [_unpack_hi(w).astype(BF) for w in ws], axis=1) for ws in words]
        gus = [_dot(xb, wgu_ref[el]) for xb, el in zip(xbs, els)]
        acts = [(jax.nn.silu(gu[:, :EXPERT_HIDDEN]) * gu[:, EXPERT_HIDDEN:]).astype(BF) for gu in gus]
        ys = [_dot(a, wd_ref[el]) for a, el in zip(acts, els)]
        return list(zip(r0s, words, ys))

    def chunk_out(r0, words, y, c, cnt):
        keep = (c * ch + lax.broadcasted_iota(jnp.int32, (ch, 1), 0)) < cnt
        for cc, packed in enumerate(_pack_row_words(y)):
            xs_ref[pl.ds(r0 * NWORD + cc, ch, stride=NWORD), :] = jnp.where(keep, packed, words[cc])

    @pl.when(j < nj)
    def _experts():
        offs = [off_ref[i * N_EXPERTS + j * eb + el] for el in range(eb)]
        cnts = [cnt_ref[i * N_EXPERTS + j * eb + el] for el in range(eb)]
        firsts = chunks_in(list(range(eb)), offs, 0)
        for el in range(eb):
            chunk_out(*firsts[el], 0, cnts[el])
        for el in range(eb):
            def chunk(c, carry, el=el):
                chunk_out(*chunks_in([el], [offs[el]], c)[0], c, cnts[el])
                return carry

            lax.fori_loop(1, (cnts[el] + ch - 1) // ch, chunk, 0)

    @pl.when(j >= nj)
    def _combine():
        base = (j - nj) * sub

        def tok(tb, carry):
            for u in range(unroll):
                tl = tb * unroll + u
                dst = pl.multiple_of(tl * NWORD, NWORD)
                for k in range(TOP_K):
                    p = pl.multiple_of(pos_ref[(base + tl) * POS_STRIDE + k], NWORD)
                    z_ref[k, pl.ds(dst, NWORD), :] = xs_ref[pl.ds(p, NWORD), :]
            return carry

        lax.fori_loop(0, sub // unroll, tok, 0)
        gcol = gcol_ref[...]
        lo = [jnp.zeros((sub, LANES), F32) for _ in range(NWORD)]
        hi = [jnp.zeros((sub, LANES), F32) for _ in range(NWORD)]
        for k in range(TOP_K):
            gk = gcol[:, k:k + 1]
            for c, w in enumerate(_load_row_words(z_ref.at[k], 0, sub)):
                lo[c] = lo[c] + gk * _unpack_lo(w)
                hi[c] = hi[c] + gk * _unpack_hi(w)
        routed = jnp.concatenate(lo + hi, axis=1)
        x = x_ref[...]
        shared = _swiglu(x.astype(BF), wsgu_ref[...], wsd_ref[...], SHARED_HIDDEN)
        o_ref[...] = _layer_norm(DN_ALPHA * x + routed + shared, g_ref[...], b_ref[...])


def _moe_experts(h2, packed, off, cnt, pos, gcol, wgu, wd, wsgu, wsd, g, b):
    T = h2.shape[0]
    tm, eb, ch, sub = MOE_TILE, EXPERTS_PER_STEP, EXPERT_CHUNK, COMBINE_SUB
    nt, nj, nsub = T // tm, N_EXPERTS // eb, tm // sub
    rows = TOP_K * tm + N_EXPERTS * ROW_ALIGN + ch
    hidden2 = wgu.shape[-1]
    wblk = lambda i, j, *_: (jnp.minimum(j, nj - 1), 0, 0)
    sub_i = lambda i, j: i * nsub + jnp.clip(j - nj, 0, nsub - 1)
    once = pl.Buffered(1)
    return pl.pallas_call(
        functools.partial(_moe_body, tm=tm, eb=eb, ch=ch, sub=sub, unroll=4),
        grid_spec=pltpu.PrefetchScalarGridSpec(
            num_scalar_prefetch=2,
            grid=(nt, nj + nsub),
            in_specs=[
                pl.BlockSpec((tm * POS_STRIDE,), lambda i, j, *_: (i,), memory_space=pltpu.SMEM),
                pl.BlockSpec((tm * NWORD, LANES), lambda i, j, *_: (i, 0), pipeline_mode=once),
                pl.BlockSpec((sub, D_MODEL), lambda i, j, *_: (sub_i(i, j), 0)),
                pl.BlockSpec((sub, 8), lambda i, j, *_: (sub_i(i, j), 0)),
                pl.BlockSpec((eb, D_MODEL, hidden2), wblk),
                pl.BlockSpec((eb, hidden2 // 2, D_MODEL), wblk),
                pl.BlockSpec(wsgu.shape, lambda i, j, *_: (0, 0), pipeline_mode=once),
                pl.BlockSpec(wsd.shape, lambda i, j, *_: (0, 0), pipeline_mode=once),
                pl.BlockSpec((1, D_MODEL), lambda i, j, *_: (0, 0)),
                pl.BlockSpec((1, D_MODEL), lambda i, j, *_: (0, 0)),
            ],
            out_specs=pl.BlockSpec((sub, D_MODEL), lambda i, j, *_: (sub_i(i, j), 0)),
            scratch_shapes=[
                pltpu.VMEM((rows * NWORD, LANES), jnp.uint32),
                pltpu.VMEM((TOP_K, sub * NWORD, LANES), jnp.uint32),
            ],
        ),
        out_shape=jax.ShapeDtypeStruct((T, D_MODEL), F32),
        compiler_params=_cparams(("parallel", "arbitrary"), MOE_VMEM_LIMIT),
        name="moe_experts",
    )(off, cnt, pos, packed, h2, gcol, wgu, wd, wsgu, wsd, g, b)


S5_BATCH = 8
S5_STEPS = 64
S5_SLABS = 4
SLAB_CH = D_MODEL // S5_SLABS
SLAB_ST = SSM_GROUPS * SSM_STATE // S5_SLABS
N_STATE = SSM_GROUPS * SSM_STATE
SCAN_LANES = 512
SCAN_UNROLL = 8


def _s5_disc_body(lre_ref, lim_ref, ldt_ref, bre_ref, bim_ref, are_ref, aim_ref, bbre_ref, bbim_ref):
    lre, lim = lre_ref[...], lim_ref[...]
    step = jnp.exp(ldt_ref[...])
    mag = jnp.exp(lre * step)
    a_re = mag * jnp.cos(lim * step)
    a_im = mag * jnp.sin(lim * step)
    den = lre * lre + lim * lim
    zoh_re = ((a_re - 1.0) * lre + a_im * lim) / den
    zoh_im = (a_im * lre - (a_re - 1.0) * lim) / den
    are_ref[...] = a_re
    aim_ref[...] = a_im
    bbre_ref[...] = zoh_re * bre_ref[...] - zoh_im * bim_ref[...]
    bbim_ref[...] = zoh_re * bim_ref[...] + zoh_im * bre_ref[...]


def _s5_discretize(lambda_re, lambda_im, log_dt, b_re, b_im):
    col = lambda a: a.reshape(N_STATE, 1)
    ldt = jnp.broadcast_to(log_dt[:, None], (SSM_GROUPS, SSM_STATE))
    mat = lambda a: a.reshape(N_STATE, SSM_GROUP)
    c1 = jax.ShapeDtypeStruct((N_STATE, 1), F32)
    c16 = jax.ShapeDtypeStruct((N_STATE, SSM_GROUP), F32)
    return pl.pallas_call(_s5_disc_body, out_shape=[c1, c1, c16, c16], name="s5_discretize")(
        col(lambda_re), col(lambda_im), col(ldt), mat(b_re), mat(b_im))


def _s5_body(x_ref, win_ref, are_ref, aim_ref, bd_ref, cd_ref, dsk_ref, wglu_ref, wout_ref, g_ref, b_ref,
             o_ref, p_ref, xs_ref, pk_ref, hre_ref, him_ref, sre_ref, sim_ref, *, lt):
    nb = S5_BATCH
    nlb = D_MODEL // LANES
    for c in range(nlb):
        for b in range(nb):
            xs_ref[c, pl.ds(b, lt, stride=nb), :] = x_ref[b, :, c * LANES:(c + 1) * LANES]
    x = jnp.concatenate([xs_ref[c] for c in range(nlb)], axis=1)
    u = _dot(x.astype(BF), win_ref[...])
    ub = u.astype(BF)
    for k in range(S5_SLABS):
        bu = _dot(ub[:, k * SLAB_CH:(k + 1) * SLAB_CH], bd_ref[k])
        hre_ref[:, k * SLAB_ST:(k + 1) * SLAB_ST] = bu[:, :SLAB_ST]
        him_ref[:, k * SLAB_ST:(k + 1) * SLAB_ST] = bu[:, SLAB_ST:]

    @pl.when(pl.program_id(1) == 0)
    def _():
        sre_ref[...] = jnp.zeros_like(sre_ref)
        sim_ref[...] = jnp.zeros_like(sim_ref)

    for c in range(N_STATE // SCAN_LANES):
        ls = slice(c * SCAN_LANES, (c + 1) * SCAN_LANES)
        a_re = jnp.broadcast_to(are_ref[:, ls], (nb, SCAN_LANES))
        a_im = jnp.broadcast_to(aim_ref[:, ls], (nb, SCAN_LANES))

        def steps(tb, state, ls=ls, a_re=a_re, a_im=a_im):
            s_re, s_im = state
            for uu in range(SCAN_UNROLL):
                r0 = pl.multiple_of((tb * SCAN_UNROLL + uu) * nb, nb)
                n_re = a_re * s_re - a_im * s_im + hre_ref[pl.ds(r0, nb), ls]
                n_im = a_re * s_im + a_im * s_re + him_ref[pl.ds(r0, nb), ls]
                hre_ref[pl.ds(r0, nb), ls] = n_re
                him_ref[pl.ds(r0, nb), ls] = n_im
                s_re, s_im = n_re, n_im
            return s_re, s_im

        s_re, s_im = lax.fori_loop(0, lt // SCAN_UNROLL, steps, (sre_ref[:, ls], sim_ref[:, ls]))
        sre_ref[:, ls] = s_re
        sim_ref[:, ls] = s_im

    ys = []
    for k in range(S5_SLABS):
        hk = jnp.concatenate([hre_ref[:, k * SLAB_ST:(k + 1) * SLAB_ST].astype(BF),
                              him_ref[:, k * SLAB_ST:(k + 1) * SLAB_ST].astype(BF)], axis=1)
        ys.append(_dot(hk, cd_ref[k]))
    y = jax.nn.gelu(jnp.concatenate(ys, axis=1) + dsk_ref[...] * u)
    y = y * jax.nn.sigmoid(_dot(y.astype(BF), wglu_ref[...]))
    mixed = _dot(y.astype(BF), wout_ref[...])
    res = _layer_norm(DN_ALPHA * x + mixed, g_ref[...], b_ref[...])
    for c in range(nlb):
        xs_ref[c] = res[:, c * LANES:(c + 1) * LANES]
    for c, w in enumerate(_pack_row_words(res)):
        pk_ref[c] = w
    for c in range(nlb):
        for b in range(nb):
            o_ref[b, :, c * LANES:(c + 1) * LANES] = xs_ref[c, pl.ds(b, lt, stride=nb), :]
    for c in range(NWORD):
        for b in range(nb):
            p_ref[b, pl.ds(c, lt, stride=NWORD), :] = pk_ref[c, pl.ds(b, lt, stride=nb), :]


def _mixer1(x3, w_in, lambda_re, lambda_im, b_re, b_im, c_re, c_im, d_skip, log_dt, w_glu, w_out,
            ln_g, ln_b):
    B, S, _ = x3.shape
    lt = S5_STEPS
    rows = S5_BATCH * lt
    a_re, a_im, bb_re, bb_im = _s5_discretize(lambda_re, lambda_im, log_dt, b_re, b_im)
    gps = SSM_GROUPS // S5_SLABS
    eye = jnp.eye(gps, dtype=F32)

    def bdiag(bb):
        b4 = bb.reshape(S5_SLABS, gps, SSM_STATE, SSM_GROUP)
        return jnp.einsum('kgph,gf->kghfp', b4, eye).reshape(S5_SLABS, SLAB_CH, SLAB_ST)

    def cdiag(cc):
        c4 = cc.reshape(S5_SLABS, gps, SSM_GROUP, SSM_STATE)
        return jnp.einsum('kghp,gf->kfpgh', c4, eye).reshape(S5_SLABS, SLAB_ST, SLAB_CH)

    bd = jnp.concatenate([bdiag(bb_re), bdiag(bb_im)], axis=2).astype(BF)
    cd = jnp.concatenate([cdiag(c_re), -cdiag(c_im)], axis=1).astype(BF)
    c2 = lambda shp: pl.BlockSpec(shp, lambda bi, ti: (0,) * len(shp))
    return pl.pallas_call(
        functools.partial(_s5_body, lt=lt),
        grid=(B // S5_BATCH, S // lt),
        in_specs=[
            pl.BlockSpec((S5_BATCH, lt, D_MODEL), lambda bi, ti: (bi, ti, 0)),
            c2((D_MODEL, D_MODEL)), c2((1, N_STATE)), c2((1, N_STATE)),
            c2(bd.shape), c2(cd.shape), c2((1, D_MODEL)),
            c2((D_MODEL, D_MODEL)), c2((D_MODEL, D_MODEL)), c2((1, D_MODEL)), c2((1, D_MODEL)),
        ],
        out_specs=[pl.BlockSpec((S5_BATCH, lt, D_MODEL), lambda bi, ti: (bi, ti, 0)),
                   pl.BlockSpec((S5_BATCH, lt * NWORD, LANES), lambda bi, ti: (bi, ti, 0))],
        out_shape=[jax.ShapeDtypeStruct((B, S, D_MODEL), F32),
                   jax.ShapeDtypeStruct((B, S * NWORD, LANES), jnp.uint32)],
        scratch_shapes=[
            pltpu.VMEM((D_MODEL // LANES, rows, LANES), F32),
            pltpu.VMEM((NWORD, rows, LANES), jnp.uint32),
            pltpu.VMEM((rows, N_STATE), F32),
            pltpu.VMEM((rows, N_STATE), F32),
            pltpu.VMEM((S5_BATCH, N_STATE), F32),
            pltpu.VMEM((S5_BATCH, N_STATE), F32),
        ],
        compiler_params=_cparams(("parallel", "arbitrary")),
        name="s5_mixer",
    )(x3, w_in.astype(BF), a_re.reshape(1, N_STATE), a_im.reshape(1, N_STATE), bd, cd,
      d_skip.reshape(1, D_MODEL), w_glu.astype(BF), w_out.astype(BF), ln_g.reshape(1, -1), ln_b.reshape(1, -1))


def _moe_block(h2, packed, router, router_bias, w_gate, w_up, w_down, sh_gate, sh_up, sh_down, ln_g, ln_b):
    pos, gate, meta = _router(h2, router.T, router_bias.reshape(-1, 1))
    off = meta[:, 0, :N_EXPERTS].reshape(-1)
    cnt = meta[:, 1, :N_EXPERTS].reshape(-1)
    wgu = jnp.concatenate([w_gate, w_up], axis=-1).astype(BF)
    wsgu = jnp.concatenate([sh_gate, sh_up], axis=-1).astype(BF)
    pos = (pos.T * NWORD).reshape(-1)
    return _moe_experts(h2, packed, off, cnt, pos, gate.T, wgu, w_down.astype(BF), wsgu, sh_down.astype(BF),
                        ln_g.reshape(1, -1), ln_b.reshape(1, -1))


def kernel(x, l0_w_in, l0_cmp_pos_k, l0_cmp_pos_v, l0_cmp_k_w1, l0_cmp_k_w2, l0_cmp_v_w1, l0_cmp_v_w2, l0_w_out, l0_ln1_g, l0_ln1_b, l0_router, l0_router_bias, l0_w_gate, l0_w_up, l0_w_down, l0_sh_gate, l0_sh_up, l0_sh_down, l0_ln2_g, l0_ln2_b, l1_w_in, l1_lambda_re, l1_lambda_im, l1_b_re, l1_b_im, l1_c_re, l1_c_im, l1_d, l1_log_dt, l1_w_glu, l1_w_out, l1_ln1_g, l1_ln1_b, l1_router, l1_router_bias, l1_w_gate, l1_w_up, l1_w_down, l1_sh_gate, l1_sh_up, l1_sh_down, l1_ln2_g, l1_ln2_b):
    B, S, D = x.shape
    assert D == D_MODEL and S % 512 == 0 and B % S5_BATCH == 0 and (B * S) % MOE_TILE == 0
    T = B * S
    h, hp = _mixer0(x.reshape(T, D), B, S, l0_w_in, l0_cmp_pos_k, l0_cmp_pos_v, l0_cmp_k_w1, l0_cmp_k_w2,
                    l0_cmp_v_w1, l0_cmp_v_w2, l0_w_out, l0_ln1_g, l0_ln1_b)
    h = _moe_block(h, hp, l0_router, l0_router_bias, l0_w_gate, l0_w_up, l0_w_down, l0_sh_gate, l0_sh_up,
                   l0_sh_down, l0_ln2_g, l0_ln2_b)
    h, hp = _mixer1(h.reshape(B, S, D), l1_w_in, l1_lambda_re, l1_lambda_im, l1_b_re, l1_b_im, l1_c_re,
                    l1_c_im, l1_d, l1_log_dt, l1_w_glu, l1_w_out, l1_ln1_g, l1_ln1_b)
    h = _moe_block(h.reshape(T, D), hp.reshape(T * NWORD, LANES), l1_router, l1_router_bias, l1_w_gate, l1_w_up,
                   l1_w_down, l1_sh_gate, l1_sh_up, l1_sh_down, l1_ln2_g, l1_ln2_b)
    return h.reshape(B, S, D)
```

```python
import functools
import math

import numpy as np
import jax
import jax.numpy as jnp
from jax import lax
from jax.experimental import pallas as pl
from jax.experimental.pallas import tpu as pltpu

F32 = jnp.float32
BF = jnp.bfloat16

D_MODEL = 1024
DEPTH = 2
HEAD_DIM = 64
LANES = 128
SB_HEADS = 8
SB_WIDTH = SB_HEADS * HEAD_DIM
NSA_HEADS = 8
NSA_KV_GROUPS = 2
NSA_REP = NSA_HEADS // NSA_KV_GROUPS
NSA_WIDTH = NSA_HEADS * HEAD_DIM
NSA_N_BRANCH = 3
CMP_LEN = 32
CMP_STRIDE = 16
CMP_HIDDEN = 256
SLC_LEN = 64
SLC_TOPN = 8
SLC_LOCAL = 2
SLC_FORCE_BONUS = 1e4
WINDOW = 512
ROPE_THETA = 500000.0
ROPE_DIM = HEAD_DIM // 4
SSM_GROUP = 16
SSM_GROUPS = D_MODEL // SSM_GROUP
SSM_STATE = 64
N_EXPERTS = 64
N_EXPERT_GROUPS = 8
TOPK_GROUPS = 4
TOP_K = 6
EXPERT_HIDDEN = 256
SHARED_HIDDEN = 256
ROUTED_SCALE = 2.5
DN_ALPHA = (2 * DEPTH) ** 0.25
LN_EPS = 1e-5
NEG_INF = -1e30
ATT_SCALE = HEAD_DIM ** -0.5
SB_UNDERFLOW = -104.0
SB_ROWS = 128

V7X_VMEM_BYTES = 64 * 1024 * 1024
VMEM_LIMIT = V7X_VMEM_BYTES - 8 * 1024 * 1024
MOE_VMEM_LIMIT = V7X_VMEM_BYTES - 4 * 1024 * 1024


def _cparams(sem, vmem=VMEM_LIMIT):
    return pltpu.CompilerParams(dimension_semantics=sem, vmem_limit_bytes=vmem)


def _dot(a, b):
    return jnp.dot(a, b, preferred_element_type=F32)


def _dot_nt(a, b):
    return lax.dot_general(a, b, (((1,), (1,)), ((), ())), preferred_element_type=F32)


def _split2(x):
    hi = x.astype(BF)
    lo = (x - hi.astype(F32)).astype(BF)
    return hi, lo


def _layer_norm(h, g, b):
    mu = jnp.mean(h, axis=-1, keepdims=True)
    d = h - mu
    var = jnp.mean(d * d, axis=-1, keepdims=True)
    return d * lax.rsqrt(var + LN_EPS) * g + b


HALF = D_MODEL // 2
NWORD = HALF // LANES
HI_MASK = 0xFFFF0000


def _pack_pairs(a, b):
    lo = pltpu.bitcast(a.astype(BF).astype(F32), jnp.uint32)
    hi = pltpu.bitcast(b.astype(BF).astype(F32), jnp.uint32)
    return (lo >> 16) | (hi & jnp.uint32(HI_MASK))


def _pack_row_words(h):
    return [_pack_pairs(h[:, c * LANES:(c + 1) * LANES], h[:, HALF + c * LANES:HALF + (c + 1) * LANES])
            for c in range(NWORD)]


def _store_row_words(ref, start, n, h):
    for c, w in enumerate(_pack_row_words(h)):
        ref[pl.ds(start * NWORD + c, n, stride=NWORD), :] = w


def _load_row_words(ref, start, n):
    return [ref[pl.ds(start * NWORD + c, n, stride=NWORD), :] for c in range(NWORD)]


def _unpack_lo(w):
    return pltpu.bitcast(w << 16, F32)


def _unpack_hi(w):
    return pltpu.bitcast(w & jnp.uint32(HI_MASK), F32)


N_PLAIN = 3 * SB_WIDTH + 3 * LANES + LANES
N_ROPE = NSA_WIDTH + 3 * LANES


def _proj0_body(x_ref, wp_ref, wr_ref, cos_ref, sin_ref,
                qkva_ref, vals_ref, gate_ref, qb_ref, kr_ref):
    xb = x_ref[...].astype(BF)
    a0, a1, a2 = 3 * SB_WIDTH, 3 * SB_WIDTH + 3 * LANES, N_PLAIN
    qkva_ref[...] = _dot(xb, wp_ref[:, 0:a0]).astype(BF)
    vals_ref[...] = _dot(xb, wp_ref[:, a0:a1]).astype(BF)
    gate_ref[...] = jax.nn.sigmoid(_dot(xb, wp_ref[:, a1:a2]))
    y = _dot(xb, wr_ref[...])
    half = ROPE_DIM // 2
    pos = lax.broadcasted_iota(jnp.int32, (1, N_ROPE), 1) % HEAD_DIM
    partner = jnp.where(pos < half, -pltpu.roll(y, N_ROPE - half, 1), pltpu.roll(y, half, 1))
    y = y * cos_ref[...] + partner * sin_ref[...]
    qb_ref[...] = y[:, 0:NSA_WIDTH].astype(BF)
    kr_ref[...] = y[:, NSA_WIDTH:N_ROPE].astype(BF)


def _proj0(x2, wp, wr, cos_t, sin_t, seq):
    T = x2.shape[0]
    tm = 512
    nseq = seq // tm
    row = lambda i: (i, 0)
    full = lambda i: (0, 0)
    tab = lambda i: (i % nseq, 0)
    return pl.pallas_call(
        _proj0_body,
        grid=(T // tm,),
        in_specs=[
            pl.BlockSpec((tm, D_MODEL), row),
            pl.BlockSpec((D_MODEL, N_PLAIN), full),
            pl.BlockSpec((D_MODEL, N_ROPE), full),
            pl.BlockSpec((tm, N_ROPE), tab),
            pl.BlockSpec((tm, N_ROPE), tab),
        ],
        out_specs=[
            pl.BlockSpec((tm, 3 * SB_WIDTH), row),
            pl.BlockSpec((tm, 3 * LANES), row),
            pl.BlockSpec((tm, LANES), row),
            pl.BlockSpec((tm, NSA_WIDTH), row),
            pl.BlockSpec((tm, 3 * LANES), row),
        ],
        out_shape=[
            jax.ShapeDtypeStruct((T, 3 * SB_WIDTH), BF),
            jax.ShapeDtypeStruct((T, 3 * LANES), BF),
            jax.ShapeDtypeStruct((T, LANES), F32),
            jax.ShapeDtypeStruct((T, NSA_WIDTH), BF),
            jax.ShapeDtypeStruct((T, 3 * LANES), BF),
        ],
        compiler_params=_cparams(("parallel",)),
        name="proj0",
    )(x2, wp, wr, cos_t, sin_t)


def _sb_body(q_ref, k_ref, v_ref, o_ref, *, tq):
    i = pl.program_id(2)
    q = q_ref[0]
    lane = lax.broadcasted_iota(jnp.int32, (1, LANES), 1)
    row = lax.broadcasted_iota(jnp.int32, (tq, tq), 0)
    col = lax.broadcasted_iota(jnp.int32, (tq, tq), 1)
    tri = jnp.where(row > col, 1.0, 0.0).astype(BF)
    diag_causal = col < row

    hms = [(lane // HEAD_DIM) == hh for hh in range(2)]
    nchunk = tq // SB_ROWS
    chains = [(hh, rc) for hh in range(2) for rc in range(nchunk)]
    qcs = [jnp.where(hms[hh], q[rc * SB_ROWS:(rc + 1) * SB_ROWS], jnp.zeros((SB_ROWS, LANES), BF))
           for hh, rc in chains]

    def blocks(specs, state):
        nc = len(qcs)
        kvs = []
        for j, _, _ in specs:
            off = pl.multiple_of(j * tq, tq)
            kvs.append((k_ref[0, pl.ds(off, tq), :], v_ref[0, pl.ds(off, tq), :]))
        zs = [[_dot_nt(qc, k) for qc in qcs] for k, _ in kvs]
        mids = []
        for zrow, (_, masks, flag) in zip(zs, specs):
            row = []
            for n, z in enumerate(zrow):
                nl = jnp.maximum(z, 0.0) + jnp.log(1.0 + jnp.exp(-jnp.abs(z)))
                if masks is not None:
                    nl = jnp.where(masks[n], nl, 0.0)
                if flag is not None:
                    nl = jnp.where(flag, nl, 0.0)
                row.append((nl, z, nl.astype(BF)))
            mids.append(row)
        sufs = [[_dot(hi, tri) for _, _, hi in row] for row in mids]
        carries = [state[2 * n] for n in range(nc)]
        ws = []
        for row, srow, (_, masks, flag) in zip(mids, sufs, specs):
            wrow = []
            for n, ((nl, z, _), suffix) in enumerate(zip(row, srow)):
                w = jnp.exp(z - nl - suffix + carries[n])
                if masks is not None:
                    w = jnp.where(masks[n], w, 0.0)
                if flag is not None:
                    w = jnp.where(flag, w, 0.0)
                wrow.append(w.astype(BF))
                carries[n] = carries[n] - jnp.sum(nl, axis=-1, keepdims=True)
            ws.append(wrow)
        new = []
        for n in range(nc):
            acc = state[2 * n + 1]
            for wrow, (_, v) in zip(ws, kvs):
                acc = acc + _dot(wrow[n], v)
            new.extend([carries[n], acc])
        return new

    zc, za = jnp.zeros((SB_ROWS, 1), F32), jnp.zeros((SB_ROWS, LANES), F32)
    diag_masks = [diag_causal[rc * SB_ROWS:(rc + 1) * SB_ROWS] for _, rc in chains]
    state = blocks([(i, diag_masks, None), (jnp.maximum(i - 1, 0), None, i > 0)], [zc, za] * len(chains))

    def live(state):
        top = jnp.max(state[0])
        for c in state[2::2]:
            top = jnp.maximum(top, jnp.max(c))
        return top > SB_UNDERFLOW

    def cond(st):
        return jnp.logical_and(st[0] >= 0, st[1])

    def body(st):
        new = blocks([(st[0], None, None)], st[2])
        return st[0] - 1, live(new), tuple(new)

    _, _, state = lax.while_loop(cond, body, (i - 2, live(state), tuple(state)))
    accs = [jnp.concatenate([state[2 * (hh * nchunk + rc) + 1] for rc in range(nchunk)], axis=0)
            for hh in range(2)]
    o_ref[0] = jnp.where(hms[0], accs[0], accs[1]).astype(BF)


def _sb_attention(qkva3):
    B, S, _ = qkva3.shape
    tq = 256
    npair = SB_WIDTH // LANES
    return pl.pallas_call(
        functools.partial(_sb_body, tq=tq),
        grid=(B, npair, S // tq),
        in_specs=[
            pl.BlockSpec((1, tq, LANES), lambda b, p, i: (b, i, p)),
            pl.BlockSpec((1, S, LANES), lambda b, p, i: (b, 0, npair + p)),
            pl.BlockSpec((1, S, LANES), lambda b, p, i: (b, 0, 2 * npair + p)),
        ],
        out_specs=pl.BlockSpec((1, tq, LANES), lambda b, p, i: (b, i, p)),
        out_shape=jax.ShapeDtypeStruct((B, S, SB_WIDTH), BF),
        compiler_params=_cparams(("parallel", "parallel", "arbitrary")),
        name="sb_attention",
    )(qkva3, qkva3, qkva3)


def _cmp_body(ak_ref, av_ref, posk_ref, posv_ref, w1k_ref, w1kt_ref, w1kb_ref, w2k_ref,
              w1v_ref, w1vt_ref, w1vb_ref, w2v_ref, kc_ref, vc_ref):
    def one(a_ref, pos_ref, w1_ref, w1t_ref, w1b_ref, w2_ref, o_ref):
        a = a_ref[0]
        n = a.shape[0]
        bias = _dot(pos_ref[...], w1_ref[...])[0:1]
        out = jnp.zeros((n, LANES), F32)
        for g in range(NSA_KV_GROUPS):
            top = _dot(a, w1t_ref[g])
            bot = _dot(a, w1b_ref[g])
            h = top + pltpu.roll(bot, n - 1, 0) + bias
            out = out + _dot(jax.nn.gelu(h).astype(BF), w2_ref[g])
        o_ref[0] = out.astype(BF)

    one(ak_ref, posk_ref, w1k_ref, w1kt_ref, w1kb_ref, w2k_ref, kc_ref)
    one(av_ref, posv_ref, w1v_ref, w1vt_ref, w1vb_ref, w2v_ref, vc_ref)


def _compress(ak, av, posk, posv, wk, wv):
    B, n, width = ak.shape
    blk = pl.BlockSpec((1, n, width), lambda b: (b, 0, 0))
    c2 = lambda shp: pl.BlockSpec(shp, lambda b: (0, 0))
    c3 = lambda shp: pl.BlockSpec(shp, lambda b: (0, 0, 0))
    wspecs = [c2((CMP_LEN * HEAD_DIM, CMP_HIDDEN)), c3((2, width, CMP_HIDDEN)),
              c3((2, width, CMP_HIDDEN)), c3((2, CMP_HIDDEN, LANES))]
    out = pl.BlockSpec((1, n, LANES), lambda b: (b, 0, 0))
    return pl.pallas_call(
        _cmp_body,
        grid=(B,),
        in_specs=[blk, blk, c2((8, CMP_LEN * HEAD_DIM)), c2((8, CMP_LEN * HEAD_DIM))] + wspecs + wspecs,
        out_specs=[out, out],
        out_shape=[jax.ShapeDtypeStruct((B, n, LANES), BF)] * 2,
        compiler_params=_cparams(("parallel",)),
        name="nsa_compress",
    )(ak, av, posk, posv, *wk, *wv)


def _nsa_body(q_ref, kc_ref, vct_ref, ks_ref, vst_ref, kw_ref, vwt_ref, gt_ref, ovl_ref, et_ref,
              o_ref, *, tq, tk, nblk):
    i = pl.program_id(1)
    R = NSA_REP
    nchain = NSA_KV_GROUPS * R
    t0 = i * tq
    lane = lax.broadcasted_iota(jnp.int32, (1, LANES), 1)
    q = q_ref[0]
    t_row = t0 + lax.broadcasted_iota(jnp.int32, (1, tq), 1)
    tile = lambda a: jnp.concatenate([a] * nchain, axis=1)
    n_col = lax.broadcasted_iota(jnp.int32, (LANES, 1), 0)
    cmp_bias = tile(jnp.where((CMP_STRIDE * n_col + (CMP_LEN - 1)) <= t_row, 0.0, NEG_INF))
    has_cmp = tile(jnp.where(t_row >= CMP_LEN - 1, 1.0, 0.0))
    nkb = (t0 + tq + tk - 1) // tk
    key_last = (nkb - 1) * tk + lax.broadcasted_iota(jnp.int32, (tk, 1), 0)
    diag_bias = tile(jnp.where(key_last <= t_row, 0.0, NEG_INF))
    wkeys = WINDOW + tq
    woff = pl.multiple_of(jnp.maximum(i - WINDOW // tq, 0) * tq, tq)
    kp = woff + lax.broadcasted_iota(jnp.int32, (wkeys, 1), 0)
    win_bias = tile(jnp.where(kp <= t_row, jnp.where(kp > t_row - WINDOW, 0.0, NEG_INF), NEG_INF))
    blk = lax.broadcasted_iota(jnp.int32, (nblk, 1), 0)
    cur = t_row // SLC_LEN
    valid = blk <= cur
    forced = (blk == 0) | (valid & (blk > cur - SLC_LOCAL))
    ovl = ovl_ref[...]

    qrs = [jnp.where((lane // HEAD_DIM) == gi, q[:, LANES * r:LANES * (r + 1)], jnp.zeros((tq, LANES), BF))
           for gi in range(NSA_KV_GROUPS) for r in range(R)]
    q_all = jnp.concatenate(qrs, axis=0)

    s = _dot_nt(kc_ref[0], q_all) + cmp_bias
    e = jnp.exp(s - jnp.max(s, axis=0, keepdims=True))
    pc = e / jnp.sum(e, axis=0, keepdims=True) * has_cmp
    o_cmp = _dot(vct_ref[0], pc.astype(BF))

    q_aug = []
    for gi in range(NSA_KV_GROUPS):
        psum = jnp.zeros((LANES, tq), F32)
        for r in range(R):
            c = gi * R + r
            psum = psum + pc[:, c * tq:(c + 1) * tq]
        p1 = psum.astype(BF)
        r1 = psum - p1.astype(F32)
        p2 = r1.astype(BF)
        p3 = (r1 - p2.astype(F32)).astype(BF)
        imp = _dot(ovl, p1) + _dot(ovl, p2) + _dot(ovl, p3)
        imp = jnp.where(forced, imp + SLC_FORCE_BONUS, jnp.where(valid, imp, NEG_INF))
        sel = jnp.zeros((nblk, tq), F32)
        for _ in range(min(SLC_TOPN, nblk)):
            mx = jnp.max(imp, axis=0, keepdims=True)
            idx = jnp.min(jnp.where(imp == mx, blk, nblk), axis=0, keepdims=True)
            hit = blk == idx
            sel = jnp.where(hit, 1.0, sel)
            imp = jnp.where(hit, -jnp.inf, imp)
        sel_t = jnp.concatenate([sel, jnp.zeros((LANES - nblk, tq), F32)], axis=0).T
        sel_bias = ((sel_t - 1.0) * -NEG_INF).astype(BF)
        q_aug.extend(jnp.concatenate([qrs[gi * R + r], sel_bias], axis=1) for r in range(R))
    qa_all = jnp.concatenate(q_aug, axis=0)

    def sel_step(j, state, bias):
        m, l, acc = state
        off = pl.multiple_of(j * tk, tk)
        ka = jnp.concatenate([ks_ref[0, pl.ds(off, tk), :], et_ref[pl.ds(off, tk), :]], axis=1)
        s = _dot_nt(ka, qa_all)
        if bias is not None:
            s = s + bias
        m_new = jnp.maximum(m, jnp.max(s, axis=0, keepdims=True))
        alpha = jnp.exp(m - m_new)
        p = jnp.exp(s - m_new)
        l = alpha * l + jnp.sum(p, axis=0, keepdims=True)
        acc = alpha * acc + _dot(vst_ref[0, :, pl.ds(off, tk)], p.astype(BF))
        return m_new, l, acc

    width = nchain * tq
    init = (jnp.full((1, width), NEG_INF, F32), jnp.zeros((1, width), F32), jnp.zeros((LANES, width), F32))
    state = lax.fori_loop(0, nkb - 1, lambda j, st: sel_step(j, st, None), init)
    _, l_s, acc_s = sel_step(nkb - 1, state, diag_bias)
    o_sel = acc_s / l_s

    s = _dot_nt(kw_ref[0, pl.ds(woff, wkeys), :], q_all) + win_bias
    p = jnp.exp(s - jnp.max(s, axis=0, keepdims=True))
    o_win = _dot(vwt_ref[0, :, pl.ds(woff, wkeys)], p.astype(BF)) / jnp.sum(p, axis=0, keepdims=True)

    gt = gt_ref[0]
    sub = lax.broadcasted_iota(jnp.int32, (LANES, 1), 0)
    g0 = (sub // HEAD_DIM) == 0
    for r in range(R):
        out = jnp.zeros((LANES, tq), F32)
        for br, o_br in enumerate((o_cmp, o_sel, o_win)):
            rows = [(gi * R + r) * NSA_N_BRANCH + br for gi in range(NSA_KV_GROUPS)]
            gate = jnp.where(g0, gt[rows[0]:rows[0] + 1], gt[rows[1]:rows[1] + 1])
            both = jnp.where(g0, o_br[:, r * tq:(r + 1) * tq], o_br[:, (R + r) * tq:(R + r + 1) * tq])
            out = out + gate * both
        o_ref[0, r * LANES:(r + 1) * LANES, :] = out.astype(BF)


def _nsa_attention(qb3, kc, vct, kr3, vals_t, gt, ovl, expand):
    B, S, _ = qb3.shape
    tq, tk = 256, 512
    n = kc.shape[1]
    nblk = ovl.shape[0]
    ngate = gt.shape[1]
    assert S >= WINDOW + tq and S % tk == 0 and nblk % 8 == 0 and n == LANES
    tok = lambda c: pl.BlockSpec((1, S, LANES), lambda b, i, c=c: (b, 0, c))
    tr = lambda c: pl.BlockSpec((1, LANES, S), lambda b, i, c=c: (b, c, 0))
    return pl.pallas_call(
        functools.partial(_nsa_body, tq=tq, tk=tk, nblk=nblk),
        grid=(B, S // tq),
        in_specs=[
            pl.BlockSpec((1, tq, NSA_WIDTH), lambda b, i: (b, i, 0)),
            pl.BlockSpec((1, n, LANES), lambda b, i: (b, 0, 0)),
            pl.BlockSpec((1, LANES, n), lambda b, i: (b, 0, 0)),
            tok(1), tr(1), tok(2), tr(2),
            pl.BlockSpec((1, ngate, tq), lambda b, i: (b, 0, i)),
            pl.BlockSpec((nblk, LANES), lambda b, i: (0, 0)),
            pl.BlockSpec((S, LANES), lambda b, i: (0, 0)),
        ],
        out_specs=pl.BlockSpec((1, NSA_WIDTH, tq), lambda b, i: (b, 0, i)),
        out_shape=jax.ShapeDtypeStruct((B, NSA_WIDTH, S), BF),
        compiler_params=_cparams(("parallel", "arbitrary")),
        name="nsa_attention",
    )(qb3, kc, vct, kr3, vals_t, kr3, vals_t, gt, ovl, expand)


def _outproj_ln_body(oa_ref, obt_ref, x_ref, wa_ref, wb_ref, g_ref, b_ref, o_ref, p_ref):
    yb = lax.dot_general(obt_ref[0], wb_ref[...], (((0,), (0,)), ((), ())), preferred_element_type=F32)
    y = _dot(oa_ref[...], wa_ref[...]) + yb
    res = _layer_norm(DN_ALPHA * x_ref[...] + y, g_ref[...], b_ref[...])
    o_ref[...] = res
    _store_row_words(p_ref, 0, res.shape[0], res)


def _outproj_ln(oa, obt, x2, wa, wb, g, b):
    T = x2.shape[0]
    tm = 512
    nseq = obt.shape[2] // tm
    row = lambda i: (i, 0)
    full = lambda i: (0, 0)
    return pl.pallas_call(
        _outproj_ln_body,
        grid=(T // tm,),
        in_specs=[
            pl.BlockSpec((tm, oa.shape[1]), row),
            pl.BlockSpec((1, obt.shape[1], tm), lambda i: (i // nseq, 0, i % nseq)),
            pl.BlockSpec((tm, D_MODEL), row),
            pl.BlockSpec(wa.shape, full),
            pl.BlockSpec(wb.shape, full),
            pl.BlockSpec((1, D_MODEL), full),
            pl.BlockSpec((1, D_MODEL), full),
        ],
        out_specs=[pl.BlockSpec((tm, D_MODEL), row),
                   pl.BlockSpec((tm * NWORD, LANES), row)],
        out_shape=[jax.ShapeDtypeStruct((T, D_MODEL), F32),
                   jax.ShapeDtypeStruct((T * NWORD, LANES), jnp.uint32)],
        compiler_params=_cparams(("parallel",)),
        name="outproj_ln",
    )(oa, obt, x2, wa, wb, g, b)


def _rope_tables(seq):
    inv = jnp.power(ROPE_THETA, -jnp.arange(0, ROPE_DIM, 2, dtype=F32) / ROPE_DIM)
    ang = jnp.arange(seq, dtype=F32)[:, None] * inv[None, :]
    half = ROPE_DIM // 2
    rest = HEAD_DIM - ROPE_DIM
    cos_h = jnp.concatenate([jnp.cos(ang), jnp.cos(ang), jnp.ones((seq, rest), F32)], axis=1)
    sin_h = jnp.concatenate([jnp.sin(ang), jnp.sin(ang), jnp.zeros((seq, rest), F32)], axis=1)
    reps = N_ROPE // HEAD_DIM
    del half
    return jnp.tile(cos_h, (1, reps)), jnp.tile(sin_h, (1, reps))


def _nsa_head_perm():
    cols = []
    for r in range(NSA_REP):
        for g in range(NSA_KV_GROUPS):
            h = g * NSA_REP + r
            cols.extend(range(h * HEAD_DIM, (h + 1) * HEAD_DIM))
    return np.asarray(cols)


def _mixer0(x2, batch, seq, w_in, cmp_pos_k, cmp_pos_v, cmp_k_w1, cmp_k_w2, cmp_v_w1, cmp_v_w2,
            w_out, ln_g, ln_b):
    T = x2.shape[0]
    sizes = [SB_WIDTH] * 3 + [NSA_WIDTH] + [NSA_KV_GROUPS * HEAD_DIM] * 6 + [NSA_HEADS * NSA_N_BRANCH]
    offs = np.concatenate([[0], np.cumsum(sizes)])
    col = lambda j: w_in[:, offs[j]:offs[j + 1]]
    perm = _nsa_head_perm()
    ngate = sizes[-1]
    wp = jnp.concatenate([col(0) * ATT_SCALE, col(1), col(2), col(5), col(7), col(9),
                          jnp.pad(col(10), ((0, 0), (0, LANES - ngate)))], axis=1)
    wr = jnp.concatenate([col(3)[:, perm] * ATT_SCALE, col(4), col(6), col(8)], axis=1)
    cos_t, sin_t = _rope_tables(seq)
    qkva, vals, gates, qb, kr = _proj0(x2, wp.astype(BF), wr.astype(BF), cos_t, sin_t, seq)

    r3 = lambda a: a.reshape(batch, seq, a.shape[-1])
    o_a = _sb_attention(r3(qkva))

    ncmp = seq // CMP_STRIDE
    ak = kr[:, 0:LANES].reshape(batch, ncmp, CMP_STRIDE * LANES)
    av = vals[:, 0:LANES].reshape(batch, ncmp, CMP_STRIDE * LANES)

    def cmp_weights(w1, w2):
        w1r = w1.reshape(2, CMP_STRIDE, HEAD_DIM, CMP_HIDDEN)
        tops, bots, w2s = [], [], []
        for g in range(NSA_KV_GROUPS):
            ext = jnp.zeros((2, CMP_STRIDE, NSA_KV_GROUPS, HEAD_DIM, CMP_HIDDEN), F32).at[:, :, g].set(w1r)
            ext = ext.reshape(2, CMP_STRIDE * LANES, CMP_HIDDEN)
            tops.append(ext[0])
            bots.append(ext[1])
            w2s.append(jnp.zeros((CMP_HIDDEN, LANES), F32).at[:, g * HEAD_DIM:(g + 1) * HEAD_DIM].set(w2))
        return (w1.astype(BF), jnp.stack(tops).astype(BF), jnp.stack(bots).astype(BF),
                jnp.stack(w2s).astype(BF))

    posk = jnp.broadcast_to(cmp_pos_k.reshape(1, -1), (8, CMP_LEN * HEAD_DIM)).astype(BF)
    posv = jnp.broadcast_to(cmp_pos_v.reshape(1, -1), (8, CMP_LEN * HEAD_DIM)).astype(BF)
    kc, vc = _compress(ak, av, posk, posv, cmp_weights(cmp_k_w1, cmp_k_w2), cmp_weights(cmp_v_w1, cmp_v_w2))
    assert ncmp <= LANES
    if ncmp < LANES:
        kc = jnp.pad(kc, ((0, 0), (0, LANES - ncmp), (0, 0)))
        vc = jnp.pad(vc, ((0, 0), (0, LANES - ncmp), (0, 0)))

    n_slc = seq // SLC_LEN
    cmp_start = np.arange(ncmp) * CMP_STRIDE
    slc_start = np.arange(n_slc) * SLC_LEN
    ovl = ((cmp_start[None, :] <= slc_start[:, None] + SLC_LEN - 1)
           & (cmp_start[None, :] + CMP_LEN - 1 >= slc_start[:, None])).astype(np.float32)
    ovl = np.pad(ovl, ((0, 0), (0, LANES - ncmp))) if ncmp < LANES else ovl
    expand = (np.arange(seq)[:, None] // SLC_LEN == np.arange(LANES)[None, :]).astype(np.float32)
    vals_t = jnp.swapaxes(r3(vals), 1, 2)
    gt = jnp.swapaxes(r3(gates)[:, :, :32], 1, 2)
    o_bt = _nsa_attention(r3(qb), kc, jnp.swapaxes(vc, 1, 2), r3(kr), vals_t, gt,
                          jnp.asarray(ovl, BF), jnp.asarray(expand, BF))

    wa = w_out[:SB_WIDTH].astype(BF)
    wb = w_out[SB_WIDTH:][perm].astype(BF)
    return _outproj_ln(o_a.reshape(T, SB_WIDTH), o_bt, x2, wa, wb,
                       ln_g.reshape(1, -1), ln_b.reshape(1, -1))


MOE_TILE = 2048
ROW_ALIGN = 8
RANK_CHUNK = 256


def _top_rows(vals, ids, n_ids, count):
    hits = []
    for _ in range(count):
        mx = jnp.max(vals, axis=0, keepdims=True)
        idx = jnp.min(jnp.where(vals == mx, ids, n_ids), axis=0, keepdims=True)
        hit = ids == idx
        hits.append(hit)
        vals = jnp.where(hit, -jnp.inf, vals)
    return hits


def _router_body(h_ref, rt_ref, rb_ref, pos_ref, gate_ref, meta_ref, *, tm):
    E, NG = N_EXPERTS, N_EXPERT_GROUPS
    per = E // NG
    hh, hl = _split2(h_ref[...])
    rh, rl = _split2(rt_ref[...])
    logits = _dot_nt(rh, hh) + _dot_nt(rh, hl) + _dot_nt(rl, hh)
    scores = jax.nn.sigmoid(logits)
    biased = scores + rb_ref[...]
    i8 = lax.broadcasted_iota(jnp.int32, (per, tm), 0)
    gs = []
    for g in range(NG):
        v = biased[g * per:(g + 1) * per]
        m1 = jnp.max(v, axis=0, keepdims=True)
        a1 = jnp.min(jnp.where(v == m1, i8, per), axis=0, keepdims=True)
        m2 = jnp.max(jnp.where(i8 == a1, -jnp.inf, v), axis=0, keepdims=True)
        gs.append(m1 + m2)
    gs = jnp.concatenate(gs, axis=0)
    gi = lax.broadcasted_iota(jnp.int32, (NG, tm), 0)
    ghits = _top_rows(gs, gi, NG, TOPK_GROUPS)
    gkeep = jnp.zeros((NG, tm), F32)
    for hit in ghits:
        gkeep = jnp.where(hit, 1.0, gkeep)
    ekeep = jnp.concatenate([jnp.broadcast_to(gkeep[g:g + 1], (per, tm)) for g in range(NG)], axis=0)
    ei = lax.broadcasted_iota(jnp.int32, (E, tm), 0)
    hits = _top_rows(jnp.where(ekeep > 0.5, biased, -jnp.inf), ei, E, TOP_K)
    gates = [jnp.sum(jnp.where(hit, scores, 0.0), axis=0, keepdims=True) for hit in hits]
    gsum = gates[0]
    for gk in gates[1:]:
        gsum = gsum + gk
    gates = [gk / gsum * ROUTED_SCALE for gk in gates]

    member = jnp.zeros((E, tm), F32)
    for hit in hits:
        member = jnp.where(hit, 1.0, member)
    cnt_col = jnp.sum(member, axis=1, keepdims=True)
    pad_col = jnp.floor((cnt_col + (ROW_ALIGN - 1)) * (1.0 / ROW_ALIGN)) * ROW_ALIGN
    sub_e = lax.broadcasted_iota(jnp.int32, (E, LANES), 0)
    lane_e = lax.broadcasted_iota(jnp.int32, (E, LANES), 1)
    cnt_row = jnp.sum(jnp.where(sub_e == lane_e, cnt_col, 0.0), axis=0, keepdims=True)
    pad_row = jnp.sum(jnp.where(sub_e == lane_e, pad_col, 0.0), axis=0, keepdims=True)
    off_row = jnp.sum(jnp.where(sub_e < lane_e, pad_col, 0.0), axis=0, keepdims=True)
    off_col = jnp.sum(jnp.where(lane_e < sub_e, pad_row, 0.0), axis=1, keepdims=True)
    r_i = lax.broadcasted_iota(jnp.int32, (RANK_CHUNK, RANK_CHUNK), 0)
    c_i = lax.broadcasted_iota(jnp.int32, (RANK_CHUNK, RANK_CHUNK), 1)
    before = jnp.where(r_i < c_i, 1.0, 0.0).astype(BF)
    running = off_col
    ranks = []
    for c in range(tm // RANK_CHUNK):
        mc = member[:, c * RANK_CHUNK:(c + 1) * RANK_CHUNK]
        ranks.append(_dot(mc.astype(BF), before) + running)
        running = running + jnp.sum(mc, axis=1, keepdims=True)
    slot = jnp.concatenate(ranks, axis=1)
    pos = [jnp.sum(jnp.where(hit, slot, 0.0), axis=0, keepdims=True) for hit in hits]
    zrow = jnp.zeros((1, tm), F32)
    pos_ref[...] = jnp.concatenate(pos + [zrow, zrow], axis=0).astype(jnp.int32)
    gate_ref[...] = jnp.concatenate(gates + [zrow, zrow], axis=0)
    z128 = jnp.zeros((1, LANES), F32)
    meta_ref[0] = jnp.concatenate([off_row, cnt_row] + [z128] * 6, axis=0).astype(jnp.int32)


def _router(h2, router_t, bias_col):
    T = h2.shape[0]
    tm = MOE_TILE
    nt = T // tm
    return pl.pallas_call(
        functools.partial(_router_body, tm=tm),
        grid=(nt,),
        in_specs=[
            pl.BlockSpec((tm, D_MODEL), lambda i: (i, 0)),
            pl.BlockSpec((N_EXPERTS, D_MODEL), lambda i: (0, 0)),
            pl.BlockSpec((N_EXPERTS, 1), lambda i: (0, 0)),
        ],
        out_specs=[
            pl.BlockSpec((8, tm), lambda i: (0, i)),
            pl.BlockSpec((8, tm), lambda i: (0, i)),
            pl.BlockSpec((1, 8, LANES), lambda i: (i, 0, 0)),
        ],
        out_shape=[
            jax.ShapeDtypeStruct((8, T), jnp.int32),
            jax.ShapeDtypeStruct((8, T), F32),
            jax.ShapeDtypeStruct((nt, 8, LANES), jnp.int32),
        ],
        compiler_params=_cparams(("parallel",)),
        name="moe_router",
    )(h2, router_t, bias_col)


EXPERTS_PER_STEP = 4
EXPERT_CHUNK = 256
COMBINE_SUB = 256
POS_STRIDE = 8


def _swiglu(xb, wgu, wd, hidden):
    gu = _dot(xb, wgu)
    a = jax.nn.silu(gu[:, :hidden]) * gu[:, hidden:]
    return _dot(a.astype(BF), wd)


def _moe_body(off_ref, cnt_ref, pos_ref, src_ref, x_ref, gcol_ref, wgu_ref, wd_ref, wsgu_ref, wsd_ref,
              g_ref, b_ref, o_ref, xs_ref, z_ref, *, tm, eb, ch, sub, unroll):
    i = pl.program_id(0)
    j = pl.program_id(1)
    nj = N_EXPERTS // eb

    @pl.when(j == 0)
    def _dispatch():
        zeros = jnp.zeros((ROW_ALIGN * NWORD, LANES), jnp.uint32)

        def pad(e, carry):
            off, cnt = off_ref[i * N_EXPERTS + e], cnt_ref[i * N_EXPERTS + e]
            last = pl.multiple_of((off + cnt // ROW_ALIGN * ROW_ALIGN) * NWORD, ROW_ALIGN * NWORD)
            xs_ref[pl.ds(last, ROW_ALIGN * NWORD), :] = zeros
            return carry

        lax.fori_loop(0, N_EXPERTS, pad, 0)
        end = off_ref[i * N_EXPERTS + N_EXPERTS - 1] + cnt_ref[i * N_EXPERTS + N_EXPERTS - 1]
        end = pl.multiple_of((end + ROW_ALIGN - 1) // ROW_ALIGN * ROW_ALIGN * NWORD, ROW_ALIGN * NWORD)
        xs_ref[pl.ds(end, ch * NWORD), :] = jnp.zeros((ch * NWORD, LANES), jnp.uint32)

        def tok(tb, carry):
            for u in range(unroll):
                t = tb * unroll + u
                slab = src_ref[pl.ds(pl.multiple_of(t * NWORD, NWORD), NWORD), :]
                for k in range(TOP_K):
                    p = pl.multiple_of(pos_ref[t * POS_STRIDE + k], NWORD)
                    xs_ref[pl.ds(p, NWORD), :] = slab
            return carry

        lax.fori_loop(0, tm // unroll, tok, 0)

    def chunks_in(els, offs, c):
        r0s = [pl.multiple_of(off + c * ch, ROW_ALIGN) for off in offs]
        words = [_load_row_words(xs_ref, r0, ch) for r0 in r0s]
        xbs = [jnp.concatenate([_unpack_lo(w).astype(BF) for w in ws]
                               + [_unpack_hi(w).astype(BF) for w in ws], axis=1) for ws in words]
        gus = [_dot(xb, wgu_ref[el]) for xb, el in zip(xbs, els)]
        acts = [(jax.nn.silu(gu[:, :EXPERT_HIDDEN]) * gu[:, EXPERT_HIDDEN:]).astype(BF) for gu in gus]
        ys = [_dot(a, wd_ref[el]) for a, el in zip(acts, els)]
        return list(zip(r0s, words, ys))

    def chunk_out(r0, words, y, c, cnt):
        keep = (c * ch + lax.broadcasted_iota(jnp.int32, (ch, 1), 0)) < cnt
        for cc, packed in enumerate(_pack_row_words(y)):
            xs_ref[pl.ds(r0 * NWORD + cc, ch, stride=NWORD), :] = jnp.where(keep, packed, words[cc])

    @pl.when(j < nj)
    def _experts():
        offs = [off_ref[i * N_EXPERTS + j * eb + el] for el in range(eb)]
        cnts = [cnt_ref[i * N_EXPERTS + j * eb + el] for el in range(eb)]
        firsts = chunks_in(list(range(eb)), offs, 0)
        for el in range(eb):
            chunk_out(*firsts[el], 0, cnts[el])
        for el in range(eb):
            def chunk(c, carry, el=el):
                chunk_out(*chunks_in([el], [offs[el]], c)[0], c, cnts[el])
                return carry

            lax.fori_loop(1, (cnts[el] + ch - 1) // ch, chunk, 0)

    @pl.when(j >= nj)
    def _combine():
        base = (j - nj) * sub

        def tok(tb, carry):
            for u in range(unroll):
                tl = tb * unroll + u
                dst = pl.multiple_of(tl * NWORD, NWORD)
                for k in range(TOP_K):
                    p = pl.multiple_of(pos_ref[(base + tl) * POS_STRIDE + k], NWORD)
                    z_ref[k, pl.ds(dst, NWORD), :] = xs_ref[pl.ds(p, NWORD), :]
            return carry

        lax.fori_loop(0, sub // unroll, tok, 0)
        gcol = gcol_ref[...]
        lo = [jnp.zeros((sub, LANES), F32) for _ in range(NWORD)]
        hi = [jnp.zeros((sub, LANES), F32) for _ in range(NWORD)]
        for k in range(TOP_K):
            gk = gcol[:, k:k + 1]
            for c, w in enumerate(_load_row_words(z_ref.at[k], 0, sub)):
                lo[c] = lo[c] + gk * _unpack_lo(w)
                hi[c] = hi[c] + gk * _unpack_hi(w)
        routed = jnp.concatenate(lo + hi, axis=1)
        x = x_ref[...]
        shared = _swiglu(x.astype(BF), wsgu_ref[...], wsd_ref[...], SHARED_HIDDEN)
        o_ref[...] = _layer_norm(DN_ALPHA * x + routed + shared, g_ref[...], b_ref[...])


def _moe_experts(h2, packed, off, cnt, pos, gcol, wgu, wd, wsgu, wsd, g, b):
    T = h2.shape[0]
    tm, eb, ch, sub = MOE_TILE, EXPERTS_PER_STEP, EXPERT_CHUNK, COMBINE_SUB
    nt, nj, nsub = T // tm, N_EXPERTS // eb, tm // sub
    rows = TOP_K * tm + N_EXPERTS * ROW_ALIGN + ch
    hidden2 = wgu.shape[-1]
    wblk = lambda i, j, *_: (jnp.minimum(j, nj - 1), 0, 0)
    sub_i = lambda i, j: i * nsub + jnp.clip(j - nj, 0, nsub - 1)
    once = pl.Buffered(1)
    return pl.pallas_call(
        functools.partial(_moe_body, tm=tm, eb=eb, ch=ch, sub=sub, unroll=4),
        grid_spec=pltpu.PrefetchScalarGridSpec(
            num_scalar_prefetch=2,
            grid=(nt, nj + nsub),
            in_specs=[
                pl.BlockSpec((tm * POS_STRIDE,), lambda i, j, *_: (i,), memory_space=pltpu.SMEM),
                pl.BlockSpec((tm * NWORD, LANES), lambda i, j, *_: (i, 0), pipeline_mode=once),
                pl.BlockSpec((sub, D_MODEL), lambda i, j, *_: (sub_i(i, j), 0)),
                pl.BlockSpec((sub, 8), lambda i, j, *_: (sub_i(i, j), 0)),
                pl.BlockSpec((eb, D_MODEL, hidden2), wblk),
                pl.BlockSpec((eb, hidden2 // 2, D_MODEL), wblk),
                pl.BlockSpec(wsgu.shape, lambda i, j, *_: (0, 0), pipeline_mode=once),
                pl.BlockSpec(wsd.shape, lambda i, j, *_: (0, 0), pipeline_mode=once),
                pl.BlockSpec((1, D_MODEL), lambda i, j, *_: (0, 0)),
                pl.BlockSpec((1, D_MODEL), lambda i, j, *_: (0, 0)),
            ],
            out_specs=pl.BlockSpec((sub, D_MODEL), lambda i, j, *_: (sub_i(i, j), 0)),
            scratch_shapes=[
                pltpu.VMEM((rows * NWORD, LANES), jnp.uint32),
                pltpu.VMEM((TOP_K, sub * NWORD, LANES), jnp.uint32),
            ],
        ),
        out_shape=jax.ShapeDtypeStruct((T, D_MODEL), F32),
        compiler_params=_cparams(("parallel", "arbitrary"), MOE_VMEM_LIMIT),
        name="moe_experts",
    )(off, cnt, pos, packed, h2, gcol, wgu, wd, wsgu, wsd, g, b)


S5_BATCH = 8
S5_STEPS = 64
S5_SLABS = 4
SLAB_CH = D_MODEL // S5_SLABS
SLAB_ST = SSM_GROUPS * SSM_STATE // S5_SLABS
N_STATE = SSM_GROUPS * SSM_STATE
SCAN_LANES = 512
SCAN_UNROLL = 8


def _s5_disc_body(lre_ref, lim_ref, ldt_ref, bre_ref, bim_ref, are_ref, aim_ref, bbre_ref, bbim_ref):
    lre, lim = lre_ref[...], lim_ref[...]
    step = jnp.exp(ldt_ref[...])
    mag = jnp.exp(lre * step)
    a_re = mag * jnp.cos(lim * step)
    a_im = mag * jnp.sin(lim * step)
    den = lre * lre + lim * lim
    zoh_re = ((a_re - 1.0) * lre + a_im * lim) / den
    zoh_im = (a_im * lre - (a_re - 1.0) * lim) / den
    are_ref[...] = a_re
    aim_ref[...] = a_im
    bbre_ref[...] = zoh_re * bre_ref[...] - zoh_im * bim_ref[...]
    bbim_ref[...] = zoh_re * bim_ref[...] + zoh_im * bre_ref[...]


def _s5_discretize(lambda_re, lambda_im, log_dt, b_re, b_im):
    col = lambda a: a.reshape(N_STATE, 1)
    ldt = jnp.broadcast_to(log_dt[:, None], (SSM_GROUPS, SSM_STATE))
    mat = lambda a: a.reshape(N_STATE, SSM_GROUP)
    c1 = jax.ShapeDtypeStruct((N_STATE, 1), F32)
    c16 = jax.ShapeDtypeStruct((N_STATE, SSM_GROUP), F32)
    return pl.pallas_call(_s5_disc_body, out_shape=[c1, c1, c16, c16], name="s5_discretize")(
        col(lambda_re), col(lambda_im), col(ldt), mat(b_re), mat(b_im))


def _s5_body(x_ref, win_ref, are_ref, aim_ref, bd_ref, cd_ref, dsk_ref, wglu_ref, wout_ref, g_ref, b_ref,
             o_ref, p_ref, xs_ref, pk_ref, hre_ref, him_ref, sre_ref, sim_ref, *, lt):
    nb = S5_BATCH
    nlb = D_MODEL // LANES
    for c in range(nlb):
        for b in range(nb):
            xs_ref[c, pl.ds(b, lt, stride=nb), :] = x_ref[b, :, c * LANES:(c + 1) * LANES]
    x = jnp.concatenate([xs_ref[c] for c in range(nlb)], axis=1)
    u = _dot(x.astype(BF), win_ref[...])
    ub = u.astype(BF)
    for k in range(S5_SLABS):
        bu = _dot(ub[:, k * SLAB_CH:(k + 1) * SLAB_CH], bd_ref[k])
        hre_ref[:, k * SLAB_ST:(k + 1) * SLAB_ST] = bu[:, :SLAB_ST]
        him_ref[:, k * SLAB_ST:(k + 1) * SLAB_ST] = bu[:, SLAB_ST:]

    @pl.when(pl.program_id(1) == 0)
    def _():
        sre_ref[...] = jnp.zeros_like(sre_ref)
        sim_ref[...] = jnp.zeros_like(sim_ref)

    for c in range(N_STATE // SCAN_LANES):
        ls = slice(c * SCAN_LANES, (c + 1) * SCAN_LANES)
        a_re = jnp.broadcast_to(are_ref[:, ls], (nb, SCAN_LANES))
        a_im = jnp.broadcast_to(aim_ref[:, ls], (nb, SCAN_LANES))

        def steps(tb, state, ls=ls, a_re=a_re, a_im=a_im):
            s_re, s_im = state
            for uu in range(SCAN_UNROLL):
                r0 = pl.multiple_of((tb * SCAN_UNROLL + uu) * nb, nb)
                n_re = a_re * s_re - a_im * s_im + hre_ref[pl.ds(r0, nb), ls]
                n_im = a_re * s_im + a_im * s_re + him_ref[pl.ds(r0, nb), ls]
                hre_ref[pl.ds(r0, nb), ls] = n_re
                him_ref[pl.ds(r0, nb), ls] = n_im
                s_re, s_im = n_re, n_im
            return s_re, s_im

        s_re, s_im = lax.fori_loop(0, lt // SCAN_UNROLL, steps, (sre_ref[:, ls], sim_ref[:, ls]))
        sre_ref[:, ls] = s_re
        sim_ref[:, ls] = s_im

    ys = []
    for k in range(S5_SLABS):
        hk = jnp.concatenate([hre_ref[:, k * SLAB_ST:(k + 1) * SLAB_ST].astype(BF),
                              him_ref[:, k * SLAB_ST:(k + 1) * SLAB_ST].astype(BF)], axis=1)
        ys.append(_dot(hk, cd_ref[k]))
    y = jax.nn.gelu(jnp.concatenate(ys, axis=1) + dsk_ref[...] * u)
    y = y * jax.nn.sigmoid(_dot(y.astype(BF), wglu_ref[...]))
    mixed = _dot(y.astype(BF), wout_ref[...])
    res = _layer_norm(DN_ALPHA * x + mixed, g_ref[...], b_ref[...])
    for c in range(nlb):
        xs_ref[c] = res[:, c * LANES:(c + 1) * LANES]
    for c, w in enumerate(_pack_row_words(res)):
        pk_ref[c] = w
    for c in range(nlb):
        for b in range(nb):
            o_ref[b, :, c * LANES:(c + 1) * LANES] = xs_ref[c, pl.ds(b, lt, stride=nb), :]
    for c in range(NWORD):
        for b in range(nb):
            p_ref[b, pl.ds(c, lt, stride=NWORD), :] = pk_ref[c, pl.ds(b, lt, stride=nb), :]


def _mixer1(x3, w_in, lambda_re, lambda_im, b_re, b_im, c_re, c_im, d_skip, log_dt, w_glu, w_out,
            ln_g, ln_b):
    B, S, _ = x3.shape
    lt = S5_STEPS
    rows = S5_BATCH * lt
    a_re, a_im, bb_re, bb_im = _s5_discretize(lambda_re, lambda_im, log_dt, b_re, b_im)
    gps = SSM_GROUPS // S5_SLABS
    eye = jnp.eye(gps, dtype=F32)

    def bdiag(bb):
        b4 = bb.reshape(S5_SLABS, gps, SSM_STATE, SSM_GROUP)
        return jnp.einsum('kgph,gf->kghfp', b4, eye).reshape(S5_SLABS, SLAB_CH, SLAB_ST)

    def cdiag(cc):
        c4 = cc.reshape(S5_SLABS, gps, SSM_GROUP, SSM_STATE)
        return jnp.einsum('kghp,gf->kfpgh', c4, eye).reshape(S5_SLABS, SLAB_ST, SLAB_CH)

    bd = jnp.concatenate([bdiag(bb_re), bdiag(bb_im)], axis=2).astype(BF)
    cd = jnp.concatenate([cdiag(c_re), -cdiag(c_im)], axis=1).astype(BF)
    c2 = lambda shp: pl.BlockSpec(shp, lambda bi, ti: (0,) * len(shp))
    return pl.pallas_call(
        functools.partial(_s5_body, lt=lt),
        grid=(B // S5_BATCH, S // lt),
        in_specs=[
            pl.BlockSpec((S5_BATCH, lt, D_MODEL), lambda bi, ti: (bi, ti, 0)),
            c2((D_MODEL, D_MODEL)), c2((1, N_STATE)), c2((1, N_STATE)),
            c2(bd.shape), c2(cd.shape), c2((1, D_MODEL)),
            c2((D_MODEL, D_MODEL)), c2((D_MODEL, D_MODEL)), c2((1, D_MODEL)), c2((1, D_MODEL)),
        ],
        out_specs=[pl.BlockSpec((S5_BATCH, lt, D_MODEL), lambda bi, ti: (bi, ti, 0)),
                   pl.BlockSpec((S5_BATCH, lt * NWORD, LANES), lambda bi, ti: (bi, ti, 0))],
        out_shape=[jax.ShapeDtypeStruct((B, S, D_MODEL), F32),
                   jax.ShapeDtypeStruct((B, S * NWORD, LANES), jnp.uint32)],
        scratch_shapes=[
            pltpu.VMEM((D_MODEL // LANES, rows, LANES), F32),
            pltpu.VMEM((NWORD, rows, LANES), jnp.uint32),
            pltpu.VMEM((rows, N_STATE), F32),
            pltpu.VMEM((rows, N_STATE), F32),
            pltpu.VMEM((S5_BATCH, N_STATE), F32),
            pltpu.VMEM((S5_BATCH, N_STATE), F32),
        ],
        compiler_params=_cparams(("parallel", "arbitrary")),
        name="s5_mixer",
    )(x3, w_in.astype(BF), a_re.reshape(1, N_STATE), a_im.reshape(1, N_STATE), bd, cd,
      d_skip.reshape(1, D_MODEL), w_glu.astype(BF), w_out.astype(BF), ln_g.reshape(1, -1), ln_b.reshape(1, -1))


def _moe_block(h2, packed, router, router_bias, w_gate, w_up, w_down, sh_gate, sh_up, sh_down, ln_g, ln_b):
    pos, gate, meta = _router(h2, router.T, router_bias.reshape(-1, 1))
    off = meta[:, 0, :N_EXPERTS].reshape(-1)
    cnt = meta[:, 1, :N_EXPERTS].reshape(-1)
    wgu = jnp.concatenate([w_gate, w_up], axis=-1).astype(BF)
    wsgu = jnp.concatenate([sh_gate, sh_up], axis=-1).astype(BF)
    pos = (pos.T * NWORD).reshape(-1)
    return _moe_experts(h2, packed, off, cnt, pos, gate.T, wgu, w_down.astype(BF), wsgu, sh_down.astype(BF),
                        ln_g.reshape(1, -1), ln_b.reshape(1, -1))


def kernel(x, l0_w_in, l0_cmp_pos_k, l0_cmp_pos_v, l0_cmp_k_w1, l0_cmp_k_w2, l0_cmp_v_w1, l0_cmp_v_w2, l0_w_out, l0_ln1_g, l0_ln1_b, l0_router, l0_router_bias, l0_w_gate, l0_w_up, l0_w_down, l0_sh_gate, l0_sh_up, l0_sh_down, l0_ln2_g, l0_ln2_b, l1_w_in, l1_lambda_re, l1_lambda_im, l1_b_re, l1_b_im, l1_c_re, l1_c_im, l1_d, l1_log_dt, l1_w_glu, l1_w_out, l1_ln1_g, l1_ln1_b, l1_router, l1_router_bias, l1_w_gate, l1_w_up, l1_w_down, l1_sh_gate, l1_sh_up, l1_sh_down, l1_ln2_g, l1_ln2_b):
    B, S, D = x.shape
    assert D == D_MODEL and S % 512 == 0 and B % S5_BATCH == 0 and (B * S) % MOE_TILE == 0
    T = B * S
    h, hp = _mixer0(x.reshape(T, D), B, S, l0_w_in, l0_cmp_pos_k, l0_cmp_pos_v, l0_cmp_k_w1, l0_cmp_k_w2,
                    l0_cmp_v_w1, l0_cmp_v_w2, l0_w_out, l0_ln1_g, l0_ln1_b)
    h = _moe_block(h, hp, l0_router, l0_router_bias, l0_w_gate, l0_w_up, l0_w_down, l0_sh_gate, l0_sh_up,
                   l0_sh_down, l0_ln2_g, l0_ln2_b)
    h, hp = _mixer1(h.reshape(B, S, D), l1_w_in, l1_lambda_re, l1_lambda_im, l1_b_re, l1_b_im, l1_c_re,
                    l1_c_im, l1_d, l1_log_dt, l1_w_glu, l1_w_out, l1_ln1_g, l1_ln1_b)
    h = _moe_block(h.reshape(T, D), hp.reshape(T * NWORD, LANES), l1_router, l1_router_bias, l1_w_gate, l1_w_up,
                   l1_w_down, l1_sh_gate, l1_sh_up, l1_sh_down, l1_ln2_g, l1_ln2_b)
    return h.reshape(B, S, D)
```

```python
import functools
import math

import numpy as np
import jax
import jax.numpy as jnp
from jax import lax
from jax.experimental import pallas as pl
from jax.experimental.pallas import tpu as pltpu

F32 = jnp.float32
BF = jnp.bfloat16

D_MODEL = 1024
DEPTH = 2
HEAD_DIM = 64
LANES = 128
SB_HEADS = 8
SB_WIDTH = SB_HEADS * HEAD_DIM
NSA_HEADS = 8
NSA_KV_GROUPS = 2
NSA_REP = NSA_HEADS // NSA_KV_GROUPS
NSA_WIDTH = NSA_HEADS * HEAD_DIM
NSA_N_BRANCH = 3
CMP_LEN = 32
CMP_STRIDE = 16
CMP_HIDDEN = 256
SLC_LEN = 64
SLC_TOPN = 8
SLC_LOCAL = 2
SLC_FORCE_BONUS = 1e4
WINDOW = 512
ROPE_THETA = 500000.0
ROPE_DIM = HEAD_DIM // 4
SSM_GROUP = 16
SSM_GROUPS = D_MODEL // SSM_GROUP
SSM_STATE = 64
N_EXPERTS = 64
N_EXPERT_GROUPS = 8
TOPK_GROUPS = 4
TOP_K = 6
EXPERT_HIDDEN = 256
SHARED_HIDDEN = 256
ROUTED_SCALE = 2.5
DN_ALPHA = (2 * DEPTH) ** 0.25
LN_EPS = 1e-5
NEG_INF = -1e30
ATT_SCALE = HEAD_DIM ** -0.5
LOG2_E = math.log2(math.e)
SB_UNDERFLOW = -104.0
WIN_Q = 128
SB_ROWS = 128

V7X_VMEM_BYTES = 64 * 1024 * 1024
VMEM_LIMIT = V7X_VMEM_BYTES - 8 * 1024 * 1024
MOE_VMEM_LIMIT = V7X_VMEM_BYTES - 4 * 1024 * 1024


def _cparams(sem, vmem=VMEM_LIMIT):
    return pltpu.CompilerParams(dimension_semantics=sem, vmem_limit_bytes=vmem)


def _dot(a, b):
    return jnp.dot(a, b, preferred_element_type=F32)


def _dot_nt(a, b):
    return lax.dot_general(a, b, (((1,), (1,)), ((), ())), preferred_element_type=F32)


def _split2(x):
    hi = x.astype(BF)
    lo = (x - hi.astype(F32)).astype(BF)
    return hi, lo


def _layer_norm(h, g, b):
    mu = jnp.mean(h, axis=-1, keepdims=True)
    d = h - mu
    var = jnp.mean(d * d, axis=-1, keepdims=True)
    return d * lax.rsqrt(var + LN_EPS) * g + b


HALF = D_MODEL // 2
NWORD = HALF // LANES
HI_MASK = 0xFFFF0000


def _pack_pairs(a, b):
    lo = pltpu.bitcast(a.astype(BF).astype(F32), jnp.uint32)
    hi = pltpu.bitcast(b.astype(BF).astype(F32), jnp.uint32)
    return (lo >> 16) | (hi & jnp.uint32(HI_MASK))


def _pack_row_words(h):
    return [_pack_pairs(h[:, c * LANES:(c + 1) * LANES], h[:, HALF + c * LANES:HALF + (c + 1) * LANES])
            for c in range(NWORD)]


def _store_row_words(ref, start, n, h):
    for c, w in enumerate(_pack_row_words(h)):
        ref[pl.ds(start * NWORD + c, n, stride=NWORD), :] = w


def _load_row_words(ref, start, n):
    return [ref[pl.ds(start * NWORD + c, n, stride=NWORD), :] for c in range(NWORD)]


def _unpack_lo(w):
    return pltpu.bitcast(w << 16, F32)


def _unpack_hi(w):
    return pltpu.bitcast(w & jnp.uint32(HI_MASK), F32)


N_PLAIN = 3 * SB_WIDTH + 3 * LANES + LANES
N_ROPE = NSA_WIDTH + 3 * LANES


def _proj0_body(x_ref, wp_ref, wr_ref, cos_ref, sin_ref,
                qkva_ref, vals_ref, gate_ref, qb_ref, kr_ref):
    xb = x_ref[...].astype(BF)
    a0, a1, a2 = 3 * SB_WIDTH, 3 * SB_WIDTH + 3 * LANES, N_PLAIN
    qkva_ref[...] = _dot(xb, wp_ref[:, 0:a0]).astype(BF)
    vals_ref[...] = _dot(xb, wp_ref[:, a0:a1]).astype(BF)
    gate_ref[...] = jax.nn.sigmoid(_dot(xb, wp_ref[:, a1:a2]))
    y = _dot(xb, wr_ref[...])
    half = ROPE_DIM // 2
    pos = lax.broadcasted_iota(jnp.int32, (1, N_ROPE), 1) % HEAD_DIM
    partner = jnp.where(pos < half, -pltpu.roll(y, N_ROPE - half, 1), pltpu.roll(y, half, 1))
    y = y * cos_ref[...] + partner * sin_ref[...]
    qb_ref[...] = y[:, 0:NSA_WIDTH].astype(BF)
    kr_ref[...] = y[:, NSA_WIDTH:N_ROPE].astype(BF)


def _proj0(x2, wp, wr, cos_t, sin_t, seq):
    T = x2.shape[0]
    tm = 512
    nseq = seq // tm
    row = lambda i: (i, 0)
    full = lambda i: (0, 0)
    tab = lambda i: (i % nseq, 0)
    return pl.pallas_call(
        _proj0_body,
        grid=(T // tm,),
        in_specs=[
            pl.BlockSpec((tm, D_MODEL), row),
            pl.BlockSpec((D_MODEL, N_PLAIN), full),
            pl.BlockSpec((D_MODEL, N_ROPE), full),
            pl.BlockSpec((tm, N_ROPE), tab),
            pl.BlockSpec((tm, N_ROPE), tab),
        ],
        out_specs=[
            pl.BlockSpec((tm, 3 * SB_WIDTH), row),
            pl.BlockSpec((tm, 3 * LANES), row),
            pl.BlockSpec((tm, LANES), row),
            pl.BlockSpec((tm, NSA_WIDTH), row),
            pl.BlockSpec((tm, 3 * LANES), row),
        ],
        out_shape=[
            jax.ShapeDtypeStruct((T, 3 * SB_WIDTH), BF),
            jax.ShapeDtypeStruct((T, 3 * LANES), BF),
            jax.ShapeDtypeStruct((T, LANES), F32),
            jax.ShapeDtypeStruct((T, NSA_WIDTH), BF),
            jax.ShapeDtypeStruct((T, 3 * LANES), BF),
        ],
        compiler_params=_cparams(("parallel",)),
        name="proj0",
    )(x2, wp, wr, cos_t, sin_t)


def _sb_body(q_ref, k_ref, v_ref, o_ref, *, tq):
    i = pl.program_id(2)
    q = q_ref[0]
    lane = lax.broadcasted_iota(jnp.int32, (1, LANES), 1)
    row = lax.broadcasted_iota(jnp.int32, (tq, tq), 0)
    col = lax.broadcasted_iota(jnp.int32, (tq, tq), 1)
    tri = jnp.where(row > col, 1.0, 0.0).astype(BF)
    diag_causal = col < row

    hms = [(lane // HEAD_DIM) == hh for hh in range(2)]
    nchunk = tq // SB_ROWS
    chains = [(hh, rc) for hh in range(2) for rc in range(nchunk)]
    qcs = [jnp.where(hms[hh], q[rc * SB_ROWS:(rc + 1) * SB_ROWS], jnp.zeros((SB_ROWS, LANES), BF))
           for hh, rc in chains]

    def blocks(specs, state):
        nc = len(qcs)
        kvs = []
        for j, _, _ in specs:
            off = pl.multiple_of(j * tq, tq)
            kvs.append((k_ref[0, pl.ds(off, tq), :], v_ref[0, pl.ds(off, tq), :]))
        zs = [[_dot_nt(qc, k) for qc in qcs] for k, _ in kvs]
        mids = []
        for zrow, (_, masks, flag) in zip(zs, specs):
            row = []
            for n, z in enumerate(zrow):
                nl = jnp.maximum(z, 0.0) + jnp.log(1.0 + jnp.exp(-jnp.abs(z)))
                if masks is not None:
                    nl = jnp.where(masks[n], nl, 0.0)
                if flag is not None:
                    nl = jnp.where(flag, nl, 0.0)
                row.append((nl, z, nl.astype(BF)))
            mids.append(row)
        sufs = [[_dot(hi, tri) for _, _, hi in row] for row in mids]
        carries = [state[2 * n] for n in range(nc)]
        ws = []
        for row, srow, (_, masks, flag) in zip(mids, sufs, specs):
            wrow = []
            for n, ((nl, z, _), suffix) in enumerate(zip(row, srow)):
                w = jnp.exp(z - nl - suffix + carries[n])
                if masks is not None:
                    w = jnp.where(masks[n], w, 0.0)
                if flag is not None:
                    w = jnp.where(flag, w, 0.0)
                wrow.append(w.astype(BF))
                carries[n] = carries[n] - jnp.sum(nl, axis=-1, keepdims=True)
            ws.append(wrow)
        new = []
        for n in range(nc):
            acc = state[2 * n + 1]
            for wrow, (_, v) in zip(ws, kvs):
                acc = acc + _dot(wrow[n], v)
            new.extend([carries[n], acc])
        return new

    zc, za = jnp.zeros((SB_ROWS, 1), F32), jnp.zeros((SB_ROWS, LANES), F32)
    diag_masks = [diag_causal[rc * SB_ROWS:(rc + 1) * SB_ROWS] for _, rc in chains]
    state = blocks([(i, diag_masks, None), (jnp.maximum(i - 1, 0), None, i > 0)], [zc, za] * len(chains))

    def live(state):
        top = jnp.max(state[0])
        for c in state[2::2]:
            top = jnp.maximum(top, jnp.max(c))
        return top > SB_UNDERFLOW

    def cond(st):
        return jnp.logical_and(st[0] >= 0, st[1])

    def body(st):
        new = blocks([(st[0], None, None)], st[2])
        return st[0] - 1, live(new), tuple(new)

    _, _, state = lax.while_loop(cond, body, (i - 2, live(state), tuple(state)))
    accs = [jnp.concatenate([state[2 * (hh * nchunk + rc) + 1] for rc in range(nchunk)], axis=0)
            for hh in range(2)]
    o_ref[0] = jnp.where(hms[0], accs[0], accs[1]).astype(BF)


def _sb_attention(qkva3):
    B, S, _ = qkva3.shape
    tq = 256
    npair = SB_WIDTH // LANES
    return pl.pallas_call(
        functools.partial(_sb_body, tq=tq),
        grid=(B, npair, S // tq),
        in_specs=[
            pl.BlockSpec((1, tq, LANES), lambda b, p, i: (b, i, p)),
            pl.BlockSpec((1, S, LANES), lambda b, p, i: (b, 0, npair + p)),
            pl.BlockSpec((1, S, LANES), lambda b, p, i: (b, 0, 2 * npair + p)),
        ],
        out_specs=pl.BlockSpec((1, tq, LANES), lambda b, p, i: (b, i, p)),
        out_shape=jax.ShapeDtypeStruct((B, S, SB_WIDTH), BF),
        compiler_params=_cparams(("parallel", "parallel", "arbitrary")),
        name="sb_attention",
    )(qkva3, qkva3, qkva3)


def _cmp_body(ak_ref, av_ref, posk_ref, posv_ref, w1k_ref, w1kt_ref, w1kb_ref, w2k_ref,
              w1v_ref, w1vt_ref, w1vb_ref, w2v_ref, kc_ref, vc_ref):
    def one(a_ref, pos_ref, w1_ref, w1t_ref, w1b_ref, w2_ref, o_ref):
        a = a_ref[0]
        n = a.shape[0]
        bias = _dot(pos_ref[...], w1_ref[...])[0:1]
        out = jnp.zeros((n, LANES), F32)
        for g in range(NSA_KV_GROUPS):
            top = _dot(a, w1t_ref[g])
            bot = _dot(a, w1b_ref[g])
            h = top + pltpu.roll(bot, n - 1, 0) + bias
            out = out + _dot(jax.nn.gelu(h).astype(BF), w2_ref[g])
        o_ref[0] = out.astype(BF)

    one(ak_ref, posk_ref, w1k_ref, w1kt_ref, w1kb_ref, w2k_ref, kc_ref)
    one(av_ref, posv_ref, w1v_ref, w1vt_ref, w1vb_ref, w2v_ref, vc_ref)


def _compress(ak, av, posk, posv, wk, wv):
    B, n, width = ak.shape
    blk = pl.BlockSpec((1, n, width), lambda b: (b, 0, 0))
    c2 = lambda shp: pl.BlockSpec(shp, lambda b: (0, 0))
    c3 = lambda shp: pl.BlockSpec(shp, lambda b: (0, 0, 0))
    wspecs = [c2((CMP_LEN * HEAD_DIM, CMP_HIDDEN)), c3((2, width, CMP_HIDDEN)),
              c3((2, width, CMP_HIDDEN)), c3((2, CMP_HIDDEN, LANES))]
    out = pl.BlockSpec((1, n, LANES), lambda b: (b, 0, 0))
    return pl.pallas_call(
        _cmp_body,
        grid=(B,),
        in_specs=[blk, blk, c2((8, CMP_LEN * HEAD_DIM)), c2((8, CMP_LEN * HEAD_DIM))] + wspecs + wspecs,
        out_specs=[out, out],
        out_shape=[jax.ShapeDtypeStruct((B, n, LANES), BF)] * 2,
        compiler_params=_cparams(("parallel",)),
        name="nsa_compress",
    )(ak, av, posk, posv, *wk, *wv)


def _nsa_body(q_ref, kc_ref, vct_ref, ks_ref, vst_ref, kw_ref, vwt_ref, gt_ref, ovl_ref, et_ref,
              o_ref, *, tq, tk, nblk):
    i = pl.program_id(1)
    R = NSA_REP
    nchain = NSA_KV_GROUPS * R
    t0 = i * tq
    lane = lax.broadcasted_iota(jnp.int32, (1, LANES), 1)
    q = q_ref[0]
    t_row = t0 + lax.broadcasted_iota(jnp.int32, (1, tq), 1)
    tile = lambda a: jnp.concatenate([a] * nchain, axis=1)
    n_col = lax.broadcasted_iota(jnp.int32, (LANES, 1), 0)
    cmp_bias = tile(jnp.where((CMP_STRIDE * n_col + (CMP_LEN - 1)) <= t_row, 0.0, NEG_INF))
    has_cmp = tile(jnp.where(t_row >= CMP_LEN - 1, 1.0, 0.0))
    nkb = (t0 + tq + tk - 1) // tk
    key_last = (nkb - 1) * tk + lax.broadcasted_iota(jnp.int32, (tk, 1), 0)
    diag_bias = tile(jnp.where(key_last <= t_row, 0.0, NEG_INF))
    blk = lax.broadcasted_iota(jnp.int32, (nblk, 1), 0)
    cur = t_row // SLC_LEN
    valid = blk <= cur
    forced = (blk == 0) | (valid & (blk > cur - SLC_LOCAL))
    ovl = ovl_ref[...]

    qrs = [jnp.where((lane // HEAD_DIM) == gi, q[:, LANES * r:LANES * (r + 1)], jnp.zeros((tq, LANES), BF))
           for gi in range(NSA_KV_GROUPS) for r in range(R)]
    q_all = jnp.concatenate(qrs, axis=0)

    s = _dot_nt(kc_ref[0], q_all) + cmp_bias
    e = jnp.exp2(s - jnp.max(s, axis=0, keepdims=True))
    pc = e / jnp.sum(e, axis=0, keepdims=True) * has_cmp
    o_cmp = _dot(vct_ref[0], pc.astype(BF))

    q_aug = []
    for gi in range(NSA_KV_GROUPS):
        psum = jnp.zeros((LANES, tq), F32)
        for r in range(R):
            c = gi * R + r
            psum = psum + pc[:, c * tq:(c + 1) * tq]
        p1 = psum.astype(BF)
        r1 = psum - p1.astype(F32)
        p2 = r1.astype(BF)
        p3 = (r1 - p2.astype(F32)).astype(BF)
        imp = _dot(ovl, p1) + _dot(ovl, p2) + _dot(ovl, p3)
        imp = jnp.where(forced, imp + SLC_FORCE_BONUS, jnp.where(valid, imp, NEG_INF))
        sel = jnp.zeros((nblk, tq), F32)
        for _ in range(min(SLC_TOPN, nblk)):
            mx = jnp.max(imp, axis=0, keepdims=True)
            idx = jnp.min(jnp.where(imp == mx, blk, nblk), axis=0, keepdims=True)
            hit = blk == idx
            sel = jnp.where(hit, 1.0, sel)
            imp = jnp.where(hit, -jnp.inf, imp)
        sel_t = jnp.concatenate([sel, jnp.zeros((LANES - nblk, tq), F32)], axis=0).T
        sel_bias = ((sel_t - 1.0) * -NEG_INF).astype(BF)
        q_aug.extend(jnp.concatenate([qrs[gi * R + r], sel_bias], axis=1) for r in range(R))
    qa_all = jnp.concatenate(q_aug, axis=0)

    def sel_step(j, state, bias):
        m, l, acc = state
        off = pl.multiple_of(j * tk, tk)
        ka = jnp.concatenate([ks_ref[0, pl.ds(off, tk), :], et_ref[pl.ds(off, tk), :]], axis=1)
        s = _dot_nt(ka, qa_all)
        if bias is not None:
            s = s + bias
        m_new = jnp.maximum(m, jnp.max(s, axis=0, keepdims=True))
        alpha = jnp.exp2(m - m_new)
        p = jnp.exp2(s - m_new)
        l = alpha * l + jnp.sum(p, axis=0, keepdims=True)
        acc = alpha * acc + _dot(vst_ref[0, :, pl.ds(off, tk)], p.astype(BF))
        return m_new, l, acc

    width = nchain * tq
    init = (jnp.full((1, width), NEG_INF, F32), jnp.zeros((1, width), F32), jnp.zeros((LANES, width), F32))
    state = lax.fori_loop(0, nkb - 1, lambda j, st: sel_step(j, st, None), init)
    _, l_s, acc_s = sel_step(nkb - 1, state, diag_bias)
    o_sel = acc_s / l_s

    wkeys = WINDOW + WIN_Q
    o_parts = []
    for h in range(tq // WIN_Q):
        th = t_row[:, h * WIN_Q:(h + 1) * WIN_Q]
        woff = pl.multiple_of(jnp.maximum(t0 + h * WIN_Q - WINDOW, 0), WIN_Q)
        kp = woff + lax.broadcasted_iota(jnp.int32, (wkeys, 1), 0)
        bias = tile(jnp.where(kp <= th, jnp.where(kp > th - WINDOW, 0.0, NEG_INF), NEG_INF))
        qh = jnp.concatenate([qr[h * WIN_Q:(h + 1) * WIN_Q] for qr in qrs], axis=0)
        s = _dot_nt(kw_ref[0, pl.ds(woff, wkeys), :], qh) + bias
        p = jnp.exp2(s - jnp.max(s, axis=0, keepdims=True))
        o_parts.append(_dot(vwt_ref[0, :, pl.ds(woff, wkeys)], p.astype(BF)) / jnp.sum(p, axis=0, keepdims=True))
    o_win = jnp.concatenate([o[:, c * WIN_Q:(c + 1) * WIN_Q] for c in range(nchain) for o in o_parts], axis=1)

    gt = gt_ref[0]
    sub = lax.broadcasted_iota(jnp.int32, (LANES, 1), 0)
    g0 = (sub // HEAD_DIM) == 0
    for r in range(R):
        out = jnp.zeros((LANES, tq), F32)
        for br, o_br in enumerate((o_cmp, o_sel, o_win)):
            rows = [(gi * R + r) * NSA_N_BRANCH + br for gi in range(NSA_KV_GROUPS)]
            gate = jnp.where(g0, gt[rows[0]:rows[0] + 1], gt[rows[1]:rows[1] + 1])
            both = jnp.where(g0, o_br[:, r * tq:(r + 1) * tq], o_br[:, (R + r) * tq:(R + r + 1) * tq])
            out = out + gate * both
        o_ref[0, r * LANES:(r + 1) * LANES, :] = out.astype(BF)


def _nsa_attention(qb3, kc, vct, kr3, vals_t, gt, ovl, expand):
    B, S, _ = qb3.shape
    tq, tk = 256, 512
    n = kc.shape[1]
    nblk = ovl.shape[0]
    ngate = gt.shape[1]
    assert S >= WINDOW + WIN_Q and tq % WIN_Q == 0 and S % tk == 0 and nblk % 8 == 0 and n == LANES
    tok = lambda c: pl.BlockSpec((1, S, LANES), lambda b, i, c=c: (b, 0, c))
    tr = lambda c: pl.BlockSpec((1, LANES, S), lambda b, i, c=c: (b, c, 0))
    return pl.pallas_call(
        functools.partial(_nsa_body, tq=tq, tk=tk, nblk=nblk),
        grid=(B, S // tq),
        in_specs=[
            pl.BlockSpec((1, tq, NSA_WIDTH), lambda b, i: (b, i, 0)),
            pl.BlockSpec((1, n, LANES), lambda b, i: (b, 0, 0)),
            pl.BlockSpec((1, LANES, n), lambda b, i: (b, 0, 0)),
            tok(1), tr(1), tok(2), tr(2),
            pl.BlockSpec((1, ngate, tq), lambda b, i: (b, 0, i)),
            pl.BlockSpec((nblk, LANES), lambda b, i: (0, 0)),
            pl.BlockSpec((S, LANES), lambda b, i: (0, 0)),
        ],
        out_specs=pl.BlockSpec((1, NSA_WIDTH, tq), lambda b, i: (b, 0, i)),
        out_shape=jax.ShapeDtypeStruct((B, NSA_WIDTH, S), BF),
        compiler_params=_cparams(("parallel", "arbitrary")),
        name="nsa_attention",
    )(qb3, kc, vct, kr3, vals_t, kr3, vals_t, gt, ovl, expand)


def _outproj_ln_body(oa_ref, obt_ref, x_ref, wa_ref, wb_ref, g_ref, b_ref, o_ref, p_ref):
    yb = lax.dot_general(obt_ref[0], wb_ref[...], (((0,), (0,)), ((), ())), preferred_element_type=F32)
    y = _dot(oa_ref[...], wa_ref[...]) + yb
    res = _layer_norm(DN_ALPHA * x_ref[...] + y, g_ref[...], b_ref[...])
    o_ref[...] = res
    _store_row_words(p_ref, 0, res.shape[0], res)


def _outproj_ln(oa, obt, x2, wa, wb, g, b):
    T = x2.shape[0]
    tm = 512
    nseq = obt.shape[2] // tm
    row = lambda i: (i, 0)
    full = lambda i: (0, 0)
    return pl.pallas_call(
        _outproj_ln_body,
        grid=(T // tm,),
        in_specs=[
            pl.BlockSpec((tm, oa.shape[1]), row),
            pl.BlockSpec((1, obt.shape[1], tm), lambda i: (i // nseq, 0, i % nseq)),
            pl.BlockSpec((tm, D_MODEL), row),
            pl.BlockSpec(wa.shape, full),
            pl.BlockSpec(wb.shape, full),
            pl.BlockSpec((1, D_MODEL), full),
            pl.BlockSpec((1, D_MODEL), full),
        ],
        out_specs=[pl.BlockSpec((tm, D_MODEL), row),
                   pl.BlockSpec((tm * NWORD, LANES), row)],
        out_shape=[jax.ShapeDtypeStruct((T, D_MODEL), F32),
                   jax.ShapeDtypeStruct((T * NWORD, LANES), jnp.uint32)],
        compiler_params=_cparams(("parallel",)),
        name="outproj_ln",
    )(oa, obt, x2, wa, wb, g, b)


def _rope_tables(seq):
    inv = jnp.power(ROPE_THETA, -jnp.arange(0, ROPE_DIM, 2, dtype=F32) / ROPE_DIM)
    ang = jnp.arange(seq, dtype=F32)[:, None] * inv[None, :]
    half = ROPE_DIM // 2
    rest = HEAD_DIM - ROPE_DIM
    cos_h = jnp.concatenate([jnp.cos(ang), jnp.cos(ang), jnp.ones((seq, rest), F32)], axis=1)
    sin_h = jnp.concatenate([jnp.sin(ang), jnp.sin(ang), jnp.zeros((seq, rest), F32)], axis=1)
    reps = N_ROPE // HEAD_DIM
    del half
    return jnp.tile(cos_h, (1, reps)), jnp.tile(sin_h, (1, reps))


def _nsa_head_perm():
    cols = []
    for r in range(NSA_REP):
        for g in range(NSA_KV_GROUPS):
            h = g * NSA_REP + r
            cols.extend(range(h * HEAD_DIM, (h + 1) * HEAD_DIM))
    return np.asarray(cols)


def _mixer0(x2, batch, seq, w_in, cmp_pos_k, cmp_pos_v, cmp_k_w1, cmp_k_w2, cmp_v_w1, cmp_v_w2,
            w_out, ln_g, ln_b):
    T = x2.shape[0]
    sizes = [SB_WIDTH] * 3 + [NSA_WIDTH] + [NSA_KV_GROUPS * HEAD_DIM] * 6 + [NSA_HEADS * NSA_N_BRANCH]
    offs = np.concatenate([[0], np.cumsum(sizes)])
    col = lambda j: w_in[:, offs[j]:offs[j + 1]]
    perm = _nsa_head_perm()
    ngate = sizes[-1]
    wp = jnp.concatenate([col(0) * ATT_SCALE, col(1), col(2), col(5), col(7), col(9),
                          jnp.pad(col(10), ((0, 0), (0, LANES - ngate)))], axis=1)
    wr = jnp.concatenate([col(3)[:, perm] * (ATT_SCALE * LOG2_E), col(4), col(6), col(8)], axis=1)
    cos_t, sin_t = _rope_tables(seq)
    qkva, vals, gates, qb, kr = _proj0(x2, wp.astype(BF), wr.astype(BF), cos_t, sin_t, seq)

    r3 = lambda a: a.reshape(batch, seq, a.shape[-1])
    o_a = _sb_attention(r3(qkva))

    ncmp = seq // CMP_STRIDE
    ak = kr[:, 0:LANES].reshape(batch, ncmp, CMP_STRIDE * LANES)
    av = vals[:, 0:LANES].reshape(batch, ncmp, CMP_STRIDE * LANES)

    def cmp_weights(w1, w2):
        w1r = w1.reshape(2, CMP_STRIDE, HEAD_DIM, CMP_HIDDEN)
        tops, bots, w2s = [], [], []
        for g in range(NSA_KV_GROUPS):
            ext = jnp.zeros((2, CMP_STRIDE, NSA_KV_GROUPS, HEAD_DIM, CMP_HIDDEN), F32).at[:, :, g].set(w1r)
            ext = ext.reshape(2, CMP_STRIDE * LANES, CMP_HIDDEN)
            tops.append(ext[0])
            bots.append(ext[1])
            w2s.append(jnp.zeros((CMP_HIDDEN, LANES), F32).at[:, g * HEAD_DIM:(g + 1) * HEAD_DIM].set(w2))
        return (w1.astype(BF), jnp.stack(tops).astype(BF), jnp.stack(bots).astype(BF),
                jnp.stack(w2s).astype(BF))

    posk = jnp.broadcast_to(cmp_pos_k.reshape(1, -1), (8, CMP_LEN * HEAD_DIM)).astype(BF)
    posv = jnp.broadcast_to(cmp_pos_v.reshape(1, -1), (8, CMP_LEN * HEAD_DIM)).astype(BF)
    kc, vc = _compress(ak, av, posk, posv, cmp_weights(cmp_k_w1, cmp_k_w2), cmp_weights(cmp_v_w1, cmp_v_w2))
    assert ncmp <= LANES
    if ncmp < LANES:
        kc = jnp.pad(kc, ((0, 0), (0, LANES - ncmp), (0, 0)))
        vc = jnp.pad(vc, ((0, 0), (0, LANES - ncmp), (0, 0)))

    n_slc = seq // SLC_LEN
    cmp_start = np.arange(ncmp) * CMP_STRIDE
    slc_start = np.arange(n_slc) * SLC_LEN
    ovl = ((cmp_start[None, :] <= slc_start[:, None] + SLC_LEN - 1)
           & (cmp_start[None, :] + CMP_LEN - 1 >= slc_start[:, None])).astype(np.float32)
    ovl = np.pad(ovl, ((0, 0), (0, LANES - ncmp))) if ncmp < LANES else ovl
    expand = (np.arange(seq)[:, None] // SLC_LEN == np.arange(LANES)[None, :]).astype(np.float32)
    vals_t = jnp.swapaxes(r3(vals), 1, 2)
    gt = jnp.swapaxes(r3(gates)[:, :, :32], 1, 2)
    o_bt = _nsa_attention(r3(qb), kc, jnp.swapaxes(vc, 1, 2), r3(kr), vals_t, gt,
                          jnp.asarray(ovl, BF), jnp.asarray(expand, BF))

    wa = w_out[:SB_WIDTH].astype(BF)
    wb = w_out[SB_WIDTH:][perm].astype(BF)
    return _outproj_ln(o_a.reshape(T, SB_WIDTH), o_bt, x2, wa, wb,
                       ln_g.reshape(1, -1), ln_b.reshape(1, -1))


MOE_TILE = 2048
ROW_ALIGN = 8
RANK_CHUNK = 256


def _top_rows(vals, ids, n_ids, count):
    hits = []
    for _ in range(count):
        mx = jnp.max(vals, axis=0, keepdims=True)
        idx = jnp.min(jnp.where(vals == mx, ids, n_ids), axis=0, keepdims=True)
        hit = ids == idx
        hits.append(hit)
        vals = jnp.where(hit, -jnp.inf, vals)
    return hits


def _router_body(h_ref, rt_ref, rb_ref, pos_ref, gate_ref, meta_ref, *, tm):
    E, NG = N_EXPERTS, N_EXPERT_GROUPS
    per = E // NG
    hh, hl = _split2(h_ref[...])
    rh, rl = _split2(rt_ref[...])
    logits = _dot_nt(rh, hh) + _dot_nt(rh, hl) + _dot_nt(rl, hh)
    scores = jax.nn.sigmoid(logits)
    biased = scores + rb_ref[...]
    i8 = lax.broadcasted_iota(jnp.int32, (per, tm), 0)
    gs = []
    for g in range(NG):
        v = biased[g * per:(g + 1) * per]
        m1 = jnp.max(v, axis=0, keepdims=True)
        a1 = jnp.min(jnp.where(v == m1, i8, per), axis=0, keepdims=True)
        m2 = jnp.max(jnp.where(i8 == a1, -jnp.inf, v), axis=0, keepdims=True)
        gs.append(m1 + m2)
    gs = jnp.concatenate(gs, axis=0)
    gi = lax.broadcasted_iota(jnp.int32, (NG, tm), 0)
    ghits = _top_rows(gs, gi, NG, TOPK_GROUPS)
    gkeep = jnp.zeros((NG, tm), F32)
    for hit in ghits:
        gkeep = jnp.where(hit, 1.0, gkeep)
    ekeep = jnp.concatenate([jnp.broadcast_to(gkeep[g:g + 1], (per, tm)) for g in range(NG)], axis=0)
    ei = lax.broadcasted_iota(jnp.int32, (E, tm), 0)
    hits = _top_rows(jnp.where(ekeep > 0.5, biased, -jnp.inf), ei, E, TOP_K)
    gates = [jnp.sum(jnp.where(hit, scores, 0.0), axis=0, keepdims=True) for hit in hits]
    gsum = gates[0]
    for gk in gates[1:]:
        gsum = gsum + gk
    gates = [gk / gsum * ROUTED_SCALE for gk in gates]

    member = jnp.zeros((E, tm), F32)
    for hit in hits:
        member = jnp.where(hit, 1.0, member)
    cnt_col = jnp.sum(member, axis=1, keepdims=True)
    pad_col = jnp.floor((cnt_col + (ROW_ALIGN - 1)) * (1.0 / ROW_ALIGN)) * ROW_ALIGN
    sub_e = lax.broadcasted_iota(jnp.int32, (E, LANES), 0)
    lane_e = lax.broadcasted_iota(jnp.int32, (E, LANES), 1)
    cnt_row = jnp.sum(jnp.where(sub_e == lane_e, cnt_col, 0.0), axis=0, keepdims=True)
    pad_row = jnp.sum(jnp.where(sub_e == lane_e, pad_col, 0.0), axis=0, keepdims=True)
    off_row = jnp.sum(jnp.where(sub_e < lane_e, pad_col, 0.0), axis=0, keepdims=True)
    off_col = jnp.sum(jnp.where(lane_e < sub_e, pad_row, 0.0), axis=1, keepdims=True)
    r_i = lax.broadcasted_iota(jnp.int32, (RANK_CHUNK, RANK_CHUNK), 0)
    c_i = lax.broadcasted_iota(jnp.int32, (RANK_CHUNK, RANK_CHUNK), 1)
    before = jnp.where(r_i < c_i, 1.0, 0.0).astype(BF)
    running = off_col
    ranks = []
    for c in range(tm // RANK_CHUNK):
        mc = member[:, c * RANK_CHUNK:(c + 1) * RANK_CHUNK]
        ranks.append(_dot(mc.astype(BF), before) + running)
        running = running + jnp.sum(mc, axis=1, keepdims=True)
    slot = jnp.concatenate(ranks, axis=1)
    pos = [jnp.sum(jnp.where(hit, slot, 0.0), axis=0, keepdims=True) for hit in hits]
    zrow = jnp.zeros((1, tm), F32)
    pos_ref[...] = jnp.concatenate(pos + [zrow, zrow], axis=0).astype(jnp.int32)
    gate_ref[...] = jnp.concatenate(gates + [zrow, zrow], axis=0)
    z128 = jnp.zeros((1, LANES), F32)
    meta_ref[0] = jnp.concatenate([off_row, cnt_row] + [z128] * 6, axis=0).astype(jnp.int32)


def _router(h2, router_t, bias_col):
    T = h2.shape[0]
    tm = MOE_TILE
    nt = T // tm
    return pl.pallas_call(
        functools.partial(_router_body, tm=tm),
        grid=(nt,),
        in_specs=[
            pl.BlockSpec((tm, D_MODEL), lambda i: (i, 0)),
            pl.BlockSpec((N_EXPERTS, D_MODEL), lambda i: (0, 0)),
            pl.BlockSpec((N_EXPERTS, 1), lambda i: (0, 0)),
        ],
        out_specs=[
            pl.BlockSpec((8, tm), lambda i: (0, i)),
            pl.BlockSpec((8, tm), lambda i: (0, i)),
            pl.BlockSpec((1, 8, LANES), lambda i: (i, 0, 0)),
        ],
        out_shape=[
            jax.ShapeDtypeStruct((8, T), jnp.int32),
            jax.ShapeDtypeStruct((8, T), F32),
            jax.ShapeDtypeStruct((nt, 8, LANES), jnp.int32),
        ],
        compiler_params=_cparams(("parallel",)),
        name="moe_router",
    )(h2, router_t, bias_col)


EXPERTS_PER_STEP = 4
EXPERT_CHUNK = 256
COMBINE_SUB = 256
POS_STRIDE = 8


def _swiglu(xb, wgu, wd, hidden):
    gu = _dot(xb, wgu)
    a = jax.nn.silu(gu[:, :hidden]) * gu[:, hidden:]
    return _dot(a.astype(BF), wd)


def _moe_body(off_ref, cnt_ref, pos_ref, src_ref, x_ref, gcol_ref, wgu_ref, wd_ref, wsgu_ref, wsd_ref,
              g_ref, b_ref, o_ref, xs_ref, z_ref, *, tm, eb, ch, sub, unroll):
    i = pl.program_id(0)
    j = pl.program_id(1)
    nj = N_EXPERTS // eb

    @pl.when(j == 0)
    def _dispatch():
        zeros = jnp.zeros((ROW_ALIGN * NWORD, LANES), jnp.uint32)

        def pad(e, carry):
            off, cnt = off_ref[i * N_EXPERTS + e], cnt_ref[i * N_EXPERTS + e]
            last = pl.multiple_of((off + cnt // ROW_ALIGN * ROW_ALIGN) * NWORD, ROW_ALIGN * NWORD)
            xs_ref[pl.ds(last, ROW_ALIGN * NWORD), :] = zeros
            return carry

        lax.fori_loop(0, N_EXPERTS, pad, 0)
        end = off_ref[i * N_EXPERTS + N_EXPERTS - 1] + cnt_ref[i * N_EXPERTS + N_EXPERTS - 1]
        end = pl.multiple_of((end + ROW_ALIGN - 1) // ROW_ALIGN * ROW_ALIGN * NWORD, ROW_ALIGN * NWORD)
        xs_ref[pl.ds(end, ch * NWORD), :] = jnp.zeros((ch * NWORD, LANES), jnp.uint32)

        def tok(tb, carry):
            for u in range(unroll):
                t = tb * unroll + u
                slab = src_ref[pl.ds(pl.multiple_of(t * NWORD, NWORD), NWORD), :]
                for k in range(TOP_K):
                    p = pl.multiple_of(pos_ref[t * POS_STRIDE + k], NWORD)
                    xs_ref[pl.ds(p, NWORD), :] = slab
            return carry

        lax.fori_loop(0, tm // unroll, tok, 0)

    def chunks_in(els, offs, c):
        r0s = [pl.multiple_of(off + c * ch, ROW_ALIGN) for off in offs]
        words = [_load_row_words(xs_ref, r0, ch) for r0 in r0s]
        xbs = [jnp.concatenate([_unpack_lo(w).astype(BF) for w in ws]
                               + [_unpack_hi(w).astype(BF) for w in ws], axis=1) for ws in words]
        gus = [_dot(xb, wgu_ref[el]) for xb, el in zip(xbs, els)]
        acts = [(jax.nn.silu(gu[:, :EXPERT_HIDDEN]) * gu[:, EXPERT_HIDDEN:]).astype(BF) for gu in gus]
        ys = [_dot(a, wd_ref[el]) for a, el in zip(acts, els)]
        return list(zip(r0s, words, ys))

    def chunk_out(r0, words, y, c, cnt):
        keep = (c * ch + lax.broadcasted_iota(jnp.int32, (ch, 1), 0)) < cnt
        for cc, packed in enumerate(_pack_row_words(y)):
            xs_ref[pl.ds(r0 * NWORD + cc, ch, stride=NWORD), :] = jnp.where(keep, packed, words[cc])

    @pl.when(j < nj)
    def _experts():
        offs = [off_ref[i * N_EXPERTS + j * eb + el] for el in range(eb)]
        cnts = [cnt_ref[i * N_EXPERTS + j * eb + el] for el in range(eb)]
        firsts = chunks_in(list(range(eb)), offs, 0)
        for el in range(eb):
            chunk_out(*firsts[el], 0, cnts[el])
        for el in range(eb):
            def chunk(c, carry, el=el):
                chunk_out(*chunks_in([el], [offs[el]], c)[0], c, cnts[el])
                return carry

            lax.fori_loop(1, (cnts[el] + ch - 1) // ch, chunk, 0)

    @pl.when(j >= nj)
    def _combine():
        base = (j - nj) * sub

        def tok(tb, carry):
            for u in range(unroll):
                tl = tb * unroll + u
                dst = pl.multiple_of(tl * NWORD, NWORD)
                for k in range(TOP_K):
                    p = pl.multiple_of(pos_ref[(base + tl) * POS_STRIDE + k], NWORD)
                    z_ref[k, pl.ds(dst, NWORD), :] = xs_ref[pl.ds(p, NWORD), :]
            return carry

        lax.fori_loop(0, sub // unroll, tok, 0)
        gcol = gcol_ref[...]
        lo = [jnp.zeros((sub, LANES), F32) for _ in range(NWORD)]
        hi = [jnp.zeros((sub, LANES), F32) for _ in range(NWORD)]
        for k in range(TOP_K):
            gk = gcol[:, k:k + 1]
            for c, w in enumerate(_load_row_words(z_ref.at[k], 0, sub)):
                lo[c] = lo[c] + gk * _unpack_lo(w)
                hi[c] = hi[c] + gk * _unpack_hi(w)
        routed = jnp.concatenate(lo + hi, axis=1)
        x = x_ref[...]
        shared = _swiglu(x.astype(BF), wsgu_ref[...], wsd_ref[...], SHARED_HIDDEN)
        o_ref[...] = _layer_norm(DN_ALPHA * x + routed + shared, g_ref[...], b_ref[...])


def _moe_experts(h2, packed, off, cnt, pos, gcol, wgu, wd, wsgu, wsd, g, b):
    T = h2.shape[0]
    tm, eb, ch, sub = MOE_TILE, EXPERTS_PER_STEP, EXPERT_CHUNK, COMBINE_SUB
    nt, nj, nsub = T // tm, N_EXPERTS // eb, tm // sub
    rows = TOP_K * tm + N_EXPERTS * ROW_ALIGN + ch
    hidden2 = wgu.shape[-1]
    wblk = lambda i, j, *_: (jnp.minimum(j, nj - 1), 0, 0)
    sub_i = lambda i, j: i * nsub + jnp.clip(j - nj, 0, nsub - 1)
    once = pl.Buffered(1)
    return pl.pallas_call(
        functools.partial(_moe_body, tm=tm, eb=eb, ch=ch, sub=sub, unroll=4),
        grid_spec=pltpu.PrefetchScalarGridSpec(
            num_scalar_prefetch=2,
            grid=(nt, nj + nsub),
            in_specs=[
                pl.BlockSpec((tm * POS_STRIDE,), lambda i, j, *_: (i,), memory_space=pltpu.SMEM),
                pl.BlockSpec((tm * NWORD, LANES), lambda i, j, *_: (i, 0), pipeline_mode=once),
                pl.BlockSpec((sub, D_MODEL), lambda i, j, *_: (sub_i(i, j), 0)),
                pl.BlockSpec((sub, 8), lambda i, j, *_: (sub_i(i, j), 0)),
                pl.BlockSpec((eb, D_MODEL, hidden2), wblk),
                pl.BlockSpec((eb, hidden2 // 2, D_MODEL), wblk),
                pl.BlockSpec(wsgu.shape, lambda i, j, *_: (0, 0), pipeline_mode=once),
                pl.BlockSpec(wsd.shape, lambda i, j, *_: (0, 0), pipeline_mode=once),
                pl.BlockSpec((1, D_MODEL), lambda i, j, *_: (0, 0)),
                pl.BlockSpec((1, D_MODEL), lambda i, j, *_: (0, 0)),
            ],
            out_specs=pl.BlockSpec((sub, D_MODEL), lambda i, j, *_: (sub_i(i, j), 0)),
            scratch_shapes=[
                pltpu.VMEM((rows * NWORD, LANES), jnp.uint32),
                pltpu.VMEM((TOP_K, sub * NWORD, LANES), jnp.uint32),
            ],
        ),
        out_shape=jax.ShapeDtypeStruct((T, D_MODEL), F32),
        compiler_params=_cparams(("parallel", "arbitrary"), MOE_VMEM_LIMIT),
        name="moe_experts",
    )(off, cnt, pos, packed, h2, gcol, wgu, wd, wsgu, wsd, g, b)


S5_BATCH = 8
S5_STEPS = 64
S5_SLABS = 4
SLAB_CH = D_MODEL // S5_SLABS
SLAB_ST = SSM_GROUPS * SSM_STATE // S5_SLABS
N_STATE = SSM_GROUPS * SSM_STATE
SCAN_LANES = 512
SCAN_UNROLL = 8


def _s5_disc_body(lre_ref, lim_ref, ldt_ref, bre_ref, bim_ref, are_ref, aim_ref, bbre_ref, bbim_ref):
    lre, lim = lre_ref[...], lim_ref[...]
    step = jnp.exp(ldt_ref[...])
    mag = jnp.exp(lre * step)
    a_re = mag * jnp.cos(lim * step)
    a_im = mag * jnp.sin(lim * step)
    den = lre * lre + lim * lim
    zoh_re = ((a_re - 1.0) * lre + a_im * lim) / den
    zoh_im = (a_im * lre - (a_re - 1.0) * lim) / den
    are_ref[...] = a_re
    aim_ref[...] = a_im
    bbre_ref[...] = zoh_re * bre_ref[...] - zoh_im * bim_ref[...]
    bbim_ref[...] = zoh_re * bim_ref[...] + zoh_im * bre_ref[...]


def _s5_discretize(lambda_re, lambda_im, log_dt, b_re, b_im):
    col = lambda a: a.reshape(N_STATE, 1)
    ldt = jnp.broadcast_to(log_dt[:, None], (SSM_GROUPS, SSM_STATE))
    mat = lambda a: a.reshape(N_STATE, SSM_GROUP)
    c1 = jax.ShapeDtypeStruct((N_STATE, 1), F32)
    c16 = jax.ShapeDtypeStruct((N_STATE, SSM_GROUP), F32)
    return pl.pallas_call(_s5_disc_body, out_shape=[c1, c1, c16, c16], name="s5_discretize")(
        col(lambda_re), col(lambda_im), col(ldt), mat(b_re), mat(b_im))


def _s5_body(x_ref, win_ref, are_ref, aim_ref, bd_ref, cd_ref, dsk_ref, wglu_ref, wout_ref, g_ref, b_ref,
             o_ref, p_ref, xs_ref, pk_ref, hre_ref, him_ref, sre_ref, sim_ref, *, lt):
    nb = S5_BATCH
    nlb = D_MODEL // LANES
    for c in range(nlb):
        for b in range(nb):
            xs_ref[c, pl.ds(b, lt, stride=nb), :] = x_ref[b, :, c * LANES:(c + 1) * LANES]
    x = jnp.concatenate([xs_ref[c] for c in range(nlb)], axis=1)
    u = _dot(x.astype(BF), win_ref[...])
    ub = u.astype(BF)
    for k in range(S5_SLABS):
        bu = _dot(ub[:, k * SLAB_CH:(k + 1) * SLAB_CH], bd_ref[k])
        hre_ref[:, k * SLAB_ST:(k + 1) * SLAB_ST] = bu[:, :SLAB_ST]
        him_ref[:, k * SLAB_ST:(k + 1) * SLAB_ST] = bu[:, SLAB_ST:]

    @pl.when(pl.program_id(1) == 0)
    def _():
        sre_ref[...] = jnp.zeros_like(sre_ref)
        sim_ref[...] = jnp.zeros_like(sim_ref)

    for c in range(N_STATE // SCAN_LANES):
        ls = slice(c * SCAN_LANES, (c + 1) * SCAN_LANES)
        a_re = jnp.broadcast_to(are_ref[:, ls], (nb, SCAN_LANES))
        a_im = jnp.broadcast_to(aim_ref[:, ls], (nb, SCAN_LANES))

        def steps(tb, state, ls=ls, a_re=a_re, a_im=a_im):
            s_re, s_im = state
            for uu in range(SCAN_UNROLL):
                r0 = pl.multiple_of((tb * SCAN_UNROLL + uu) * nb, nb)
                n_re = a_re * s_re - a_im * s_im + hre_ref[pl.ds(r0, nb), ls]
                n_im = a_re * s_im + a_im * s_re + him_ref[pl.ds(r0, nb), ls]
                hre_ref[pl.ds(r0, nb), ls] = n_re
                him_ref[pl.ds(r0, nb), ls] = n_im
                s_re, s_im = n_re, n_im
            return s_re, s_im

        s_re, s_im = lax.fori_loop(0, lt // SCAN_UNROLL, steps, (sre_ref[:, ls], sim_ref[:, ls]))
        sre_ref[:, ls] = s_re
        sim_ref[:, ls] = s_im

    ys = []
    for k in range(S5_SLABS):
        hk = jnp.concatenate([hre_ref[:, k * SLAB_ST:(k + 1) * SLAB_ST].astype(BF),
                              him_ref[:, k * SLAB_ST:(k + 1) * SLAB_ST].astype(BF)], axis=1)
        ys.append(_dot(hk, cd_ref[k]))
    y = jax.nn.gelu(jnp.concatenate(ys, axis=1) + dsk_ref[...] * u)
    y = y * jax.nn.sigmoid(_dot(y.astype(BF), wglu_ref[...]))
    mixed = _dot(y.astype(BF), wout_ref[...])
    res = _layer_norm(DN_ALPHA * x + mixed, g_ref[...], b_ref[...])
    for c in range(nlb):
        xs_ref[c] = res[:, c * LANES:(c + 1) * LANES]
    for c, w in enumerate(_pack_row_words(res)):
        pk_ref[c] = w
    for c in range(nlb):
        for b in range(nb):
            o_ref[b, :, c * LANES:(c + 1) * LANES] = xs_ref[c, pl.ds(b, lt, stride=nb), :]
    for c in range(NWORD):
        for b in range(nb):
            p_ref[b, pl.ds(c, lt, stride=NWORD), :] = pk_ref[c, pl.ds(b, lt, stride=nb), :]


def _mixer1(x3, w_in, lambda_re, lambda_im, b_re, b_im, c_re, c_im, d_skip, log_dt, w_glu, w_out,
            ln_g, ln_b):
    B, S, _ = x3.shape
    lt = S5_STEPS
    rows = S5_BATCH * lt
    a_re, a_im, bb_re, bb_im = _s5_discretize(lambda_re, lambda_im, log_dt, b_re, b_im)
    gps = SSM_GROUPS // S5_SLABS
    eye = jnp.eye(gps, dtype=F32)

    def bdiag(bb):
        b4 = bb.reshape(S5_SLABS, gps, SSM_STATE, SSM_GROUP)
        return jnp.einsum('kgph,gf->kghfp', b4, eye).reshape(S5_SLABS, SLAB_CH, SLAB_ST)

    def cdiag(cc):
        c4 = cc.reshape(S5_SLABS, gps, SSM_GROUP, SSM_STATE)
        return jnp.einsum('kghp,gf->kfpgh', c4, eye).reshape(S5_SLABS, SLAB_ST, SLAB_CH)

    bd = jnp.concatenate([bdiag(bb_re), bdiag(bb_im)], axis=2).astype(BF)
    cd = jnp.concatenate([cdiag(c_re), -cdiag(c_im)], axis=1).astype(BF)
    c2 = lambda shp: pl.BlockSpec(shp, lambda bi, ti: (0,) * len(shp))
    return pl.pallas_call(
        functools.partial(_s5_body, lt=lt),
        grid=(B // S5_BATCH, S // lt),
        in_specs=[
            pl.BlockSpec((S5_BATCH, lt, D_MODEL), lambda bi, ti: (bi, ti, 0)),
            c2((D_MODEL, D_MODEL)), c2((1, N_STATE)), c2((1, N_STATE)),
            c2(bd.shape), c2(cd.shape), c2((1, D_MODEL)),
            c2((D_MODEL, D_MODEL)), c2((D_MODEL, D_MODEL)), c2((1, D_MODEL)), c2((1, D_MODEL)),
        ],
        out_specs=[pl.BlockSpec((S5_BATCH, lt, D_MODEL), lambda bi, ti: (bi, ti, 0)),
                   pl.BlockSpec((S5_BATCH, lt * NWORD, LANES), lambda bi, ti: (bi, ti, 0))],
        out_shape=[jax.ShapeDtypeStruct((B, S, D_MODEL), F32),
                   jax.ShapeDtypeStruct((B, S * NWORD, LANES), jnp.uint32)],
        scratch_shapes=[
            pltpu.VMEM((D_MODEL // LANES, rows, LANES), F32),
            pltpu.VMEM((NWORD, rows, LANES), jnp.uint32),
            pltpu.VMEM((rows, N_STATE), F32),
            pltpu.VMEM((rows, N_STATE), F32),
            pltpu.VMEM((S5_BATCH, N_STATE), F32),
            pltpu.VMEM((S5_BATCH, N_STATE), F32),
        ],
        compiler_params=_cparams(("parallel", "arbitrary")),
        name="s5_mixer",
    )(x3, w_in.astype(BF), a_re.reshape(1, N_STATE), a_im.reshape(1, N_STATE), bd, cd,
      d_skip.reshape(1, D_MODEL), w_glu.astype(BF), w_out.astype(BF), ln_g.reshape(1, -1), ln_b.reshape(1, -1))


def _moe_block(h2, packed, router, router_bias, w_gate, w_up, w_down, sh_gate, sh_up, sh_down, ln_g, ln_b):
    pos, gate, meta = _router(h2, router.T, router_bias.reshape(-1, 1))
    off = meta[:, 0, :N_EXPERTS].reshape(-1)
    cnt = meta[:, 1, :N_EXPERTS].reshape(-1)
    wgu = jnp.concatenate([w_gate, w_up], axis=-1).astype(BF)
    wsgu = jnp.concatenate([sh_gate, sh_up], axis=-1).astype(BF)
    pos = (pos.T * NWORD).reshape(-1)
    return _moe_experts(h2, packed, off, cnt, pos, gate.T, wgu, w_down.astype(BF), wsgu, sh_down.astype(BF),
                        ln_g.reshape(1, -1), ln_b.reshape(1, -1))


def kernel(x, l0_w_in, l0_cmp_pos_k, l0_cmp_pos_v, l0_cmp_k_w1, l0_cmp_k_w2, l0_cmp_v_w1, l0_cmp_v_w2, l0_w_out, l0_ln1_g, l0_ln1_b, l0_router, l0_router_bias, l0_w_gate, l0_w_up, l0_w_down, l0_sh_gate, l0_sh_up, l0_sh_down, l0_ln2_g, l0_ln2_b, l1_w_in, l1_lambda_re, l1_lambda_im, l1_b_re, l1_b_im, l1_c_re, l1_c_im, l1_d, l1_log_dt, l1_w_glu, l1_w_out, l1_ln1_g, l1_ln1_b, l1_router, l1_router_bias, l1_w_gate, l1_w_up, l1_w_down, l1_sh_gate, l1_sh_up, l1_sh_down, l1_ln2_g, l1_ln2_b):
    B, S, D = x.shape
    assert D == D_MODEL and S % 512 == 0 and B % S5_BATCH == 0 and (B * S) % MOE_TILE == 0
    T = B * S
    h, hp = _mixer0(x.reshape(T, D), B, S, l0_w_in, l0_cmp_pos_k, l0_cmp_pos_v, l0_cmp_k_w1, l0_cmp_k_w2,
                    l0_cmp_v_w1, l0_cmp_v_w2, l0_w_out, l0_ln1_g, l0_ln1_b)
    h = _moe_block(h, hp, l0_router, l0_router_bias, l0_w_gate, l0_w_up, l0_w_down, l0_sh_gate, l0_sh_up,
                   l0_sh_down, l0_ln2_g, l0_ln2_b)
    h, hp = _mixer1(h.reshape(B, S, D), l1_w_in, l1_lambda_re, l1_lambda_im, l1_b_re, l1_b_im, l1_c_re,
                    l1_c_im, l1_d, l1_log_dt, l1_w_glu, l1_w_out, l1_ln1_g, l1_ln1_b)
    h = _moe_block(h.reshape(T, D), hp.reshape(T * NWORD, LANES), l1_router, l1_router_bias, l1_w_gate, l1_w_up,
                   l1_w_down, l1_sh_gate, l1_sh_up, l1_sh_down, l1_ln2_g, l1_ln2_b)
    return h.reshape(B, S, D)
```

```python
import functools
import math

import numpy as np
import jax
import jax.numpy as jnp
from jax import lax
from jax.experimental import pallas as pl
from jax.experimental.pallas import tpu as pltpu

F32 = jnp.float32
BF = jnp.bfloat16

D_MODEL = 1024
DEPTH = 2
HEAD_DIM = 64
LANES = 128
SB_HEADS = 8
SB_WIDTH = SB_HEADS * HEAD_DIM
NSA_HEADS = 8
NSA_KV_GROUPS = 2
NSA_REP = NSA_HEADS // NSA_KV_GROUPS
NSA_WIDTH = NSA_HEADS * HEAD_DIM
NSA_N_BRANCH = 3
CMP_LEN = 32
CMP_STRIDE = 16
CMP_HIDDEN = 256
SLC_LEN = 64
SLC_TOPN = 8
SLC_LOCAL = 2
SLC_FORCE_BONUS = 1e4
WINDOW = 512
ROPE_THETA = 500000.0
ROPE_DIM = HEAD_DIM // 4
SSM_GROUP = 16
SSM_GROUPS = D_MODEL // SSM_GROUP
SSM_STATE = 64
N_EXPERTS = 64
N_EXPERT_GROUPS = 8
TOPK_GROUPS = 4
TOP_K = 6
EXPERT_HIDDEN = 256
SHARED_HIDDEN = 256
ROUTED_SCALE = 2.5
DN_ALPHA = (2 * DEPTH) ** 0.25
LN_EPS = 1e-5
NEG_INF = -1e30
ATT_SCALE = HEAD_DIM ** -0.5
LOG2_E = math.log2(math.e)
SB_UNDERFLOW = -104.0
WIN_Q = 128
SB_ROWS = 128

V7X_VMEM_BYTES = 64 * 1024 * 1024
VMEM_LIMIT = V7X_VMEM_BYTES - 8 * 1024 * 1024
MOE_VMEM_LIMIT = V7X_VMEM_BYTES - 4 * 1024 * 1024


def _cparams(sem, vmem=VMEM_LIMIT):
    return pltpu.CompilerParams(dimension_semantics=sem, vmem_limit_bytes=vmem)


def _dot(a, b):
    return jnp.dot(a, b, preferred_element_type=F32)


def _dot_nt(a, b):
    return lax.dot_general(a, b, (((1,), (1,)), ((), ())), preferred_element_type=F32)


def _split2(x):
    hi = x.astype(BF)
    lo = (x - hi.astype(F32)).astype(BF)
    return hi, lo


def _layer_norm(h, g, b):
    mu = jnp.mean(h, axis=-1, keepdims=True)
    d = h - mu
    var = jnp.mean(d * d, axis=-1, keepdims=True)
    return d * lax.rsqrt(var + LN_EPS) * g + b


HALF = D_MODEL // 2
NWORD = HALF // LANES
HI_MASK = 0xFFFF0000


def _pack_pairs(a, b):
    lo = pltpu.bitcast(a.astype(BF).astype(F32), jnp.uint32)
    hi = pltpu.bitcast(b.astype(BF).astype(F32), jnp.uint32)
    return (lo >> 16) | (hi & jnp.uint32(HI_MASK))


def _pack_row_words(h):
    return [_pack_pairs(h[:, c * LANES:(c + 1) * LANES], h[:, HALF + c * LANES:HALF + (c + 1) * LANES])
            for c in range(NWORD)]


def _store_row_words(ref, start, n, h):
    for c, w in enumerate(_pack_row_words(h)):
        ref[pl.ds(start * NWORD + c, n, stride=NWORD), :] = w


def _load_row_words(ref, start, n):
    return [ref[pl.ds(start * NWORD + c, n, stride=NWORD), :] for c in range(NWORD)]


def _unpack_lo(w):
    return pltpu.bitcast(w << 16, F32)


def _unpack_hi(w):
    return pltpu.bitcast(w & jnp.uint32(HI_MASK), F32)


N_PLAIN = 3 * SB_WIDTH + LANES
N_TRANS = 2 * LANES + 32
N_ROPE = NSA_WIDTH + 3 * LANES


def _proj0_body(x_ref, wp_ref, wt_ref, wr_ref, cos_ref, sin_ref,
                qkva_ref, vc_ref, vt_ref, gt_ref, qb_ref, kc_ref, ksw_ref):
    xb = x_ref[...].astype(BF)
    a0 = 3 * SB_WIDTH
    qkva_ref[...] = _dot(xb, wp_ref[:, 0:a0]).astype(BF)
    vc_ref[...] = _dot(xb, wp_ref[:, a0:N_PLAIN]).astype(BF)
    yt = _dot_nt(wt_ref[...], xb)
    vt_ref[0] = yt[0:2 * LANES].astype(BF)
    gt_ref[0] = jax.nn.sigmoid(yt[2 * LANES:N_TRANS])
    y = _dot(xb, wr_ref[...])
    half = ROPE_DIM // 2
    pos = lax.broadcasted_iota(jnp.int32, (1, N_ROPE), 1) % HEAD_DIM
    partner = jnp.where(pos < half, -pltpu.roll(y, N_ROPE - half, 1), pltpu.roll(y, half, 1))
    y = y * cos_ref[...] + partner * sin_ref[...]
    qb_ref[...] = y[:, 0:NSA_WIDTH].astype(BF)
    kc_ref[...] = y[:, NSA_WIDTH:NSA_WIDTH + LANES].astype(BF)
    ksw_ref[...] = y[:, NSA_WIDTH + LANES:N_ROPE].astype(BF)


def _proj0(x2, wp, wt, wr, cos_t, sin_t, batch, seq):
    T = x2.shape[0]
    tm = 512
    nseq = seq // tm
    row = lambda i: (i, 0)
    full = lambda i: (0, 0)
    tab = lambda i: (i % nseq, 0)
    trans = lambda i: (i // nseq, 0, i % nseq)
    return pl.pallas_call(
        _proj0_body,
        grid=(T // tm,),
        in_specs=[
            pl.BlockSpec((tm, D_MODEL), row),
            pl.BlockSpec((D_MODEL, N_PLAIN), full),
            pl.BlockSpec((N_TRANS, D_MODEL), full),
            pl.BlockSpec((D_MODEL, N_ROPE), full),
            pl.BlockSpec((tm, N_ROPE), tab),
            pl.BlockSpec((tm, N_ROPE), tab),
        ],
        out_specs=[
            pl.BlockSpec((tm, 3 * SB_WIDTH), row),
            pl.BlockSpec((tm, LANES), row),
            pl.BlockSpec((1, 2 * LANES, tm), trans),
            pl.BlockSpec((1, N_TRANS - 2 * LANES, tm), trans),
            pl.BlockSpec((tm, NSA_WIDTH), row),
            pl.BlockSpec((tm, LANES), row),
            pl.BlockSpec((tm, 2 * LANES), row),
        ],
        out_shape=[
            jax.ShapeDtypeStruct((T, 3 * SB_WIDTH), BF),
            jax.ShapeDtypeStruct((T, LANES), BF),
            jax.ShapeDtypeStruct((batch, 2 * LANES, seq), BF),
            jax.ShapeDtypeStruct((batch, N_TRANS - 2 * LANES, seq), F32),
            jax.ShapeDtypeStruct((T, NSA_WIDTH), BF),
            jax.ShapeDtypeStruct((T, LANES), BF),
            jax.ShapeDtypeStruct((T, 2 * LANES), BF),
        ],
        compiler_params=_cparams(("parallel",)),
        name="proj0",
    )(x2, wp, wt, wr, cos_t, sin_t)


def _sb_body(q_ref, k_ref, v_ref, o_ref, *, tq):
    i = pl.program_id(2)
    q = q_ref[0]
    lane = lax.broadcasted_iota(jnp.int32, (1, LANES), 1)
    row = lax.broadcasted_iota(jnp.int32, (tq, tq), 0)
    col = lax.broadcasted_iota(jnp.int32, (tq, tq), 1)
    tri = jnp.where(row > col, 1.0, 0.0).astype(BF)
    diag_causal = col < row

    hms = [(lane // HEAD_DIM) == hh for hh in range(2)]
    nchunk = tq // SB_ROWS
    chains = [(hh, rc) for hh in range(2) for rc in range(nchunk)]
    qcs = [jnp.where(hms[hh], q[rc * SB_ROWS:(rc + 1) * SB_ROWS], jnp.zeros((SB_ROWS, LANES), BF))
           for hh, rc in chains]

    def blocks(specs, state):
        nc = len(qcs)
        kvs = []
        for j, _, _ in specs:
            off = pl.multiple_of(j * tq, tq)
            kvs.append((k_ref[0, pl.ds(off, tq), :], v_ref[0, pl.ds(off, tq), :]))
        zs = [[_dot_nt(qc, k) for qc in qcs] for k, _ in kvs]
        mids = []
        for zrow, (_, masks, flag) in zip(zs, specs):
            row = []
            for n, z in enumerate(zrow):
                nl = jnp.maximum(z, 0.0) + jnp.log(1.0 + jnp.exp(-jnp.abs(z)))
                if masks is not None:
                    nl = jnp.where(masks[n], nl, 0.0)
                if flag is not None:
                    nl = jnp.where(flag, nl, 0.0)
                row.append((nl, z, nl.astype(BF)))
            mids.append(row)
        sufs = [[_dot(hi, tri) for _, _, hi in row] for row in mids]
        carries = [state[2 * n] for n in range(nc)]
        ws = []
        for row, srow, (_, masks, flag) in zip(mids, sufs, specs):
            wrow = []
            for n, ((nl, z, _), suffix) in enumerate(zip(row, srow)):
                w = jnp.exp(z - nl - suffix + carries[n])
                if masks is not None:
                    w = jnp.where(masks[n], w, 0.0)
                if flag is not None:
                    w = jnp.where(flag, w, 0.0)
                wrow.append(w.astype(BF))
                carries[n] = carries[n] - jnp.sum(nl, axis=-1, keepdims=True)
            ws.append(wrow)
        new = []
        for n in range(nc):
            acc = state[2 * n + 1]
            for wrow, (_, v) in zip(ws, kvs):
                acc = acc + _dot(wrow[n], v)
            new.extend([carries[n], acc])
        return new

    zc, za = jnp.zeros((SB_ROWS, 1), F32), jnp.zeros((SB_ROWS, LANES), F32)
    diag_masks = [diag_causal[rc * SB_ROWS:(rc + 1) * SB_ROWS] for _, rc in chains]
    state = blocks([(i, diag_masks, None), (jnp.maximum(i - 1, 0), None, i > 0)], [zc, za] * len(chains))

    def live(state):
        top = jnp.max(state[0])
        for c in state[2::2]:
            top = jnp.maximum(top, jnp.max(c))
        return top > SB_UNDERFLOW

    def cond(st):
        return jnp.logical_and(st[0] >= 0, st[1])

    def body(st):
        new = blocks([(st[0], None, None)], st[2])
        return st[0] - 1, live(new), tuple(new)

    _, _, state = lax.while_loop(cond, body, (i - 2, live(state), tuple(state)))
    accs = [jnp.concatenate([state[2 * (hh * nchunk + rc) + 1] for rc in range(nchunk)], axis=0)
            for hh in range(2)]
    o_ref[0] = jnp.where(hms[0], accs[0], accs[1]).astype(BF)


def _sb_attention(qkva3):
    B, S, _ = qkva3.shape
    tq = 256
    npair = SB_WIDTH // LANES
    return pl.pallas_call(
        functools.partial(_sb_body, tq=tq),
        grid=(B, npair, S // tq),
        in_specs=[
            pl.BlockSpec((1, tq, LANES), lambda b, p, i: (b, i, p)),
            pl.BlockSpec((1, S, LANES), lambda b, p, i: (b, 0, npair + p)),
            pl.BlockSpec((1, S, LANES), lambda b, p, i: (b, 0, 2 * npair + p)),
        ],
        out_specs=pl.BlockSpec((1, tq, LANES), lambda b, p, i: (b, i, p)),
        out_shape=jax.ShapeDtypeStruct((B, S, SB_WIDTH), BF),
        compiler_params=_cparams(("parallel", "parallel", "arbitrary")),
        name="sb_attention",
    )(qkva3, qkva3, qkva3)


def _cmp_body(ak_ref, av_ref, posk_ref, posv_ref, w1k_ref, w1kt_ref, w1kb_ref, w2k_ref,
              w1v_ref, w1vt_ref, w1vb_ref, w2v_ref, kc_ref, vc_ref):
    def one(a_ref, pos_ref, w1_ref, w1t_ref, w1b_ref, w2_ref, o_ref):
        a = a_ref[0]
        n = a.shape[0]
        bias = _dot(pos_ref[...], w1_ref[...])[0:1]
        out = jnp.zeros((n, LANES), F32)
        for g in range(NSA_KV_GROUPS):
            top = _dot(a, w1t_ref[g])
            bot = _dot(a, w1b_ref[g])
            h = top + pltpu.roll(bot, n - 1, 0) + bias
            out = out + _dot(jax.nn.gelu(h).astype(BF), w2_ref[g])
        o_ref[0] = out.astype(BF)

    one(ak_ref, posk_ref, w1k_ref, w1kt_ref, w1kb_ref, w2k_ref, kc_ref)
    one(av_ref, posv_ref, w1v_ref, w1vt_ref, w1vb_ref, w2v_ref, vc_ref)


def _compress(ak, av, posk, posv, wk, wv):
    B, n, width = ak.shape
    blk = pl.BlockSpec((1, n, width), lambda b: (b, 0, 0))
    c2 = lambda shp: pl.BlockSpec(shp, lambda b: (0, 0))
    c3 = lambda shp: pl.BlockSpec(shp, lambda b: (0, 0, 0))
    wspecs = [c2((CMP_LEN * HEAD_DIM, CMP_HIDDEN)), c3((2, width, CMP_HIDDEN)),
              c3((2, width, CMP_HIDDEN)), c3((2, CMP_HIDDEN, LANES))]
    out = pl.BlockSpec((1, n, LANES), lambda b: (b, 0, 0))
    return pl.pallas_call(
        _cmp_body,
        grid=(B,),
        in_specs=[blk, blk, c2((8, CMP_LEN * HEAD_DIM)), c2((8, CMP_LEN * HEAD_DIM))] + wspecs + wspecs,
        out_specs=[out, out],
        out_shape=[jax.ShapeDtypeStruct((B, n, LANES), BF)] * 2,
        compiler_params=_cparams(("parallel",)),
        name="nsa_compress",
    )(ak, av, posk, posv, *wk, *wv)


def _nsa_body(q_ref, kc_ref, vct_ref, ks_ref, vst_ref, kw_ref, vwt_ref, gt_ref, ovl_ref, et_ref,
              o_ref, *, tq, tk, nblk):
    i = pl.program_id(1)
    R = NSA_REP
    nchain = NSA_KV_GROUPS * R
    t0 = i * tq
    lane = lax.broadcasted_iota(jnp.int32, (1, LANES), 1)
    q = q_ref[0]
    t_row = t0 + lax.broadcasted_iota(jnp.int32, (1, tq), 1)
    tile = lambda a: jnp.concatenate([a] * nchain, axis=1)
    n_col = lax.broadcasted_iota(jnp.int32, (LANES, 1), 0)
    cmp_bias = tile(jnp.where((CMP_STRIDE * n_col + (CMP_LEN - 1)) <= t_row, 0.0, NEG_INF))
    has_cmp = tile(jnp.where(t_row >= CMP_LEN - 1, 1.0, 0.0))
    nkb = (t0 + tq + tk - 1) // tk
    key_last = (nkb - 1) * tk + lax.broadcasted_iota(jnp.int32, (tk, 1), 0)
    diag_bias = tile(jnp.where(key_last <= t_row, 0.0, NEG_INF))
    blk = lax.broadcasted_iota(jnp.int32, (nblk, 1), 0)
    cur = t_row // SLC_LEN
    valid = blk <= cur
    forced = (blk == 0) | (valid & (blk > cur - SLC_LOCAL))
    ovl = ovl_ref[...]

    qrs = [jnp.where((lane // HEAD_DIM) == gi, q[:, LANES * r:LANES * (r + 1)], jnp.zeros((tq, LANES), BF))
           for gi in range(NSA_KV_GROUPS) for r in range(R)]
    q_all = jnp.concatenate(qrs, axis=0)

    s = _dot_nt(kc_ref[0], q_all) + cmp_bias
    e = jnp.exp2(s - jnp.max(s, axis=0, keepdims=True))
    pc = e / jnp.sum(e, axis=0, keepdims=True) * has_cmp
    o_cmp = _dot(vct_ref[0], pc.astype(BF))

    q_aug = []
    for gi in range(NSA_KV_GROUPS):
        psum = jnp.zeros((LANES, tq), F32)
        for r in range(R):
            c = gi * R + r
            psum = psum + pc[:, c * tq:(c + 1) * tq]
        p1 = psum.astype(BF)
        r1 = psum - p1.astype(F32)
        p2 = r1.astype(BF)
        p3 = (r1 - p2.astype(F32)).astype(BF)
        imp = _dot(ovl, p1) + _dot(ovl, p2) + _dot(ovl, p3)
        imp = jnp.where(forced, imp + SLC_FORCE_BONUS, jnp.where(valid, imp, NEG_INF))
        sel = jnp.zeros((nblk, tq), F32)
        for _ in range(min(SLC_TOPN, nblk)):
            mx = jnp.max(imp, axis=0, keepdims=True)
            idx = jnp.min(jnp.where(imp == mx, blk, nblk), axis=0, keepdims=True)
            hit = blk == idx
            sel = jnp.where(hit, 1.0, sel)
            imp = jnp.where(hit, -jnp.inf, imp)
        sel_t = jnp.concatenate([sel, jnp.zeros((LANES - nblk, tq), F32)], axis=0).T
        sel_bias = ((sel_t - 1.0) * -NEG_INF).astype(BF)
        q_aug.extend(jnp.concatenate([qrs[gi * R + r], sel_bias], axis=1) for r in range(R))
    qa_all = jnp.concatenate(q_aug, axis=0)

    def sel_step(j, state, bias):
        m, l, acc = state
        off = pl.multiple_of(j * tk, tk)
        ka = jnp.concatenate([ks_ref[0, pl.ds(off, tk), :], et_ref[pl.ds(off, tk), :]], axis=1)
        s = _dot_nt(ka, qa_all)
        if bias is not None:
            s = s + bias
        m_new = jnp.maximum(m, jnp.max(s, axis=0, keepdims=True))
        alpha = jnp.exp2(m - m_new)
        p = jnp.exp2(s - m_new)
        l = alpha * l + jnp.sum(p, axis=0, keepdims=True)
        acc = alpha * acc + _dot(vst_ref[0, :, pl.ds(off, tk)], p.astype(BF))
        return m_new, l, acc

    width = nchain * tq
    init = (jnp.full((1, width), NEG_INF, F32), jnp.zeros((1, width), F32), jnp.zeros((LANES, width), F32))
    state = lax.fori_loop(0, nkb - 1, lambda j, st: sel_step(j, st, None), init)
    _, l_s, acc_s = sel_step(nkb - 1, state, diag_bias)
    o_sel = acc_s / l_s

    wkeys = WINDOW + WIN_Q
    o_parts = []
    for h in range(tq // WIN_Q):
        th = t_row[:, h * WIN_Q:(h + 1) * WIN_Q]
        woff = pl.multiple_of(jnp.maximum(t0 + h * WIN_Q - WINDOW, 0), WIN_Q)
        kp = woff + lax.broadcasted_iota(jnp.int32, (wkeys, 1), 0)
        bias = tile(jnp.where(kp <= th, jnp.where(kp > th - WINDOW, 0.0, NEG_INF), NEG_INF))
        qh = jnp.concatenate([qr[h * WIN_Q:(h + 1) * WIN_Q] for qr in qrs], axis=0)
        s = _dot_nt(kw_ref[0, pl.ds(woff, wkeys), :], qh) + bias
        p = jnp.exp2(s - jnp.max(s, axis=0, keepdims=True))
        o_parts.append(_dot(vwt_ref[0, :, pl.ds(woff, wkeys)], p.astype(BF)) / jnp.sum(p, axis=0, keepdims=True))
    o_win = jnp.concatenate([o[:, c * WIN_Q:(c + 1) * WIN_Q] for c in range(nchain) for o in o_parts], axis=1)

    gt = gt_ref[0]
    sub = lax.broadcasted_iota(jnp.int32, (LANES, 1), 0)
    g0 = (sub // HEAD_DIM) == 0
    for r in range(R):
        out = jnp.zeros((LANES, tq), F32)
        for br, o_br in enumerate((o_cmp, o_sel, o_win)):
            rows = [(gi * R + r) * NSA_N_BRANCH + br for gi in range(NSA_KV_GROUPS)]
            gate = jnp.where(g0, gt[rows[0]:rows[0] + 1], gt[rows[1]:rows[1] + 1])
            both = jnp.where(g0, o_br[:, r * tq:(r + 1) * tq], o_br[:, (R + r) * tq:(R + r + 1) * tq])
            out = out + gate * both
        o_ref[0, r * LANES:(r + 1) * LANES, :] = out.astype(BF)


def _nsa_attention(qb3, kc, vct, ksw3, vsw_t, gt, ovl, expand):
    B, S, _ = qb3.shape
    tq, tk = 256, 512
    n = kc.shape[1]
    nblk = ovl.shape[0]
    ngate = gt.shape[1]
    assert S >= WINDOW + WIN_Q and tq % WIN_Q == 0 and S % tk == 0 and nblk % 8 == 0 and n == LANES
    tok = lambda c: pl.BlockSpec((1, S, LANES), lambda b, i, c=c: (b, 0, c))
    tr = lambda c: pl.BlockSpec((1, LANES, S), lambda b, i, c=c: (b, c, 0))
    return pl.pallas_call(
        functools.partial(_nsa_body, tq=tq, tk=tk, nblk=nblk),
        grid=(B, S // tq),
        in_specs=[
            pl.BlockSpec((1, tq, NSA_WIDTH), lambda b, i: (b, i, 0)),
            pl.BlockSpec((1, n, LANES), lambda b, i: (b, 0, 0)),
            pl.BlockSpec((1, LANES, n), lambda b, i: (b, 0, 0)),
            tok(0), tr(0), tok(1), tr(1),
            pl.BlockSpec((1, ngate, tq), lambda b, i: (b, 0, i)),
            pl.BlockSpec((nblk, LANES), lambda b, i: (0, 0)),
            pl.BlockSpec((S, LANES), lambda b, i: (0, 0)),
        ],
        out_specs=pl.BlockSpec((1, NSA_WIDTH, tq), lambda b, i: (b, 0, i)),
        out_shape=jax.ShapeDtypeStruct((B, NSA_WIDTH, S), BF),
        compiler_params=_cparams(("parallel", "arbitrary")),
        name="nsa_attention",
    )(qb3, kc, vct, ksw3, vsw_t, ksw3, vsw_t, gt, ovl, expand)


def _outproj_ln_body(oa_ref, obt_ref, x_ref, wa_ref, wb_ref, g_ref, b_ref, o_ref, p_ref):
    yb = lax.dot_general(obt_ref[0], wb_ref[...], (((0,), (0,)), ((), ())), preferred_element_type=F32)
    y = _dot(oa_ref[...], wa_ref[...]) + yb
    res = _layer_norm(DN_ALPHA * x_ref[...] + y, g_ref[...], b_ref[...])
    o_ref[...] = res
    _store_row_words(p_ref, 0, res.shape[0], res)


def _outproj_ln(oa, obt, x2, wa, wb, g, b):
    T = x2.shape[0]
    tm = 512
    nseq = obt.shape[2] // tm
    row = lambda i: (i, 0)
    full = lambda i: (0, 0)
    return pl.pallas_call(
        _outproj_ln_body,
        grid=(T // tm,),
        in_specs=[
            pl.BlockSpec((tm, oa.shape[1]), row),
            pl.BlockSpec((1, obt.shape[1], tm), lambda i: (i // nseq, 0, i % nseq)),
            pl.BlockSpec((tm, D_MODEL), row),
            pl.BlockSpec(wa.shape, full),
            pl.BlockSpec(wb.shape, full),
            pl.BlockSpec((1, D_MODEL), full),
            pl.BlockSpec((1, D_MODEL), full),
        ],
        out_specs=[pl.BlockSpec((tm, D_MODEL), row),
                   pl.BlockSpec((tm * NWORD, LANES), row)],
        out_shape=[jax.ShapeDtypeStruct((T, D_MODEL), F32),
                   jax.ShapeDtypeStruct((T * NWORD, LANES), jnp.uint32)],
        compiler_params=_cparams(("parallel",)),
        name="outproj_ln",
    )(oa, obt, x2, wa, wb, g, b)


def _rope_tables(seq):
    inv = jnp.power(ROPE_THETA, -jnp.arange(0, ROPE_DIM, 2, dtype=F32) / ROPE_DIM)
    ang = jnp.arange(seq, dtype=F32)[:, None] * inv[None, :]
    half = ROPE_DIM // 2
    rest = HEAD_DIM - ROPE_DIM
    cos_h = jnp.concatenate([jnp.cos(ang), jnp.cos(ang), jnp.ones((seq, rest), F32)], axis=1)
    sin_h = jnp.concatenate([jnp.sin(ang), jnp.sin(ang), jnp.zeros((seq, rest), F32)], axis=1)
    reps = N_ROPE // HEAD_DIM
    del half
    return jnp.tile(cos_h, (1, reps)), jnp.tile(sin_h, (1, reps))


def _nsa_head_perm():
    cols = []
    for r in range(NSA_REP):
        for g in range(NSA_KV_GROUPS):
            h = g * NSA_REP + r
            cols.extend(range(h * HEAD_DIM, (h + 1) * HEAD_DIM))
    return np.asarray(cols)


def _mixer0(x2, batch, seq, w_in, cmp_pos_k, cmp_pos_v, cmp_k_w1, cmp_k_w2, cmp_v_w1, cmp_v_w2,
            w_out, ln_g, ln_b):
    T = x2.shape[0]
    sizes = [SB_WIDTH] * 3 + [NSA_WIDTH] + [NSA_KV_GROUPS * HEAD_DIM] * 6 + [NSA_HEADS * NSA_N_BRANCH]
    offs = np.concatenate([[0], np.cumsum(sizes)])
    col = lambda j: w_in[:, offs[j]:offs[j + 1]]
    perm = _nsa_head_perm()
    ngate = sizes[-1]
    wp = jnp.concatenate([col(0) * ATT_SCALE, col(1), col(2), col(5)], axis=1)
    wt = jnp.concatenate([col(7), col(9), jnp.pad(col(10), ((0, 0), (0, N_TRANS - 2 * LANES - ngate)))], axis=1).T
    wr = jnp.concatenate([col(3)[:, perm] * (ATT_SCALE * LOG2_E), col(4), col(6), col(8)], axis=1)
    cos_t, sin_t = _rope_tables(seq)
    qkva, vc_tok, vsw_t, gt, qb, kc_tok, ksw = _proj0(x2, wp.astype(BF), wt.astype(BF), wr.astype(BF), cos_t, sin_t,
                                                      batch, seq)

    r3 = lambda a: a.reshape(batch, seq, a.shape[-1])
    o_a = _sb_attention(r3(qkva))

    ncmp = seq // CMP_STRIDE
    ak = kc_tok.reshape(batch, ncmp, CMP_STRIDE * LANES)
    av = vc_tok.reshape(batch, ncmp, CMP_STRIDE * LANES)

    def cmp_weights(w1, w2):
        w1r = w1.reshape(2, CMP_STRIDE, HEAD_DIM, CMP_HIDDEN)
        tops, bots, w2s = [], [], []
        for g in range(NSA_KV_GROUPS):
            ext = jnp.zeros((2, CMP_STRIDE, NSA_KV_GROUPS, HEAD_DIM, CMP_HIDDEN), F32).at[:, :, g].set(w1r)
            ext = ext.reshape(2, CMP_STRIDE * LANES, CMP_HIDDEN)
            tops.append(ext[0])
            bots.append(ext[1])
            w2s.append(jnp.zeros((CMP_HIDDEN, LANES), F32).at[:, g * HEAD_DIM:(g + 1) * HEAD_DIM].set(w2))
        return (w1.astype(BF), jnp.stack(tops).astype(BF), jnp.stack(bots).astype(BF),
                jnp.stack(w2s).astype(BF))

    posk = jnp.broadcast_to(cmp_pos_k.reshape(1, -1), (8, CMP_LEN * HEAD_DIM)).astype(BF)
    posv = jnp.broadcast_to(cmp_pos_v.reshape(1, -1), (8, CMP_LEN * HEAD_DIM)).astype(BF)
    kc, vc = _compress(ak, av, posk, posv, cmp_weights(cmp_k_w1, cmp_k_w2), cmp_weights(cmp_v_w1, cmp_v_w2))
    assert ncmp <= LANES
    if ncmp < LANES:
        kc = jnp.pad(kc, ((0, 0), (0, LANES - ncmp), (0, 0)))
        vc = jnp.pad(vc, ((0, 0), (0, LANES - ncmp), (0, 0)))

    n_slc = seq // SLC_LEN
    cmp_start = np.arange(ncmp) * CMP_STRIDE
    slc_start = np.arange(n_slc) * SLC_LEN
    ovl = ((cmp_start[None, :] <= slc_start[:, None] + SLC_LEN - 1)
           & (cmp_start[None, :] + CMP_LEN - 1 >= slc_start[:, None])).astype(np.float32)
    ovl = np.pad(ovl, ((0, 0), (0, LANES - ncmp))) if ncmp < LANES else ovl
    expand = (np.arange(seq)[:, None] // SLC_LEN == np.arange(LANES)[None, :]).astype(np.float32)
    o_bt = _nsa_attention(r3(qb), kc, jnp.swapaxes(vc, 1, 2), r3(ksw), vsw_t, gt,
                          jnp.asarray(ovl, BF), jnp.asarray(expand, BF))

    wa = w_out[:SB_WIDTH].astype(BF)
    wb = w_out[SB_WIDTH:][perm].astype(BF)
    return _outproj_ln(o_a.reshape(T, SB_WIDTH), o_bt, x2, wa, wb,
                       ln_g.reshape(1, -1), ln_b.reshape(1, -1))


MOE_TILE = 2048
ROW_ALIGN = 8
RANK_CHUNK = 256


def _top_rows(vals, ids, n_ids, count):
    hits = []
    for _ in range(count):
        mx = jnp.max(vals, axis=0, keepdims=True)
        idx = jnp.min(jnp.where(vals == mx, ids, n_ids), axis=0, keepdims=True)
        hit = ids == idx
        hits.append(hit)
        vals = jnp.where(hit, -jnp.inf, vals)
    return hits


def _router_body(h_ref, rt_ref, rb_ref, pos_ref, gate_ref, meta_ref, *, tm):
    E, NG = N_EXPERTS, N_EXPERT_GROUPS
    per = E // NG
    hh, hl = _split2(h_ref[...])
    rh, rl = _split2(rt_ref[...])
    logits = _dot_nt(rh, hh) + _dot_nt(rh, hl) + _dot_nt(rl, hh)
    scores = jax.nn.sigmoid(logits)
    biased = scores + rb_ref[...]
    i8 = lax.broadcasted_iota(jnp.int32, (per, tm), 0)
    gs = []
    for g in range(NG):
        v = biased[g * per:(g + 1) * per]
        m1 = jnp.max(v, axis=0, keepdims=True)
        a1 = jnp.min(jnp.where(v == m1, i8, per), axis=0, keepdims=True)
        m2 = jnp.max(jnp.where(i8 == a1, -jnp.inf, v), axis=0, keepdims=True)
        gs.append(m1 + m2)
    gs = jnp.concatenate(gs, axis=0)
    gi = lax.broadcasted_iota(jnp.int32, (NG, tm), 0)
    ghits = _top_rows(gs, gi, NG, TOPK_GROUPS)
    gkeep = jnp.zeros((NG, tm), F32)
    for hit in ghits:
        gkeep = jnp.where(hit, 1.0, gkeep)
    ekeep = jnp.concatenate([jnp.broadcast_to(gkeep[g:g + 1], (per, tm)) for g in range(NG)], axis=0)
    ei = lax.broadcasted_iota(jnp.int32, (E, tm), 0)
    hits = _top_rows(jnp.where(ekeep > 0.5, biased, -jnp.inf), ei, E, TOP_K)
    gates = [jnp.sum(jnp.where(hit, scores, 0.0), axis=0, keepdims=True) for hit in hits]
    gsum = gates[0]
    for gk in gates[1:]:
        gsum = gsum + gk
    gates = [gk / gsum * ROUTED_SCALE for gk in gates]

    member = jnp.zeros((E, tm), F32)
    for hit in hits:
        member = jnp.where(hit, 1.0, member)
    cnt_col = jnp.sum(member, axis=1, keepdims=True)
    pad_col = jnp.floor((cnt_col + (ROW_ALIGN - 1)) * (1.0 / ROW_ALIGN)) * ROW_ALIGN
    sub_e = lax.broadcasted_iota(jnp.int32, (E, LANES), 0)
    lane_e = lax.broadcasted_iota(jnp.int32, (E, LANES), 1)
    cnt_row = jnp.sum(jnp.where(sub_e == lane_e, cnt_col, 0.0), axis=0, keepdims=True)
    pad_row = jnp.sum(jnp.where(sub_e == lane_e, pad_col, 0.0), axis=0, keepdims=True)
    off_row = jnp.sum(jnp.where(sub_e < lane_e, pad_col, 0.0), axis=0, keepdims=True)
    off_col = jnp.sum(jnp.where(lane_e < sub_e, pad_row, 0.0), axis=1, keepdims=True)
    r_i = lax.broadcasted_iota(jnp.int32, (RANK_CHUNK, RANK_CHUNK), 0)
    c_i = lax.broadcasted_iota(jnp.int32, (RANK_CHUNK, RANK_CHUNK), 1)
    before = jnp.where(r_i < c_i, 1.0, 0.0).astype(BF)
    running = off_col
    ranks = []
    for c in range(tm // RANK_CHUNK):
        mc = member[:, c * RANK_CHUNK:(c + 1) * RANK_CHUNK]
        ranks.append(_dot(mc.astype(BF), before) + running)
        running = running + jnp.sum(mc, axis=1, keepdims=True)
    slot = jnp.concatenate(ranks, axis=1)
    pos = [jnp.sum(jnp.where(hit, slot, 0.0), axis=0, keepdims=True) for hit in hits]
    zrow = jnp.zeros((1, tm), F32)
    pos_ref[...] = jnp.concatenate(pos + [zrow, zrow], axis=0).astype(jnp.int32)
    gate_ref[...] = jnp.concatenate(gates + [zrow, zrow], axis=0)
    z128 = jnp.zeros((1, LANES), F32)
    meta_ref[0] = jnp.concatenate([off_row, cnt_row] + [z128] * 6, axis=0).astype(jnp.int32)


def _router(h2, router_t, bias_col):
    T = h2.shape[0]
    tm = MOE_TILE
    nt = T // tm
    return pl.pallas_call(
        functools.partial(_router_body, tm=tm),
        grid=(nt,),
        in_specs=[
            pl.BlockSpec((tm, D_MODEL), lambda i: (i, 0)),
            pl.BlockSpec((N_EXPERTS, D_MODEL), lambda i: (0, 0)),
            pl.BlockSpec((N_EXPERTS, 1), lambda i: (0, 0)),
        ],
        out_specs=[
            pl.BlockSpec((8, tm), lambda i: (0, i)),
            pl.BlockSpec((8, tm), lambda i: (0, i)),
            pl.BlockSpec((1, 8, LANES), lambda i: (i, 0, 0)),
        ],
        out_shape=[
            jax.ShapeDtypeStruct((8, T), jnp.int32),
            jax.ShapeDtypeStruct((8, T), F32),
            jax.ShapeDtypeStruct((nt, 8, LANES), jnp.int32),
        ],
        compiler_params=_cparams(("parallel",)),
        name="moe_router",
    )(h2, router_t, bias_col)


EXPERTS_PER_STEP = 4
EXPERT_CHUNK = 256
COMBINE_SUB = 256
POS_STRIDE = 8


def _swiglu(xb, wgu, wd, hidden):
    gu = _dot(xb, wgu)
    a = jax.nn.silu(gu[:, :hidden]) * gu[:, hidden:]
    return _dot(a.astype(BF), wd)


def _moe_body(off_ref, cnt_ref, pos_ref, src_ref, x_ref, gcol_ref, wgu_ref, wd_ref, wsgu_ref, wsd_ref,
              g_ref, b_ref, o_ref, xs_ref, z_ref, *, tm, eb, ch, sub, unroll):
    i = pl.program_id(0)
    j = pl.program_id(1)
    nj = N_EXPERTS // eb

    @pl.when(j == 0)
    def _dispatch():
        zeros = jnp.zeros((ROW_ALIGN * NWORD, LANES), jnp.uint32)

        def pad(e, carry):
            off, cnt = off_ref[i * N_EXPERTS + e], cnt_ref[i * N_EXPERTS + e]
            last = pl.multiple_of((off + cnt // ROW_ALIGN * ROW_ALIGN) * NWORD, ROW_ALIGN * NWORD)
            xs_ref[pl.ds(last, ROW_ALIGN * NWORD), :] = zeros
            return carry

        lax.fori_loop(0, N_EXPERTS, pad, 0)
        end = off_ref[i * N_EXPERTS + N_EXPERTS - 1] + cnt_ref[i * N_EXPERTS + N_EXPERTS - 1]
        end = pl.multiple_of((end + ROW_ALIGN - 1) // ROW_ALIGN * ROW_ALIGN * NWORD, ROW_ALIGN * NWORD)
        xs_ref[pl.ds(end, ch * NWORD), :] = jnp.zeros((ch * NWORD, LANES), jnp.uint32)

        def tok(tb, carry):
            for u in range(unroll):
                t = tb * unroll + u
                slab = src_ref[pl.ds(pl.multiple_of(t * NWORD, NWORD), NWORD), :]
                for k in range(TOP_K):
                    p = pl.multiple_of(pos_ref[t * POS_STRIDE + k], NWORD)
                    xs_ref[pl.ds(p, NWORD), :] = slab
            return carry

        lax.fori_loop(0, tm // unroll, tok, 0)

    def chunks_in(els, offs, c):
        r0s = [pl.multiple_of(off + c * ch, ROW_ALIGN) for off in offs]
        words = [_load_row_words(xs_ref, r0, ch) for r0 in r0s]
        xbs = [jnp.concatenate([_unpack_lo(w).astype(BF) for w in ws]
                               + [_unpack_hi(w).astype(BF) for w in ws], axis=1) for ws in words]
        gus = [_dot(xb, wgu_ref[el]) for xb, el in zip(xbs, els)]
        acts = [(jax.nn.silu(gu[:, :EXPERT_HIDDEN]) * gu[:, EXPERT_HIDDEN:]).astype(BF) for gu in gus]
        ys = [_dot(a, wd_ref[el]) for a, el in zip(acts, els)]
        return list(zip(r0s, words, ys))

    def chunk_out(r0, words, y, c, cnt):
        keep = (c * ch + lax.broadcasted_iota(jnp.int32, (ch, 1), 0)) < cnt
        for cc, packed in enumerate(_pack_row_words(y)):
            xs_ref[pl.ds(r0 * NWORD + cc, ch, stride=NWORD), :] = jnp.where(keep, packed, words[cc])

    @pl.when(j < nj)
    def _experts():
        offs = [off_ref[i * N_EXPERTS + j * eb + el] for el in range(eb)]
        cnts = [cnt_ref[i * N_EXPERTS + j * eb + el] for el in range(eb)]
        firsts = chunks_in(list(range(eb)), offs, 0)
        for el in range(eb):
            chunk_out(*firsts[el], 0, cnts[el])
        for el in range(eb):
            def chunk(c, carry, el=el):
                chunk_out(*chunks_in([el], [offs[el]], c)[0], c, cnts[el])
                return carry

            lax.fori_loop(1, (cnts[el] + ch - 1) // ch, chunk, 0)

    @pl.when(j >= nj)
    def _combine():
        base = (j - nj) * sub

        def tok(tb, carry):
            for u in range(unroll):
                tl = tb * unroll + u
                dst = pl.multiple_of(tl * NWORD, NWORD)
                for k in range(TOP_K):
                    p = pl.multiple_of(pos_ref[(base + tl) * POS_STRIDE + k], NWORD)
                    z_ref[k, pl.ds(dst, NWORD), :] = xs_ref[pl.ds(p, NWORD), :]
            return carry

        lax.fori_loop(0, sub // unroll, tok, 0)
        gcol = gcol_ref[...]
        lo = [jnp.zeros((sub, LANES), F32) for _ in range(NWORD)]
        hi = [jnp.zeros((sub, LANES), F32) for _ in range(NWORD)]
        for k in range(TOP_K):
            gk = gcol[:, k:k + 1]
            for c, w in enumerate(_load_row_words(z_ref.at[k], 0, sub)):
                lo[c] = lo[c] + gk * _unpack_lo(w)
                hi[c] = hi[c] + gk * _unpack_hi(w)
        routed = jnp.concatenate(lo + hi, axis=1)
        x = x_ref[...]
        shared = _swiglu(x.astype(BF), wsgu_ref[...], wsd_ref[...], SHARED_HIDDEN)
        o_ref[...] = _layer_norm(DN_ALPHA * x + routed + shared, g_ref[...], b_ref[...])


def _moe_experts(h2, packed, off, cnt, pos, gcol, wgu, wd, wsgu, wsd, g, b):
    T = h2.shape[0]
    tm, eb, ch, sub = MOE_TILE, EXPERTS_PER_STEP, EXPERT_CHUNK, COMBINE_SUB
    nt, nj, nsub = T // tm, N_EXPERTS // eb, tm // sub
    rows = TOP_K * tm + N_EXPERTS * ROW_ALIGN + ch
    hidden2 = wgu.shape[-1]
    wblk = lambda i, j, *_: (jnp.minimum(j, nj - 1), 0, 0)
    sub_i = lambda i, j: i * nsub + jnp.clip(j - nj, 0, nsub - 1)
    once = pl.Buffered(1)
    return pl.pallas_call(
        functools.partial(_moe_body, tm=tm, eb=eb, ch=ch, sub=sub, unroll=4),
        grid_spec=pltpu.PrefetchScalarGridSpec(
            num_scalar_prefetch=2,
            grid=(nt, nj + nsub),
            in_specs=[
                pl.BlockSpec((tm * POS_STRIDE,), lambda i, j, *_: (i,), memory_space=pltpu.SMEM),
                pl.BlockSpec((tm * NWORD, LANES), lambda i, j, *_: (i, 0), pipeline_mode=once),
                pl.BlockSpec((sub, D_MODEL), lambda i, j, *_: (sub_i(i, j), 0)),
                pl.BlockSpec((sub, 8), lambda i, j, *_: (sub_i(i, j), 0)),
                pl.BlockSpec((eb, D_MODEL, hidden2), wblk),
                pl.BlockSpec((eb, hidden2 // 2, D_MODEL), wblk),
                pl.BlockSpec(wsgu.shape, lambda i, j, *_: (0, 0), pipeline_mode=once),
                pl.BlockSpec(wsd.shape, lambda i, j, *_: (0, 0), pipeline_mode=once),
                pl.BlockSpec((1, D_MODEL), lambda i, j, *_: (0, 0)),
                pl.BlockSpec((1, D_MODEL), lambda i, j, *_: (0, 0)),
            ],
            out_specs=pl.BlockSpec((sub, D_MODEL), lambda i, j, *_: (sub_i(i, j), 0)),
            scratch_shapes=[
                pltpu.VMEM((rows * NWORD, LANES), jnp.uint32),
                pltpu.VMEM((TOP_K, sub * NWORD, LANES), jnp.uint32),
            ],
        ),
        out_shape=jax.ShapeDtypeStruct((T, D_MODEL), F32),
        compiler_params=_cparams(("parallel", "arbitrary"), MOE_VMEM_LIMIT),
        name="moe_experts",
    )(off, cnt, pos, packed, h2, gcol, wgu, wd, wsgu, wsd, g, b)


S5_BATCH = 8
S5_STEPS = 64
S5_SLABS = 4
SLAB_CH = D_MODEL // S5_SLABS
SLAB_ST = SSM_GROUPS * SSM_STATE // S5_SLABS
N_STATE = SSM_GROUPS * SSM_STATE
SCAN_LANES = 512
SCAN_UNROLL = 8


def _s5_disc_body(lre_ref, lim_ref, ldt_ref, bre_ref, bim_ref, are_ref, aim_ref, bbre_ref, bbim_ref):
    lre, lim = lre_ref[...], lim_ref[...]
    step = jnp.exp(ldt_ref[...])
    mag = jnp.exp(lre * step)
    a_re = mag * jnp.cos(lim * step)
    a_im = mag * jnp.sin(lim * step)
    den = lre * lre + lim * lim
    zoh_re = ((a_re - 1.0) * lre + a_im * lim) / den
    zoh_im = (a_im * lre - (a_re - 1.0) * lim) / den
    are_ref[...] = a_re
    aim_ref[...] = a_im
    bbre_ref[...] = zoh_re * bre_ref[...] - zoh_im * bim_ref[...]
    bbim_ref[...] = zoh_re * bim_ref[...] + zoh_im * bre_ref[...]


def _s5_discretize(lambda_re, lambda_im, log_dt, b_re, b_im):
    col = lambda a: a.reshape(N_STATE, 1)
    ldt = jnp.broadcast_to(log_dt[:, None], (SSM_GROUPS, SSM_STATE))
    mat = lambda a: a.reshape(N_STATE, SSM_GROUP)
    c1 = jax.ShapeDtypeStruct((N_STATE, 1), F32)
    c16 = jax.ShapeDtypeStruct((N_STATE, SSM_GROUP), F32)
    return pl.pallas_call(_s5_disc_body, out_shape=[c1, c1, c16, c16], name="s5_discretize")(
        col(lambda_re), col(lambda_im), col(ldt), mat(b_re), mat(b_im))


def _s5_body(x_ref, win_ref, are_ref, aim_ref, bd_ref, cd_ref, dsk_ref, wglu_ref, wout_ref, g_ref, b_ref,
             o_ref, p_ref, xs_ref, pk_ref, hre_ref, him_ref, sre_ref, sim_ref, *, lt):
    nb = S5_BATCH
    nlb = D_MODEL // LANES
    for c in range(nlb):
        for b in range(nb):
            xs_ref[c, pl.ds(b, lt, stride=nb), :] = x_ref[b, :, c * LANES:(c + 1) * LANES]
    x = jnp.concatenate([xs_ref[c] for c in range(nlb)], axis=1)
    u = _dot(x.astype(BF), win_ref[...])
    ub = u.astype(BF)
    for k in range(S5_SLABS):
        bu = _dot(ub[:, k * SLAB_CH:(k + 1) * SLAB_CH], bd_ref[k])
        hre_ref[:, k * SLAB_ST:(k + 1) * SLAB_ST] = bu[:, :SLAB_ST]
        him_ref[:, k * SLAB_ST:(k + 1) * SLAB_ST] = bu[:, SLAB_ST:]

    @pl.when(pl.program_id(1) == 0)
    def _():
        sre_ref[...] = jnp.zeros_like(sre_ref)
        sim_ref[...] = jnp.zeros_like(sim_ref)

    for c in range(N_STATE // SCAN_LANES):
        ls = slice(c * SCAN_LANES, (c + 1) * SCAN_LANES)
        a_re = jnp.broadcast_to(are_ref[:, ls], (nb, SCAN_LANES))
        a_im = jnp.broadcast_to(aim_ref[:, ls], (nb, SCAN_LANES))

        def steps(tb, state, ls=ls, a_re=a_re, a_im=a_im):
            s_re, s_im = state
            for uu in range(SCAN_UNROLL):
                r0 = pl.multiple_of((tb * SCAN_UNROLL + uu) * nb, nb)
                n_re = a_re * s_re - a_im * s_im + hre_ref[pl.ds(r0, nb), ls]
                n_im = a_re * s_im + a_im * s_re + him_ref[pl.ds(r0, nb), ls]
                hre_ref[pl.ds(r0, nb), ls] = n_re
                him_ref[pl.ds(r0, nb), ls] = n_im
                s_re, s_im = n_re, n_im
            return s_re, s_im

        s_re, s_im = lax.fori_loop(0, lt // SCAN_UNROLL, steps, (sre_ref[:, ls], sim_ref[:, ls]))
        sre_ref[:, ls] = s_re
        sim_ref[:, ls] = s_im

    ys = []
    for k in range(S5_SLABS):
        hk = jnp.concatenate([hre_ref[:, k * SLAB_ST:(k + 1) * SLAB_ST].astype(BF),
                              him_ref[:, k * SLAB_ST:(k + 1) * SLAB_ST].astype(BF)], axis=1)
        ys.append(_dot(hk, cd_ref[k]))
    y = jax.nn.gelu(jnp.concatenate(ys, axis=1) + dsk_ref[...] * u)
    y = y * jax.nn.sigmoid(_dot(y.astype(BF), wglu_ref[...]))
    mixed = _dot(y.astype(BF), wout_ref[...])
    res = _layer_norm(DN_ALPHA * x + mixed, g_ref[...], b_ref[...])
    for c in range(nlb):
        xs_ref[c] = res[:, c * LANES:(c + 1) * LANES]
    for c, w in enumerate(_pack_row_words(res)):
        pk_ref[c] = w
    for c in range(nlb):
        for b in range(nb):
            o_ref[b, :, c * LANES:(c + 1) * LANES] = xs_ref[c, pl.ds(b, lt, stride=nb), :]
    for c in range(NWORD):
        for b in range(nb):
            p_ref[b, pl.ds(c, lt, stride=NWORD), :] = pk_ref[c, pl.ds(b, lt, stride=nb), :]


def _mixer1(x3, w_in, lambda_re, lambda_im, b_re, b_im, c_re, c_im, d_skip, log_dt, w_glu, w_out,
            ln_g, ln_b):
    B, S, _ = x3.shape
    lt = S5_STEPS
    rows = S5_BATCH * lt
    a_re, a_im, bb_re, bb_im = _s5_discretize(lambda_re, lambda_im, log_dt, b_re, b_im)
    gps = SSM_GROUPS // S5_SLABS
    eye = jnp.eye(gps, dtype=F32)

    def bdiag(bb):
        b4 = bb.reshape(S5_SLABS, gps, SSM_STATE, SSM_GROUP)
        return jnp.einsum('kgph,gf->kghfp', b4, eye).reshape(S5_SLABS, SLAB_CH, SLAB_ST)

    def cdiag(cc):
        c4 = cc.reshape(S5_SLABS, gps, SSM_GROUP, SSM_STATE)
        return jnp.einsum('kghp,gf->kfpgh', c4, eye).reshape(S5_SLABS, SLAB_ST, SLAB_CH)

    bd = jnp.concatenate([bdiag(bb_re), bdiag(bb_im)], axis=2).astype(BF)
    cd = jnp.concatenate([cdiag(c_re), -cdiag(c_im)], axis=1).astype(BF)
    c2 = lambda shp: pl.BlockSpec(shp, lambda bi, ti: (0,) * len(shp))
    return pl.pallas_call(
        functools.partial(_s5_body, lt=lt),
        grid=(B // S5_BATCH, S // lt),
        in_specs=[
            pl.BlockSpec((S5_BATCH, lt, D_MODEL), lambda bi, ti: (bi, ti, 0)),
            c2((D_MODEL, D_MODEL)), c2((1, N_STATE)), c2((1, N_STATE)),
            c2(bd.shape), c2(cd.shape), c2((1, D_MODEL)),
            c2((D_MODEL, D_MODEL)), c2((D_MODEL, D_MODEL)), c2((1, D_MODEL)), c2((1, D_MODEL)),
        ],
        out_specs=[pl.BlockSpec((S5_BATCH, lt, D_MODEL), lambda bi, ti: (bi, ti, 0)),
                   pl.BlockSpec((S5_BATCH, lt * NWORD, LANES), lambda bi, ti: (bi, ti, 0))],
        out_shape=[jax.ShapeDtypeStruct((B, S, D_MODEL), F32),
                   jax.ShapeDtypeStruct((B, S * NWORD, LANES), jnp.uint32)],
        scratch_shapes=[
            pltpu.VMEM((D_MODEL // LANES, rows, LANES), F32),
            pltpu.VMEM((NWORD, rows, LANES), jnp.uint32),
            pltpu.VMEM((rows, N_STATE), F32),
            pltpu.VMEM((rows, N_STATE), F32),
            pltpu.VMEM((S5_BATCH, N_STATE), F32),
            pltpu.VMEM((S5_BATCH, N_STATE), F32),
        ],
        compiler_params=_cparams(("parallel", "arbitrary")),
        name="s5_mixer",
    )(x3, w_in.astype(BF), a_re.reshape(1, N_STATE), a_im.reshape(1, N_STATE), bd, cd,
      d_skip.reshape(1, D_MODEL), w_glu.astype(BF), w_out.astype(BF), ln_g.reshape(1, -1), ln_b.reshape(1, -1))


def _moe_block(h2, packed, router, router_bias, w_gate, w_up, w_down, sh_gate, sh_up, sh_down, ln_g, ln_b):
    pos, gate, meta = _router(h2, router.T, router_bias.reshape(-1, 1))
    off = meta[:, 0, :N_EXPERTS].reshape(-1)
    cnt = meta[:, 1, :N_EXPERTS].reshape(-1)
    wgu = jnp.concatenate([w_gate, w_up], axis=-1).astype(BF)
    wsgu = jnp.concatenate([sh_gate, sh_up], axis=-1).astype(BF)
    pos = (pos.T * NWORD).reshape(-1)
    return _moe_experts(h2, packed, off, cnt, pos, gate.T, wgu, w_down.astype(BF), wsgu, sh_down.astype(BF),
                        ln_g.reshape(1, -1), ln_b.reshape(1, -1))


def kernel(x, l0_w_in, l0_cmp_pos_k, l0_cmp_pos_v, l0_cmp_k_w1, l0_cmp_k_w2, l0_cmp_v_w1, l0_cmp_v_w2, l0_w_out, l0_ln1_g, l0_ln1_b, l0_router, l0_router_bias, l0_w_gate, l0_w_up, l0_w_down, l0_sh_gate, l0_sh_up, l0_sh_down, l0_ln2_g, l0_ln2_b, l1_w_in, l1_lambda_re, l1_lambda_im, l1_b_re, l1_b_im, l1_c_re, l1_c_im, l1_d, l1_log_dt, l1_w_glu, l1_w_out, l1_ln1_g, l1_ln1_b, l1_router, l1_router_bias, l1_w_gate, l1_w_up, l1_w_down, l1_sh_gate, l1_sh_up, l1_sh_down, l1_ln2_g, l1_ln2_b):
    B, S, D = x.shape
    assert D == D_MODEL and S % 512 == 0 and B % S5_BATCH == 0 and (B * S) % MOE_TILE == 0
    T = B * S
    h, hp = _mixer0(x.reshape(T, D), B, S, l0_w_in, l0_cmp_pos_k, l0_cmp_pos_v, l0_cmp_k_w1, l0_cmp_k_w2,
                    l0_cmp_v_w1, l0_cmp_v_w2, l0_w_out, l0_ln1_g, l0_ln1_b)
    h = _moe_block(h, hp, l0_router, l0_router_bias, l0_w_gate, l0_w_up, l0_w_down, l0_sh_gate, l0_sh_up,
                   l0_sh_down, l0_ln2_g, l0_ln2_b)
    h, hp = _mixer1(h.reshape(B, S, D), l1_w_in, l1_lambda_re, l1_lambda_im, l1_b_re, l1_b_im, l1_c_re,
                    l1_c_im, l1_d, l1_log_dt, l1_w_glu, l1_w_out, l1_ln1_g, l1_ln1_b)
    h = _moe_block(h.reshape(T, D), hp.reshape(T * NWORD, LANES), l1_router, l1_router_bias, l1_w_gate, l1_w_up,
                   l1_w_down, l1_sh_gate, l1_sh_up, l1_sh_down, l1_ln2_g, l1_ln2_b)
    return h.reshape(B, S, D)
```

```python
import functools
import math

import numpy as np
import jax
import jax.numpy as jnp
from jax import lax
from jax.experimental import pallas as pl
from jax.experimental.pallas import tpu as pltpu

F32 = jnp.float32
BF = jnp.bfloat16

D_MODEL = 1024
DEPTH = 2
HEAD_DIM = 64
LANES = 128
SB_HEADS = 8
SB_WIDTH = SB_HEADS * HEAD_DIM
NSA_HEADS = 8
NSA_KV_GROUPS = 2
NSA_REP = NSA_HEADS // NSA_KV_GROUPS
NSA_WIDTH = NSA_HEADS * HEAD_DIM
NSA_N_BRANCH = 3
CMP_LEN = 32
CMP_STRIDE = 16
CMP_HIDDEN = 256
SLC_LEN = 64
SLC_TOPN = 8
SLC_LOCAL = 2
SLC_FORCE_BONUS = 1e4
WINDOW = 512
ROPE_THETA = 500000.0
ROPE_DIM = HEAD_DIM // 4
SSM_GROUP = 16
SSM_GROUPS = D_MODEL // SSM_GROUP
SSM_STATE = 64
N_EXPERTS = 64
N_EXPERT_GROUPS = 8
TOPK_GROUPS = 4
TOP_K = 6
EXPERT_HIDDEN = 256
SHARED_HIDDEN = 256
ROUTED_SCALE = 2.5
DN_ALPHA = (2 * DEPTH) ** 0.25
LN_EPS = 1e-5
NEG_INF = -1e30
ATT_SCALE = HEAD_DIM ** -0.5
LOG2_E = math.log2(math.e)
SB_UNDERFLOW = -104.0
WIN_Q = 128
SB_ROWS = 128

V7X_VMEM_BYTES = 64 * 1024 * 1024
VMEM_LIMIT = V7X_VMEM_BYTES - 8 * 1024 * 1024
MOE_VMEM_LIMIT = V7X_VMEM_BYTES - 4 * 1024 * 1024


def _cparams(sem, vmem=VMEM_LIMIT):
    return pltpu.CompilerParams(dimension_semantics=sem, vmem_limit_bytes=vmem)


def _dot(a, b):
    return jnp.dot(a, b, preferred_element_type=F32)


def _dot_nt(a, b):
    return lax.dot_general(a, b, (((1,), (1,)), ((), ())), preferred_element_type=F32)


def _split2(x):
    hi = x.astype(BF)
    lo = (x - hi.astype(F32)).astype(BF)
    return hi, lo


def _layer_norm(h, g, b):
    mu = jnp.mean(h, axis=-1, keepdims=True)
    d = h - mu
    var = jnp.mean(d * d, axis=-1, keepdims=True)
    return d * lax.rsqrt(var + LN_EPS) * g + b


HALF = D_MODEL // 2
NWORD = HALF // LANES
HI_MASK = 0xFFFF0000


def _pack_pairs(a, b):
    lo = pltpu.bitcast(a.astype(BF).astype(F32), jnp.uint32)
    hi = pltpu.bitcast(b.astype(BF).astype(F32), jnp.uint32)
    return (lo >> 16) | (hi & jnp.uint32(HI_MASK))


def _pack_row_words(h):
    return [_pack_pairs(h[:, c * LANES:(c + 1) * LANES], h[:, HALF + c * LANES:HALF + (c + 1) * LANES])
            for c in range(NWORD)]


def _store_row_words(ref, start, n, h):
    for c, w in enumerate(_pack_row_words(h)):
        ref[pl.ds(start * NWORD + c, n, stride=NWORD), :] = w


def _load_row_words(ref, start, n):
    return [ref[pl.ds(start * NWORD + c, n, stride=NWORD), :] for c in range(NWORD)]


def _unpack_lo(w):
    return pltpu.bitcast(w << 16, F32)


def _unpack_hi(w):
    return pltpu.bitcast(w & jnp.uint32(HI_MASK), F32)


N_PLAIN = 3 * SB_WIDTH + LANES
N_TRANS = 2 * LANES + 32
N_ROPE = NSA_WIDTH + 3 * LANES


def _proj0_body(x_ref, wp_ref, wt_ref, wr_ref, cos_ref, sin_ref,
                qkva_ref, vc_ref, vt_ref, gt_ref, qb_ref, kc_ref, ksw_ref):
    xb = x_ref[...].astype(BF)
    a0 = 3 * SB_WIDTH
    qkva_ref[...] = _dot(xb, wp_ref[:, 0:a0]).astype(BF)
    vc_ref[...] = _dot(xb, wp_ref[:, a0:N_PLAIN]).astype(BF)
    yt = _dot_nt(wt_ref[...], xb)
    vt_ref[0] = yt[0:2 * LANES].astype(BF)
    gt_ref[0] = jax.nn.sigmoid(yt[2 * LANES:N_TRANS])
    y = _dot(xb, wr_ref[...])
    half = ROPE_DIM // 2
    pos = lax.broadcasted_iota(jnp.int32, (1, N_ROPE), 1) % HEAD_DIM
    partner = jnp.where(pos < half, -pltpu.roll(y, N_ROPE - half, 1), pltpu.roll(y, half, 1))
    y = y * cos_ref[...] + partner * sin_ref[...]
    qb_ref[...] = y[:, 0:NSA_WIDTH].astype(BF)
    kc_ref[...] = y[:, NSA_WIDTH:NSA_WIDTH + LANES].astype(BF)
    ksw_ref[...] = y[:, NSA_WIDTH + LANES:N_ROPE].astype(BF)


def _proj0(x2, wp, wt, wr, cos_t, sin_t, batch, seq):
    T = x2.shape[0]
    tm = 512
    nseq = seq // tm
    row = lambda i: (i, 0)
    full = lambda i: (0, 0)
    tab = lambda i: (i % nseq, 0)
    trans = lambda i: (i // nseq, 0, i % nseq)
    return pl.pallas_call(
        _proj0_body,
        grid=(T // tm,),
        in_specs=[
            pl.BlockSpec((tm, D_MODEL), row),
            pl.BlockSpec((D_MODEL, N_PLAIN), full),
            pl.BlockSpec((N_TRANS, D_MODEL), full),
            pl.BlockSpec((D_MODEL, N_ROPE), full),
            pl.BlockSpec((tm, N_ROPE), tab),
            pl.BlockSpec((tm, N_ROPE), tab),
        ],
        out_specs=[
            pl.BlockSpec((tm, 3 * SB_WIDTH), row),
            pl.BlockSpec((tm, LANES), row),
            pl.BlockSpec((1, 2 * LANES, tm), trans),
            pl.BlockSpec((1, N_TRANS - 2 * LANES, tm), trans),
            pl.BlockSpec((tm, NSA_WIDTH), row),
            pl.BlockSpec((tm, LANES), row),
            pl.BlockSpec((tm, 2 * LANES), row),
        ],
        out_shape=[
            jax.ShapeDtypeStruct((T, 3 * SB_WIDTH), BF),
            jax.ShapeDtypeStruct((T, LANES), BF),
            jax.ShapeDtypeStruct((batch, 2 * LANES, seq), BF),
            jax.ShapeDtypeStruct((batch, N_TRANS - 2 * LANES, seq), F32),
            jax.ShapeDtypeStruct((T, NSA_WIDTH), BF),
            jax.ShapeDtypeStruct((T, LANES), BF),
            jax.ShapeDtypeStruct((T, 2 * LANES), BF),
        ],
        compiler_params=_cparams(("parallel",)),
        name="proj0",
    )(x2, wp, wt, wr, cos_t, sin_t)


def _sb_body(q_ref, k_ref, v_ref, o_ref, *, tq):
    i = pl.program_id(2)
    q = q_ref[0]
    lane = lax.broadcasted_iota(jnp.int32, (1, LANES), 1)
    row = lax.broadcasted_iota(jnp.int32, (tq, tq), 0)
    col = lax.broadcasted_iota(jnp.int32, (tq, tq), 1)
    tri = jnp.where(row > col, 1.0, 0.0).astype(BF)
    diag_causal = col < row

    hms = [(lane // HEAD_DIM) == hh for hh in range(2)]
    nchunk = tq // SB_ROWS
    chains = [(hh, rc) for hh in range(2) for rc in range(nchunk)]
    qcs = [jnp.where(hms[hh], q[rc * SB_ROWS:(rc + 1) * SB_ROWS], jnp.zeros((SB_ROWS, LANES), BF))
           for hh, rc in chains]

    def blocks(specs, state):
        nc = len(qcs)
        kvs = []
        for j, _, _ in specs:
            off = pl.multiple_of(j * tq, tq)
            kvs.append((k_ref[0, pl.ds(off, tq), :], v_ref[0, pl.ds(off, tq), :]))
        zs = [[_dot_nt(qc, k) for qc in qcs] for k, _ in kvs]
        mids = []
        for zrow, (_, masks, flag) in zip(zs, specs):
            row = []
            for n, z in enumerate(zrow):
                nl = jnp.maximum(z, 0.0) + jnp.log(1.0 + jnp.exp(-jnp.abs(z)))
                if masks is not None:
                    nl = jnp.where(masks[n], nl, 0.0)
                if flag is not None:
                    nl = jnp.where(flag, nl, 0.0)
                row.append((nl, z, nl.astype(BF)))
            mids.append(row)
        sufs = [[_dot(hi, tri) for _, _, hi in row] for row in mids]
        carries = [state[2 * n] for n in range(nc)]
        ws = []
        for row, srow, (_, masks, flag) in zip(mids, sufs, specs):
            wrow = []
            for n, ((nl, z, _), suffix) in enumerate(zip(row, srow)):
                w = jnp.exp(z - nl - suffix + carries[n])
                if masks is not None:
                    w = jnp.where(masks[n], w, 0.0)
                if flag is not None:
                    w = jnp.where(flag, w, 0.0)
                wrow.append(w.astype(BF))
                carries[n] = carries[n] - jnp.sum(nl, axis=-1, keepdims=True)
            ws.append(wrow)
        new = []
        for n in range(nc):
            acc = state[2 * n + 1]
            for wrow, (_, v) in zip(ws, kvs):
                acc = acc + _dot(wrow[n], v)
            new.extend([carries[n], acc])
        return new

    zc, za = jnp.zeros((SB_ROWS, 1), F32), jnp.zeros((SB_ROWS, LANES), F32)
    diag_masks = [diag_causal[rc * SB_ROWS:(rc + 1) * SB_ROWS] for _, rc in chains]
    state = blocks([(i, diag_masks, None), (jnp.maximum(i - 1, 0), None, i > 0)], [zc, za] * len(chains))

    def live(state):
        top = jnp.max(state[0])
        for c in state[2::2]:
            top = jnp.maximum(top, jnp.max(c))
        return top > SB_UNDERFLOW

    def cond(st):
        return jnp.logical_and(st[0] >= 0, st[1])

    def body(st):
        new = blocks([(st[0], None, None)], st[2])
        return st[0] - 1, live(new), tuple(new)

    _, _, state = lax.while_loop(cond, body, (i - 2, live(state), tuple(state)))
    accs = [jnp.concatenate([state[2 * (hh * nchunk + rc) + 1] for rc in range(nchunk)], axis=0)
            for hh in range(2)]
    o_ref[0] = jnp.where(hms[0], accs[0], accs[1]).astype(BF)


def _sb_attention(qkva3):
    B, S, _ = qkva3.shape
    tq = 256
    npair = SB_WIDTH // LANES
    return pl.pallas_call(
        functools.partial(_sb_body, tq=tq),
        grid=(B, npair, S // tq),
        in_specs=[
            pl.BlockSpec((1, tq, LANES), lambda b, p, i: (b, i, p)),
            pl.BlockSpec((1, S, LANES), lambda b, p, i: (b, 0, npair + p)),
            pl.BlockSpec((1, S, LANES), lambda b, p, i: (b, 0, 2 * npair + p)),
        ],
        out_specs=pl.BlockSpec((1, tq, LANES), lambda b, p, i: (b, i, p)),
        out_shape=jax.ShapeDtypeStruct((B, S, SB_WIDTH), BF),
        compiler_params=_cparams(("parallel", "parallel", "arbitrary")),
        name="sb_attention",
    )(qkva3, qkva3, qkva3)


def _cmp_body(ak_ref, av_ref, posk_ref, posv_ref, w1k_ref, w1kt_ref, w1kb_ref, w2k_ref,
              w1v_ref, w1vt_ref, w1vb_ref, w2v_ref, kc_ref, vc_ref):
    def one(a_ref, pos_ref, w1_ref, w1t_ref, w1b_ref, w2_ref, o_ref):
        a = a_ref[0]
        n = a.shape[0]
        bias = _dot(pos_ref[...], w1_ref[...])[0:1]
        out = jnp.zeros((n, LANES), F32)
        for g in range(NSA_KV_GROUPS):
            top = _dot(a, w1t_ref[g])
            bot = _dot(a, w1b_ref[g])
            h = top + pltpu.roll(bot, n - 1, 0) + bias
            out = out + _dot(jax.nn.gelu(h).astype(BF), w2_ref[g])
        o_ref[0] = out.astype(BF)

    one(ak_ref, posk_ref, w1k_ref, w1kt_ref, w1kb_ref, w2k_ref, kc_ref)
    one(av_ref, posv_ref, w1v_ref, w1vt_ref, w1vb_ref, w2v_ref, vc_ref)


def _compress(ak, av, posk, posv, wk, wv):
    B, n, width = ak.shape
    blk = pl.BlockSpec((1, n, width), lambda b: (b, 0, 0))
    c2 = lambda shp: pl.BlockSpec(shp, lambda b: (0, 0))
    c3 = lambda shp: pl.BlockSpec(shp, lambda b: (0, 0, 0))
    wspecs = [c2((CMP_LEN * HEAD_DIM, CMP_HIDDEN)), c3((2, width, CMP_HIDDEN)),
              c3((2, width, CMP_HIDDEN)), c3((2, CMP_HIDDEN, LANES))]
    out = pl.BlockSpec((1, n, LANES), lambda b: (b, 0, 0))
    return pl.pallas_call(
        _cmp_body,
        grid=(B,),
        in_specs=[blk, blk, c2((8, CMP_LEN * HEAD_DIM)), c2((8, CMP_LEN * HEAD_DIM))] + wspecs + wspecs,
        out_specs=[out, out],
        out_shape=[jax.ShapeDtypeStruct((B, n, LANES), BF)] * 2,
        compiler_params=_cparams(("parallel",)),
        name="nsa_compress",
    )(ak, av, posk, posv, *wk, *wv)


def _nsa_body(q_ref, kc_ref, vct_ref, ks_ref, vst_ref, kw_ref, vwt_ref, gt_ref, ovl_ref, et_ref,
              o_ref, *, tq, tk, nblk):
    i = pl.program_id(1)
    R = NSA_REP
    nchain = NSA_KV_GROUPS * R
    t0 = i * tq
    lane = lax.broadcasted_iota(jnp.int32, (1, LANES), 1)
    q = q_ref[0]
    t_row = t0 + lax.broadcasted_iota(jnp.int32, (1, tq), 1)
    tile = lambda a: jnp.concatenate([a] * nchain, axis=1)
    n_col = lax.broadcasted_iota(jnp.int32, (LANES, 1), 0)
    cmp_bias = tile(jnp.where((CMP_STRIDE * n_col + (CMP_LEN - 1)) <= t_row, 0.0, NEG_INF))
    has_cmp = tile(jnp.where(t_row >= CMP_LEN - 1, 1.0, 0.0))
    nkb = (t0 + tq + tk - 1) // tk
    key_last = (nkb - 1) * tk + lax.broadcasted_iota(jnp.int32, (tk, 1), 0)
    diag_bias = tile(jnp.where(key_last <= t_row, 0.0, NEG_INF))
    blk = lax.broadcasted_iota(jnp.int32, (nblk, 1), 0)
    cur = t_row // SLC_LEN
    valid = blk <= cur
    forced = (blk == 0) | (valid & (blk > cur - SLC_LOCAL))
    ovl = ovl_ref[...]

    qrs = [jnp.where((lane // HEAD_DIM) == gi, q[:, LANES * r:LANES * (r + 1)], jnp.zeros((tq, LANES), BF))
           for gi in range(NSA_KV_GROUPS) for r in range(R)]
    q_all = jnp.concatenate(qrs, axis=0)

    s = _dot_nt(kc_ref[0], q_all) + cmp_bias
    e = jnp.exp2(s - jnp.max(s, axis=0, keepdims=True))
    pc = e / jnp.sum(e, axis=0, keepdims=True) * has_cmp
    o_cmp = _dot(vct_ref[0], pc.astype(BF))

    q_aug = []
    for gi in range(NSA_KV_GROUPS):
        psum = jnp.zeros((LANES, tq), F32)
        for r in range(R):
            c = gi * R + r
            psum = psum + pc[:, c * tq:(c + 1) * tq]
        p1 = psum.astype(BF)
        r1 = psum - p1.astype(F32)
        p2 = r1.astype(BF)
        p3 = (r1 - p2.astype(F32)).astype(BF)
        imp = _dot(ovl, p1) + _dot(ovl, p2) + _dot(ovl, p3)
        imp = jnp.where(forced, imp + SLC_FORCE_BONUS, jnp.where(valid, imp, NEG_INF))
        sel = jnp.zeros((nblk, tq), F32)
        for _ in range(min(SLC_TOPN, nblk)):
            mx = jnp.max(imp, axis=0, keepdims=True)
            idx = jnp.min(jnp.where(imp == mx, blk, nblk), axis=0, keepdims=True)
            hit = blk == idx
            sel = jnp.where(hit, 1.0, sel)
            imp = jnp.where(hit, -jnp.inf, imp)
        sel_t = jnp.concatenate([sel, jnp.zeros((LANES - nblk, tq), F32)], axis=0).T
        sel_bias = ((sel_t - 1.0) * -NEG_INF).astype(BF)
        q_aug.extend(jnp.concatenate([qrs[gi * R + r], sel_bias], axis=1) for r in range(R))
    qa_all = jnp.concatenate(q_aug, axis=0)

    def sel_step(j, state, bias):
        m, l, acc = state
        off = pl.multiple_of(j * tk, tk)
        ka = jnp.concatenate([ks_ref[0, pl.ds(off, tk), :], et_ref[pl.ds(off, tk), :]], axis=1)
        s = _dot_nt(ka, qa_all)
        if bias is not None:
            s = s + bias
        m_new = jnp.maximum(m, jnp.max(s, axis=0, keepdims=True))
        alpha = jnp.exp2(m - m_new)
        p = jnp.exp2(s - m_new)
        l = alpha * l + jnp.sum(p, axis=0, keepdims=True)
        acc = alpha * acc + _dot(vst_ref[0, :, pl.ds(off, tk)], p.astype(BF))
        return m_new, l, acc

    width = nchain * tq
    init = (jnp.full((1, width), NEG_INF, F32), jnp.zeros((1, width), F32), jnp.zeros((LANES, width), F32))
    state = lax.fori_loop(0, nkb - 1, lambda j, st: sel_step(j, st, None), init)
    _, l_s, acc_s = sel_step(nkb - 1, state, diag_bias)
    o_sel = acc_s / l_s

    wkeys = WINDOW + WIN_Q
    o_parts = []
    for h in range(tq // WIN_Q):
        th = t_row[:, h * WIN_Q:(h + 1) * WIN_Q]
        woff = pl.multiple_of(jnp.maximum(t0 + h * WIN_Q - WINDOW, 0), WIN_Q)
        kp = woff + lax.broadcasted_iota(jnp.int32, (wkeys, 1), 0)
        bias = tile(jnp.where(kp <= th, jnp.where(kp > th - WINDOW, 0.0, NEG_INF), NEG_INF))
        qh = jnp.concatenate([qr[h * WIN_Q:(h + 1) * WIN_Q] for qr in qrs], axis=0)
        s = _dot_nt(kw_ref[0, pl.ds(woff, wkeys), :], qh) + bias
        p = jnp.exp2(s - jnp.max(s, axis=0, keepdims=True))
        o_parts.append(_dot(vwt_ref[0, :, pl.ds(woff, wkeys)], p.astype(BF)) / jnp.sum(p, axis=0, keepdims=True))
    o_win = jnp.concatenate([o[:, c * WIN_Q:(c + 1) * WIN_Q] for c in range(nchain) for o in o_parts], axis=1)

    gt = gt_ref[0]
    sub = lax.broadcasted_iota(jnp.int32, (LANES, 1), 0)
    g0 = (sub // HEAD_DIM) == 0
    for r in range(R):
        out = jnp.zeros((LANES, tq), F32)
        for br, o_br in enumerate((o_cmp, o_sel, o_win)):
            rows = [(gi * R + r) * NSA_N_BRANCH + br for gi in range(NSA_KV_GROUPS)]
            gate = jnp.where(g0, gt[rows[0]:rows[0] + 1], gt[rows[1]:rows[1] + 1])
            both = jnp.where(g0, o_br[:, r * tq:(r + 1) * tq], o_br[:, (R + r) * tq:(R + r + 1) * tq])
            out = out + gate * both
        o_ref[0, r * LANES:(r + 1) * LANES, :] = out.astype(BF)


def _nsa_attention(qb3, kc, vct, ksw3, vsw_t, gt, ovl, expand):
    B, S, _ = qb3.shape
    tq, tk = 512, 512
    n = kc.shape[1]
    nblk = ovl.shape[0]
    ngate = gt.shape[1]
    assert S >= WINDOW + WIN_Q and tq % WIN_Q == 0 and S % tk == 0 and nblk % 8 == 0 and n == LANES
    tok = lambda c: pl.BlockSpec((1, S, LANES), lambda b, i, c=c: (b, 0, c))
    tr = lambda c: pl.BlockSpec((1, LANES, S), lambda b, i, c=c: (b, c, 0))
    return pl.pallas_call(
        functools.partial(_nsa_body, tq=tq, tk=tk, nblk=nblk),
        grid=(B, S // tq),
        in_specs=[
            pl.BlockSpec((1, tq, NSA_WIDTH), lambda b, i: (b, i, 0)),
            pl.BlockSpec((1, n, LANES), lambda b, i: (b, 0, 0)),
            pl.BlockSpec((1, LANES, n), lambda b, i: (b, 0, 0)),
            tok(0), tr(0), tok(1), tr(1),
            pl.BlockSpec((1, ngate, tq), lambda b, i: (b, 0, i)),
            pl.BlockSpec((nblk, LANES), lambda b, i: (0, 0)),
            pl.BlockSpec((S, LANES), lambda b, i: (0, 0)),
        ],
        out_specs=pl.BlockSpec((1, NSA_WIDTH, tq), lambda b, i: (b, 0, i)),
        out_shape=jax.ShapeDtypeStruct((B, NSA_WIDTH, S), BF),
        compiler_params=_cparams(("parallel", "arbitrary")),
        name="nsa_attention",
    )(qb3, kc, vct, ksw3, vsw_t, ksw3, vsw_t, gt, ovl, expand)


def _outproj_ln_body(oa_ref, obt_ref, x_ref, wa_ref, wb_ref, g_ref, b_ref, o_ref, p_ref):
    yb = lax.dot_general(obt_ref[0], wb_ref[...], (((0,), (0,)), ((), ())), preferred_element_type=F32)
    y = _dot(oa_ref[...], wa_ref[...]) + yb
    res = _layer_norm(DN_ALPHA * x_ref[...] + y, g_ref[...], b_ref[...])
    o_ref[...] = res
    _store_row_words(p_ref, 0, res.shape[0], res)


def _outproj_ln(oa, obt, x2, wa, wb, g, b):
    T = x2.shape[0]
    tm = 1024
    nseq = obt.shape[2] // tm
    row = lambda i: (i, 0)
    full = lambda i: (0, 0)
    return pl.pallas_call(
        _outproj_ln_body,
        grid=(T // tm,),
        in_specs=[
            pl.BlockSpec((tm, oa.shape[1]), row),
            pl.BlockSpec((1, obt.shape[1], tm), lambda i: (i // nseq, 0, i % nseq)),
            pl.BlockSpec((tm, D_MODEL), row),
            pl.BlockSpec(wa.shape, full),
            pl.BlockSpec(wb.shape, full),
            pl.BlockSpec((1, D_MODEL), full),
            pl.BlockSpec((1, D_MODEL), full),
        ],
        out_specs=[pl.BlockSpec((tm, D_MODEL), row),
                   pl.BlockSpec((tm * NWORD, LANES), row)],
        out_shape=[jax.ShapeDtypeStruct((T, D_MODEL), F32),
                   jax.ShapeDtypeStruct((T * NWORD, LANES), jnp.uint32)],
        compiler_params=_cparams(("parallel",)),
        name="outproj_ln",
    )(oa, obt, x2, wa, wb, g, b)


def _rope_tables(seq):
    inv = jnp.power(ROPE_THETA, -jnp.arange(0, ROPE_DIM, 2, dtype=F32) / ROPE_DIM)
    ang = jnp.arange(seq, dtype=F32)[:, None] * inv[None, :]
    half = ROPE_DIM // 2
    rest = HEAD_DIM - ROPE_DIM
    cos_h = jnp.concatenate([jnp.cos(ang), jnp.cos(ang), jnp.ones((seq, rest), F32)], axis=1)
    sin_h = jnp.concatenate([jnp.sin(ang), jnp.sin(ang), jnp.zeros((seq, rest), F32)], axis=1)
    reps = N_ROPE // HEAD_DIM
    del half
    return jnp.tile(cos_h, (1, reps)), jnp.tile(sin_h, (1, reps))


def _nsa_head_perm():
    cols = []
    for r in range(NSA_REP):
        for g in range(NSA_KV_GROUPS):
            h = g * NSA_REP + r
            cols.extend(range(h * HEAD_DIM, (h + 1) * HEAD_DIM))
    return np.asarray(cols)


def _mixer0(x2, batch, seq, w_in, cmp_pos_k, cmp_pos_v, cmp_k_w1, cmp_k_w2, cmp_v_w1, cmp_v_w2,
            w_out, ln_g, ln_b):
    T = x2.shape[0]
    sizes = [SB_WIDTH] * 3 + [NSA_WIDTH] + [NSA_KV_GROUPS * HEAD_DIM] * 6 + [NSA_HEADS * NSA_N_BRANCH]
    offs = np.concatenate([[0], np.cumsum(sizes)])
    col = lambda j: w_in[:, offs[j]:offs[j + 1]]
    perm = _nsa_head_perm()
    ngate = sizes[-1]
    wp = jnp.concatenate([col(0) * ATT_SCALE, col(1), col(2), col(5)], axis=1)
    wt = jnp.concatenate([col(7), col(9), jnp.pad(col(10), ((0, 0), (0, N_TRANS - 2 * LANES - ngate)))], axis=1).T
    wr = jnp.concatenate([col(3)[:, perm] * (ATT_SCALE * LOG2_E), col(4), col(6), col(8)], axis=1)
    cos_t, sin_t = _rope_tables(seq)
    qkva, vc_tok, vsw_t, gt, qb, kc_tok, ksw = _proj0(x2, wp.astype(BF), wt.astype(BF), wr.astype(BF), cos_t, sin_t,
                                                      batch, seq)

    r3 = lambda a: a.reshape(batch, seq, a.shape[-1])
    o_a = _sb_attention(r3(qkva))

    ncmp = seq // CMP_STRIDE
    ak = kc_tok.reshape(batch, ncmp, CMP_STRIDE * LANES)
    av = vc_tok.reshape(batch, ncmp, CMP_STRIDE * LANES)

    def cmp_weights(w1, w2):
        w1r = w1.reshape(2, CMP_STRIDE, HEAD_DIM, CMP_HIDDEN)
        tops, bots, w2s = [], [], []
        for g in range(NSA_KV_GROUPS):
            ext = jnp.zeros((2, CMP_STRIDE, NSA_KV_GROUPS, HEAD_DIM, CMP_HIDDEN), F32).at[:, :, g].set(w1r)
            ext = ext.reshape(2, CMP_STRIDE * LANES, CMP_HIDDEN)
            tops.append(ext[0])
            bots.append(ext[1])
            w2s.append(jnp.zeros((CMP_HIDDEN, LANES), F32).at[:, g * HEAD_DIM:(g + 1) * HEAD_DIM].set(w2))
        return (w1.astype(BF), jnp.stack(tops).astype(BF), jnp.stack(bots).astype(BF),
                jnp.stack(w2s).astype(BF))

    posk = jnp.broadcast_to(cmp_pos_k.reshape(1, -1), (8, CMP_LEN * HEAD_DIM)).astype(BF)
    posv = jnp.broadcast_to(cmp_pos_v.reshape(1, -1), (8, CMP_LEN * HEAD_DIM)).astype(BF)
    kc, vc = _compress(ak, av, posk, posv, cmp_weights(cmp_k_w1, cmp_k_w2), cmp_weights(cmp_v_w1, cmp_v_w2))
    assert ncmp <= LANES
    if ncmp < LANES:
        kc = jnp.pad(kc, ((0, 0), (0, LANES - ncmp), (0, 0)))
        vc = jnp.pad(vc, ((0, 0), (0, LANES - ncmp), (0, 0)))

    n_slc = seq // SLC_LEN
    cmp_start = np.arange(ncmp) * CMP_STRIDE
    slc_start = np.arange(n_slc) * SLC_LEN
    ovl = ((cmp_start[None, :] <= slc_start[:, None] + SLC_LEN - 1)
           & (cmp_start[None, :] + CMP_LEN - 1 >= slc_start[:, None])).astype(np.float32)
    ovl = np.pad(ovl, ((0, 0), (0, LANES - ncmp))) if ncmp < LANES else ovl
    expand = (np.arange(seq)[:, None] // SLC_LEN == np.arange(LANES)[None, :]).astype(np.float32)
    o_bt = _nsa_attention(r3(qb), kc, jnp.swapaxes(vc, 1, 2), r3(ksw), vsw_t, gt,
                          jnp.asarray(ovl, BF), jnp.asarray(expand, BF))

    wa = w_out[:SB_WIDTH].astype(BF)
    wb = w_out[SB_WIDTH:][perm].astype(BF)
    return _outproj_ln(o_a.reshape(T, SB_WIDTH), o_bt, x2, wa, wb,
                       ln_g.reshape(1, -1), ln_b.reshape(1, -1))


MOE_TILE = 2048
ROW_ALIGN = 8
RANK_CHUNK = 256


def _top_rows(vals, ids, n_ids, count):
    hits = []
    for _ in range(count):
        mx = jnp.max(vals, axis=0, keepdims=True)
        idx = jnp.min(jnp.where(vals == mx, ids, n_ids), axis=0, keepdims=True)
        hit = ids == idx
        hits.append(hit)
        vals = jnp.where(hit, -jnp.inf, vals)
    return hits


def _router_body(h_ref, rt_ref, rb_ref, pos_ref, gate_ref, meta_ref, *, tm):
    E, NG = N_EXPERTS, N_EXPERT_GROUPS
    per = E // NG
    hh, hl = _split2(h_ref[...])
    rh, rl = _split2(rt_ref[...])
    logits = _dot_nt(rh, hh) + _dot_nt(rh, hl) + _dot_nt(rl, hh)
    scores = jax.nn.sigmoid(logits)
    biased = scores + rb_ref[...]
    i8 = lax.broadcasted_iota(jnp.int32, (per, tm), 0)
    gs = []
    for g in range(NG):
        v = biased[g * per:(g + 1) * per]
        m1 = jnp.max(v, axis=0, keepdims=True)
        a1 = jnp.min(jnp.where(v == m1, i8, per), axis=0, keepdims=True)
        m2 = jnp.max(jnp.where(i8 == a1, -jnp.inf, v), axis=0, keepdims=True)
        gs.append(m1 + m2)
    gs = jnp.concatenate(gs, axis=0)
    gi = lax.broadcasted_iota(jnp.int32, (NG, tm), 0)
    ghits = _top_rows(gs, gi, NG, TOPK_GROUPS)
    gkeep = jnp.zeros((NG, tm), F32)
    for hit in ghits:
        gkeep = jnp.where(hit, 1.0, gkeep)
    ekeep = jnp.concatenate([jnp.broadcast_to(gkeep[g:g + 1], (per, tm)) for g in range(NG)], axis=0)
    ei = lax.broadcasted_iota(jnp.int32, (E, tm), 0)
    hits = _top_rows(jnp.where(ekeep > 0.5, biased, -jnp.inf), ei, E, TOP_K)
    gates = [jnp.sum(jnp.where(hit, scores, 0.0), axis=0, keepdims=True) for hit in hits]
    gsum = gates[0]
    for gk in gates[1:]:
        gsum = gsum + gk
    gates = [gk / gsum * ROUTED_SCALE for gk in gates]

    member = jnp.zeros((E, tm), F32)
    for hit in hits:
        member = jnp.where(hit, 1.0, member)
    cnt_col = jnp.sum(member, axis=1, keepdims=True)
    pad_col = jnp.floor((cnt_col + (ROW_ALIGN - 1)) * (1.0 / ROW_ALIGN)) * ROW_ALIGN
    sub_e = lax.broadcasted_iota(jnp.int32, (E, LANES), 0)
    lane_e = lax.broadcasted_iota(jnp.int32, (E, LANES), 1)
    cnt_row = jnp.sum(jnp.where(sub_e == lane_e, cnt_col, 0.0), axis=0, keepdims=True)
    pad_row = jnp.sum(jnp.where(sub_e == lane_e, pad_col, 0.0), axis=0, keepdims=True)
    off_row = jnp.sum(jnp.where(sub_e < lane_e, pad_col, 0.0), axis=0, keepdims=True)
    off_col = jnp.sum(jnp.where(lane_e < sub_e, pad_row, 0.0), axis=1, keepdims=True)
    r_i = lax.broadcasted_iota(jnp.int32, (RANK_CHUNK, RANK_CHUNK), 0)
    c_i = lax.broadcasted_iota(jnp.int32, (RANK_CHUNK, RANK_CHUNK), 1)
    before = jnp.where(r_i < c_i, 1.0, 0.0).astype(BF)
    running = off_col
    ranks = []
    for c in range(tm // RANK_CHUNK):
        mc = member[:, c * RANK_CHUNK:(c + 1) * RANK_CHUNK]
        ranks.append(_dot(mc.astype(BF), before) + running)
        running = running + jnp.sum(mc, axis=1, keepdims=True)
    slot = jnp.concatenate(ranks, axis=1)
    pos = [jnp.sum(jnp.where(hit, slot, 0.0), axis=0, keepdims=True) for hit in hits]
    zrow = jnp.zeros((1, tm), F32)
    pos_ref[...] = jnp.concatenate(pos + [zrow, zrow], axis=0).astype(jnp.int32)
    gate_ref[...] = jnp.concatenate(gates + [zrow, zrow], axis=0)
    z128 = jnp.zeros((1, LANES), F32)
    meta_ref[0] = jnp.concatenate([off_row, cnt_row] + [z128] * 6, axis=0).astype(jnp.int32)


def _router(h2, router_t, bias_col):
    T = h2.shape[0]
    tm = MOE_TILE
    nt = T // tm
    return pl.pallas_call(
        functools.partial(_router_body, tm=tm),
        grid=(nt,),
        in_specs=[
            pl.BlockSpec((tm, D_MODEL), lambda i: (i, 0)),
            pl.BlockSpec((N_EXPERTS, D_MODEL), lambda i: (0, 0)),
            pl.BlockSpec((N_EXPERTS, 1), lambda i: (0, 0)),
        ],
        out_specs=[
            pl.BlockSpec((8, tm), lambda i: (0, i)),
            pl.BlockSpec((8, tm), lambda i: (0, i)),
            pl.BlockSpec((1, 8, LANES), lambda i: (i, 0, 0)),
        ],
        out_shape=[
            jax.ShapeDtypeStruct((8, T), jnp.int32),
            jax.ShapeDtypeStruct((8, T), F32),
            jax.ShapeDtypeStruct((nt, 8, LANES), jnp.int32),
        ],
        compiler_params=_cparams(("parallel",)),
        name="moe_router",
    )(h2, router_t, bias_col)


EXPERTS_PER_STEP = 4
EXPERT_CHUNK = 256
COMBINE_SUB = 256
POS_STRIDE = 8


def _swiglu(xb, wgu, wd, hidden):
    gu = _dot(xb, wgu)
    a = jax.nn.silu(gu[:, :hidden]) * gu[:, hidden:]
    return _dot(a.astype(BF), wd)


def _moe_body(off_ref, cnt_ref, pos_ref, src_ref, x_ref, gcol_ref, wgu_ref, wd_ref, wsgu_ref, wsd_ref,
              g_ref, b_ref, o_ref, xs_ref, z_ref, *, tm, eb, ch, sub, unroll):
    i = pl.program_id(0)
    j = pl.program_id(1)
    nj = N_EXPERTS // eb

    @pl.when(j == 0)
    def _dispatch():
        zeros = jnp.zeros((ROW_ALIGN * NWORD, LANES), jnp.uint32)

        def pad(e, carry):
            off, cnt = off_ref[i * N_EXPERTS + e], cnt_ref[i * N_EXPERTS + e]
            last = pl.multiple_of((off + cnt // ROW_ALIGN * ROW_ALIGN) * NWORD, ROW_ALIGN * NWORD)
            xs_ref[pl.ds(last, ROW_ALIGN * NWORD), :] = zeros
            return carry

        lax.fori_loop(0, N_EXPERTS, pad, 0)
        end = off_ref[i * N_EXPERTS + N_EXPERTS - 1] + cnt_ref[i * N_EXPERTS + N_EXPERTS - 1]
        end = pl.multiple_of((end + ROW_ALIGN - 1) // ROW_ALIGN * ROW_ALIGN * NWORD, ROW_ALIGN * NWORD)
        xs_ref[pl.ds(end, ch * NWORD), :] = jnp.zeros((ch * NWORD, LANES), jnp.uint32)

        def tok(tb, carry):
            for u in range(unroll):
                t = tb * unroll + u
                slab = src_ref[pl.ds(pl.multiple_of(t * NWORD, NWORD), NWORD), :]
                for k in range(TOP_K):
                    p = pl.multiple_of(pos_ref[t * POS_STRIDE + k], NWORD)
                    xs_ref[pl.ds(p, NWORD), :] = slab
            return carry

        lax.fori_loop(0, tm // unroll, tok, 0)

    def chunks_in(els, offs, c):
        r0s = [pl.multiple_of(off + c * ch, ROW_ALIGN) for off in offs]
        words = [_load_row_words(xs_ref, r0, ch) for r0 in r0s]
        xbs = [jnp.concatenate([_unpack_lo(w).astype(BF) for w in ws]
                               + [_unpack_hi(w).astype(BF) for w in ws], axis=1) for ws in words]
        gus = [_dot(xb, wgu_ref[el]) for xb, el in zip(xbs, els)]
        acts = [(jax.nn.silu(gu[:, :EXPERT_HIDDEN]) * gu[:, EXPERT_HIDDEN:]).astype(BF) for gu in gus]
        ys = [_dot(a, wd_ref[el]) for a, el in zip(acts, els)]
        return list(zip(r0s, words, ys))

    def chunk_out(r0, words, y, c, cnt):
        keep = (c * ch + lax.broadcasted_iota(jnp.int32, (ch, 1), 0)) < cnt
        for cc, packed in enumerate(_pack_row_words(y)):
            xs_ref[pl.ds(r0 * NWORD + cc, ch, stride=NWORD), :] = jnp.where(keep, packed, words[cc])

    @pl.when(j < nj)
    def _experts():
        offs = [off_ref[i * N_EXPERTS + j * eb + el] for el in range(eb)]
        cnts = [cnt_ref[i * N_EXPERTS + j * eb + el] for el in range(eb)]
        firsts = chunks_in(list(range(eb)), offs, 0)
        for el in range(eb):
            chunk_out(*firsts[el], 0, cnts[el])
        for el in range(eb):
            def chunk(c, carry, el=el):
                chunk_out(*chunks_in([el], [offs[el]], c)[0], c, cnts[el])
                return carry

            lax.fori_loop(1, (cnts[el] + ch - 1) // ch, chunk, 0)

    @pl.when(j >= nj)
    def _combine():
        base = (j - nj) * sub

        def tok(tb, carry):
            for u in range(unroll):
                tl = tb * unroll + u
                dst = pl.multiple_of(tl * NWORD, NWORD)
                for k in range(TOP_K):
                    p = pl.multiple_of(pos_ref[(base + tl) * POS_STRIDE + k], NWORD)
                    z_ref[k, pl.ds(dst, NWORD), :] = xs_ref[pl.ds(p, NWORD), :]
            return carry

        lax.fori_loop(0, sub // unroll, tok, 0)
        gcol = gcol_ref[...]
        lo = [jnp.zeros((sub, LANES), F32) for _ in range(NWORD)]
        hi = [jnp.zeros((sub, LANES), F32) for _ in range(NWORD)]
        for k in range(TOP_K):
            gk = gcol[:, k:k + 1]
            for c, w in enumerate(_load_row_words(z_ref.at[k], 0, sub)):
                lo[c] = lo[c] + gk * _unpack_lo(w)
                hi[c] = hi[c] + gk * _unpack_hi(w)
        routed = jnp.concatenate(lo + hi, axis=1)
        x = x_ref[...]
        shared = _swiglu(x.astype(BF), wsgu_ref[...], wsd_ref[...], SHARED_HIDDEN)
        o_ref[...] = _layer_norm(DN_ALPHA * x + routed + shared, g_ref[...], b_ref[...])


def _moe_experts(h2, packed, off, cnt, pos, gcol, wgu, wd, wsgu, wsd, g, b):
    T = h2.shape[0]
    tm, eb, ch, sub = MOE_TILE, EXPERTS_PER_STEP, EXPERT_CHUNK, COMBINE_SUB
    nt, nj, nsub = T // tm, N_EXPERTS // eb, tm // sub
    rows = TOP_K * tm + N_EXPERTS * ROW_ALIGN + ch
    hidden2 = wgu.shape[-1]
    wblk = lambda i, j, *_: (jnp.minimum(j, nj - 1), 0, 0)
    sub_i = lambda i, j: i * nsub + jnp.clip(j - nj, 0, nsub - 1)
    once = pl.Buffered(1)
    return pl.pallas_call(
        functools.partial(_moe_body, tm=tm, eb=eb, ch=ch, sub=sub, unroll=4),
        grid_spec=pltpu.PrefetchScalarGridSpec(
            num_scalar_prefetch=2,
            grid=(nt, nj + nsub),
            in_specs=[
                pl.BlockSpec((tm * POS_STRIDE,), lambda i, j, *_: (i,), memory_space=pltpu.SMEM),
                pl.BlockSpec((tm * NWORD, LANES), lambda i, j, *_: (i, 0), pipeline_mode=once),
                pl.BlockSpec((sub, D_MODEL), lambda i, j, *_: (sub_i(i, j), 0)),
                pl.BlockSpec((sub, 8), lambda i, j, *_: (sub_i(i, j), 0)),
                pl.BlockSpec((eb, D_MODEL, hidden2), wblk),
                pl.BlockSpec((eb, hidden2 // 2, D_MODEL), wblk),
                pl.BlockSpec(wsgu.shape, lambda i, j, *_: (0, 0), pipeline_mode=once),
                pl.BlockSpec(wsd.shape, lambda i, j, *_: (0, 0), pipeline_mode=once),
                pl.BlockSpec((1, D_MODEL), lambda i, j, *_: (0, 0)),
                pl.BlockSpec((1, D_MODEL), lambda i, j, *_: (0, 0)),
            ],
            out_specs=pl.BlockSpec((sub, D_MODEL), lambda i, j, *_: (sub_i(i, j), 0)),
            scratch_shapes=[
                pltpu.VMEM((rows * NWORD, LANES), jnp.uint32),
                pltpu.VMEM((TOP_K, sub * NWORD, LANES), jnp.uint32),
            ],
        ),
        out_shape=jax.ShapeDtypeStruct((T, D_MODEL), F32),
        compiler_params=_cparams(("parallel", "arbitrary"), MOE_VMEM_LIMIT),
        name="moe_experts",
    )(off, cnt, pos, packed, h2, gcol, wgu, wd, wsgu, wsd, g, b)


S5_BATCH = 8
S5_STEPS = 64
S5_SLABS = 4
SLAB_CH = D_MODEL // S5_SLABS
SLAB_ST = SSM_GROUPS * SSM_STATE // S5_SLABS
N_STATE = SSM_GROUPS * SSM_STATE
SCAN_LANES = 512
SCAN_UNROLL = 8


def _s5_disc_body(lre_ref, lim_ref, ldt_ref, bre_ref, bim_ref, are_ref, aim_ref, bbre_ref, bbim_ref):
    lre, lim = lre_ref[...], lim_ref[...]
    step = jnp.exp(ldt_ref[...])
    mag = jnp.exp(lre * step)
    a_re = mag * jnp.cos(lim * step)
    a_im = mag * jnp.sin(lim * step)
    den = lre * lre + lim * lim
    zoh_re = ((a_re - 1.0) * lre + a_im * lim) / den
    zoh_im = (a_im * lre - (a_re - 1.0) * lim) / den
    are_ref[...] = a_re
    aim_ref[...] = a_im
    bbre_ref[...] = zoh_re * bre_ref[...] - zoh_im * bim_ref[...]
    bbim_ref[...] = zoh_re * bim_ref[...] + zoh_im * bre_ref[...]


def _s5_discretize(lambda_re, lambda_im, log_dt, b_re, b_im):
    col = lambda a: a.reshape(N_STATE, 1)
    ldt = jnp.broadcast_to(log_dt[:, None], (SSM_GROUPS, SSM_STATE))
    mat = lambda a: a.reshape(N_STATE, SSM_GROUP)
    c1 = jax.ShapeDtypeStruct((N_STATE, 1), F32)
    c16 = jax.ShapeDtypeStruct((N_STATE, SSM_GROUP), F32)
    return pl.pallas_call(_s5_disc_body, out_shape=[c1, c1, c16, c16], name="s5_discretize")(
        col(lambda_re), col(lambda_im), col(ldt), mat(b_re), mat(b_im))


def _s5_body(x_ref, win_ref, are_ref, aim_ref, bd_ref, cd_ref, dsk_ref, wglu_ref, wout_ref, g_ref, b_ref,
             o_ref, p_ref, xs_ref, pk_ref, hre_ref, him_ref, sre_ref, sim_ref, *, lt):
    nb = S5_BATCH
    nlb = D_MODEL // LANES
    for c in range(nlb):
        for b in range(nb):
            xs_ref[c, pl.ds(b, lt, stride=nb), :] = x_ref[b, :, c * LANES:(c + 1) * LANES]
    x = jnp.concatenate([xs_ref[c] for c in range(nlb)], axis=1)
    u = _dot(x.astype(BF), win_ref[...])
    ub = u.astype(BF)
    for k in range(S5_SLABS):
        bu = _dot(ub[:, k * SLAB_CH:(k + 1) * SLAB_CH], bd_ref[k])
        hre_ref[:, k * SLAB_ST:(k + 1) * SLAB_ST] = bu[:, :SLAB_ST]
        him_ref[:, k * SLAB_ST:(k + 1) * SLAB_ST] = bu[:, SLAB_ST:]

    @pl.when(pl.program_id(1) == 0)
    def _():
        sre_ref[...] = jnp.zeros_like(sre_ref)
        sim_ref[...] = jnp.zeros_like(sim_ref)

    for c in range(N_STATE // SCAN_LANES):
        ls = slice(c * SCAN_LANES, (c + 1) * SCAN_LANES)
        a_re = jnp.broadcast_to(are_ref[:, ls], (nb, SCAN_LANES))
        a_im = jnp.broadcast_to(aim_ref[:, ls], (nb, SCAN_LANES))

        def steps(tb, state, ls=ls, a_re=a_re, a_im=a_im):
            s_re, s_im = state
            for uu in range(SCAN_UNROLL):
                r0 = pl.multiple_of((tb * SCAN_UNROLL + uu) * nb, nb)
                n_re = a_re * s_re - a_im * s_im + hre_ref[pl.ds(r0, nb), ls]
                n_im = a_re * s_im + a_im * s_re + him_ref[pl.ds(r0, nb), ls]
                hre_ref[pl.ds(r0, nb), ls] = n_re
                him_ref[pl.ds(r0, nb), ls] = n_im
                s_re, s_im = n_re, n_im
            return s_re, s_im

        s_re, s_im = lax.fori_loop(0, lt // SCAN_UNROLL, steps, (sre_ref[:, ls], sim_ref[:, ls]))
        sre_ref[:, ls] = s_re
        sim_ref[:, ls] = s_im

    ys = []
    for k in range(S5_SLABS):
        hk = jnp.concatenate([hre_ref[:, k * SLAB_ST:(k + 1) * SLAB_ST].astype(BF),
                              him_ref[:, k * SLAB_ST:(k + 1) * SLAB_ST].astype(BF)], axis=1)
        ys.append(_dot(hk, cd_ref[k]))
    y = jax.nn.gelu(jnp.concatenate(ys, axis=1) + dsk_ref[...] * u)
    y = y * jax.nn.sigmoid(_dot(y.astype(BF), wglu_ref[...]))
    mixed = _dot(y.astype(BF), wout_ref[...])
    res = _layer_norm(DN_ALPHA * x + mixed, g_ref[...], b_ref[...])
    for c in range(nlb):
        xs_ref[c] = res[:, c * LANES:(c + 1) * LANES]
    for c, w in enumerate(_pack_row_words(res)):
        pk_ref[c] = w
    for c in range(nlb):
        for b in range(nb):
            o_ref[b, :, c * LANES:(c + 1) * LANES] = xs_ref[c, pl.ds(b, lt, stride=nb), :]
    for c in range(NWORD):
        for b in range(nb):
            p_ref[b, pl.ds(c, lt, stride=NWORD), :] = pk_ref[c, pl.ds(b, lt, stride=nb), :]


def _mixer1(x3, w_in, lambda_re, lambda_im, b_re, b_im, c_re, c_im, d_skip, log_dt, w_glu, w_out,
            ln_g, ln_b):
    B, S, _ = x3.shape
    lt = S5_STEPS
    rows = S5_BATCH * lt
    a_re, a_im, bb_re, bb_im = _s5_discretize(lambda_re, lambda_im, log_dt, b_re, b_im)
    gps = SSM_GROUPS // S5_SLABS
    eye = jnp.eye(gps, dtype=F32)

    def bdiag(bb):
        b4 = bb.reshape(S5_SLABS, gps, SSM_STATE, SSM_GROUP)
        return jnp.einsum('kgph,gf->kghfp', b4, eye).reshape(S5_SLABS, SLAB_CH, SLAB_ST)

    def cdiag(cc):
        c4 = cc.reshape(S5_SLABS, gps, SSM_GROUP, SSM_STATE)
        return jnp.einsum('kghp,gf->kfpgh', c4, eye).reshape(S5_SLABS, SLAB_ST, SLAB_CH)

    bd = jnp.concatenate([bdiag(bb_re), bdiag(bb_im)], axis=2).astype(BF)
    cd = jnp.concatenate([cdiag(c_re), -cdiag(c_im)], axis=1).astype(BF)
    c2 = lambda shp: pl.BlockSpec(shp, lambda bi, ti: (0,) * len(shp))
    return pl.pallas_call(
        functools.partial(_s5_body, lt=lt),
        grid=(B // S5_BATCH, S // lt),
        in_specs=[
            pl.BlockSpec((S5_BATCH, lt, D_MODEL), lambda bi, ti: (bi, ti, 0)),
            c2((D_MODEL, D_MODEL)), c2((1, N_STATE)), c2((1, N_STATE)),
            c2(bd.shape), c2(cd.shape), c2((1, D_MODEL)),
            c2((D_MODEL, D_MODEL)), c2((D_MODEL, D_MODEL)), c2((1, D_MODEL)), c2((1, D_MODEL)),
        ],
        out_specs=[pl.BlockSpec((S5_BATCH, lt, D_MODEL), lambda bi, ti: (bi, ti, 0)),
                   pl.BlockSpec((S5_BATCH, lt * NWORD, LANES), lambda bi, ti: (bi, ti, 0))],
        out_shape=[jax.ShapeDtypeStruct((B, S, D_MODEL), F32),
                   jax.ShapeDtypeStruct((B, S * NWORD, LANES), jnp.uint32)],
        scratch_shapes=[
            pltpu.VMEM((D_MODEL // LANES, rows, LANES), F32),
            pltpu.VMEM((NWORD, rows, LANES), jnp.uint32),
            pltpu.VMEM((rows, N_STATE), F32),
            pltpu.VMEM((rows, N_STATE), F32),
            pltpu.VMEM((S5_BATCH, N_STATE), F32),
            pltpu.VMEM((S5_BATCH, N_STATE), F32),
        ],
        compiler_params=_cparams(("parallel", "arbitrary")),
        name="s5_mixer",
    )(x3, w_in.astype(BF), a_re.reshape(1, N_STATE), a_im.reshape(1, N_STATE), bd, cd,
      d_skip.reshape(1, D_MODEL), w_glu.astype(BF), w_out.astype(BF), ln_g.reshape(1, -1), ln_b.reshape(1, -1))


def _moe_block(h2, packed, router, router_bias, w_gate, w_up, w_down, sh_gate, sh_up, sh_down, ln_g, ln_b):
    pos, gate, meta = _router(h2, router.T, router_bias.reshape(-1, 1))
    off = meta[:, 0, :N_EXPERTS].reshape(-1)
    cnt = meta[:, 1, :N_EXPERTS].reshape(-1)
    wgu = jnp.concatenate([w_gate, w_up], axis=-1).astype(BF)
    wsgu = jnp.concatenate([sh_gate, sh_up], axis=-1).astype(BF)
    pos = (pos.T * NWORD).reshape(-1)
    return _moe_experts(h2, packed, off, cnt, pos, gate.T, wgu, w_down.astype(BF), wsgu, sh_down.astype(BF),
                        ln_g.reshape(1, -1), ln_b.reshape(1, -1))


def kernel(x, l0_w_in, l0_cmp_pos_k, l0_cmp_pos_v, l0_cmp_k_w1, l0_cmp_k_w2, l0_cmp_v_w1, l0_cmp_v_w2, l0_w_out, l0_ln1_g, l0_ln1_b, l0_router, l0_router_bias, l0_w_gate, l0_w_up, l0_w_down, l0_sh_gate, l0_sh_up, l0_sh_down, l0_ln2_g, l0_ln2_b, l1_w_in, l1_lambda_re, l1_lambda_im, l1_b_re, l1_b_im, l1_c_re, l1_c_im, l1_d, l1_log_dt, l1_w_glu, l1_w_out, l1_ln1_g, l1_ln1_b, l1_router, l1_router_bias, l1_w_gate, l1_w_up, l1_w_down, l1_sh_gate, l1_sh_up, l1_sh_down, l1_ln2_g, l1_ln2_b):
    B, S, D = x.shape
    assert D == D_MODEL and S % 512 == 0 and B % S5_BATCH == 0 and (B * S) % MOE_TILE == 0
    T = B * S
    h, hp = _mixer0(x.reshape(T, D), B, S, l0_w_in, l0_cmp_pos_k, l0_cmp_pos_v, l0_cmp_k_w1, l0_cmp_k_w2,
                    l0_cmp_v_w1, l0_cmp_v_w2, l0_w_out, l0_ln1_g, l0_ln1_b)
    h = _moe_block(h, hp, l0_router, l0_router_bias, l0_w_gate, l0_w_up, l0_w_down, l0_sh_gate, l0_sh_up,
                   l0_sh_down, l0_ln2_g, l0_ln2_b)
    h, hp = _mixer1(h.reshape(B, S, D), l1_w_in, l1_lambda_re, l1_lambda_im, l1_b_re, l1_b_im, l1_c_re,
                    l1_c_im, l1_d, l1_log_dt, l1_w_glu, l1_w_out, l1_ln1_g, l1_ln1_b)
    h = _moe_block(h.reshape(T, D), hp.reshape(T * NWORD, LANES), l1_router, l1_router_bias, l1_w_gate, l1_w_up,
                   l1_w_down, l1_sh_gate, l1_sh_up, l1_sh_down, l1_ln2_g, l1_ln2_b)
    return h.reshape(B, S, D)
```

```python
import functools
import math

import numpy as np
import jax
import jax.numpy as jnp
from jax import lax
from jax.experimental import pallas as pl
from jax.experimental.pallas import tpu as pltpu

F32 = jnp.float32
BF = jnp.bfloat16

D_MODEL = 1024
DEPTH = 2
HEAD_DIM = 64
LANES = 128
SB_HEADS = 8
SB_WIDTH = SB_HEADS * HEAD_DIM
NSA_HEADS = 8
NSA_KV_GROUPS = 2
NSA_REP = NSA_HEADS // NSA_KV_GROUPS
NSA_WIDTH = NSA_HEADS * HEAD_DIM
NSA_N_BRANCH = 3
CMP_LEN = 32
CMP_STRIDE = 16
CMP_HIDDEN = 256
SLC_LEN = 64
SLC_TOPN = 8
SLC_LOCAL = 2
SLC_FORCE_BONUS = 1e4
WINDOW = 512
ROPE_THETA = 500000.0
ROPE_DIM = HEAD_DIM // 4
SSM_GROUP = 16
SSM_GROUPS = D_MODEL // SSM_GROUP
SSM_STATE = 64
N_EXPERTS = 64
N_EXPERT_GROUPS = 8
TOPK_GROUPS = 4
TOP_K = 6
EXPERT_HIDDEN = 256
SHARED_HIDDEN = 256
ROUTED_SCALE = 2.5
DN_ALPHA = (2 * DEPTH) ** 0.25
LN_EPS = 1e-5
NEG_INF = -1e30
ATT_SCALE = HEAD_DIM ** -0.5
LOG2_E = math.log2(math.e)
SB_UNDERFLOW = -104.0
WIN_Q = 128
SB_ROWS = 128

V7X_VMEM_BYTES = 64 * 1024 * 1024
VMEM_LIMIT = V7X_VMEM_BYTES - 8 * 1024 * 1024
MOE_VMEM_LIMIT = V7X_VMEM_BYTES - 4 * 1024 * 1024


def _cparams(sem, vmem=VMEM_LIMIT):
    return pltpu.CompilerParams(dimension_semantics=sem, vmem_limit_bytes=vmem)


def _dot(a, b):
    return jnp.dot(a, b, preferred_element_type=F32)


def _dot_nt(a, b):
    return lax.dot_general(a, b, (((1,), (1,)), ((), ())), preferred_element_type=F32)


def _split2(x):
    hi = x.astype(BF)
    lo = (x - hi.astype(F32)).astype(BF)
    return hi, lo


def _layer_norm(h, g, b):
    mu = jnp.mean(h, axis=-1, keepdims=True)
    d = h - mu
    var = jnp.mean(d * d, axis=-1, keepdims=True)
    return d * lax.rsqrt(var + LN_EPS) * g + b


HALF = D_MODEL // 2
NWORD = HALF // LANES
HI_MASK = 0xFFFF0000


def _pack_pairs(a, b):
    lo = pltpu.bitcast(a.astype(BF).astype(F32), jnp.uint32)
    hi = pltpu.bitcast(b.astype(BF).astype(F32), jnp.uint32)
    return (lo >> 16) | (hi & jnp.uint32(HI_MASK))


def _pack_row_words(h):
    return [_pack_pairs(h[:, c * LANES:(c + 1) * LANES], h[:, HALF + c * LANES:HALF + (c + 1) * LANES])
            for c in range(NWORD)]


def _store_row_words(ref, start, n, h):
    for c, w in enumerate(_pack_row_words(h)):
        ref[pl.ds(start * NWORD + c, n, stride=NWORD), :] = w


def _load_row_words(ref, start, n):
    return [ref[pl.ds(start * NWORD + c, n, stride=NWORD), :] for c in range(NWORD)]


def _unpack_lo(w):
    return pltpu.bitcast(w << 16, F32)


def _unpack_hi(w):
    return pltpu.bitcast(w & jnp.uint32(HI_MASK), F32)


N_PLAIN = 3 * SB_WIDTH + LANES
N_TRANS = 2 * LANES + 32
N_ROPE = NSA_WIDTH + 3 * LANES


def _proj0_body(x_ref, wp_ref, wt_ref, wr_ref, cos_ref, sin_ref,
                qkva_ref, vc_ref, vt_ref, gt_ref, qb_ref, kc_ref, ksw_ref):
    xb = x_ref[...].astype(BF)
    a0 = 3 * SB_WIDTH
    qkva_ref[...] = _dot(xb, wp_ref[:, 0:a0]).astype(BF)
    vc_ref[...] = _dot(xb, wp_ref[:, a0:N_PLAIN]).astype(BF)
    yt = _dot_nt(wt_ref[...], xb)
    vt_ref[0] = yt[0:2 * LANES].astype(BF)
    gt_ref[0] = jax.nn.sigmoid(yt[2 * LANES:N_TRANS])
    y = _dot(xb, wr_ref[...])
    half = ROPE_DIM // 2
    pos = lax.broadcasted_iota(jnp.int32, (1, N_ROPE), 1) % HEAD_DIM
    partner = jnp.where(pos < half, -pltpu.roll(y, N_ROPE - half, 1), pltpu.roll(y, half, 1))
    y = y * cos_ref[...] + partner * sin_ref[...]
    qb_ref[...] = y[:, 0:NSA_WIDTH].astype(BF)
    kc_ref[...] = y[:, NSA_WIDTH:NSA_WIDTH + LANES].astype(BF)
    ksw_ref[...] = y[:, NSA_WIDTH + LANES:N_ROPE].astype(BF)


def _proj0(x2, wp, wt, wr, cos_t, sin_t, batch, seq):
    T = x2.shape[0]
    tm = 1024
    nseq = seq // tm
    row = lambda i: (i, 0)
    full = lambda i: (0, 0)
    tab = lambda i: (i % nseq, 0)
    trans = lambda i: (i // nseq, 0, i % nseq)
    return pl.pallas_call(
        _proj0_body,
        grid=(T // tm,),
        in_specs=[
            pl.BlockSpec((tm, D_MODEL), row),
            pl.BlockSpec((D_MODEL, N_PLAIN), full),
            pl.BlockSpec((N_TRANS, D_MODEL), full),
            pl.BlockSpec((D_MODEL, N_ROPE), full),
            pl.BlockSpec((tm, N_ROPE), tab),
            pl.BlockSpec((tm, N_ROPE), tab),
        ],
        out_specs=[
            pl.BlockSpec((tm, 3 * SB_WIDTH), row),
            pl.BlockSpec((tm, LANES), row),
            pl.BlockSpec((1, 2 * LANES, tm), trans),
            pl.BlockSpec((1, N_TRANS - 2 * LANES, tm), trans),
            pl.BlockSpec((tm, NSA_WIDTH), row),
            pl.BlockSpec((tm, LANES), row),
            pl.BlockSpec((tm, 2 * LANES), row),
        ],
        out_shape=[
            jax.ShapeDtypeStruct((T, 3 * SB_WIDTH), BF),
            jax.ShapeDtypeStruct((T, LANES), BF),
            jax.ShapeDtypeStruct((batch, 2 * LANES, seq), BF),
            jax.ShapeDtypeStruct((batch, N_TRANS - 2 * LANES, seq), F32),
            jax.ShapeDtypeStruct((T, NSA_WIDTH), BF),
            jax.ShapeDtypeStruct((T, LANES), BF),
            jax.ShapeDtypeStruct((T, 2 * LANES), BF),
        ],
        compiler_params=_cparams(("parallel",)),
        name="proj0",
    )(x2, wp, wt, wr, cos_t, sin_t)


def _sb_body(q_ref, k_ref, v_ref, o_ref, *, tq):
    i = pl.program_id(2)
    q = q_ref[0]
    lane = lax.broadcasted_iota(jnp.int32, (1, LANES), 1)
    row = lax.broadcasted_iota(jnp.int32, (tq, tq), 0)
    col = lax.broadcasted_iota(jnp.int32, (tq, tq), 1)
    tri = jnp.where(row > col, 1.0, 0.0).astype(BF)
    diag_causal = col < row

    hms = [(lane // HEAD_DIM) == hh for hh in range(2)]
    nchunk = tq // SB_ROWS
    chains = [(hh, rc) for hh in range(2) for rc in range(nchunk)]
    qcs = [jnp.where(hms[hh], q[rc * SB_ROWS:(rc + 1) * SB_ROWS], jnp.zeros((SB_ROWS, LANES), BF))
           for hh, rc in chains]

    def blocks(specs, state):
        nc = len(qcs)
        kvs = []
        for j, _, _ in specs:
            off = pl.multiple_of(j * tq, tq)
            kvs.append((k_ref[0, pl.ds(off, tq), :], v_ref[0, pl.ds(off, tq), :]))
        zs = [[_dot_nt(qc, k) for qc in qcs] for k, _ in kvs]
        mids = []
        for zrow, (_, masks, flag) in zip(zs, specs):
            row = []
            for n, z in enumerate(zrow):
                nl = jnp.maximum(z, 0.0) + jnp.log(1.0 + jnp.exp(-jnp.abs(z)))
                if masks is not None:
                    nl = jnp.where(masks[n], nl, 0.0)
                if flag is not None:
                    nl = jnp.where(flag, nl, 0.0)
                row.append((nl, z, nl.astype(BF)))
            mids.append(row)
        sufs = [[_dot(hi, tri) for _, _, hi in row] for row in mids]
        carries = [state[2 * n] for n in range(nc)]
        ws = []
        for row, srow, (_, masks, flag) in zip(mids, sufs, specs):
            wrow = []
            for n, ((nl, z, _), suffix) in enumerate(zip(row, srow)):
                w = jnp.exp(z - nl - suffix + carries[n])
                if masks is not None:
                    w = jnp.where(masks[n], w, 0.0)
                if flag is not None:
                    w = jnp.where(flag, w, 0.0)
                wrow.append(w.astype(BF))
                carries[n] = carries[n] - jnp.sum(nl, axis=-1, keepdims=True)
            ws.append(wrow)
        new = []
        for n in range(nc):
            acc = state[2 * n + 1]
            for wrow, (_, v) in zip(ws, kvs):
                acc = acc + _dot(wrow[n], v)
            new.extend([carries[n], acc])
        return new

    zc, za = jnp.zeros((SB_ROWS, 1), F32), jnp.zeros((SB_ROWS, LANES), F32)
    diag_masks = [diag_causal[rc * SB_ROWS:(rc + 1) * SB_ROWS] for _, rc in chains]
    state = blocks([(i, diag_masks, None), (jnp.maximum(i - 1, 0), None, i > 0)], [zc, za] * len(chains))

    def live(state):
        top = jnp.max(state[0])
        for c in state[2::2]:
            top = jnp.maximum(top, jnp.max(c))
        return top > SB_UNDERFLOW

    def cond(st):
        return jnp.logical_and(st[0] >= 0, st[1])

    def body(st):
        new = blocks([(st[0], None, None)], st[2])
        return st[0] - 1, live(new), tuple(new)

    _, _, state = lax.while_loop(cond, body, (i - 2, live(state), tuple(state)))
    accs = [jnp.concatenate([state[2 * (hh * nchunk + rc) + 1] for rc in range(nchunk)], axis=0)
            for hh in range(2)]
    o_ref[0] = jnp.where(hms[0], accs[0], accs[1]).astype(BF)


def _sb_attention(qkva3):
    B, S, _ = qkva3.shape
    tq = 256
    npair = SB_WIDTH // LANES
    return pl.pallas_call(
        functools.partial(_sb_body, tq=tq),
        grid=(B, npair, S // tq),
        in_specs=[
            pl.BlockSpec((1, tq, LANES), lambda b, p, i: (b, i, p)),
            pl.BlockSpec((1, S, LANES), lambda b, p, i: (b, 0, npair + p)),
            pl.BlockSpec((1, S, LANES), lambda b, p, i: (b, 0, 2 * npair + p)),
        ],
        out_specs=pl.BlockSpec((1, tq, LANES), lambda b, p, i: (b, i, p)),
        out_shape=jax.ShapeDtypeStruct((B, S, SB_WIDTH), BF),
        compiler_params=_cparams(("parallel", "parallel", "arbitrary")),
        name="sb_attention",
    )(qkva3, qkva3, qkva3)


def _cmp_body(ak_ref, av_ref, posk_ref, posv_ref, w1k_ref, w1kt_ref, w1kb_ref, w2k_ref,
              w1v_ref, w1vt_ref, w1vb_ref, w2v_ref, kc_ref, vc_ref):
    def one(a_ref, pos_ref, w1_ref, w1t_ref, w1b_ref, w2_ref, o_ref):
        a = a_ref[0]
        n = a.shape[0]
        bias = _dot(pos_ref[...], w1_ref[...])[0:1]
        out = jnp.zeros((n, LANES), F32)
        for g in range(NSA_KV_GROUPS):
            top = _dot(a, w1t_ref[g])
            bot = _dot(a, w1b_ref[g])
            h = top + pltpu.roll(bot, n - 1, 0) + bias
            out = out + _dot(jax.nn.gelu(h).astype(BF), w2_ref[g])
        o_ref[0] = out.astype(BF)

    one(ak_ref, posk_ref, w1k_ref, w1kt_ref, w1kb_ref, w2k_ref, kc_ref)
    one(av_ref, posv_ref, w1v_ref, w1vt_ref, w1vb_ref, w2v_ref, vc_ref)


def _compress(ak, av, posk, posv, wk, wv):
    B, n, width = ak.shape
    blk = pl.BlockSpec((1, n, width), lambda b: (b, 0, 0))
    c2 = lambda shp: pl.BlockSpec(shp, lambda b: (0, 0))
    c3 = lambda shp: pl.BlockSpec(shp, lambda b: (0, 0, 0))
    wspecs = [c2((CMP_LEN * HEAD_DIM, CMP_HIDDEN)), c3((2, width, CMP_HIDDEN)),
              c3((2, width, CMP_HIDDEN)), c3((2, CMP_HIDDEN, LANES))]
    out = pl.BlockSpec((1, n, LANES), lambda b: (b, 0, 0))
    return pl.pallas_call(
        _cmp_body,
        grid=(B,),
        in_specs=[blk, blk, c2((8, CMP_LEN * HEAD_DIM)), c2((8, CMP_LEN * HEAD_DIM))] + wspecs + wspecs,
        out_specs=[out, out],
        out_shape=[jax.ShapeDtypeStruct((B, n, LANES), BF)] * 2,
        compiler_params=_cparams(("parallel",)),
        name="nsa_compress",
    )(ak, av, posk, posv, *wk, *wv)


def _nsa_body(q_ref, kc_ref, vct_ref, ks_ref, vst_ref, kw_ref, vwt_ref, gt_ref, ovl_ref, et_ref,
              o_ref, *, tq, tk, nblk):
    i = pl.program_id(1)
    R = NSA_REP
    nchain = NSA_KV_GROUPS * R
    t0 = i * tq
    lane = lax.broadcasted_iota(jnp.int32, (1, LANES), 1)
    q = q_ref[0]
    t_row = t0 + lax.broadcasted_iota(jnp.int32, (1, tq), 1)
    tile = lambda a: jnp.concatenate([a] * nchain, axis=1)
    n_col = lax.broadcasted_iota(jnp.int32, (LANES, 1), 0)
    cmp_bias = tile(jnp.where((CMP_STRIDE * n_col + (CMP_LEN - 1)) <= t_row, 0.0, NEG_INF))
    has_cmp = tile(jnp.where(t_row >= CMP_LEN - 1, 1.0, 0.0))
    nkb = (t0 + tq + tk - 1) // tk
    key_last = (nkb - 1) * tk + lax.broadcasted_iota(jnp.int32, (tk, 1), 0)
    diag_bias = tile(jnp.where(key_last <= t_row, 0.0, NEG_INF))
    blk = lax.broadcasted_iota(jnp.int32, (nblk, 1), 0)
    cur = t_row // SLC_LEN
    valid = blk <= cur
    forced = (blk == 0) | (valid & (blk > cur - SLC_LOCAL))
    ovl = ovl_ref[...]

    qrs = [jnp.where((lane // HEAD_DIM) == gi, q[:, LANES * r:LANES * (r + 1)], jnp.zeros((tq, LANES), BF))
           for gi in range(NSA_KV_GROUPS) for r in range(R)]
    q_all = jnp.concatenate(qrs, axis=0)

    s = _dot_nt(kc_ref[0], q_all) + cmp_bias
    e = jnp.exp2(s - jnp.max(s, axis=0, keepdims=True))
    pc = e / jnp.sum(e, axis=0, keepdims=True) * has_cmp
    o_cmp = _dot(vct_ref[0], pc.astype(BF))

    q_aug = []
    for gi in range(NSA_KV_GROUPS):
        psum = jnp.zeros((LANES, tq), F32)
        for r in range(R):
            c = gi * R + r
            psum = psum + pc[:, c * tq:(c + 1) * tq]
        p1 = psum.astype(BF)
        r1 = psum - p1.astype(F32)
        p2 = r1.astype(BF)
        p3 = (r1 - p2.astype(F32)).astype(BF)
        imp = _dot(ovl, p1) + _dot(ovl, p2) + _dot(ovl, p3)
        imp = jnp.where(forced, imp + SLC_FORCE_BONUS, jnp.where(valid, imp, NEG_INF))
        sel = jnp.zeros((nblk, tq), F32)
        for _ in range(min(SLC_TOPN, nblk)):
            mx = jnp.max(imp, axis=0, keepdims=True)
            idx = jnp.min(jnp.where(imp == mx, blk, nblk), axis=0, keepdims=True)
            hit = blk == idx
            sel = jnp.where(hit, 1.0, sel)
            imp = jnp.where(hit, -jnp.inf, imp)
        sel_t = jnp.concatenate([sel, jnp.zeros((LANES - nblk, tq), F32)], axis=0).T
        sel_bias = ((sel_t - 1.0) * -NEG_INF).astype(BF)
        q_aug.extend(jnp.concatenate([qrs[gi * R + r], sel_bias], axis=1) for r in range(R))
    qa_all = jnp.concatenate(q_aug, axis=0)

    def sel_step(j, state, bias):
        m, l, acc = state
        off = pl.multiple_of(j * tk, tk)
        ka = jnp.concatenate([ks_ref[0, pl.ds(off, tk), :], et_ref[pl.ds(off, tk), :]], axis=1)
        s = _dot_nt(ka, qa_all)
        if bias is not None:
            s = s + bias
        m_new = jnp.maximum(m, jnp.max(s, axis=0, keepdims=True))
        alpha = jnp.exp2(m - m_new)
        p = jnp.exp2(s - m_new)
        l = alpha * l + jnp.sum(p, axis=0, keepdims=True)
        acc = alpha * acc + _dot(vst_ref[0, :, pl.ds(off, tk)], p.astype(BF))
        return m_new, l, acc

    width = nchain * tq
    init = (jnp.full((1, width), NEG_INF, F32), jnp.zeros((1, width), F32), jnp.zeros((LANES, width), F32))
    state = lax.fori_loop(0, nkb - 1, lambda j, st: sel_step(j, st, None), init)
    _, l_s, acc_s = sel_step(nkb - 1, state, diag_bias)
    o_sel = acc_s / l_s

    wkeys = WINDOW + WIN_Q
    o_parts = []
    for h in range(tq // WIN_Q):
        th = t_row[:, h * WIN_Q:(h + 1) * WIN_Q]
        woff = pl.multiple_of(jnp.maximum(t0 + h * WIN_Q - WINDOW, 0), WIN_Q)
        kp = woff + lax.broadcasted_iota(jnp.int32, (wkeys, 1), 0)
        bias = tile(jnp.where(kp <= th, jnp.where(kp > th - WINDOW, 0.0, NEG_INF), NEG_INF))
        qh = jnp.concatenate([qr[h * WIN_Q:(h + 1) * WIN_Q] for qr in qrs], axis=0)
        s = _dot_nt(kw_ref[0, pl.ds(woff, wkeys), :], qh) + bias
        p = jnp.exp2(s - jnp.max(s, axis=0, keepdims=True))
        o_parts.append(_dot(vwt_ref[0, :, pl.ds(woff, wkeys)], p.astype(BF)) / jnp.sum(p, axis=0, keepdims=True))
    o_win = jnp.concatenate([o[:, c * WIN_Q:(c + 1) * WIN_Q] for c in range(nchain) for o in o_parts], axis=1)

    gt = gt_ref[0]
    sub = lax.broadcasted_iota(jnp.int32, (LANES, 1), 0)
    g0 = (sub // HEAD_DIM) == 0
    for r in range(R):
        out = jnp.zeros((LANES, tq), F32)
        for br, o_br in enumerate((o_cmp, o_sel, o_win)):
            rows = [(gi * R + r) * NSA_N_BRANCH + br for gi in range(NSA_KV_GROUPS)]
            gate = jnp.where(g0, gt[rows[0]:rows[0] + 1], gt[rows[1]:rows[1] + 1])
            both = jnp.where(g0, o_br[:, r * tq:(r + 1) * tq], o_br[:, (R + r) * tq:(R + r + 1) * tq])
            out = out + gate * both
        o_ref[0, r * LANES:(r + 1) * LANES, :] = out.astype(BF)


def _nsa_attention(qb3, kc, vct, ksw3, vsw_t, gt, ovl, expand):
    B, S, _ = qb3.shape
    tq, tk = 512, 512
    n = kc.shape[1]
    nblk = ovl.shape[0]
    ngate = gt.shape[1]
    assert S >= WINDOW + WIN_Q and tq % WIN_Q == 0 and S % tk == 0 and nblk % 8 == 0 and n == LANES
    tok = lambda c: pl.BlockSpec((1, S, LANES), lambda b, i, c=c: (b, 0, c))
    tr = lambda c: pl.BlockSpec((1, LANES, S), lambda b, i, c=c: (b, c, 0))
    return pl.pallas_call(
        functools.partial(_nsa_body, tq=tq, tk=tk, nblk=nblk),
        grid=(B, S // tq),
        in_specs=[
            pl.BlockSpec((1, tq, NSA_WIDTH), lambda b, i: (b, i, 0)),
            pl.BlockSpec((1, n, LANES), lambda b, i: (b, 0, 0)),
            pl.BlockSpec((1, LANES, n), lambda b, i: (b, 0, 0)),
            tok(0), tr(0), tok(1), tr(1),
            pl.BlockSpec((1, ngate, tq), lambda b, i: (b, 0, i)),
            pl.BlockSpec((nblk, LANES), lambda b, i: (0, 0)),
            pl.BlockSpec((S, LANES), lambda b, i: (0, 0)),
        ],
        out_specs=pl.BlockSpec((1, NSA_WIDTH, tq), lambda b, i: (b, 0, i)),
        out_shape=jax.ShapeDtypeStruct((B, NSA_WIDTH, S), BF),
        compiler_params=_cparams(("parallel", "arbitrary")),
        name="nsa_attention",
    )(qb3, kc, vct, ksw3, vsw_t, ksw3, vsw_t, gt, ovl, expand)


def _outproj_ln_body(oa_ref, obt_ref, x_ref, wa_ref, wb_ref, g_ref, b_ref, o_ref, p_ref):
    yb = lax.dot_general(obt_ref[0], wb_ref[...], (((0,), (0,)), ((), ())), preferred_element_type=F32)
    y = _dot(oa_ref[...], wa_ref[...]) + yb
    res = _layer_norm(DN_ALPHA * x_ref[...] + y, g_ref[...], b_ref[...])
    o_ref[...] = res
    _store_row_words(p_ref, 0, res.shape[0], res)


def _outproj_ln(oa, obt, x2, wa, wb, g, b):
    T = x2.shape[0]
    tm = 1024
    nseq = obt.shape[2] // tm
    row = lambda i: (i, 0)
    full = lambda i: (0, 0)
    return pl.pallas_call(
        _outproj_ln_body,
        grid=(T // tm,),
        in_specs=[
            pl.BlockSpec((tm, oa.shape[1]), row),
            pl.BlockSpec((1, obt.shape[1], tm), lambda i: (i // nseq, 0, i % nseq)),
            pl.BlockSpec((tm, D_MODEL), row),
            pl.BlockSpec(wa.shape, full),
            pl.BlockSpec(wb.shape, full),
            pl.BlockSpec((1, D_MODEL), full),
            pl.BlockSpec((1, D_MODEL), full),
        ],
        out_specs=[pl.BlockSpec((tm, D_MODEL), row),
                   pl.BlockSpec((tm * NWORD, LANES), row)],
        out_shape=[jax.ShapeDtypeStruct((T, D_MODEL), F32),
                   jax.ShapeDtypeStruct((T * NWORD, LANES), jnp.uint32)],
        compiler_params=_cparams(("parallel",)),
        name="outproj_ln",
    )(oa, obt, x2, wa, wb, g, b)


def _rope_tables(seq):
    inv = jnp.power(ROPE_THETA, -jnp.arange(0, ROPE_DIM, 2, dtype=F32) / ROPE_DIM)
    ang = jnp.arange(seq, dtype=F32)[:, None] * inv[None, :]
    half = ROPE_DIM // 2
    rest = HEAD_DIM - ROPE_DIM
    cos_h = jnp.concatenate([jnp.cos(ang), jnp.cos(ang), jnp.ones((seq, rest), F32)], axis=1)
    sin_h = jnp.concatenate([jnp.sin(ang), jnp.sin(ang), jnp.zeros((seq, rest), F32)], axis=1)
    reps = N_ROPE // HEAD_DIM
    del half
    return jnp.tile(cos_h, (1, reps)), jnp.tile(sin_h, (1, reps))


def _nsa_head_perm():
    cols = []
    for r in range(NSA_REP):
        for g in range(NSA_KV_GROUPS):
            h = g * NSA_REP + r
            cols.extend(range(h * HEAD_DIM, (h + 1) * HEAD_DIM))
    return np.asarray(cols)


def _mixer0(x2, batch, seq, w_in, cmp_pos_k, cmp_pos_v, cmp_k_w1, cmp_k_w2, cmp_v_w1, cmp_v_w2,
            w_out, ln_g, ln_b):
    T = x2.shape[0]
    sizes = [SB_WIDTH] * 3 + [NSA_WIDTH] + [NSA_KV_GROUPS * HEAD_DIM] * 6 + [NSA_HEADS * NSA_N_BRANCH]
    offs = np.concatenate([[0], np.cumsum(sizes)])
    col = lambda j: w_in[:, offs[j]:offs[j + 1]]
    perm = _nsa_head_perm()
    ngate = sizes[-1]
    wp = jnp.concatenate([col(0) * ATT_SCALE, col(1), col(2), col(5)], axis=1)
    wt = jnp.concatenate([col(7), col(9), jnp.pad(col(10), ((0, 0), (0, N_TRANS - 2 * LANES - ngate)))], axis=1).T
    wr = jnp.concatenate([col(3)[:, perm] * (ATT_SCALE * LOG2_E), col(4), col(6), col(8)], axis=1)
    cos_t, sin_t = _rope_tables(seq)
    qkva, vc_tok, vsw_t, gt, qb, kc_tok, ksw = _proj0(x2, wp.astype(BF), wt.astype(BF), wr.astype(BF), cos_t, sin_t,
                                                      batch, seq)

    r3 = lambda a: a.reshape(batch, seq, a.shape[-1])
    o_a = _sb_attention(r3(qkva))

    ncmp = seq // CMP_STRIDE
    ak = kc_tok.reshape(batch, ncmp, CMP_STRIDE * LANES)
    av = vc_tok.reshape(batch, ncmp, CMP_STRIDE * LANES)

    def cmp_weights(w1, w2):
        w1r = w1.reshape(2, CMP_STRIDE, HEAD_DIM, CMP_HIDDEN)
        tops, bots, w2s = [], [], []
        for g in range(NSA_KV_GROUPS):
            ext = jnp.zeros((2, CMP_STRIDE, NSA_KV_GROUPS, HEAD_DIM, CMP_HIDDEN), F32).at[:, :, g].set(w1r)
            ext = ext.reshape(2, CMP_STRIDE * LANES, CMP_HIDDEN)
            tops.append(ext[0])
            bots.append(ext[1])
            w2s.append(jnp.zeros((CMP_HIDDEN, LANES), F32).at[:, g * HEAD_DIM:(g + 1) * HEAD_DIM].set(w2))
        return (w1.astype(BF), jnp.stack(tops).astype(BF), jnp.stack(bots).astype(BF),
                jnp.stack(w2s).astype(BF))

    posk = jnp.broadcast_to(cmp_pos_k.reshape(1, -1), (8, CMP_LEN * HEAD_DIM)).astype(BF)
    posv = jnp.broadcast_to(cmp_pos_v.reshape(1, -1), (8, CMP_LEN * HEAD_DIM)).astype(BF)
    kc, vc = _compress(ak, av, posk, posv, cmp_weights(cmp_k_w1, cmp_k_w2), cmp_weights(cmp_v_w1, cmp_v_w2))
    assert ncmp <= LANES
    if ncmp < LANES:
        kc = jnp.pad(kc, ((0, 0), (0, LANES - ncmp), (0, 0)))
        vc = jnp.pad(vc, ((0, 0), (0, LANES - ncmp), (0, 0)))

    n_slc = seq // SLC_LEN
    cmp_start = np.arange(ncmp) * CMP_STRIDE
    slc_start = np.arange(n_slc) * SLC_LEN
    ovl = ((cmp_start[None, :] <= slc_start[:, None] + SLC_LEN - 1)
           & (cmp_start[None, :] + CMP_LEN - 1 >= slc_start[:, None])).astype(np.float32)
    ovl = np.pad(ovl, ((0, 0), (0, LANES - ncmp))) if ncmp < LANES else ovl
    expand = (np.arange(seq)[:, None] // SLC_LEN == np.arange(LANES)[None, :]).astype(np.float32)
    o_bt = _nsa_attention(r3(qb), kc, jnp.swapaxes(vc, 1, 2), r3(ksw), vsw_t, gt,
                          jnp.asarray(ovl, BF), jnp.asarray(expand, BF))

    wa = w_out[:SB_WIDTH].astype(BF)
    wb = w_out[SB_WIDTH:][perm].astype(BF)
    return _outproj_ln(o_a.reshape(T, SB_WIDTH), o_bt, x2, wa, wb,
                       ln_g.reshape(1, -1), ln_b.reshape(1, -1))


MOE_TILE = 2048
ROW_ALIGN = 8
RANK_CHUNK = 256


def _top_rows(vals, ids, n_ids, count):
    hits = []
    for _ in range(count):
        mx = jnp.max(vals, axis=0, keepdims=True)
        idx = jnp.min(jnp.where(vals == mx, ids, n_ids), axis=0, keepdims=True)
        hit = ids == idx
        hits.append(hit)
        vals = jnp.where(hit, -jnp.inf, vals)
    return hits


def _router_body(h_ref, rt_ref, rb_ref, pos_ref, gate_ref, meta_ref, *, tm):
    E, NG = N_EXPERTS, N_EXPERT_GROUPS
    per = E // NG
    hh, hl = _split2(h_ref[...])
    rh, rl = _split2(rt_ref[...])
    logits = _dot_nt(rh, hh) + _dot_nt(rh, hl) + _dot_nt(rl, hh)
    scores = jax.nn.sigmoid(logits)
    biased = scores + rb_ref[...]
    i8 = lax.broadcasted_iota(jnp.int32, (per, tm), 0)
    gs = []
    for g in range(NG):
        v = biased[g * per:(g + 1) * per]
        m1 = jnp.max(v, axis=0, keepdims=True)
        a1 = jnp.min(jnp.where(v == m1, i8, per), axis=0, keepdims=True)
        m2 = jnp.max(jnp.where(i8 == a1, -jnp.inf, v), axis=0, keepdims=True)
        gs.append(m1 + m2)
    gs = jnp.concatenate(gs, axis=0)
    gi = lax.broadcasted_iota(jnp.int32, (NG, tm), 0)
    ghits = _top_rows(gs, gi, NG, TOPK_GROUPS)
    gkeep = jnp.zeros((NG, tm), F32)
    for hit in ghits:
        gkeep = jnp.where(hit, 1.0, gkeep)
    ekeep = jnp.concatenate([jnp.broadcast_to(gkeep[g:g + 1], (per, tm)) for g in range(NG)], axis=0)
    ei = lax.broadcasted_iota(jnp.int32, (E, tm), 0)
    hits = _top_rows(jnp.where(ekeep > 0.5, biased, -jnp.inf), ei, E, TOP_K)
    gates = [jnp.sum(jnp.where(hit, scores, 0.0), axis=0, keepdims=True) for hit in hits]
    gsum = gates[0]
    for gk in gates[1:]:
        gsum = gsum + gk
    gates = [gk / gsum * ROUTED_SCALE for gk in gates]

    member = jnp.zeros((E, tm), F32)
    for hit in hits:
        member = jnp.where(hit, 1.0, member)
    cnt_col = jnp.sum(member, axis=1, keepdims=True)
    pad_col = jnp.floor((cnt_col + (ROW_ALIGN - 1)) * (1.0 / ROW_ALIGN)) * ROW_ALIGN
    sub_e = lax.broadcasted_iota(jnp.int32, (E, LANES), 0)
    lane_e = lax.broadcasted_iota(jnp.int32, (E, LANES), 1)
    cnt_row = jnp.sum(jnp.where(sub_e == lane_e, cnt_col, 0.0), axis=0, keepdims=True)
    pad_row = jnp.sum(jnp.where(sub_e == lane_e, pad_col, 0.0), axis=0, keepdims=True)
    off_row = jnp.sum(jnp.where(sub_e < lane_e, pad_col, 0.0), axis=0, keepdims=True)
    off_col = jnp.sum(jnp.where(lane_e < sub_e, pad_row, 0.0), axis=1, keepdims=True)
    r_i = lax.broadcasted_iota(jnp.int32, (RANK_CHUNK, RANK_CHUNK), 0)
    c_i = lax.broadcasted_iota(jnp.int32, (RANK_CHUNK, RANK_CHUNK), 1)
    before = jnp.where(r_i < c_i, 1.0, 0.0).astype(BF)
    running = off_col
    ranks = []
    for c in range(tm // RANK_CHUNK):
        mc = member[:, c * RANK_CHUNK:(c + 1) * RANK_CHUNK]
        ranks.append(_dot(mc.astype(BF), before) + running)
        running = running + jnp.sum(mc, axis=1, keepdims=True)
    slot = jnp.concatenate(ranks, axis=1)
    pos = [jnp.sum(jnp.where(hit, slot, 0.0), axis=0, keepdims=True) for hit in hits]
    zrow = jnp.zeros((1, tm), F32)
    pos_ref[...] = jnp.concatenate(pos + [zrow, zrow], axis=0).astype(jnp.int32)
    gate_ref[...] = jnp.concatenate(gates + [zrow, zrow], axis=0)
    z128 = jnp.zeros((1, LANES), F32)
    meta_ref[0] = jnp.concatenate([off_row, cnt_row] + [z128] * 6, axis=0).astype(jnp.int32)


def _router(h2, router_t, bias_col):
    T = h2.shape[0]
    tm = MOE_TILE
    nt = T // tm
    return pl.pallas_call(
        functools.partial(_router_body, tm=tm),
        grid=(nt,),
        in_specs=[
            pl.BlockSpec((tm, D_MODEL), lambda i: (i, 0)),
            pl.BlockSpec((N_EXPERTS, D_MODEL), lambda i: (0, 0)),
            pl.BlockSpec((N_EXPERTS, 1), lambda i: (0, 0)),
        ],
        out_specs=[
            pl.BlockSpec((8, tm), lambda i: (0, i)),
            pl.BlockSpec((8, tm), lambda i: (0, i)),
            pl.BlockSpec((1, 8, LANES), lambda i: (i, 0, 0)),
        ],
        out_shape=[
            jax.ShapeDtypeStruct((8, T), jnp.int32),
            jax.ShapeDtypeStruct((8, T), F32),
            jax.ShapeDtypeStruct((nt, 8, LANES), jnp.int32),
        ],
        compiler_params=_cparams(("parallel",)),
        name="moe_router",
    )(h2, router_t, bias_col)


EXPERTS_PER_STEP = 4
EXPERT_CHUNK = 224
COMBINE_SUB = 256
POS_STRIDE = 8


def _swiglu(xb, wgu, wd, hidden):
    gu = _dot(xb, wgu)
    a = jax.nn.silu(gu[:, :hidden]) * gu[:, hidden:]
    return _dot(a.astype(BF), wd)


def _moe_body(off_ref, cnt_ref, pos_ref, src_ref, x_ref, gcol_ref, wgu_ref, wd_ref, wsgu_ref, wsd_ref,
              g_ref, b_ref, o_ref, xs_ref, z_ref, *, tm, eb, ch, sub, unroll):
    i = pl.program_id(0)
    j = pl.program_id(1)
    nj = N_EXPERTS // eb

    @pl.when(j == 0)
    def _dispatch():
        zeros = jnp.zeros((ROW_ALIGN * NWORD, LANES), jnp.uint32)

        def pad(e, carry):
            off, cnt = off_ref[i * N_EXPERTS + e], cnt_ref[i * N_EXPERTS + e]
            last = pl.multiple_of((off + cnt // ROW_ALIGN * ROW_ALIGN) * NWORD, ROW_ALIGN * NWORD)
            xs_ref[pl.ds(last, ROW_ALIGN * NWORD), :] = zeros
            return carry

        lax.fori_loop(0, N_EXPERTS, pad, 0)
        end = off_ref[i * N_EXPERTS + N_EXPERTS - 1] + cnt_ref[i * N_EXPERTS + N_EXPERTS - 1]
        end = pl.multiple_of((end + ROW_ALIGN - 1) // ROW_ALIGN * ROW_ALIGN * NWORD, ROW_ALIGN * NWORD)
        xs_ref[pl.ds(end, ch * NWORD), :] = jnp.zeros((ch * NWORD, LANES), jnp.uint32)

        def tok(tb, carry):
            for u in range(unroll):
                t = tb * unroll + u
                slab = src_ref[pl.ds(pl.multiple_of(t * NWORD, NWORD), NWORD), :]
                for k in range(TOP_K):
                    p = pl.multiple_of(pos_ref[t * POS_STRIDE + k], NWORD)
                    xs_ref[pl.ds(p, NWORD), :] = slab
            return carry

        lax.fori_loop(0, tm // unroll, tok, 0)

    def chunks_in(els, offs, c):
        r0s = [pl.multiple_of(off + c * ch, ROW_ALIGN) for off in offs]
        words = [_load_row_words(xs_ref, r0, ch) for r0 in r0s]
        xbs = [jnp.concatenate([_unpack_lo(w).astype(BF) for w in ws]
                               + [_unpack_hi(w).astype(BF) for w in ws], axis=1) for ws in words]
        gus = [_dot(xb, wgu_ref[el]) for xb, el in zip(xbs, els)]
        acts = [(jax.nn.silu(gu[:, :EXPERT_HIDDEN]) * gu[:, EXPERT_HIDDEN:]).astype(BF) for gu in gus]
        ys = [_dot(a, wd_ref[el]) for a, el in zip(acts, els)]
        return list(zip(r0s, words, ys))

    def chunk_out(r0, words, y, c, cnt):
        keep = (c * ch + lax.broadcasted_iota(jnp.int32, (ch, 1), 0)) < cnt
        for cc, packed in enumerate(_pack_row_words(y)):
            xs_ref[pl.ds(r0 * NWORD + cc, ch, stride=NWORD), :] = jnp.where(keep, packed, words[cc])

    @pl.when(j < nj)
    def _experts():
        offs = [off_ref[i * N_EXPERTS + j * eb + el] for el in range(eb)]
        cnts = [cnt_ref[i * N_EXPERTS + j * eb + el] for el in range(eb)]
        firsts = chunks_in(list(range(eb)), offs, 0)
        for el in range(eb):
            chunk_out(*firsts[el], 0, cnts[el])
        for el in range(eb):
            def chunk(c, carry, el=el):
                chunk_out(*chunks_in([el], [offs[el]], c)[0], c, cnts[el])
                return carry

            lax.fori_loop(1, (cnts[el] + ch - 1) // ch, chunk, 0)

    @pl.when(j >= nj)
    def _combine():
        base = (j - nj) * sub

        def tok(tb, carry):
            for u in range(unroll):
                tl = tb * unroll + u
                dst = pl.multiple_of(tl * NWORD, NWORD)
                for k in range(TOP_K):
                    p = pl.multiple_of(pos_ref[(base + tl) * POS_STRIDE + k], NWORD)
                    z_ref[k, pl.ds(dst, NWORD), :] = xs_ref[pl.ds(p, NWORD), :]
            return carry

        lax.fori_loop(0, sub // unroll, tok, 0)
        gcol = gcol_ref[...]
        lo = [jnp.zeros((sub, LANES), F32) for _ in range(NWORD)]
        hi = [jnp.zeros((sub, LANES), F32) for _ in range(NWORD)]
        for k in range(TOP_K):
            gk = gcol[:, k:k + 1]
            for c, w in enumerate(_load_row_words(z_ref.at[k], 0, sub)):
                lo[c] = lo[c] + gk * _unpack_lo(w)
                hi[c] = hi[c] + gk * _unpack_hi(w)
        routed = jnp.concatenate(lo + hi, axis=1)
        x = x_ref[...]
        shared = _swiglu(x.astype(BF), wsgu_ref[...], wsd_ref[...], SHARED_HIDDEN)
        o_ref[...] = _layer_norm(DN_ALPHA * x + routed + shared, g_ref[...], b_ref[...])


def _moe_experts(h2, packed, off, cnt, pos, gcol, wgu, wd, wsgu, wsd, g, b):
    T = h2.shape[0]
    tm, eb, ch, sub = MOE_TILE, EXPERTS_PER_STEP, EXPERT_CHUNK, COMBINE_SUB
    nt, nj, nsub = T // tm, N_EXPERTS // eb, tm // sub
    rows = TOP_K * tm + N_EXPERTS * ROW_ALIGN + ch
    hidden2 = wgu.shape[-1]
    wblk = lambda i, j, *_: (jnp.minimum(j, nj - 1), 0, 0)
    sub_i = lambda i, j: i * nsub + jnp.clip(j - nj, 0, nsub - 1)
    once = pl.Buffered(1)
    return pl.pallas_call(
        functools.partial(_moe_body, tm=tm, eb=eb, ch=ch, sub=sub, unroll=4),
        grid_spec=pltpu.PrefetchScalarGridSpec(
            num_scalar_prefetch=2,
            grid=(nt, nj + nsub),
            in_specs=[
                pl.BlockSpec((tm * POS_STRIDE,), lambda i, j, *_: (i,), memory_space=pltpu.SMEM),
                pl.BlockSpec((tm * NWORD, LANES), lambda i, j, *_: (i, 0), pipeline_mode=once),
                pl.BlockSpec((sub, D_MODEL), lambda i, j, *_: (sub_i(i, j), 0)),
                pl.BlockSpec((sub, 8), lambda i, j, *_: (sub_i(i, j), 0)),
                pl.BlockSpec((eb, D_MODEL, hidden2), wblk),
                pl.BlockSpec((eb, hidden2 // 2, D_MODEL), wblk),
                pl.BlockSpec(wsgu.shape, lambda i, j, *_: (0, 0), pipeline_mode=once),
                pl.BlockSpec(wsd.shape, lambda i, j, *_: (0, 0), pipeline_mode=once),
                pl.BlockSpec((1, D_MODEL), lambda i, j, *_: (0, 0)),
                pl.BlockSpec((1, D_MODEL), lambda i, j, *_: (0, 0)),
            ],
            out_specs=pl.BlockSpec((sub, D_MODEL), lambda i, j, *_: (sub_i(i, j), 0)),
            scratch_shapes=[
                pltpu.VMEM((rows * NWORD, LANES), jnp.uint32),
                pltpu.VMEM((TOP_K, sub * NWORD, LANES), jnp.uint32),
            ],
        ),
        out_shape=jax.ShapeDtypeStruct((T, D_MODEL), F32),
        compiler_params=_cparams(("parallel", "arbitrary"), MOE_VMEM_LIMIT),
        name="moe_experts",
    )(off, cnt, pos, packed, h2, gcol, wgu, wd, wsgu, wsd, g, b)


S5_BATCH = 8
S5_STEPS = 64
S5_SLABS = 4
SLAB_CH = D_MODEL // S5_SLABS
SLAB_ST = SSM_GROUPS * SSM_STATE // S5_SLABS
N_STATE = SSM_GROUPS * SSM_STATE
SCAN_LANES = 512
SCAN_UNROLL = 8


def _s5_disc_body(lre_ref, lim_ref, ldt_ref, bre_ref, bim_ref, are_ref, aim_ref, bbre_ref, bbim_ref):
    lre, lim = lre_ref[...], lim_ref[...]
    step = jnp.exp(ldt_ref[...])
    mag = jnp.exp(lre * step)
    a_re = mag * jnp.cos(lim * step)
    a_im = mag * jnp.sin(lim * step)
    den = lre * lre + lim * lim
    zoh_re = ((a_re - 1.0) * lre + a_im * lim) / den
    zoh_im = (a_im * lre - (a_re - 1.0) * lim) / den
    are_ref[...] = a_re
    aim_ref[...] = a_im
    bbre_ref[...] = zoh_re * bre_ref[...] - zoh_im * bim_ref[...]
    bbim_ref[...] = zoh_re * bim_ref[...] + zoh_im * bre_ref[...]


def _s5_discretize(lambda_re, lambda_im, log_dt, b_re, b_im):
    col = lambda a: a.reshape(N_STATE, 1)
    ldt = jnp.broadcast_to(log_dt[:, None], (SSM_GROUPS, SSM_STATE))
    mat = lambda a: a.reshape(N_STATE, SSM_GROUP)
    c1 = jax.ShapeDtypeStruct((N_STATE, 1), F32)
    c16 = jax.ShapeDtypeStruct((N_STATE, SSM_GROUP), F32)
    return pl.pallas_call(_s5_disc_body, out_shape=[c1, c1, c16, c16], name="s5_discretize")(
        col(lambda_re), col(lambda_im), col(ldt), mat(b_re), mat(b_im))


def _s5_body(x_ref, win_ref, are_ref, aim_ref, bd_ref, cd_ref, dsk_ref, wglu_ref, wout_ref, g_ref, b_ref,
             o_ref, p_ref, xs_ref, pk_ref, hre_ref, him_ref, sre_ref, sim_ref, *, lt):
    nb = S5_BATCH
    nlb = D_MODEL // LANES
    for c in range(nlb):
        for b in range(nb):
            xs_ref[c, pl.ds(b, lt, stride=nb), :] = x_ref[b, :, c * LANES:(c + 1) * LANES]
    x = jnp.concatenate([xs_ref[c] for c in range(nlb)], axis=1)
    u = _dot(x.astype(BF), win_ref[...])
    ub = u.astype(BF)
    for k in range(S5_SLABS):
        bu = _dot(ub[:, k * SLAB_CH:(k + 1) * SLAB_CH], bd_ref[k])
        hre_ref[:, k * SLAB_ST:(k + 1) * SLAB_ST] = bu[:, :SLAB_ST]
        him_ref[:, k * SLAB_ST:(k + 1) * SLAB_ST] = bu[:, SLAB_ST:]

    @pl.when(pl.program_id(1) == 0)
    def _():
        sre_ref[...] = jnp.zeros_like(sre_ref)
        sim_ref[...] = jnp.zeros_like(sim_ref)

    for c in range(N_STATE // SCAN_LANES):
        ls = slice(c * SCAN_LANES, (c + 1) * SCAN_LANES)
        a_re = jnp.broadcast_to(are_ref[:, ls], (nb, SCAN_LANES))
        a_im = jnp.broadcast_to(aim_ref[:, ls], (nb, SCAN_LANES))

        def steps(tb, state, ls=ls, a_re=a_re, a_im=a_im):
            s_re, s_im = state
            for uu in range(SCAN_UNROLL):
                r0 = pl.multiple_of((tb * SCAN_UNROLL + uu) * nb, nb)
                n_re = a_re * s_re - a_im * s_im + hre_ref[pl.ds(r0, nb), ls]
                n_im = a_re * s_im + a_im * s_re + him_ref[pl.ds(r0, nb), ls]
                hre_ref[pl.ds(r0, nb), ls] = n_re
                him_ref[pl.ds(r0, nb), ls] = n_im
                s_re, s_im = n_re, n_im
            return s_re, s_im

        s_re, s_im = lax.fori_loop(0, lt // SCAN_UNROLL, steps, (sre_ref[:, ls], sim_ref[:, ls]))
        sre_ref[:, ls] = s_re
        sim_ref[:, ls] = s_im

    ys = []
    for k in range(S5_SLABS):
        hk = jnp.concatenate([hre_ref[:, k * SLAB_ST:(k + 1) * SLAB_ST].astype(BF),
                              him_ref[:, k * SLAB_ST:(k + 1) * SLAB_ST].astype(BF)], axis=1)
        ys.append(_dot(hk, cd_ref[k]))
    y = jax.nn.gelu(jnp.concatenate(ys, axis=1) + dsk_ref[...] * u)
    y = y * jax.nn.sigmoid(_dot(y.astype(BF), wglu_ref[...]))
    mixed = _dot(y.astype(BF), wout_ref[...])
    res = _layer_norm(DN_ALPHA * x + mixed, g_ref[...], b_ref[...])
    for c in range(nlb):
        xs_ref[c] = res[:, c * LANES:(c + 1) * LANES]
    for c, w in enumerate(_pack_row_words(res)):
        pk_ref[c] = w
    for c in range(nlb):
        for b in range(nb):
            o_ref[b, :, c * LANES:(c + 1) * LANES] = xs_ref[c, pl.ds(b, lt, stride=nb), :]
    for c in range(NWORD):
        for b in range(nb):
            p_ref[b, pl.ds(c, lt, stride=NWORD), :] = pk_ref[c, pl.ds(b, lt, stride=nb), :]


def _mixer1(x3, w_in, lambda_re, lambda_im, b_re, b_im, c_re, c_im, d_skip, log_dt, w_glu, w_out,
            ln_g, ln_b):
    B, S, _ = x3.shape
    lt = S5_STEPS
    rows = S5_BATCH * lt
    a_re, a_im, bb_re, bb_im = _s5_discretize(lambda_re, lambda_im, log_dt, b_re, b_im)
    gps = SSM_GROUPS // S5_SLABS
    eye = jnp.eye(gps, dtype=F32)

    def bdiag(bb):
        b4 = bb.reshape(S5_SLABS, gps, SSM_STATE, SSM_GROUP)
        return jnp.einsum('kgph,gf->kghfp', b4, eye).reshape(S5_SLABS, SLAB_CH, SLAB_ST)

    def cdiag(cc):
        c4 = cc.reshape(S5_SLABS, gps, SSM_GROUP, SSM_STATE)
        return jnp.einsum('kghp,gf->kfpgh', c4, eye).reshape(S5_SLABS, SLAB_ST, SLAB_CH)

    bd = jnp.concatenate([bdiag(bb_re), bdiag(bb_im)], axis=2).astype(BF)
    cd = jnp.concatenate([cdiag(c_re), -cdiag(c_im)], axis=1).astype(BF)
    c2 = lambda shp: pl.BlockSpec(shp, lambda bi, ti: (0,) * len(shp))
    return pl.pallas_call(
        functools.partial(_s5_body, lt=lt),
        grid=(B // S5_BATCH, S // lt),
        in_specs=[
            pl.BlockSpec((S5_BATCH, lt, D_MODEL), lambda bi, ti: (bi, ti, 0)),
            c2((D_MODEL, D_MODEL)), c2((1, N_STATE)), c2((1, N_STATE)),
            c2(bd.shape), c2(cd.shape), c2((1, D_MODEL)),
            c2((D_MODEL, D_MODEL)), c2((D_MODEL, D_MODEL)), c2((1, D_MODEL)), c2((1, D_MODEL)),
        ],
        out_specs=[pl.BlockSpec((S5_BATCH, lt, D_MODEL), lambda bi, ti: (bi, ti, 0)),
                   pl.BlockSpec((S5_BATCH, lt * NWORD, LANES), lambda bi, ti: (bi, ti, 0))],
        out_shape=[jax.ShapeDtypeStruct((B, S, D_MODEL), F32),
                   jax.ShapeDtypeStruct((B, S * NWORD, LANES), jnp.uint32)],
        scratch_shapes=[
            pltpu.VMEM((D_MODEL // LANES, rows, LANES), F32),
            pltpu.VMEM((NWORD, rows, LANES), jnp.uint32),
            pltpu.VMEM((rows, N_STATE), F32),
            pltpu.VMEM((rows, N_STATE), F32),
            pltpu.VMEM((S5_BATCH, N_STATE), F32),
            pltpu.VMEM((S5_BATCH, N_STATE), F32),
        ],
        compiler_params=_cparams(("parallel", "arbitrary")),
        name="s5_mixer",
    )(x3, w_in.astype(BF), a_re.reshape(1, N_STATE), a_im.reshape(1, N_STATE), bd, cd,
      d_skip.reshape(1, D_MODEL), w_glu.astype(BF), w_out.astype(BF), ln_g.reshape(1, -1), ln_b.reshape(1, -1))


def _moe_block(h2, packed, router, router_bias, w_gate, w_up, w_down, sh_gate, sh_up, sh_down, ln_g, ln_b):
    pos, gate, meta = _router(h2, router.T, router_bias.reshape(-1, 1))
    off = meta[:, 0, :N_EXPERTS].reshape(-1)
    cnt = meta[:, 1, :N_EXPERTS].reshape(-1)
    wgu = jnp.concatenate([w_gate, w_up], axis=-1).astype(BF)
    wsgu = jnp.concatenate([sh_gate, sh_up], axis=-1).astype(BF)
    pos = (pos.T * NWORD).reshape(-1)
    return _moe_experts(h2, packed, off, cnt, pos, gate.T, wgu, w_down.astype(BF), wsgu, sh_down.astype(BF),
                        ln_g.reshape(1, -1), ln_b.reshape(1, -1))


def kernel(x, l0_w_in, l0_cmp_pos_k, l0_cmp_pos_v, l0_cmp_k_w1, l0_cmp_k_w2, l0_cmp_v_w1, l0_cmp_v_w2, l0_w_out, l0_ln1_g, l0_ln1_b, l0_router, l0_router_bias, l0_w_gate, l0_w_up, l0_w_down, l0_sh_gate, l0_sh_up, l0_sh_down, l0_ln2_g, l0_ln2_b, l1_w_in, l1_lambda_re, l1_lambda_im, l1_b_re, l1_b_im, l1_c_re, l1_c_im, l1_d, l1_log_dt, l1_w_glu, l1_w_out, l1_ln1_g, l1_ln1_b, l1_router, l1_router_bias, l1_w_gate, l1_w_up, l1_w_down, l1_sh_gate, l1_sh_up, l1_sh_down, l1_ln2_g, l1_ln2_b):
    B, S, D = x.shape
    assert D == D_MODEL and S % 1024 == 0 and B % S5_BATCH == 0 and (B * S) % MOE_TILE == 0
    T = B * S
    h, hp = _mixer0(x.reshape(T, D), B, S, l0_w_in, l0_cmp_pos_k, l0_cmp_pos_v, l0_cmp_k_w1, l0_cmp_k_w2,
                    l0_cmp_v_w1, l0_cmp_v_w2, l0_w_out, l0_ln1_g, l0_ln1_b)
    h = _moe_block(h, hp, l0_router, l0_router_bias, l0_w_gate, l0_w_up, l0_w_down, l0_sh_gate, l0_sh_up,
                   l0_sh_down, l0_ln2_g, l0_ln2_b)
    h, hp = _mixer1(h.reshape(B, S, D), l1_w_in, l1_lambda_re, l1_lambda_im, l1_b_re, l1_b_im, l1_c_re,
                    l1_c_im, l1_d, l1_log_dt, l1_w_glu, l1_w_out, l1_ln1_g, l1_ln1_b)
    h = _moe_block(h.reshape(T, D), hp.reshape(T * NWORD, LANES), l1_router, l1_router_bias, l1_w_gate, l1_w_up,
                   l1_w_down, l1_sh_gate, l1_sh_up, l1_sh_down, l1_ln2_g, l1_ln2_b)
    return h.reshape(B, S, D)
```

```python
import functools
import math

import numpy as np
import jax
import jax.numpy as jnp
from jax import lax
from jax.experimental import pallas as pl
from jax.experimental.pallas import tpu as pltpu

F32 = jnp.float32
BF = jnp.bfloat16

D_MODEL = 1024
DEPTH = 2
HEAD_DIM = 64
LANES = 128
SB_HEADS = 8
SB_WIDTH = SB_HEADS * HEAD_DIM
NSA_HEADS = 8
NSA_KV_GROUPS = 2
NSA_REP = NSA_HEADS // NSA_KV_GROUPS
NSA_WIDTH = NSA_HEADS * HEAD_DIM
NSA_N_BRANCH = 3
CMP_LEN = 32
CMP_STRIDE = 16
CMP_HIDDEN = 256
SLC_LEN = 64
SLC_TOPN = 8
SLC_LOCAL = 2
SLC_FORCE_BONUS = 1e4
WINDOW = 512
ROPE_THETA = 500000.0
ROPE_DIM = HEAD_DIM // 4
SSM_GROUP = 16
SSM_GROUPS = D_MODEL // SSM_GROUP
SSM_STATE = 64
N_EXPERTS = 64
N_EXPERT_GROUPS = 8
TOPK_GROUPS = 4
TOP_K = 6
EXPERT_HIDDEN = 256
SHARED_HIDDEN = 256
ROUTED_SCALE = 2.5
DN_ALPHA = (2 * DEPTH) ** 0.25
LN_EPS = 1e-5
NEG_INF = -1e30
ATT_SCALE = HEAD_DIM ** -0.5
LOG2_E = math.log2(math.e)
SB_UNDERFLOW = -104.0
WIN_Q = 128
SB_ROWS = 128

V7X_VMEM_BYTES = 64 * 1024 * 1024
VMEM_LIMIT = V7X_VMEM_BYTES - 8 * 1024 * 1024
MOE_VMEM_LIMIT = V7X_VMEM_BYTES - 4 * 1024 * 1024


def _cparams(sem, vmem=VMEM_LIMIT):
    return pltpu.CompilerParams(dimension_semantics=sem, vmem_limit_bytes=vmem)


def _dot(a, b):
    return jnp.dot(a, b, preferred_element_type=F32)


def _dot_nt(a, b):
    return lax.dot_general(a, b, (((1,), (1,)), ((), ())), preferred_element_type=F32)


def _split2(x):
    hi = x.astype(BF)
    lo = (x - hi.astype(F32)).astype(BF)
    return hi, lo


def _layer_norm(h, g, b):
    mu = jnp.mean(h, axis=-1, keepdims=True)
    d = h - mu
    var = jnp.mean(d * d, axis=-1, keepdims=True)
    return d * lax.rsqrt(var + LN_EPS) * g + b


HALF = D_MODEL // 2
NWORD = HALF // LANES
HI_MASK = 0xFFFF0000


def _pack_pairs(a, b):
    lo = pltpu.bitcast(a.astype(BF).astype(F32), jnp.uint32)
    hi = pltpu.bitcast(b.astype(BF).astype(F32), jnp.uint32)
    return (lo >> 16) | (hi & jnp.uint32(HI_MASK))


def _pack_row_words(h):
    return [_pack_pairs(h[:, c * LANES:(c + 1) * LANES], h[:, HALF + c * LANES:HALF + (c + 1) * LANES])
            for c in range(NWORD)]


def _store_row_words(ref, start, n, h):
    for c, w in enumerate(_pack_row_words(h)):
        ref[pl.ds(start * NWORD + c, n, stride=NWORD), :] = w


def _load_row_words(ref, start, n):
    return [ref[pl.ds(start * NWORD + c, n, stride=NWORD), :] for c in range(NWORD)]


def _unpack_lo(w):
    return pltpu.bitcast(w << 16, F32)


def _unpack_hi(w):
    return pltpu.bitcast(w & jnp.uint32(HI_MASK), F32)


N_PLAIN = 3 * SB_WIDTH + LANES
N_TRANS = 2 * LANES + 32
N_ROPE = NSA_WIDTH + 3 * LANES


def _proj0_body(x_ref, wp_ref, wt_ref, wr_ref, cos_ref, sin_ref,
                qkva_ref, vc_ref, vt_ref, gt_ref, qb_ref, kc_ref, ksw_ref):
    xb = x_ref[...].astype(BF)
    a0 = 3 * SB_WIDTH
    qkva_ref[...] = _dot(xb, wp_ref[:, 0:a0]).astype(BF)
    vc_ref[...] = _dot(xb, wp_ref[:, a0:N_PLAIN]).astype(BF)
    yt = _dot_nt(wt_ref[...], xb)
    vt_ref[0] = yt[0:2 * LANES].astype(BF)
    gt_ref[0] = jax.nn.sigmoid(yt[2 * LANES:N_TRANS])
    y = _dot(xb, wr_ref[...])
    half = ROPE_DIM // 2
    pos = lax.broadcasted_iota(jnp.int32, (1, N_ROPE), 1) % HEAD_DIM
    partner = jnp.where(pos < half, -pltpu.roll(y, N_ROPE - half, 1), pltpu.roll(y, half, 1))
    y = y * cos_ref[...] + partner * sin_ref[...]
    qb_ref[...] = y[:, 0:NSA_WIDTH].astype(BF)
    kc_ref[...] = y[:, NSA_WIDTH:NSA_WIDTH + LANES].astype(BF)
    ksw_ref[...] = y[:, NSA_WIDTH + LANES:N_ROPE].astype(BF)


def _proj0(x2, wp, wt, wr, cos_t, sin_t, batch, seq):
    T = x2.shape[0]
    tm = 1024
    nseq = seq // tm
    row = lambda i: (i, 0)
    full = lambda i: (0, 0)
    tab = lambda i: (i % nseq, 0)
    trans = lambda i: (i // nseq, 0, i % nseq)
    return pl.pallas_call(
        _proj0_body,
        grid=(T // tm,),
        in_specs=[
            pl.BlockSpec((tm, D_MODEL), row),
            pl.BlockSpec((D_MODEL, N_PLAIN), full),
            pl.BlockSpec((N_TRANS, D_MODEL), full),
            pl.BlockSpec((D_MODEL, N_ROPE), full),
            pl.BlockSpec((tm, N_ROPE), tab),
            pl.BlockSpec((tm, N_ROPE), tab),
        ],
        out_specs=[
            pl.BlockSpec((tm, 3 * SB_WIDTH), row),
            pl.BlockSpec((tm, LANES), row),
            pl.BlockSpec((1, 2 * LANES, tm), trans),
            pl.BlockSpec((1, N_TRANS - 2 * LANES, tm), trans),
            pl.BlockSpec((tm, NSA_WIDTH), row),
            pl.BlockSpec((tm, LANES), row),
            pl.BlockSpec((tm, 2 * LANES), row),
        ],
        out_shape=[
            jax.ShapeDtypeStruct((T, 3 * SB_WIDTH), BF),
            jax.ShapeDtypeStruct((T, LANES), BF),
            jax.ShapeDtypeStruct((batch, 2 * LANES, seq), BF),
            jax.ShapeDtypeStruct((batch, N_TRANS - 2 * LANES, seq), F32),
            jax.ShapeDtypeStruct((T, NSA_WIDTH), BF),
            jax.ShapeDtypeStruct((T, LANES), BF),
            jax.ShapeDtypeStruct((T, 2 * LANES), BF),
        ],
        compiler_params=_cparams(("parallel",)),
        name="proj0",
    )(x2, wp, wt, wr, cos_t, sin_t)


def _sb_body(q_ref, k_ref, v_ref, o_ref, *, tq):
    i = pl.program_id(2)
    q = q_ref[0]
    lane = lax.broadcasted_iota(jnp.int32, (1, LANES), 1)
    row = lax.broadcasted_iota(jnp.int32, (tq, tq), 0)
    col = lax.broadcasted_iota(jnp.int32, (tq, tq), 1)
    tri = jnp.where(row > col, 1.0, 0.0).astype(BF)
    diag_causal = col < row

    hms = [(lane // HEAD_DIM) == hh for hh in range(2)]
    nchunk = tq // SB_ROWS
    chains = [(hh, rc) for hh in range(2) for rc in range(nchunk)]
    qcs = [jnp.where(hms[hh], q[rc * SB_ROWS:(rc + 1) * SB_ROWS], jnp.zeros((SB_ROWS, LANES), BF))
           for hh, rc in chains]

    def blocks(specs, state):
        nc = len(qcs)
        kvs = []
        for j, _, _ in specs:
            off = pl.multiple_of(j * tq, tq)
            kvs.append((k_ref[0, pl.ds(off, tq), :], v_ref[0, pl.ds(off, tq), :]))
        zs = [[_dot_nt(qc, k) for qc in qcs] for k, _ in kvs]
        mids = []
        for zrow, (_, masks, flag) in zip(zs, specs):
            row = []
            for n, z in enumerate(zrow):
                nl = jnp.maximum(z, 0.0) + jnp.log(1.0 + jnp.exp(-jnp.abs(z)))
                if masks is not None:
                    nl = jnp.where(masks[n], nl, 0.0)
                if flag is not None:
                    nl = jnp.where(flag, nl, 0.0)
                row.append((nl, z, nl.astype(BF)))
            mids.append(row)
        sufs = [[_dot(hi, tri) for _, _, hi in row] for row in mids]
        carries = [state[2 * n] for n in range(nc)]
        ws = []
        for row, srow, (_, masks, flag) in zip(mids, sufs, specs):
            wrow = []
            for n, ((nl, z, _), suffix) in enumerate(zip(row, srow)):
                w = jnp.exp(z - nl - suffix + carries[n])
                if masks is not None:
                    w = jnp.where(masks[n], w, 0.0)
                if flag is not None:
                    w = jnp.where(flag, w, 0.0)
                wrow.append(w.astype(BF))
                carries[n] = carries[n] - jnp.sum(nl, axis=-1, keepdims=True)
            ws.append(wrow)
        new = []
        for n in range(nc):
            acc = state[2 * n + 1]
            for wrow, (_, v) in zip(ws, kvs):
                acc = acc + _dot(wrow[n], v)
            new.extend([carries[n], acc])
        return new

    zc, za = jnp.zeros((SB_ROWS, 1), F32), jnp.zeros((SB_ROWS, LANES), F32)
    diag_masks = [diag_causal[rc * SB_ROWS:(rc + 1) * SB_ROWS] for _, rc in chains]
    state = blocks([(i, diag_masks, None), (jnp.maximum(i - 1, 0), None, i > 0)], [zc, za] * len(chains))

    def live(state):
        top = jnp.max(state[0])
        for c in state[2::2]:
            top = jnp.maximum(top, jnp.max(c))
        return top > SB_UNDERFLOW

    def cond(st):
        return jnp.logical_and(st[0] >= 0, st[1])

    def body(st):
        new = blocks([(st[0], None, None)], st[2])
        return st[0] - 1, live(new), tuple(new)

    _, _, state = lax.while_loop(cond, body, (i - 2, live(state), tuple(state)))
    accs = [jnp.concatenate([state[2 * (hh * nchunk + rc) + 1] for rc in range(nchunk)], axis=0)
            for hh in range(2)]
    o_ref[0] = jnp.where(hms[0], accs[0], accs[1]).astype(BF)


def _sb_attention(qkva3):
    B, S, _ = qkva3.shape
    tq = 256
    npair = SB_WIDTH // LANES
    return pl.pallas_call(
        functools.partial(_sb_body, tq=tq),
        grid=(B, npair, S // tq),
        in_specs=[
            pl.BlockSpec((1, tq, LANES), lambda b, p, i: (b, i, p)),
            pl.BlockSpec((1, S, LANES), lambda b, p, i: (b, 0, npair + p)),
            pl.BlockSpec((1, S, LANES), lambda b, p, i: (b, 0, 2 * npair + p)),
        ],
        out_specs=pl.BlockSpec((1, tq, LANES), lambda b, p, i: (b, i, p)),
        out_shape=jax.ShapeDtypeStruct((B, S, SB_WIDTH), BF),
        compiler_params=_cparams(("parallel", "parallel", "arbitrary")),
        name="sb_attention",
    )(qkva3, qkva3, qkva3)


def _cmp_body(ak_ref, av_ref, posk_ref, posv_ref, w1k_ref, w1kt_ref, w1kb_ref, w2k_ref,
              w1v_ref, w1vt_ref, w1vb_ref, w2v_ref, kc_ref, vc_ref):
    def one(a_ref, pos_ref, w1_ref, w1t_ref, w1b_ref, w2_ref, o_ref):
        a = a_ref[0]
        n = a.shape[0]
        bias = _dot(pos_ref[...], w1_ref[...])[0:1]
        out = jnp.zeros((n, LANES), F32)
        for g in range(NSA_KV_GROUPS):
            top = _dot(a, w1t_ref[g])
            bot = _dot(a, w1b_ref[g])
            h = top + pltpu.roll(bot, n - 1, 0) + bias
            out = out + _dot(jax.nn.gelu(h).astype(BF), w2_ref[g])
        o_ref[0] = out.astype(BF)

    one(ak_ref, posk_ref, w1k_ref, w1kt_ref, w1kb_ref, w2k_ref, kc_ref)
    one(av_ref, posv_ref, w1v_ref, w1vt_ref, w1vb_ref, w2v_ref, vc_ref)


def _compress(ak, av, posk, posv, wk, wv):
    B, n, width = ak.shape
    blk = pl.BlockSpec((1, n, width), lambda b: (b, 0, 0))
    c2 = lambda shp: pl.BlockSpec(shp, lambda b: (0, 0))
    c3 = lambda shp: pl.BlockSpec(shp, lambda b: (0, 0, 0))
    wspecs = [c2((CMP_LEN * HEAD_DIM, CMP_HIDDEN)), c3((2, width, CMP_HIDDEN)),
              c3((2, width, CMP_HIDDEN)), c3((2, CMP_HIDDEN, LANES))]
    out = pl.BlockSpec((1, n, LANES), lambda b: (b, 0, 0))
    return pl.pallas_call(
        _cmp_body,
        grid=(B,),
        in_specs=[blk, blk, c2((8, CMP_LEN * HEAD_DIM)), c2((8, CMP_LEN * HEAD_DIM))] + wspecs + wspecs,
        out_specs=[out, out],
        out_shape=[jax.ShapeDtypeStruct((B, n, LANES), BF)] * 2,
        compiler_params=_cparams(("parallel",)),
        name="nsa_compress",
    )(ak, av, posk, posv, *wk, *wv)


def _nsa_body(q_ref, kc_ref, vct_ref, ks_ref, vst_ref, kw_ref, vwt_ref, gt_ref, ovl_ref, et_ref,
              o_ref, *, tq, tk, nblk):
    i = pl.program_id(1)
    R = NSA_REP
    nchain = NSA_KV_GROUPS * R
    t0 = i * tq
    lane = lax.broadcasted_iota(jnp.int32, (1, LANES), 1)
    q = q_ref[0]
    t_row = t0 + lax.broadcasted_iota(jnp.int32, (1, tq), 1)
    tile = lambda a: jnp.concatenate([a] * nchain, axis=1)
    n_col = lax.broadcasted_iota(jnp.int32, (LANES, 1), 0)
    cmp_bias = tile(jnp.where((CMP_STRIDE * n_col + (CMP_LEN - 1)) <= t_row, 0.0, NEG_INF))
    has_cmp = tile(jnp.where(t_row >= CMP_LEN - 1, 1.0, 0.0))
    nkb = (t0 + tq + tk - 1) // tk
    key_last = (nkb - 1) * tk + lax.broadcasted_iota(jnp.int32, (tk, 1), 0)
    diag_bias = tile(jnp.where(key_last <= t_row, 0.0, NEG_INF))
    blk = lax.broadcasted_iota(jnp.int32, (nblk, 1), 0)
    cur = t_row // SLC_LEN
    valid = blk <= cur
    forced = (blk == 0) | (valid & (blk > cur - SLC_LOCAL))
    ovl = ovl_ref[...]

    qrs = [jnp.where((lane // HEAD_DIM) == gi, q[:, LANES * r:LANES * (r + 1)], jnp.zeros((tq, LANES), BF))
           for gi in range(NSA_KV_GROUPS) for r in range(R)]
    q_all = jnp.concatenate(qrs, axis=0)

    s = _dot_nt(kc_ref[0], q_all) + cmp_bias
    e = jnp.exp2(s - jnp.max(s, axis=0, keepdims=True))
    pc = e / jnp.sum(e, axis=0, keepdims=True) * has_cmp
    o_cmp = _dot(vct_ref[0], pc.astype(BF))

    q_aug = []
    for gi in range(NSA_KV_GROUPS):
        psum = jnp.zeros((LANES, tq), F32)
        for r in range(R):
            c = gi * R + r
            psum = psum + pc[:, c * tq:(c + 1) * tq]
        p1 = psum.astype(BF)
        r1 = psum - p1.astype(F32)
        p2 = r1.astype(BF)
        p3 = (r1 - p2.astype(F32)).astype(BF)
        imp = _dot(ovl, p1) + _dot(ovl, p2) + _dot(ovl, p3)
        imp = jnp.where(forced, imp + SLC_FORCE_BONUS, jnp.where(valid, imp, NEG_INF))
        sel = jnp.zeros((nblk, tq), F32)
        for _ in range(min(SLC_TOPN, nblk)):
            mx = jnp.max(imp, axis=0, keepdims=True)
            idx = jnp.min(jnp.where(imp == mx, blk, nblk), axis=0, keepdims=True)
            hit = blk == idx
            sel = jnp.where(hit, 1.0, sel)
            imp = jnp.where(hit, -jnp.inf, imp)
        sel_t = jnp.concatenate([sel, jnp.zeros((LANES - nblk, tq), F32)], axis=0).T
        sel_bias = ((sel_t - 1.0) * -NEG_INF).astype(BF)
        q_aug.extend(jnp.concatenate([qrs[gi * R + r], sel_bias], axis=1) for r in range(R))
    qa_all = jnp.concatenate(q_aug, axis=0)

    def sel_step(j, state, bias):
        m, l, acc = state
        off = pl.multiple_of(j * tk, tk)
        ka = jnp.concatenate([ks_ref[0, pl.ds(off, tk), :], et_ref[pl.ds(off, tk), :]], axis=1)
        s = _dot_nt(ka, qa_all)
        if bias is not None:
            s = s + bias
        m_new = jnp.maximum(m, jnp.max(s, axis=0, keepdims=True))
        alpha = jnp.exp2(m - m_new)
        p = jnp.exp2(s - m_new)
        l = alpha * l + jnp.sum(p, axis=0, keepdims=True)
        acc = alpha * acc + _dot(vst_ref[0, :, pl.ds(off, tk)], p.astype(BF))
        return m_new, l, acc

    width = nchain * tq
    init = (jnp.full((1, width), NEG_INF, F32), jnp.zeros((1, width), F32), jnp.zeros((LANES, width), F32))
    state = lax.fori_loop(0, nkb - 1, lambda j, st: sel_step(j, st, None), init)
    _, l_s, acc_s = sel_step(nkb - 1, state, diag_bias)
    o_sel = acc_s / l_s

    wkeys = WINDOW + WIN_Q
    o_parts = []
    for h in range(tq // WIN_Q):
        th = t_row[:, h * WIN_Q:(h + 1) * WIN_Q]
        woff = pl.multiple_of(jnp.maximum(t0 + h * WIN_Q - WINDOW, 0), WIN_Q)
        kp = woff + lax.broadcasted_iota(jnp.int32, (wkeys, 1), 0)
        bias = tile(jnp.where(kp <= th, jnp.where(kp > th - WINDOW, 0.0, NEG_INF), NEG_INF))
        qh = jnp.concatenate([qr[h * WIN_Q:(h + 1) * WIN_Q] for qr in qrs], axis=0)
        s = _dot_nt(kw_ref[0, pl.ds(woff, wkeys), :], qh) + bias
        p = jnp.exp2(s - jnp.max(s, axis=0, keepdims=True))
        o_parts.append(_dot(vwt_ref[0, :, pl.ds(woff, wkeys)], p.astype(BF)) / jnp.sum(p, axis=0, keepdims=True))
    o_win = jnp.concatenate([o[:, c * WIN_Q:(c + 1) * WIN_Q] for c in range(nchain) for o in o_parts], axis=1)

    gt = gt_ref[0]
    sub = lax.broadcasted_iota(jnp.int32, (LANES, 1), 0)
    g0 = (sub // HEAD_DIM) == 0
    for r in range(R):
        out = jnp.zeros((LANES, tq), F32)
        for br, o_br in enumerate((o_cmp, o_sel, o_win)):
            rows = [(gi * R + r) * NSA_N_BRANCH + br for gi in range(NSA_KV_GROUPS)]
            gate = jnp.where(g0, gt[rows[0]:rows[0] + 1], gt[rows[1]:rows[1] + 1])
            both = jnp.where(g0, o_br[:, r * tq:(r + 1) * tq], o_br[:, (R + r) * tq:(R + r + 1) * tq])
            out = out + gate * both
        o_ref[0, r * LANES:(r + 1) * LANES, :] = out.astype(BF)


def _nsa_attention(qb3, kc, vct, ksw3, vsw_t, gt, ovl, expand):
    B, S, _ = qb3.shape
    tq, tk = 512, 512
    n = kc.shape[1]
    nblk = ovl.shape[0]
    ngate = gt.shape[1]
    assert S >= WINDOW + WIN_Q and tq % WIN_Q == 0 and S % tk == 0 and nblk % 8 == 0 and n == LANES
    tok = lambda c: pl.BlockSpec((1, S, LANES), lambda b, i, c=c: (b, 0, c))
    tr = lambda c: pl.BlockSpec((1, LANES, S), lambda b, i, c=c: (b, c, 0))
    return pl.pallas_call(
        functools.partial(_nsa_body, tq=tq, tk=tk, nblk=nblk),
        grid=(B, S // tq),
        in_specs=[
            pl.BlockSpec((1, tq, NSA_WIDTH), lambda b, i: (b, i, 0)),
            pl.BlockSpec((1, n, LANES), lambda b, i: (b, 0, 0)),
            pl.BlockSpec((1, LANES, n), lambda b, i: (b, 0, 0)),
            tok(0), tr(0), tok(1), tr(1),
            pl.BlockSpec((1, ngate, tq), lambda b, i: (b, 0, i)),
            pl.BlockSpec((nblk, LANES), lambda b, i: (0, 0)),
            pl.BlockSpec((S, LANES), lambda b, i: (0, 0)),
        ],
        out_specs=pl.BlockSpec((1, NSA_WIDTH, tq), lambda b, i: (b, 0, i)),
        out_shape=jax.ShapeDtypeStruct((B, NSA_WIDTH, S), BF),
        compiler_params=_cparams(("parallel", "arbitrary")),
        name="nsa_attention",
    )(qb3, kc, vct, ksw3, vsw_t, ksw3, vsw_t, gt, ovl, expand)


def _outproj_ln_body(oa_ref, obt_ref, x_ref, wa_ref, wb_ref, g_ref, b_ref, o_ref, p_ref):
    yb = lax.dot_general(obt_ref[0], wb_ref[...], (((0,), (0,)), ((), ())), preferred_element_type=F32)
    y = _dot(oa_ref[...], wa_ref[...]) + yb
    res = _layer_norm(DN_ALPHA * x_ref[...] + y, g_ref[...], b_ref[...])
    o_ref[...] = res
    _store_row_words(p_ref, 0, res.shape[0], res)


def _outproj_ln(oa, obt, x2, wa, wb, g, b):
    T = x2.shape[0]
    tm = 1024
    nseq = obt.shape[2] // tm
    row = lambda i: (i, 0)
    full = lambda i: (0, 0)
    return pl.pallas_call(
        _outproj_ln_body,
        grid=(T // tm,),
        in_specs=[
            pl.BlockSpec((tm, oa.shape[1]), row),
            pl.BlockSpec((1, obt.shape[1], tm), lambda i: (i // nseq, 0, i % nseq)),
            pl.BlockSpec((tm, D_MODEL), row),
            pl.BlockSpec(wa.shape, full),
            pl.BlockSpec(wb.shape, full),
            pl.BlockSpec((1, D_MODEL), full),
            pl.BlockSpec((1, D_MODEL), full),
        ],
        out_specs=[pl.BlockSpec((tm, D_MODEL), row),
                   pl.BlockSpec((tm * NWORD, LANES), row)],
        out_shape=[jax.ShapeDtypeStruct((T, D_MODEL), F32),
                   jax.ShapeDtypeStruct((T * NWORD, LANES), jnp.uint32)],
        compiler_params=_cparams(("parallel",)),
        name="outproj_ln",
    )(oa, obt, x2, wa, wb, g, b)


def _rope_tables(seq):
    inv = jnp.power(ROPE_THETA, -jnp.arange(0, ROPE_DIM, 2, dtype=F32) / ROPE_DIM)
    ang = jnp.arange(seq, dtype=F32)[:, None] * inv[None, :]
    half = ROPE_DIM // 2
    rest = HEAD_DIM - ROPE_DIM
    cos_h = jnp.concatenate([jnp.cos(ang), jnp.cos(ang), jnp.ones((seq, rest), F32)], axis=1)
    sin_h = jnp.concatenate([jnp.sin(ang), jnp.sin(ang), jnp.zeros((seq, rest), F32)], axis=1)
    reps = N_ROPE // HEAD_DIM
    del half
    return jnp.tile(cos_h, (1, reps)), jnp.tile(sin_h, (1, reps))


def _nsa_head_perm():
    cols = []
    for r in range(NSA_REP):
        for g in range(NSA_KV_GROUPS):
            h = g * NSA_REP + r
            cols.extend(range(h * HEAD_DIM, (h + 1) * HEAD_DIM))
    return np.asarray(cols)


def _mixer0(x2, batch, seq, w_in, cmp_pos_k, cmp_pos_v, cmp_k_w1, cmp_k_w2, cmp_v_w1, cmp_v_w2,
            w_out, ln_g, ln_b):
    T = x2.shape[0]
    sizes = [SB_WIDTH] * 3 + [NSA_WIDTH] + [NSA_KV_GROUPS * HEAD_DIM] * 6 + [NSA_HEADS * NSA_N_BRANCH]
    offs = np.concatenate([[0], np.cumsum(sizes)])
    col = lambda j: w_in[:, offs[j]:offs[j + 1]]
    perm = _nsa_head_perm()
    ngate = sizes[-1]
    wp = jnp.concatenate([col(0) * ATT_SCALE, col(1), col(2), col(5)], axis=1)
    wt = jnp.concatenate([col(7), col(9), jnp.pad(col(10), ((0, 0), (0, N_TRANS - 2 * LANES - ngate)))], axis=1).T
    wr = jnp.concatenate([col(3)[:, perm] * (ATT_SCALE * LOG2_E), col(4), col(6), col(8)], axis=1)
    cos_t, sin_t = _rope_tables(seq)
    qkva, vc_tok, vsw_t, gt, qb, kc_tok, ksw = _proj0(x2, wp.astype(BF), wt.astype(BF), wr.astype(BF), cos_t, sin_t,
                                                      batch, seq)

    r3 = lambda a: a.reshape(batch, seq, a.shape[-1])
    o_a = _sb_attention(r3(qkva))

    ncmp = seq // CMP_STRIDE
    ak = kc_tok.reshape(batch, ncmp, CMP_STRIDE * LANES)
    av = vc_tok.reshape(batch, ncmp, CMP_STRIDE * LANES)

    def cmp_weights(w1, w2):
        w1r = w1.reshape(2, CMP_STRIDE, HEAD_DIM, CMP_HIDDEN)
        tops, bots, w2s = [], [], []
        for g in range(NSA_KV_GROUPS):
            ext = jnp.zeros((2, CMP_STRIDE, NSA_KV_GROUPS, HEAD_DIM, CMP_HIDDEN), F32).at[:, :, g].set(w1r)
            ext = ext.reshape(2, CMP_STRIDE * LANES, CMP_HIDDEN)
            tops.append(ext[0])
            bots.append(ext[1])
            w2s.append(jnp.zeros((CMP_HIDDEN, LANES), F32).at[:, g * HEAD_DIM:(g + 1) * HEAD_DIM].set(w2))
        return (w1.astype(BF), jnp.stack(tops).astype(BF), jnp.stack(bots).astype(BF),
                jnp.stack(w2s).astype(BF))

    posk = jnp.broadcast_to(cmp_pos_k.reshape(1, -1), (8, CMP_LEN * HEAD_DIM)).astype(BF)
    posv = jnp.broadcast_to(cmp_pos_v.reshape(1, -1), (8, CMP_LEN * HEAD_DIM)).astype(BF)
    kc, vc = _compress(ak, av, posk, posv, cmp_weights(cmp_k_w1, cmp_k_w2), cmp_weights(cmp_v_w1, cmp_v_w2))
    assert ncmp <= LANES
    if ncmp < LANES:
        kc = jnp.pad(kc, ((0, 0), (0, LANES - ncmp), (0, 0)))
        vc = jnp.pad(vc, ((0, 0), (0, LANES - ncmp), (0, 0)))

    n_slc = seq // SLC_LEN
    cmp_start = np.arange(ncmp) * CMP_STRIDE
    slc_start = np.arange(n_slc) * SLC_LEN
    ovl = ((cmp_start[None, :] <= slc_start[:, None] + SLC_LEN - 1)
           & (cmp_start[None, :] + CMP_LEN - 1 >= slc_start[:, None])).astype(np.float32)
    ovl = np.pad(ovl, ((0, 0), (0, LANES - ncmp))) if ncmp < LANES else ovl
    expand = (np.arange(seq)[:, None] // SLC_LEN == np.arange(LANES)[None, :]).astype(np.float32)
    o_bt = _nsa_attention(r3(qb), kc, jnp.swapaxes(vc, 1, 2), r3(ksw), vsw_t, gt,
                          jnp.asarray(ovl, BF), jnp.asarray(expand, BF))

    wa = w_out[:SB_WIDTH].astype(BF)
    wb = w_out[SB_WIDTH:][perm].astype(BF)
    return _outproj_ln(o_a.reshape(T, SB_WIDTH), o_bt, x2, wa, wb,
                       ln_g.reshape(1, -1), ln_b.reshape(1, -1))


MOE_TILE = 2048
ROW_ALIGN = 8
RANK_CHUNK = 256


def _top_rows(vals, ids, n_ids, count):
    hits = []
    for _ in range(count):
        mx = jnp.max(vals, axis=0, keepdims=True)
        idx = jnp.min(jnp.where(vals == mx, ids, n_ids), axis=0, keepdims=True)
        hit = ids == idx
        hits.append(hit)
        vals = jnp.where(hit, -jnp.inf, vals)
    return hits


def _router_body(h_ref, rt_ref, rb_ref, pos_ref, gate_ref, meta_ref, *, tm):
    E, NG = N_EXPERTS, N_EXPERT_GROUPS
    per = E // NG
    hh, hl = _split2(h_ref[...])
    rh, rl = _split2(rt_ref[...])
    logits = _dot_nt(rh, hh) + _dot_nt(rh, hl) + _dot_nt(rl, hh)
    scores = jax.nn.sigmoid(logits)
    biased = scores + rb_ref[...]
    i8 = lax.broadcasted_iota(jnp.int32, (per, tm), 0)
    gs = []
    for g in range(NG):
        v = biased[g * per:(g + 1) * per]
        m1 = jnp.max(v, axis=0, keepdims=True)
        a1 = jnp.min(jnp.where(v == m1, i8, per), axis=0, keepdims=True)
        m2 = jnp.max(jnp.where(i8 == a1, -jnp.inf, v), axis=0, keepdims=True)
        gs.append(m1 + m2)
    gs = jnp.concatenate(gs, axis=0)
    gi = lax.broadcasted_iota(jnp.int32, (NG, tm), 0)
    ghits = _top_rows(gs, gi, NG, TOPK_GROUPS)
    gkeep = jnp.zeros((NG, tm), F32)
    for hit in ghits:
        gkeep = jnp.where(hit, 1.0, gkeep)
    ekeep = jnp.concatenate([jnp.broadcast_to(gkeep[g:g + 1], (per, tm)) for g in range(NG)], axis=0)
    ei = lax.broadcasted_iota(jnp.int32, (E, tm), 0)
    hits = _top_rows(jnp.where(ekeep > 0.5, biased, -jnp.inf), ei, E, TOP_K)
    gates = [jnp.sum(jnp.where(hit, scores, 0.0), axis=0, keepdims=True) for hit in hits]
    gsum = gates[0]
    for gk in gates[1:]:
        gsum = gsum + gk
    gates = [gk / gsum * ROUTED_SCALE for gk in gates]

    member = jnp.zeros((E, tm), F32)
    for hit in hits:
        member = jnp.where(hit, 1.0, member)
    cnt_col = jnp.sum(member, axis=1, keepdims=True)
    pad_col = jnp.floor((cnt_col + (ROW_ALIGN - 1)) * (1.0 / ROW_ALIGN)) * ROW_ALIGN
    sub_e = lax.broadcasted_iota(jnp.int32, (E, LANES), 0)
    lane_e = lax.broadcasted_iota(jnp.int32, (E, LANES), 1)
    cnt_row = jnp.sum(jnp.where(sub_e == lane_e, cnt_col, 0.0), axis=0, keepdims=True)
    pad_row = jnp.sum(jnp.where(sub_e == lane_e, pad_col, 0.0), axis=0, keepdims=True)
    off_row = jnp.sum(jnp.where(sub_e < lane_e, pad_col, 0.0), axis=0, keepdims=True)
    off_col = jnp.sum(jnp.where(lane_e < sub_e, pad_row, 0.0), axis=1, keepdims=True)
    r_i = lax.broadcasted_iota(jnp.int32, (RANK_CHUNK, RANK_CHUNK), 0)
    c_i = lax.broadcasted_iota(jnp.int32, (RANK_CHUNK, RANK_CHUNK), 1)
    before = jnp.where(r_i < c_i, 1.0, 0.0).astype(BF)
    running = off_col
    ranks = []
    for c in range(tm // RANK_CHUNK):
        mc = member[:, c * RANK_CHUNK:(c + 1) * RANK_CHUNK]
        ranks.append(_dot(mc.astype(BF), before) + running)
        running = running + jnp.sum(mc, axis=1, keepdims=True)
    slot = jnp.concatenate(ranks, axis=1)
    pos = [jnp.sum(jnp.where(hit, slot, 0.0), axis=0, keepdims=True) for hit in hits]
    zrow = jnp.zeros((1, tm), F32)
    pos_ref[...] = jnp.concatenate(pos + [zrow, zrow], axis=0).astype(jnp.int32)
    gate_ref[...] = jnp.concatenate(gates + [zrow, zrow], axis=0)
    z128 = jnp.zeros((1, LANES), F32)
    meta_ref[0] = jnp.concatenate([off_row, cnt_row] + [z128] * 6, axis=0).astype(jnp.int32)


def _router(h2, router_t, bias_col):
    T = h2.shape[0]
    tm = MOE_TILE
    nt = T // tm
    return pl.pallas_call(
        functools.partial(_router_body, tm=tm),
        grid=(nt,),
        in_specs=[
            pl.BlockSpec((tm, D_MODEL), lambda i: (i, 0)),
            pl.BlockSpec((N_EXPERTS, D_MODEL), lambda i: (0, 0)),
            pl.BlockSpec((N_EXPERTS, 1), lambda i: (0, 0)),
        ],
        out_specs=[
            pl.BlockSpec((8, tm), lambda i: (0, i)),
            pl.BlockSpec((8, tm), lambda i: (0, i)),
            pl.BlockSpec((1, 8, LANES), lambda i: (i, 0, 0)),
        ],
        out_shape=[
            jax.ShapeDtypeStruct((8, T), jnp.int32),
            jax.ShapeDtypeStruct((8, T), F32),
            jax.ShapeDtypeStruct((nt, 8, LANES), jnp.int32),
        ],
        compiler_params=_cparams(("parallel",)),
        name="moe_router",
    )(h2, router_t, bias_col)


EXPERTS_PER_STEP = 4
EXPERT_CHUNK = 256
COMBINE_SUB = 256
POS_STRIDE = 8
GATHER_UNROLL = 8


def _swiglu(xb, wgu, wd, hidden):
    gu = _dot(xb, wgu)
    a = jax.nn.silu(gu[:, :hidden]) * gu[:, hidden:]
    return _dot(a.astype(BF), wd)


def _moe_body(off_ref, cnt_ref, pos_ref, src_ref, x_ref, gcol_ref, wgu_ref, wd_ref, wsgu_ref, wsd_ref,
              g_ref, b_ref, o_ref, xs_ref, z_ref, *, tm, eb, ch, sub, unroll):
    i = pl.program_id(0)
    j = pl.program_id(1)
    nj = N_EXPERTS // eb

    @pl.when(j == 0)
    def _dispatch():
        zeros = jnp.zeros((ROW_ALIGN * NWORD, LANES), jnp.uint32)

        def pad(e, carry):
            off, cnt = off_ref[i * N_EXPERTS + e], cnt_ref[i * N_EXPERTS + e]
            last = pl.multiple_of((off + cnt // ROW_ALIGN * ROW_ALIGN) * NWORD, ROW_ALIGN * NWORD)
            xs_ref[pl.ds(last, ROW_ALIGN * NWORD), :] = zeros
            return carry

        lax.fori_loop(0, N_EXPERTS, pad, 0)
        end = off_ref[i * N_EXPERTS + N_EXPERTS - 1] + cnt_ref[i * N_EXPERTS + N_EXPERTS - 1]
        end = pl.multiple_of((end + ROW_ALIGN - 1) // ROW_ALIGN * ROW_ALIGN * NWORD, ROW_ALIGN * NWORD)
        xs_ref[pl.ds(end, ch * NWORD), :] = jnp.zeros((ch * NWORD, LANES), jnp.uint32)

        def tok(tb, carry):
            for u in range(unroll):
                t = tb * unroll + u
                slab = src_ref[pl.ds(pl.multiple_of(t * NWORD, NWORD), NWORD), :]
                for k in range(TOP_K):
                    p = pl.multiple_of(pos_ref[t * POS_STRIDE + k], NWORD)
                    xs_ref[pl.ds(p, NWORD), :] = slab
            return carry

        lax.fori_loop(0, tm // unroll, tok, 0)

    def chunks_in(els, offs, c):
        r0s = [pl.multiple_of(off + c * ch, ROW_ALIGN) for off in offs]
        words = [_load_row_words(xs_ref, r0, ch) for r0 in r0s]
        xbs = [jnp.concatenate([_unpack_lo(w).astype(BF) for w in ws]
                               + [_unpack_hi(w).astype(BF) for w in ws], axis=1) for ws in words]
        gus = [_dot(xb, wgu_ref[el]) for xb, el in zip(xbs, els)]
        acts = [(jax.nn.silu(gu[:, :EXPERT_HIDDEN]) * gu[:, EXPERT_HIDDEN:]).astype(BF) for gu in gus]
        ys = [_dot(a, wd_ref[el]) for a, el in zip(acts, els)]
        return list(zip(r0s, words, ys))

    def chunk_out(r0, words, y, c, cnt):
        keep = (c * ch + lax.broadcasted_iota(jnp.int32, (ch, 1), 0)) < cnt
        for cc, packed in enumerate(_pack_row_words(y)):
            xs_ref[pl.ds(r0 * NWORD + cc, ch, stride=NWORD), :] = jnp.where(keep, packed, words[cc])

    @pl.when(j < nj)
    def _experts():
        offs = [off_ref[i * N_EXPERTS + j * eb + el] for el in range(eb)]
        cnts = [cnt_ref[i * N_EXPERTS + j * eb + el] for el in range(eb)]
        firsts = chunks_in(list(range(eb)), offs, 0)
        for el in range(eb):
            chunk_out(*firsts[el], 0, cnts[el])
        for el in range(eb):
            def chunk(c, carry, el=el):
                chunk_out(*chunks_in([el], [offs[el]], c)[0], c, cnts[el])
                return carry

            lax.fori_loop(1, (cnts[el] + ch - 1) // ch, chunk, 0)

    @pl.when(j >= nj)
    def _combine():
        base = (j - nj) * sub

        def tok(tb, carry):
            for u in range(unroll):
                tl = tb * unroll + u
                dst = pl.multiple_of(tl * NWORD, NWORD)
                for k in range(TOP_K):
                    p = pl.multiple_of(pos_ref[(base + tl) * POS_STRIDE + k], NWORD)
                    z_ref[k, pl.ds(dst, NWORD), :] = xs_ref[pl.ds(p, NWORD), :]
            return carry

        lax.fori_loop(0, sub // unroll, tok, 0)
        gcol = gcol_ref[...]
        lo = [jnp.zeros((sub, LANES), F32) for _ in range(NWORD)]
        hi = [jnp.zeros((sub, LANES), F32) for _ in range(NWORD)]
        for k in range(TOP_K):
            gk = gcol[:, k:k + 1]
            for c, w in enumerate(_load_row_words(z_ref.at[k], 0, sub)):
                lo[c] = lo[c] + gk * _unpack_lo(w)
                hi[c] = hi[c] + gk * _unpack_hi(w)
        routed = jnp.concatenate(lo + hi, axis=1)
        x = x_ref[...]
        shared = _swiglu(x.astype(BF), wsgu_ref[...], wsd_ref[...], SHARED_HIDDEN)
        o_ref[...] = _layer_norm(DN_ALPHA * x + routed + shared, g_ref[...], b_ref[...])


def _moe_experts(h2, packed, off, cnt, pos, gcol, wgu, wd, wsgu, wsd, g, b):
    T = h2.shape[0]
    tm, eb, ch, sub = MOE_TILE, EXPERTS_PER_STEP, EXPERT_CHUNK, COMBINE_SUB
    nt, nj, nsub = T // tm, N_EXPERTS // eb, tm // sub
    rows = TOP_K * tm + N_EXPERTS * ROW_ALIGN + ch
    hidden2 = wgu.shape[-1]
    wblk = lambda i, j, *_: (jnp.minimum(j, nj - 1), 0, 0)
    sub_i = lambda i, j: i * nsub + jnp.clip(j - nj, 0, nsub - 1)
    once = pl.Buffered(1)
    return pl.pallas_call(
        functools.partial(_moe_body, tm=tm, eb=eb, ch=ch, sub=sub, unroll=GATHER_UNROLL),
        grid_spec=pltpu.PrefetchScalarGridSpec(
            num_scalar_prefetch=2,
            grid=(nt, nj + nsub),
            in_specs=[
                pl.BlockSpec((tm * POS_STRIDE,), lambda i, j, *_: (i,), memory_space=pltpu.SMEM),
                pl.BlockSpec((tm * NWORD, LANES), lambda i, j, *_: (i, 0), pipeline_mode=once),
                pl.BlockSpec((sub, D_MODEL), lambda i, j, *_: (sub_i(i, j), 0)),
                pl.BlockSpec((sub, 8), lambda i, j, *_: (sub_i(i, j), 0)),
                pl.BlockSpec((eb, D_MODEL, hidden2), wblk),
                pl.BlockSpec((eb, hidden2 // 2, D_MODEL), wblk),
                pl.BlockSpec(wsgu.shape, lambda i, j, *_: (0, 0), pipeline_mode=once),
                pl.BlockSpec(wsd.shape, lambda i, j, *_: (0, 0), pipeline_mode=once),
                pl.BlockSpec((1, D_MODEL), lambda i, j, *_: (0, 0)),
                pl.BlockSpec((1, D_MODEL), lambda i, j, *_: (0, 0)),
            ],
            out_specs=pl.BlockSpec((sub, D_MODEL), lambda i, j, *_: (sub_i(i, j), 0)),
            scratch_shapes=[
                pltpu.VMEM((rows * NWORD, LANES), jnp.uint32),
                pltpu.VMEM((TOP_K, sub * NWORD, LANES), jnp.uint32),
            ],
        ),
        out_shape=jax.ShapeDtypeStruct((T, D_MODEL), F32),
        compiler_params=_cparams(("parallel", "arbitrary"), MOE_VMEM_LIMIT),
        name="moe_experts",
    )(off, cnt, pos, packed, h2, gcol, wgu, wd, wsgu, wsd, g, b)


S5_BATCH = 8
S5_STEPS = 64
S5_SLABS = 4
SLAB_CH = D_MODEL // S5_SLABS
SLAB_ST = SSM_GROUPS * SSM_STATE // S5_SLABS
N_STATE = SSM_GROUPS * SSM_STATE
SCAN_LANES = 1024
SCAN_UNROLL = 8


def _s5_disc_body(lre_ref, lim_ref, ldt_ref, bre_ref, bim_ref, are_ref, aim_ref, bbre_ref, bbim_ref):
    lre, lim = lre_ref[...], lim_ref[...]
    step = jnp.exp(ldt_ref[...])
    mag = jnp.exp(lre * step)
    a_re = mag * jnp.cos(lim * step)
    a_im = mag * jnp.sin(lim * step)
    den = lre * lre + lim * lim
    zoh_re = ((a_re - 1.0) * lre + a_im * lim) / den
    zoh_im = (a_im * lre - (a_re - 1.0) * lim) / den
    are_ref[...] = a_re
    aim_ref[...] = a_im
    bbre_ref[...] = zoh_re * bre_ref[...] - zoh_im * bim_ref[...]
    bbim_ref[...] = zoh_re * bim_ref[...] + zoh_im * bre_ref[...]


def _s5_discretize(lambda_re, lambda_im, log_dt, b_re, b_im):
    col = lambda a: a.reshape(N_STATE, 1)
    ldt = jnp.broadcast_to(log_dt[:, None], (SSM_GROUPS, SSM_STATE))
    mat = lambda a: a.reshape(N_STATE, SSM_GROUP)
    c1 = jax.ShapeDtypeStruct((N_STATE, 1), F32)
    c16 = jax.ShapeDtypeStruct((N_STATE, SSM_GROUP), F32)
    return pl.pallas_call(_s5_disc_body, out_shape=[c1, c1, c16, c16], name="s5_discretize")(
        col(lambda_re), col(lambda_im), col(ldt), mat(b_re), mat(b_im))


def _s5_body(x_ref, win_ref, are_ref, aim_ref, bd_ref, cd_ref, dsk_ref, wglu_ref, wout_ref, g_ref, b_ref,
             o_ref, p_ref, xs_ref, pk_ref, hre_ref, him_ref, sre_ref, sim_ref, *, lt):
    nb = S5_BATCH
    nlb = D_MODEL // LANES
    for c in range(nlb):
        for b in range(nb):
            xs_ref[c, pl.ds(b, lt, stride=nb), :] = x_ref[b, :, c * LANES:(c + 1) * LANES]
    x = jnp.concatenate([xs_ref[c] for c in range(nlb)], axis=1)
    u = _dot(x.astype(BF), win_ref[...])
    ub = u.astype(BF)
    for k in range(S5_SLABS):
        bu = _dot(ub[:, k * SLAB_CH:(k + 1) * SLAB_CH], bd_ref[k])
        hre_ref[:, k * SLAB_ST:(k + 1) * SLAB_ST] = bu[:, :SLAB_ST]
        him_ref[:, k * SLAB_ST:(k + 1) * SLAB_ST] = bu[:, SLAB_ST:]

    @pl.when(pl.program_id(1) == 0)
    def _():
        sre_ref[...] = jnp.zeros_like(sre_ref)
        sim_ref[...] = jnp.zeros_like(sim_ref)

    for c in range(N_STATE // SCAN_LANES):
        ls = slice(c * SCAN_LANES, (c + 1) * SCAN_LANES)
        a_re = jnp.broadcast_to(are_ref[:, ls], (nb, SCAN_LANES))
        a_im = jnp.broadcast_to(aim_ref[:, ls], (nb, SCAN_LANES))

        def steps(tb, state, ls=ls, a_re=a_re, a_im=a_im):
            s_re, s_im = state
            for uu in range(SCAN_UNROLL):
                r0 = pl.multiple_of((tb * SCAN_UNROLL + uu) * nb, nb)
                n_re = a_re * s_re - a_im * s_im + hre_ref[pl.ds(r0, nb), ls]
                n_im = a_re * s_im + a_im * s_re + him_ref[pl.ds(r0, nb), ls]
                hre_ref[pl.ds(r0, nb), ls] = n_re
                him_ref[pl.ds(r0, nb), ls] = n_im
                s_re, s_im = n_re, n_im
            return s_re, s_im

        s_re, s_im = lax.fori_loop(0, lt // SCAN_UNROLL, steps, (sre_ref[:, ls], sim_ref[:, ls]))
        sre_ref[:, ls] = s_re
        sim_ref[:, ls] = s_im

    ys = []
    for k in range(S5_SLABS):
        hk = jnp.concatenate([hre_ref[:, k * SLAB_ST:(k + 1) * SLAB_ST].astype(BF),
                              him_ref[:, k * SLAB_ST:(k + 1) * SLAB_ST].astype(BF)], axis=1)
        ys.append(_dot(hk, cd_ref[k]))
    y = jax.nn.gelu(jnp.concatenate(ys, axis=1) + dsk_ref[...] * u)
    y = y * jax.nn.sigmoid(_dot(y.astype(BF), wglu_ref[...]))
    mixed = _dot(y.astype(BF), wout_ref[...])
    res = _layer_norm(DN_ALPHA * x + mixed, g_ref[...], b_ref[...])
    for c in range(nlb):
        xs_ref[c] = res[:, c * LANES:(c + 1) * LANES]
    for c, w in enumerate(_pack_row_words(res)):
        pk_ref[c] = w
    for c in range(nlb):
        for b in range(nb):
            o_ref[b, :, c * LANES:(c + 1) * LANES] = xs_ref[c, pl.ds(b, lt, stride=nb), :]
    for c in range(NWORD):
        for b in range(nb):
            p_ref[b, pl.ds(c, lt, stride=NWORD), :] = pk_ref[c, pl.ds(b, lt, stride=nb), :]


def _mixer1(x3, w_in, lambda_re, lambda_im, b_re, b_im, c_re, c_im, d_skip, log_dt, w_glu, w_out,
            ln_g, ln_b):
    B, S, _ = x3.shape
    lt = S5_STEPS
    rows = S5_BATCH * lt
    a_re, a_im, bb_re, bb_im = _s5_discretize(lambda_re, lambda_im, log_dt, b_re, b_im)
    gps = SSM_GROUPS // S5_SLABS
    eye = jnp.eye(gps, dtype=F32)

    def bdiag(bb):
        b4 = bb.reshape(S5_SLABS, gps, SSM_STATE, SSM_GROUP)
        return jnp.einsum('kgph,gf->kghfp', b4, eye).reshape(S5_SLABS, SLAB_CH, SLAB_ST)

    def cdiag(cc):
        c4 = cc.reshape(S5_SLABS, gps, SSM_GROUP, SSM_STATE)
        return jnp.einsum('kghp,gf->kfpgh', c4, eye).reshape(S5_SLABS, SLAB_ST, SLAB_CH)

    bd = jnp.concatenate([bdiag(bb_re), bdiag(bb_im)], axis=2).astype(BF)
    cd = jnp.concatenate([cdiag(c_re), -cdiag(c_im)], axis=1).astype(BF)
    c2 = lambda shp: pl.BlockSpec(shp, lambda bi, ti: (0,) * len(shp))
    return pl.pallas_call(
        functools.partial(_s5_body, lt=lt),
        grid=(B // S5_BATCH, S // lt),
        in_specs=[
            pl.BlockSpec((S5_BATCH, lt, D_MODEL), lambda bi, ti: (bi, ti, 0)),
            c2((D_MODEL, D_MODEL)), c2((1, N_STATE)), c2((1, N_STATE)),
            c2(bd.shape), c2(cd.shape), c2((1, D_MODEL)),
            c2((D_MODEL, D_MODEL)), c2((D_MODEL, D_MODEL)), c2((1, D_MODEL)), c2((1, D_MODEL)),
        ],
        out_specs=[pl.BlockSpec((S5_BATCH, lt, D_MODEL), lambda bi, ti: (bi, ti, 0)),
                   pl.BlockSpec((S5_BATCH, lt * NWORD, LANES), lambda bi, ti: (bi, ti, 0))],
        out_shape=[jax.ShapeDtypeStruct((B, S, D_MODEL), F32),
                   jax.ShapeDtypeStruct((B, S * NWORD, LANES), jnp.uint32)],
        scratch_shapes=[
            pltpu.VMEM((D_MODEL // LANES, rows, LANES), F32),
            pltpu.VMEM((NWORD, rows, LANES), jnp.uint32),
            pltpu.VMEM((rows, N_STATE), F32),
            pltpu.VMEM((rows, N_STATE), F32),
            pltpu.VMEM((S5_BATCH, N_STATE), F32),
            pltpu.VMEM((S5_BATCH, N_STATE), F32),
        ],
        compiler_params=_cparams(("parallel", "arbitrary")),
        name="s5_mixer",
    )(x3, w_in.astype(BF), a_re.reshape(1, N_STATE), a_im.reshape(1, N_STATE), bd, cd,
      d_skip.reshape(1, D_MODEL), w_glu.astype(BF), w_out.astype(BF), ln_g.reshape(1, -1), ln_b.reshape(1, -1))


def _moe_block(h2, packed, router, router_bias, w_gate, w_up, w_down, sh_gate, sh_up, sh_down, ln_g, ln_b):
    pos, gate, meta = _router(h2, router.T, router_bias.reshape(-1, 1))
    off = meta[:, 0, :N_EXPERTS].reshape(-1)
    cnt = meta[:, 1, :N_EXPERTS].reshape(-1)
    wgu = jnp.concatenate([w_gate, w_up], axis=-1).astype(BF)
    wsgu = jnp.concatenate([sh_gate, sh_up], axis=-1).astype(BF)
    pos = (pos.T * NWORD).reshape(-1)
    return _moe_experts(h2, packed, off, cnt, pos, gate.T, wgu, w_down.astype(BF), wsgu, sh_down.astype(BF),
                        ln_g.reshape(1, -1), ln_b.reshape(1, -1))


def kernel(x, l0_w_in, l0_cmp_pos_k, l0_cmp_pos_v, l0_cmp_k_w1, l0_cmp_k_w2, l0_cmp_v_w1, l0_cmp_v_w2, l0_w_out, l0_ln1_g, l0_ln1_b, l0_router, l0_router_bias, l0_w_gate, l0_w_up, l0_w_down, l0_sh_gate, l0_sh_up, l0_sh_down, l0_ln2_g, l0_ln2_b, l1_w_in, l1_lambda_re, l1_lambda_im, l1_b_re, l1_b_im, l1_c_re, l1_c_im, l1_d, l1_log_dt, l1_w_glu, l1_w_out, l1_ln1_g, l1_ln1_b, l1_router, l1_router_bias, l1_w_gate, l1_w_up, l1_w_down, l1_sh_gate, l1_sh_up, l1_sh_down, l1_ln2_g, l1_ln2_b):
    B, S, D = x.shape
    assert D == D_MODEL and S % 1024 == 0 and B % S5_BATCH == 0 and (B * S) % MOE_TILE == 0
    T = B * S
    h, hp = _mixer0(x.reshape(T, D), B, S, l0_w_in, l0_cmp_pos_k, l0_cmp_pos_v, l0_cmp_k_w1, l0_cmp_k_w2,
                    l0_cmp_v_w1, l0_cmp_v_w2, l0_w_out, l0_ln1_g, l0_ln1_b)
    h = _moe_block(h, hp, l0_router, l0_router_bias, l0_w_gate, l0_w_up, l0_w_down, l0_sh_gate, l0_sh_up,
                   l0_sh_down, l0_ln2_g, l0_ln2_b)
    h, hp = _mixer1(h.reshape(B, S, D), l1_w_in, l1_lambda_re, l1_lambda_im, l1_b_re, l1_b_im, l1_c_re,
                    l1_c_im, l1_d, l1_log_dt, l1_w_glu, l1_w_out, l1_ln1_g, l1_ln1_b)
    h = _moe_block(h.reshape(T, D), hp.reshape(T * NWORD, LANES), l1_router, l1_router_bias, l1_w_gate, l1_w_up,
                   l1_w_down, l1_sh_gate, l1_sh_up, l1_sh_down, l1_ln2_g, l1_ln2_b)
    return h.reshape(B, S, D)
```

```python
import functools
import math

import numpy as np
import jax
import jax.numpy as jnp
from jax import lax
from jax.experimental import pallas as pl
from jax.experimental.pallas import tpu as pltpu

F32 = jnp.float32
BF = jnp.bfloat16

D_MODEL = 1024
DEPTH = 2
HEAD_DIM = 64
LANES = 128
SB_HEADS = 8
SB_WIDTH = SB_HEADS * HEAD_DIM
NSA_HEADS = 8
NSA_KV_GROUPS = 2
NSA_REP = NSA_HEADS // NSA_KV_GROUPS
NSA_WIDTH = NSA_HEADS * HEAD_DIM
NSA_N_BRANCH = 3
CMP_LEN = 32
CMP_STRIDE = 16
CMP_HIDDEN = 256
SLC_LEN = 64
SLC_TOPN = 8
SLC_LOCAL = 2
SLC_FORCE_BONUS = 1e4
WINDOW = 512
ROPE_THETA = 500000.0
ROPE_DIM = HEAD_DIM // 4
SSM_GROUP = 16
SSM_GROUPS = D_MODEL // SSM_GROUP
SSM_STATE = 64
N_EXPERTS = 64
N_EXPERT_GROUPS = 8
TOPK_GROUPS = 4
TOP_K = 6
EXPERT_HIDDEN = 256
SHARED_HIDDEN = 256
ROUTED_SCALE = 2.5
DN_ALPHA = (2 * DEPTH) ** 0.25
LN_EPS = 1e-5
NEG_INF = -1e30
ATT_SCALE = HEAD_DIM ** -0.5
LOG2_E = math.log2(math.e)
SB_UNDERFLOW = -104.0
WIN_Q = 128
SB_ROWS = 128

V7X_VMEM_BYTES = 64 * 1024 * 1024
VMEM_LIMIT = V7X_VMEM_BYTES - 8 * 1024 * 1024
MOE_VMEM_LIMIT = V7X_VMEM_BYTES - 4 * 1024 * 1024


def _cparams(sem, vmem=VMEM_LIMIT):
    return pltpu.CompilerParams(dimension_semantics=sem, vmem_limit_bytes=vmem)


def _dot(a, b):
    return jnp.dot(a, b, preferred_element_type=F32)


def _dot_nt(a, b):
    return lax.dot_general(a, b, (((1,), (1,)), ((), ())), preferred_element_type=F32)


def _split2(x):
    hi = x.astype(BF)
    lo = (x - hi.astype(F32)).astype(BF)
    return hi, lo


def _layer_norm(h, g, b):
    mu = jnp.mean(h, axis=-1, keepdims=True)
    d = h - mu
    var = jnp.mean(d * d, axis=-1, keepdims=True)
    return d * lax.rsqrt(var + LN_EPS) * g + b


HALF = D_MODEL // 2
NWORD = HALF // LANES
HI_MASK = 0xFFFF0000


def _pack_pairs(a, b):
    lo = pltpu.bitcast(a.astype(BF).astype(F32), jnp.uint32)
    hi = pltpu.bitcast(b.astype(BF).astype(F32), jnp.uint32)
    return (lo >> 16) | (hi & jnp.uint32(HI_MASK))


def _pack_row_words(h):
    return [_pack_pairs(h[:, c * LANES:(c + 1) * LANES], h[:, HALF + c * LANES:HALF + (c + 1) * LANES])
            for c in range(NWORD)]


def _store_row_words(ref, start, n, h):
    for c, w in enumerate(_pack_row_words(h)):
        ref[pl.ds(start * NWORD + c, n, stride=NWORD), :] = w


def _load_row_words(ref, start, n):
    return [ref[pl.ds(start * NWORD + c, n, stride=NWORD), :] for c in range(NWORD)]


def _unpack_lo(w):
    return pltpu.bitcast(w << 16, F32)


def _unpack_hi(w):
    return pltpu.bitcast(w & jnp.uint32(HI_MASK), F32)


N_PLAIN = 3 * SB_WIDTH + LANES
N_TRANS = 2 * LANES + 32
N_ROPE = NSA_WIDTH + 3 * LANES


def _proj0_body(x_ref, wp_ref, wt_ref, wr_ref, cos_ref, sin_ref,
                qkva_ref, vc_ref, vt_ref, gt_ref, qb_ref, kc_ref, ksw_ref):
    xb = x_ref[...].astype(BF)
    a0 = 3 * SB_WIDTH
    qkva_ref[...] = _dot(xb, wp_ref[:, 0:a0]).astype(BF)
    vc_ref[...] = _dot(xb, wp_ref[:, a0:N_PLAIN]).astype(BF)
    yt = _dot_nt(wt_ref[...], xb)
    vt_ref[0] = yt[0:2 * LANES].astype(BF)
    gt_ref[0] = jax.nn.sigmoid(yt[2 * LANES:N_TRANS])
    y = _dot(xb, wr_ref[...])
    half = ROPE_DIM // 2
    pos = lax.broadcasted_iota(jnp.int32, (1, N_ROPE), 1) % HEAD_DIM
    partner = jnp.where(pos < half, -pltpu.roll(y, N_ROPE - half, 1), pltpu.roll(y, half, 1))
    y = y * cos_ref[...] + partner * sin_ref[...]
    qb_ref[...] = y[:, 0:NSA_WIDTH].astype(BF)
    kc_ref[...] = y[:, NSA_WIDTH:NSA_WIDTH + LANES].astype(BF)
    ksw_ref[...] = y[:, NSA_WIDTH + LANES:N_ROPE].astype(BF)


def _proj0(x2, wp, wt, wr, cos_t, sin_t, batch, seq):
    T = x2.shape[0]
    tm = 1024
    nseq = seq // tm
    row = lambda i: (i, 0)
    full = lambda i: (0, 0)
    tab = lambda i: (i % nseq, 0)
    trans = lambda i: (i // nseq, 0, i % nseq)
    return pl.pallas_call(
        _proj0_body,
        grid=(T // tm,),
        in_specs=[
            pl.BlockSpec((tm, D_MODEL), row),
            pl.BlockSpec((D_MODEL, N_PLAIN), full),
            pl.BlockSpec((N_TRANS, D_MODEL), full),
            pl.BlockSpec((D_MODEL, N_ROPE), full),
            pl.BlockSpec((tm, N_ROPE), tab),
            pl.BlockSpec((tm, N_ROPE), tab),
        ],
        out_specs=[
            pl.BlockSpec((tm, 3 * SB_WIDTH), row),
            pl.BlockSpec((tm, LANES), row),
            pl.BlockSpec((1, 2 * LANES, tm), trans),
            pl.BlockSpec((1, N_TRANS - 2 * LANES, tm), trans),
            pl.BlockSpec((tm, NSA_WIDTH), row),
            pl.BlockSpec((tm, LANES), row),
            pl.BlockSpec((tm, 2 * LANES), row),
        ],
        out_shape=[
            jax.ShapeDtypeStruct((T, 3 * SB_WIDTH), BF),
            jax.ShapeDtypeStruct((T, LANES), BF),
            jax.ShapeDtypeStruct((batch, 2 * LANES, seq), BF),
            jax.ShapeDtypeStruct((batch, N_TRANS - 2 * LANES, seq), F32),
            jax.ShapeDtypeStruct((T, NSA_WIDTH), BF),
            jax.ShapeDtypeStruct((T, LANES), BF),
            jax.ShapeDtypeStruct((T, 2 * LANES), BF),
        ],
        compiler_params=_cparams(("parallel",)),
        name="proj0",
    )(x2, wp, wt, wr, cos_t, sin_t)


def _sb_body(q_ref, k_ref, v_ref, o_ref, *, tq):
    i = pl.program_id(2)
    q = q_ref[0]
    lane = lax.broadcasted_iota(jnp.int32, (1, LANES), 1)
    row = lax.broadcasted_iota(jnp.int32, (tq, tq), 0)
    col = lax.broadcasted_iota(jnp.int32, (tq, tq), 1)
    tri = jnp.where(row > col, 1.0, 0.0).astype(BF)
    diag_causal = col < row

    hms = [(lane // HEAD_DIM) == hh for hh in range(2)]
    nchunk = tq // SB_ROWS
    chains = [(hh, rc) for hh in range(2) for rc in range(nchunk)]
    qcs = [jnp.where(hms[hh], q[rc * SB_ROWS:(rc + 1) * SB_ROWS], jnp.zeros((SB_ROWS, LANES), BF))
           for hh, rc in chains]

    def blocks(specs, state):
        nc = len(qcs)
        kvs = []
        for j, _, _ in specs:
            off = pl.multiple_of(j * tq, tq)
            kvs.append((k_ref[0, pl.ds(off, tq), :], v_ref[0, pl.ds(off, tq), :]))
        zs = [[_dot_nt(qc, k) for qc in qcs] for k, _ in kvs]
        mids = []
        for zrow, (_, masks, flag) in zip(zs, specs):
            row = []
            for n, z in enumerate(zrow):
                nl = jnp.maximum(z, 0.0) + jnp.log(1.0 + jnp.exp(-jnp.abs(z)))
                if masks is not None:
                    nl = jnp.where(masks[n], nl, 0.0)
                if flag is not None:
                    nl = jnp.where(flag, nl, 0.0)
                row.append((nl, z, nl.astype(BF)))
            mids.append(row)
        sufs = [[_dot(hi, tri) for _, _, hi in row] for row in mids]
        carries = [state[2 * n] for n in range(nc)]
        ws = []
        for row, srow, (_, masks, flag) in zip(mids, sufs, specs):
            wrow = []
            for n, ((nl, z, _), suffix) in enumerate(zip(row, srow)):
                w = jnp.exp(z - nl - suffix + carries[n])
                if masks is not None:
                    w = jnp.where(masks[n], w, 0.0)
                if flag is not None:
                    w = jnp.where(flag, w, 0.0)
                wrow.append(w.astype(BF))
                carries[n] = carries[n] - jnp.sum(nl, axis=-1, keepdims=True)
            ws.append(wrow)
        new = []
        for n in range(nc):
            acc = state[2 * n + 1]
            for wrow, (_, v) in zip(ws, kvs):
                acc = acc + _dot(wrow[n], v)
            new.extend([carries[n], acc])
        return new

    zc, za = jnp.zeros((SB_ROWS, 1), F32), jnp.zeros((SB_ROWS, LANES), F32)
    diag_masks = [diag_causal[rc * SB_ROWS:(rc + 1) * SB_ROWS] for _, rc in chains]
    state = blocks([(i, diag_masks, None), (jnp.maximum(i - 1, 0), None, i > 0)], [zc, za] * len(chains))

    def live(state):
        top = jnp.max(state[0])
        for c in state[2::2]:
            top = jnp.maximum(top, jnp.max(c))
        return top > SB_UNDERFLOW

    def cond(st):
        return jnp.logical_and(st[0] >= 0, st[1])

    def body(st):
        new = blocks([(st[0], None, None)], st[2])
        return st[0] - 1, live(new), tuple(new)

    _, _, state = lax.while_loop(cond, body, (i - 2, live(state), tuple(state)))
    accs = [jnp.concatenate([state[2 * (hh * nchunk + rc) + 1] for rc in range(nchunk)], axis=0)
            for hh in range(2)]
    o_ref[0] = jnp.where(hms[0], accs[0], accs[1]).astype(BF)


def _sb_attention(qkva3):
    B, S, _ = qkva3.shape
    tq = 256
    npair = SB_WIDTH // LANES
    return pl.pallas_call(
        functools.partial(_sb_body, tq=tq),
        grid=(B, npair, S // tq),
        in_specs=[
            pl.BlockSpec((1, tq, LANES), lambda b, p, i: (b, i, p)),
            pl.BlockSpec((1, S, LANES), lambda b, p, i: (b, 0, npair + p)),
            pl.BlockSpec((1, S, LANES), lambda b, p, i: (b, 0, 2 * npair + p)),
        ],
        out_specs=pl.BlockSpec((1, tq, LANES), lambda b, p, i: (b, i, p)),
        out_shape=jax.ShapeDtypeStruct((B, S, SB_WIDTH), BF),
        compiler_params=_cparams(("parallel", "parallel", "arbitrary")),
        name="sb_attention",
    )(qkva3, qkva3, qkva3)


def _cmp_body(ak_ref, av_ref, posk_ref, posv_ref, w1k_ref, w1kt_ref, w1kb_ref, w2k_ref,
              w1v_ref, w1vt_ref, w1vb_ref, w2v_ref, kc_ref, vc_ref):
    def one(a_ref, pos_ref, w1_ref, w1t_ref, w1b_ref, w2_ref, o_ref):
        a = a_ref[0]
        n = a.shape[0]
        bias = _dot(pos_ref[...], w1_ref[...])[0:1]
        out = jnp.zeros((n, LANES), F32)
        for g in range(NSA_KV_GROUPS):
            top = _dot(a, w1t_ref[g])
            bot = _dot(a, w1b_ref[g])
            h = top + pltpu.roll(bot, n - 1, 0) + bias
            out = out + _dot(jax.nn.gelu(h).astype(BF), w2_ref[g])
        o_ref[0] = out.astype(BF)

    one(ak_ref, posk_ref, w1k_ref, w1kt_ref, w1kb_ref, w2k_ref, kc_ref)
    one(av_ref, posv_ref, w1v_ref, w1vt_ref, w1vb_ref, w2v_ref, vc_ref)


def _compress(ak, av, posk, posv, wk, wv):
    B, n, width = ak.shape
    blk = pl.BlockSpec((1, n, width), lambda b: (b, 0, 0))
    c2 = lambda shp: pl.BlockSpec(shp, lambda b: (0, 0))
    c3 = lambda shp: pl.BlockSpec(shp, lambda b: (0, 0, 0))
    wspecs = [c2((CMP_LEN * HEAD_DIM, CMP_HIDDEN)), c3((2, width, CMP_HIDDEN)),
              c3((2, width, CMP_HIDDEN)), c3((2, CMP_HIDDEN, LANES))]
    out = pl.BlockSpec((1, n, LANES), lambda b: (b, 0, 0))
    return pl.pallas_call(
        _cmp_body,
        grid=(B,),
        in_specs=[blk, blk, c2((8, CMP_LEN * HEAD_DIM)), c2((8, CMP_LEN * HEAD_DIM))] + wspecs + wspecs,
        out_specs=[out, out],
        out_shape=[jax.ShapeDtypeStruct((B, n, LANES), BF)] * 2,
        compiler_params=_cparams(("parallel",)),
        name="nsa_compress",
    )(ak, av, posk, posv, *wk, *wv)


def _nsa_body(q_ref, kc_ref, vct_ref, ks_ref, vst_ref, kw_ref, vwt_ref, gt_ref, ovl_ref, et_ref,
              o_ref, *, tq, tk, nblk):
    i = pl.program_id(1)
    R = NSA_REP
    nchain = NSA_KV_GROUPS * R
    t0 = i * tq
    lane = lax.broadcasted_iota(jnp.int32, (1, LANES), 1)
    q = q_ref[0]
    t_row = t0 + lax.broadcasted_iota(jnp.int32, (1, tq), 1)
    tile = lambda a: jnp.concatenate([a] * nchain, axis=1)
    n_col = lax.broadcasted_iota(jnp.int32, (LANES, 1), 0)
    cmp_bias = tile(jnp.where((CMP_STRIDE * n_col + (CMP_LEN - 1)) <= t_row, 0.0, NEG_INF))
    has_cmp = tile(jnp.where(t_row >= CMP_LEN - 1, 1.0, 0.0))
    nkb = (t0 + tq + tk - 1) // tk
    key_last = (nkb - 1) * tk + lax.broadcasted_iota(jnp.int32, (tk, 1), 0)
    diag_bias = tile(jnp.where(key_last <= t_row, 0.0, NEG_INF))
    blk = lax.broadcasted_iota(jnp.int32, (nblk, 1), 0)
    cur = t_row // SLC_LEN
    valid = blk <= cur
    forced = (blk == 0) | (valid & (blk > cur - SLC_LOCAL))
    ovl = ovl_ref[...]

    qrs = [jnp.where((lane // HEAD_DIM) == gi, q[:, LANES * r:LANES * (r + 1)], jnp.zeros((tq, LANES), BF))
           for gi in range(NSA_KV_GROUPS) for r in range(R)]
    q_all = jnp.concatenate(qrs, axis=0)

    s = _dot_nt(kc_ref[0], q_all) + cmp_bias
    e = jnp.exp2(s - jnp.max(s, axis=0, keepdims=True))
    pc = e / jnp.sum(e, axis=0, keepdims=True) * has_cmp
    o_cmp = _dot(vct_ref[0], pc.astype(BF))

    q_aug = []
    for gi in range(NSA_KV_GROUPS):
        psum = jnp.zeros((LANES, tq), F32)
        for r in range(R):
            c = gi * R + r
            psum = psum + pc[:, c * tq:(c + 1) * tq]
        p1 = psum.astype(BF)
        r1 = psum - p1.astype(F32)
        p2 = r1.astype(BF)
        p3 = (r1 - p2.astype(F32)).astype(BF)
        imp = _dot(ovl, p1) + _dot(ovl, p2) + _dot(ovl, p3)
        imp = jnp.where(forced, imp + SLC_FORCE_BONUS, jnp.where(valid, imp, NEG_INF))
        sel = jnp.zeros((nblk, tq), F32)
        for _ in range(min(SLC_TOPN, nblk)):
            mx = jnp.max(imp, axis=0, keepdims=True)
            idx = jnp.min(jnp.where(imp == mx, blk, nblk), axis=0, keepdims=True)
            hit = blk == idx
            sel = jnp.where(hit, 1.0, sel)
            imp = jnp.where(hit, -jnp.inf, imp)
        sel_t = jnp.concatenate([sel, jnp.zeros((LANES - nblk, tq), F32)], axis=0).T
        sel_bias = ((sel_t - 1.0) * -NEG_INF).astype(BF)
        q_aug.extend(jnp.concatenate([qrs[gi * R + r], sel_bias], axis=1) for r in range(R))
    qa_all = jnp.concatenate(q_aug, axis=0)

    def sel_step(j, state, bias):
        m, l, acc = state
        off = pl.multiple_of(j * tk, tk)
        ka = jnp.concatenate([ks_ref[0, pl.ds(off, tk), :], et_ref[pl.ds(off, tk), :]], axis=1)
        s = _dot_nt(ka, qa_all)
        if bias is not None:
            s = s + bias
        m_new = jnp.maximum(m, jnp.max(s, axis=0, keepdims=True))
        alpha = jnp.exp2(m - m_new)
        p = jnp.exp2(s - m_new)
        l = alpha * l + jnp.sum(p, axis=0, keepdims=True)
        acc = alpha * acc + _dot(vst_ref[0, :, pl.ds(off, tk)], p.astype(BF))
        return m_new, l, acc

    width = nchain * tq
    init = (jnp.full((1, width), NEG_INF, F32), jnp.zeros((1, width), F32), jnp.zeros((LANES, width), F32))
    state = lax.fori_loop(0, nkb - 1, lambda j, st: sel_step(j, st, None), init)
    _, l_s, acc_s = sel_step(nkb - 1, state, diag_bias)
    o_sel = acc_s / l_s

    wkeys = WINDOW + WIN_Q
    o_parts = []
    for h in range(tq // WIN_Q):
        th = t_row[:, h * WIN_Q:(h + 1) * WIN_Q]
        woff = pl.multiple_of(jnp.maximum(t0 + h * WIN_Q - WINDOW, 0), WIN_Q)
        kp = woff + lax.broadcasted_iota(jnp.int32, (wkeys, 1), 0)
        bias = tile(jnp.where(kp <= th, jnp.where(kp > th - WINDOW, 0.0, NEG_INF), NEG_INF))
        qh = jnp.concatenate([qr[h * WIN_Q:(h + 1) * WIN_Q] for qr in qrs], axis=0)
        s = _dot_nt(kw_ref[0, pl.ds(woff, wkeys), :], qh) + bias
        p = jnp.exp2(s - jnp.max(s, axis=0, keepdims=True))
        o_parts.append(_dot(vwt_ref[0, :, pl.ds(woff, wkeys)], p.astype(BF)) / jnp.sum(p, axis=0, keepdims=True))
    o_win = jnp.concatenate([o[:, c * WIN_Q:(c + 1) * WIN_Q] for c in range(nchain) for o in o_parts], axis=1)

    gt = gt_ref[0]
    sub = lax.broadcasted_iota(jnp.int32, (LANES, 1), 0)
    g0 = (sub // HEAD_DIM) == 0
    for r in range(R):
        out = jnp.zeros((LANES, tq), F32)
        for br, o_br in enumerate((o_cmp, o_sel, o_win)):
            rows = [(gi * R + r) * NSA_N_BRANCH + br for gi in range(NSA_KV_GROUPS)]
            gate = jnp.where(g0, gt[rows[0]:rows[0] + 1], gt[rows[1]:rows[1] + 1])
            both = jnp.where(g0, o_br[:, r * tq:(r + 1) * tq], o_br[:, (R + r) * tq:(R + r + 1) * tq])
            out = out + gate * both
        o_ref[0, r * LANES:(r + 1) * LANES, :] = out.astype(BF)


def _nsa_attention(qb3, kc, vct, ksw3, vsw_t, gt, ovl, expand):
    B, S, _ = qb3.shape
    tq, tk = 512, 512
    n = kc.shape[1]
    nblk = ovl.shape[0]
    ngate = gt.shape[1]
    assert S >= WINDOW + WIN_Q and tq % WIN_Q == 0 and S % tk == 0 and nblk % 8 == 0 and n == LANES
    tok = lambda c: pl.BlockSpec((1, S, LANES), lambda b, i, c=c: (b, 0, c))
    tr = lambda c: pl.BlockSpec((1, LANES, S), lambda b, i, c=c: (b, c, 0))
    return pl.pallas_call(
        functools.partial(_nsa_body, tq=tq, tk=tk, nblk=nblk),
        grid=(B, S // tq),
        in_specs=[
            pl.BlockSpec((1, tq, NSA_WIDTH), lambda b, i: (b, i, 0)),
            pl.BlockSpec((1, n, LANES), lambda b, i: (b, 0, 0)),
            pl.BlockSpec((1, LANES, n), lambda b, i: (b, 0, 0)),
            tok(0), tr(0), tok(1), tr(1),
            pl.BlockSpec((1, ngate, tq), lambda b, i: (b, 0, i)),
            pl.BlockSpec((nblk, LANES), lambda b, i: (0, 0)),
            pl.BlockSpec((S, LANES), lambda b, i: (0, 0)),
        ],
        out_specs=pl.BlockSpec((1, NSA_WIDTH, tq), lambda b, i: (b, 0, i)),
        out_shape=jax.ShapeDtypeStruct((B, NSA_WIDTH, S), BF),
        compiler_params=_cparams(("parallel", "arbitrary")),
        name="nsa_attention",
    )(qb3, kc, vct, ksw3, vsw_t, ksw3, vsw_t, gt, ovl, expand)


def _outproj_ln_body(oa_ref, obt_ref, x_ref, wa_ref, wb_ref, g_ref, b_ref, o_ref, p_ref):
    yb = lax.dot_general(obt_ref[0], wb_ref[...], (((0,), (0,)), ((), ())), preferred_element_type=F32)
    y = _dot(oa_ref[...], wa_ref[...]) + yb
    res = _layer_norm(DN_ALPHA * x_ref[...] + y, g_ref[...], b_ref[...])
    o_ref[...] = res
    _store_row_words(p_ref, 0, res.shape[0], res)


def _outproj_ln(oa, obt, x2, wa, wb, g, b):
    T = x2.shape[0]
    tm = 1024
    nseq = obt.shape[2] // tm
    row = lambda i: (i, 0)
    full = lambda i: (0, 0)
    return pl.pallas_call(
        _outproj_ln_body,
        grid=(T // tm,),
        in_specs=[
            pl.BlockSpec((tm, oa.shape[1]), row),
            pl.BlockSpec((1, obt.shape[1], tm), lambda i: (i // nseq, 0, i % nseq)),
            pl.BlockSpec((tm, D_MODEL), row),
            pl.BlockSpec(wa.shape, full),
            pl.BlockSpec(wb.shape, full),
            pl.BlockSpec((1, D_MODEL), full),
            pl.BlockSpec((1, D_MODEL), full),
        ],
        out_specs=[pl.BlockSpec((tm, D_MODEL), row),
                   pl.BlockSpec((tm * NWORD, LANES), row)],
        out_shape=[jax.ShapeDtypeStruct((T, D_MODEL), F32),
                   jax.ShapeDtypeStruct((T * NWORD, LANES), jnp.uint32)],
        compiler_params=_cparams(("parallel",)),
        name="outproj_ln",
    )(oa, obt, x2, wa, wb, g, b)


def _rope_tables(seq):
    inv = jnp.power(ROPE_THETA, -jnp.arange(0, ROPE_DIM, 2, dtype=F32) / ROPE_DIM)
    ang = jnp.arange(seq, dtype=F32)[:, None] * inv[None, :]
    half = ROPE_DIM // 2
    rest = HEAD_DIM - ROPE_DIM
    cos_h = jnp.concatenate([jnp.cos(ang), jnp.cos(ang), jnp.ones((seq, rest), F32)], axis=1)
    sin_h = jnp.concatenate([jnp.sin(ang), jnp.sin(ang), jnp.zeros((seq, rest), F32)], axis=1)
    reps = N_ROPE // HEAD_DIM
    del half
    return jnp.tile(cos_h, (1, reps)), jnp.tile(sin_h, (1, reps))


def _nsa_head_perm():
    cols = []
    for r in range(NSA_REP):
        for g in range(NSA_KV_GROUPS):
            h = g * NSA_REP + r
            cols.extend(range(h * HEAD_DIM, (h + 1) * HEAD_DIM))
    return np.asarray(cols)


def _mixer0(x2, batch, seq, w_in, cmp_pos_k, cmp_pos_v, cmp_k_w1, cmp_k_w2, cmp_v_w1, cmp_v_w2,
            w_out, ln_g, ln_b):
    T = x2.shape[0]
    sizes = [SB_WIDTH] * 3 + [NSA_WIDTH] + [NSA_KV_GROUPS * HEAD_DIM] * 6 + [NSA_HEADS * NSA_N_BRANCH]
    offs = np.concatenate([[0], np.cumsum(sizes)])
    col = lambda j: w_in[:, offs[j]:offs[j + 1]]
    perm = _nsa_head_perm()
    ngate = sizes[-1]
    wp = jnp.concatenate([col(0) * ATT_SCALE, col(1), col(2), col(5)], axis=1)
    wt = jnp.concatenate([col(7), col(9), jnp.pad(col(10), ((0, 0), (0, N_TRANS - 2 * LANES - ngate)))], axis=1).T
    wr = jnp.concatenate([col(3)[:, perm] * (ATT_SCALE * LOG2_E), col(4), col(6), col(8)], axis=1)
    cos_t, sin_t = _rope_tables(seq)
    qkva, vc_tok, vsw_t, gt, qb, kc_tok, ksw = _proj0(x2, wp.astype(BF), wt.astype(BF), wr.astype(BF), cos_t, sin_t,
                                                      batch, seq)

    r3 = lambda a: a.reshape(batch, seq, a.shape[-1])
    o_a = _sb_attention(r3(qkva))

    ncmp = seq // CMP_STRIDE
    ak = kc_tok.reshape(batch, ncmp, CMP_STRIDE * LANES)
    av = vc_tok.reshape(batch, ncmp, CMP_STRIDE * LANES)

    def cmp_weights(w1, w2):
        w1r = w1.reshape(2, CMP_STRIDE, HEAD_DIM, CMP_HIDDEN)
        tops, bots, w2s = [], [], []
        for g in range(NSA_KV_GROUPS):
            ext = jnp.zeros((2, CMP_STRIDE, NSA_KV_GROUPS, HEAD_DIM, CMP_HIDDEN), F32).at[:, :, g].set(w1r)
            ext = ext.reshape(2, CMP_STRIDE * LANES, CMP_HIDDEN)
            tops.append(ext[0])
            bots.append(ext[1])
            w2s.append(jnp.zeros((CMP_HIDDEN, LANES), F32).at[:, g * HEAD_DIM:(g + 1) * HEAD_DIM].set(w2))
        return (w1.astype(BF), jnp.stack(tops).astype(BF), jnp.stack(bots).astype(BF),
                jnp.stack(w2s).astype(BF))

    posk = jnp.broadcast_to(cmp_pos_k.reshape(1, -1), (8, CMP_LEN * HEAD_DIM)).astype(BF)
    posv = jnp.broadcast_to(cmp_pos_v.reshape(1, -1), (8, CMP_LEN * HEAD_DIM)).astype(BF)
    kc, vc = _compress(ak, av, posk, posv, cmp_weights(cmp_k_w1, cmp_k_w2), cmp_weights(cmp_v_w1, cmp_v_w2))
    assert ncmp <= LANES
    if ncmp < LANES:
        kc = jnp.pad(kc, ((0, 0), (0, LANES - ncmp), (0, 0)))
        vc = jnp.pad(vc, ((0, 0), (0, LANES - ncmp), (0, 0)))

    n_slc = seq // SLC_LEN
    cmp_start = np.arange(ncmp) * CMP_STRIDE
    slc_start = np.arange(n_slc) * SLC_LEN
    ovl = ((cmp_start[None, :] <= slc_start[:, None] + SLC_LEN - 1)
           & (cmp_start[None, :] + CMP_LEN - 1 >= slc_start[:, None])).astype(np.float32)
    ovl = np.pad(ovl, ((0, 0), (0, LANES - ncmp))) if ncmp < LANES else ovl
    expand = (np.arange(seq)[:, None] // SLC_LEN == np.arange(LANES)[None, :]).astype(np.float32)
    o_bt = _nsa_attention(r3(qb), kc, jnp.swapaxes(vc, 1, 2), r3(ksw), vsw_t, gt,
                          jnp.asarray(ovl, BF), jnp.asarray(expand, BF))

    wa = w_out[:SB_WIDTH].astype(BF)
    wb = w_out[SB_WIDTH:][perm].astype(BF)
    return _outproj_ln(o_a.reshape(T, SB_WIDTH), o_bt, x2, wa, wb,
                       ln_g.reshape(1, -1), ln_b.reshape(1, -1))


MOE_TILE = 2048
ROW_ALIGN = 8
RANK_CHUNK = 256


def _top_rows(vals, ids, n_ids, count):
    hits = []
    for _ in range(count):
        mx = jnp.max(vals, axis=0, keepdims=True)
        idx = jnp.min(jnp.where(vals == mx, ids, n_ids), axis=0, keepdims=True)
        hit = ids == idx
        hits.append(hit)
        vals = jnp.where(hit, -jnp.inf, vals)
    return hits


def _router_body(h_ref, rt_ref, rb_ref, pos_ref, gate_ref, meta_ref, *, tm):
    E, NG = N_EXPERTS, N_EXPERT_GROUPS
    per = E // NG
    hh, hl = _split2(h_ref[...])
    rh, rl = _split2(rt_ref[...])
    logits = _dot_nt(rh, hh) + _dot_nt(rh, hl) + _dot_nt(rl, hh)
    scores = jax.nn.sigmoid(logits)
    biased = scores + rb_ref[...]
    i8 = lax.broadcasted_iota(jnp.int32, (per, tm), 0)
    gs = []
    for g in range(NG):
        v = biased[g * per:(g + 1) * per]
        m1 = jnp.max(v, axis=0, keepdims=True)
        a1 = jnp.min(jnp.where(v == m1, i8, per), axis=0, keepdims=True)
        m2 = jnp.max(jnp.where(i8 == a1, -jnp.inf, v), axis=0, keepdims=True)
        gs.append(m1 + m2)
    gs = jnp.concatenate(gs, axis=0)
    gi = lax.broadcasted_iota(jnp.int32, (NG, tm), 0)
    ghits = _top_rows(gs, gi, NG, TOPK_GROUPS)
    gkeep = jnp.zeros((NG, tm), F32)
    for hit in ghits:
        gkeep = jnp.where(hit, 1.0, gkeep)
    ekeep = jnp.concatenate([jnp.broadcast_to(gkeep[g:g + 1], (per, tm)) for g in range(NG)], axis=0)
    ei = lax.broadcasted_iota(jnp.int32, (E, tm), 0)
    hits = _top_rows(jnp.where(ekeep > 0.5, biased, -jnp.inf), ei, E, TOP_K)
    gates = [jnp.sum(jnp.where(hit, scores, 0.0), axis=0, keepdims=True) for hit in hits]
    gsum = gates[0]
    for gk in gates[1:]:
        gsum = gsum + gk
    gates = [gk / gsum * ROUTED_SCALE for gk in gates]

    member = jnp.zeros((E, tm), F32)
    for hit in hits:
        member = jnp.where(hit, 1.0, member)
    cnt_col = jnp.sum(member, axis=1, keepdims=True)
    pad_col = jnp.floor((cnt_col + (ROW_ALIGN - 1)) * (1.0 / ROW_ALIGN)) * ROW_ALIGN
    sub_e = lax.broadcasted_iota(jnp.int32, (E, LANES), 0)
    lane_e = lax.broadcasted_iota(jnp.int32, (E, LANES), 1)
    cnt_row = jnp.sum(jnp.where(sub_e == lane_e, cnt_col, 0.0), axis=0, keepdims=True)
    pad_row = jnp.sum(jnp.where(sub_e == lane_e, pad_col, 0.0), axis=0, keepdims=True)
    off_row = jnp.sum(jnp.where(sub_e < lane_e, pad_col, 0.0), axis=0, keepdims=True)
    off_col = jnp.sum(jnp.where(lane_e < sub_e, pad_row, 0.0), axis=1, keepdims=True)
    r_i = lax.broadcasted_iota(jnp.int32, (RANK_CHUNK, RANK_CHUNK), 0)
    c_i = lax.broadcasted_iota(jnp.int32, (RANK_CHUNK, RANK_CHUNK), 1)
    before = jnp.where(r_i < c_i, 1.0, 0.0).astype(BF)
    running = off_col
    ranks = []
    for c in range(tm // RANK_CHUNK):
        mc = member[:, c * RANK_CHUNK:(c + 1) * RANK_CHUNK]
        ranks.append(_dot(mc.astype(BF), before) + running)
        running = running + jnp.sum(mc, axis=1, keepdims=True)
    slot = jnp.concatenate(ranks, axis=1)
    pos = [jnp.sum(jnp.where(hit, slot, 0.0), axis=0, keepdims=True) for hit in hits]
    zrow = jnp.zeros((1, tm), F32)
    pos_ref[...] = jnp.concatenate(pos + [zrow, zrow], axis=0).astype(jnp.int32)
    gate_ref[...] = jnp.concatenate(gates + [zrow, zrow], axis=0)
    z128 = jnp.zeros((1, LANES), F32)
    meta_ref[0] = jnp.concatenate([off_row, cnt_row] + [z128] * 6, axis=0).astype(jnp.int32)


def _router(h2, router_t, bias_col):
    T = h2.shape[0]
    tm = MOE_TILE
    nt = T // tm
    return pl.pallas_call(
        functools.partial(_router_body, tm=tm),
        grid=(nt,),
        in_specs=[
            pl.BlockSpec((tm, D_MODEL), lambda i: (i, 0)),
            pl.BlockSpec((N_EXPERTS, D_MODEL), lambda i: (0, 0)),
            pl.BlockSpec((N_EXPERTS, 1), lambda i: (0, 0)),
        ],
        out_specs=[
            pl.BlockSpec((8, tm), lambda i: (0, i)),
            pl.BlockSpec((8, tm), lambda i: (0, i)),
            pl.BlockSpec((1, 8, LANES), lambda i: (i, 0, 0)),
        ],
        out_shape=[
            jax.ShapeDtypeStruct((8, T), jnp.int32),
            jax.ShapeDtypeStruct((8, T), F32),
            jax.ShapeDtypeStruct((nt, 8, LANES), jnp.int32),
        ],
        compiler_params=_cparams(("parallel",)),
        name="moe_router",
    )(h2, router_t, bias_col)


EXPERTS_PER_STEP = 4
EXPERT_CHUNK = 256
COMBINE_SUB = 256
POS_STRIDE = 8
GATHER_UNROLL = 8


def _swiglu(xb, wgu, wd, hidden):
    gu = _dot(xb, wgu)
    a = jax.nn.silu(gu[:, :hidden]) * gu[:, hidden:]
    return _dot(a.astype(BF), wd)


def _moe_body(off_ref, cnt_ref, pos_ref, src_ref, x_ref, gcol_ref, wgu_ref, wd_ref, wsgu_ref, wsd_ref,
              g_ref, b_ref, o_ref, xs_ref, z_ref, *, tm, eb, ch, sub, unroll):
    i = pl.program_id(0)
    j = pl.program_id(1)
    nj = N_EXPERTS // eb

    @pl.when(j == 0)
    def _dispatch():
        zeros = jnp.zeros((ROW_ALIGN * NWORD, LANES), jnp.uint32)

        def pad(e, carry):
            off, cnt = off_ref[i * N_EXPERTS + e], cnt_ref[i * N_EXPERTS + e]
            last = pl.multiple_of((off + cnt // ROW_ALIGN * ROW_ALIGN) * NWORD, ROW_ALIGN * NWORD)
            xs_ref[pl.ds(last, ROW_ALIGN * NWORD), :] = zeros
            return carry

        lax.fori_loop(0, N_EXPERTS, pad, 0)
        end = off_ref[i * N_EXPERTS + N_EXPERTS - 1] + cnt_ref[i * N_EXPERTS + N_EXPERTS - 1]
        end = pl.multiple_of((end + ROW_ALIGN - 1) // ROW_ALIGN * ROW_ALIGN * NWORD, ROW_ALIGN * NWORD)
        xs_ref[pl.ds(end, ch * NWORD), :] = jnp.zeros((ch * NWORD, LANES), jnp.uint32)

        def tok(tb, carry):
            for u in range(unroll):
                t = tb * unroll + u
                slab = src_ref[pl.ds(pl.multiple_of(t * NWORD, NWORD), NWORD), :]
                for k in range(TOP_K):
                    p = pl.multiple_of(pos_ref[t * POS_STRIDE + k], NWORD)
                    xs_ref[pl.ds(p, NWORD), :] = slab
            return carry

        lax.fori_loop(0, tm // unroll, tok, 0)

    def chunks_in(els, offs, c):
        r0s = [pl.multiple_of(off + c * ch, ROW_ALIGN) for off in offs]
        words = [_load_row_words(xs_ref, r0, ch) for r0 in r0s]
        xbs = [jnp.concatenate([_unpack_lo(w).astype(BF) for w in ws]
                               + [_unpack_hi(w).astype(BF) for w in ws], axis=1) for ws in words]
        gus = [_dot(xb, wgu_ref[el]) for xb, el in zip(xbs, els)]
        acts = [(jax.nn.silu(gu[:, :EXPERT_HIDDEN]) * gu[:, EXPERT_HIDDEN:]).astype(BF) for gu in gus]
        ys = [_dot(a, wd_ref[el]) for a, el in zip(acts, els)]
        return list(zip(r0s, words, ys))

    def chunk_out(r0, words, y, c, cnt):
        keep = (c * ch + lax.broadcasted_iota(jnp.int32, (ch, 1), 0)) < cnt
        for cc, packed in enumerate(_pack_row_words(y)):
            xs_ref[pl.ds(r0 * NWORD + cc, ch, stride=NWORD), :] = jnp.where(keep, packed, words[cc])

    @pl.when(j < nj)
    def _experts():
        offs = [off_ref[i * N_EXPERTS + j * eb + el] for el in range(eb)]
        cnts = [cnt_ref[i * N_EXPERTS + j * eb + el] for el in range(eb)]
        firsts = chunks_in(list(range(eb)), offs, 0)
        for el in range(eb):
            chunk_out(*firsts[el], 0, cnts[el])
        for el in range(eb):
            def chunk(c, carry, el=el):
                chunk_out(*chunks_in([el], [offs[el]], c)[0], c, cnts[el])
                return carry

            lax.fori_loop(1, (cnts[el] + ch - 1) // ch, chunk, 0)

    @pl.when(j >= nj)
    def _combine():
        base = (j - nj) * sub

        def tok(tb, carry):
            for u in range(unroll):
                tl = tb * unroll + u
                dst = pl.multiple_of(tl * NWORD, NWORD)
                for k in range(TOP_K):
                    p = pl.multiple_of(pos_ref[(base + tl) * POS_STRIDE + k], NWORD)
                    z_ref[k, pl.ds(dst, NWORD), :] = xs_ref[pl.ds(p, NWORD), :]
            return carry

        lax.fori_loop(0, sub // unroll, tok, 0)
        gcol = gcol_ref[...]
        lo = [jnp.zeros((sub, LANES), F32) for _ in range(NWORD)]
        hi = [jnp.zeros((sub, LANES), F32) for _ in range(NWORD)]
        for k in range(TOP_K):
            gk = gcol[:, k:k + 1]
            for c, w in enumerate(_load_row_words(z_ref.at[k], 0, sub)):
                lo[c] = lo[c] + gk * _unpack_lo(w)
                hi[c] = hi[c] + gk * _unpack_hi(w)
        routed = jnp.concatenate(lo + hi, axis=1)
        x = x_ref[...]
        shared = _swiglu(x.astype(BF), wsgu_ref[...], wsd_ref[...], SHARED_HIDDEN)
        o_ref[...] = _layer_norm(DN_ALPHA * x + routed + shared, g_ref[...], b_ref[...])


def _moe_experts(h2, packed, off, cnt, pos, gcol, wgu, wd, wsgu, wsd, g, b):
    T = h2.shape[0]
    tm, eb, ch, sub = MOE_TILE, EXPERTS_PER_STEP, EXPERT_CHUNK, COMBINE_SUB
    nt, nj, nsub = T // tm, N_EXPERTS // eb, tm // sub
    rows = TOP_K * tm + N_EXPERTS * ROW_ALIGN + ch
    hidden2 = wgu.shape[-1]
    wblk = lambda i, j, *_: (jnp.minimum(j, nj - 1), 0, 0)
    sub_i = lambda i, j: i * nsub + jnp.clip(j - nj, 0, nsub - 1)
    once = pl.Buffered(1)
    return pl.pallas_call(
        functools.partial(_moe_body, tm=tm, eb=eb, ch=ch, sub=sub, unroll=GATHER_UNROLL),
        grid_spec=pltpu.PrefetchScalarGridSpec(
            num_scalar_prefetch=2,
            grid=(nt, nj + nsub),
            in_specs=[
                pl.BlockSpec((tm * POS_STRIDE,), lambda i, j, *_: (i,), memory_space=pltpu.SMEM),
                pl.BlockSpec((tm * NWORD, LANES), lambda i, j, *_: (i, 0)),
                pl.BlockSpec((sub, D_MODEL), lambda i, j, *_: (sub_i(i, j), 0)),
                pl.BlockSpec((sub, 8), lambda i, j, *_: (sub_i(i, j), 0)),
                pl.BlockSpec((eb, D_MODEL, hidden2), wblk),
                pl.BlockSpec((eb, hidden2 // 2, D_MODEL), wblk),
                pl.BlockSpec(wsgu.shape, lambda i, j, *_: (0, 0), pipeline_mode=once),
                pl.BlockSpec(wsd.shape, lambda i, j, *_: (0, 0), pipeline_mode=once),
                pl.BlockSpec((1, D_MODEL), lambda i, j, *_: (0, 0)),
                pl.BlockSpec((1, D_MODEL), lambda i, j, *_: (0, 0)),
            ],
            out_specs=pl.BlockSpec((sub, D_MODEL), lambda i, j, *_: (sub_i(i, j), 0)),
            scratch_shapes=[
                pltpu.VMEM((rows * NWORD, LANES), jnp.uint32),
                pltpu.VMEM((TOP_K, sub * NWORD, LANES), jnp.uint32),
            ],
        ),
        out_shape=jax.ShapeDtypeStruct((T, D_MODEL), F32),
        compiler_params=_cparams(("parallel", "arbitrary"), MOE_VMEM_LIMIT),
        name="moe_experts",
    )(off, cnt, pos, packed, h2, gcol, wgu, wd, wsgu, wsd, g, b)


S5_BATCH = 8
S5_STEPS = 64
S5_SLABS = 4
SLAB_CH = D_MODEL // S5_SLABS
SLAB_ST = SSM_GROUPS * SSM_STATE // S5_SLABS
N_STATE = SSM_GROUPS * SSM_STATE
SCAN_LANES = 1024
SCAN_UNROLL = 8


def _s5_disc_body(lre_ref, lim_ref, ldt_ref, bre_ref, bim_ref, are_ref, aim_ref, bbre_ref, bbim_ref):
    lre, lim = lre_ref[...], lim_ref[...]
    step = jnp.exp(ldt_ref[...])
    mag = jnp.exp(lre * step)
    a_re = mag * jnp.cos(lim * step)
    a_im = mag * jnp.sin(lim * step)
    den = lre * lre + lim * lim
    zoh_re = ((a_re - 1.0) * lre + a_im * lim) / den
    zoh_im = (a_im * lre - (a_re - 1.0) * lim) / den
    are_ref[...] = a_re
    aim_ref[...] = a_im
    bbre_ref[...] = zoh_re * bre_ref[...] - zoh_im * bim_ref[...]
    bbim_ref[...] = zoh_re * bim_ref[...] + zoh_im * bre_ref[...]


def _s5_discretize(lambda_re, lambda_im, log_dt, b_re, b_im):
    col = lambda a: a.reshape(N_STATE, 1)
    ldt = jnp.broadcast_to(log_dt[:, None], (SSM_GROUPS, SSM_STATE))
    mat = lambda a: a.reshape(N_STATE, SSM_GROUP)
    c1 = jax.ShapeDtypeStruct((N_STATE, 1), F32)
    c16 = jax.ShapeDtypeStruct((N_STATE, SSM_GROUP), F32)
    return pl.pallas_call(_s5_disc_body, out_shape=[c1, c1, c16, c16], name="s5_discretize")(
        col(lambda_re), col(lambda_im), col(ldt), mat(b_re), mat(b_im))


def _s5_body(x_ref, win_ref, are_ref, aim_ref, bd_ref, cd_ref, dsk_ref, wglu_ref, wout_ref, g_ref, b_ref,
             o_ref, p_ref, xs_ref, pk_ref, hre_ref, him_ref, sre_ref, sim_ref, *, lt):
    nb = S5_BATCH
    nlb = D_MODEL // LANES
    for c in range(nlb):
        for b in range(nb):
            xs_ref[c, pl.ds(b, lt, stride=nb), :] = x_ref[b, :, c * LANES:(c + 1) * LANES]
    x = jnp.concatenate([xs_ref[c] for c in range(nlb)], axis=1)
    u = _dot(x.astype(BF), win_ref[...])
    ub = u.astype(BF)
    for k in range(S5_SLABS):
        bu = _dot(ub[:, k * SLAB_CH:(k + 1) * SLAB_CH], bd_ref[k])
        hre_ref[:, k * SLAB_ST:(k + 1) * SLAB_ST] = bu[:, :SLAB_ST]
        him_ref[:, k * SLAB_ST:(k + 1) * SLAB_ST] = bu[:, SLAB_ST:]

    @pl.when(pl.program_id(1) == 0)
    def _():
        sre_ref[...] = jnp.zeros_like(sre_ref)
        sim_ref[...] = jnp.zeros_like(sim_ref)

    for c in range(N_STATE // SCAN_LANES):
        ls = slice(c * SCAN_LANES, (c + 1) * SCAN_LANES)
        a_re = jnp.broadcast_to(are_ref[:, ls], (nb, SCAN_LANES))
        a_im = jnp.broadcast_to(aim_ref[:, ls], (nb, SCAN_LANES))

        def steps(tb, state, ls=ls, a_re=a_re, a_im=a_im):
            s_re, s_im = state
            for uu in range(SCAN_UNROLL):
                r0 = pl.multiple_of((tb * SCAN_UNROLL + uu) * nb, nb)
                n_re = a_re * s_re - a_im * s_im + hre_ref[pl.ds(r0, nb), ls]
                n_im = a_re * s_im + a_im * s_re + him_ref[pl.ds(r0, nb), ls]
                hre_ref[pl.ds(r0, nb), ls] = n_re
                him_ref[pl.ds(r0, nb), ls] = n_im
                s_re, s_im = n_re, n_im
            return s_re, s_im

        s_re, s_im = lax.fori_loop(0, lt // SCAN_UNROLL, steps, (sre_ref[:, ls], sim_ref[:, ls]))
        sre_ref[:, ls] = s_re
        sim_ref[:, ls] = s_im

    ys = []
    for k in range(S5_SLABS):
        hk = jnp.concatenate([hre_ref[:, k * SLAB_ST:(k + 1) * SLAB_ST].astype(BF),
                              him_ref[:, k * SLAB_ST:(k + 1) * SLAB_ST].astype(BF)], axis=1)
        ys.append(_dot(hk, cd_ref[k]))
    y = jax.nn.gelu(jnp.concatenate(ys, axis=1) + dsk_ref[...] * u)
    y = y * jax.nn.sigmoid(_dot(y.astype(BF), wglu_ref[...]))
    mixed = _dot(y.astype(BF), wout_ref[...])
    res = _layer_norm(DN_ALPHA * x + mixed, g_ref[...], b_ref[...])
    for c in range(nlb):
        xs_ref[c] = res[:, c * LANES:(c + 1) * LANES]
    for c, w in enumerate(_pack_row_words(res)):
        pk_ref[c] = w
    for c in range(nlb):
        for b in range(nb):
            o_ref[b, :, c * LANES:(c + 1) * LANES] = xs_ref[c, pl.ds(b, lt, stride=nb), :]
    for c in range(NWORD):
        for b in range(nb):
            p_ref[b, pl.ds(c, lt, stride=NWORD), :] = pk_ref[c, pl.ds(b, lt, stride=nb), :]


def _mixer1(x3, w_in, lambda_re, lambda_im, b_re, b_im, c_re, c_im, d_skip, log_dt, w_glu, w_out,
            ln_g, ln_b):
    B, S, _ = x3.shape
    lt = S5_STEPS
    rows = S5_BATCH * lt
    a_re, a_im, bb_re, bb_im = _s5_discretize(lambda_re, lambda_im, log_dt, b_re, b_im)
    gps = SSM_GROUPS // S5_SLABS
    eye = jnp.eye(gps, dtype=F32)

    def bdiag(bb):
        b4 = bb.reshape(S5_SLABS, gps, SSM_STATE, SSM_GROUP)
        return jnp.einsum('kgph,gf->kghfp', b4, eye).reshape(S5_SLABS, SLAB_CH, SLAB_ST)

    def cdiag(cc):
        c4 = cc.reshape(S5_SLABS, gps, SSM_GROUP, SSM_STATE)
        return jnp.einsum('kghp,gf->kfpgh', c4, eye).reshape(S5_SLABS, SLAB_ST, SLAB_CH)

    bd = jnp.concatenate([bdiag(bb_re), bdiag(bb_im)], axis=2).astype(BF)
    cd = jnp.concatenate([cdiag(c_re), -cdiag(c_im)], axis=1).astype(BF)
    c2 = lambda shp: pl.BlockSpec(shp, lambda bi, ti: (0,) * len(shp))
    return pl.pallas_call(
        functools.partial(_s5_body, lt=lt),
        grid=(B // S5_BATCH, S // lt),
        in_specs=[
            pl.BlockSpec((S5_BATCH, lt, D_MODEL), lambda bi, ti: (bi, ti, 0)),
            c2((D_MODEL, D_MODEL)), c2((1, N_STATE)), c2((1, N_STATE)),
            c2(bd.shape), c2(cd.shape), c2((1, D_MODEL)),
            c2((D_MODEL, D_MODEL)), c2((D_MODEL, D_MODEL)), c2((1, D_MODEL)), c2((1, D_MODEL)),
        ],
        out_specs=[pl.BlockSpec((S5_BATCH, lt, D_MODEL), lambda bi, ti: (bi, ti, 0)),
                   pl.BlockSpec((S5_BATCH, lt * NWORD, LANES), lambda bi, ti: (bi, ti, 0))],
        out_shape=[jax.ShapeDtypeStruct((B, S, D_MODEL), F32),
                   jax.ShapeDtypeStruct((B, S * NWORD, LANES), jnp.uint32)],
        scratch_shapes=[
            pltpu.VMEM((D_MODEL // LANES, rows, LANES), F32),
            pltpu.VMEM((NWORD, rows, LANES), jnp.uint32),
            pltpu.VMEM((rows, N_STATE), F32),
            pltpu.VMEM((rows, N_STATE), F32),
            pltpu.VMEM((S5_BATCH, N_STATE), F32),
            pltpu.VMEM((S5_BATCH, N_STATE), F32),
        ],
        compiler_params=_cparams(("parallel", "arbitrary")),
        name="s5_mixer",
    )(x3, w_in.astype(BF), a_re.reshape(1, N_STATE), a_im.reshape(1, N_STATE), bd, cd,
      d_skip.reshape(1, D_MODEL), w_glu.astype(BF), w_out.astype(BF), ln_g.reshape(1, -1), ln_b.reshape(1, -1))


def _moe_block(h2, packed, router, router_bias, w_gate, w_up, w_down, sh_gate, sh_up, sh_down, ln_g, ln_b):
    pos, gate, meta = _router(h2, router.T, router_bias.reshape(-1, 1))
    off = meta[:, 0, :N_EXPERTS].reshape(-1)
    cnt = meta[:, 1, :N_EXPERTS].reshape(-1)
    wgu = jnp.concatenate([w_gate, w_up], axis=-1).astype(BF)
    wsgu = jnp.concatenate([sh_gate, sh_up], axis=-1).astype(BF)
    pos = (pos.T * NWORD).reshape(-1)
    return _moe_experts(h2, packed, off, cnt, pos, gate.T, wgu, w_down.astype(BF), wsgu, sh_down.astype(BF),
                        ln_g.reshape(1, -1), ln_b.reshape(1, -1))


def kernel(x, l0_w_in, l0_cmp_pos_k, l0_cmp_pos_v, l0_cmp_k_w1, l0_cmp_k_w2, l0_cmp_v_w1, l0_cmp_v_w2, l0_w_out, l0_ln1_g, l0_ln1_b, l0_router, l0_router_bias, l0_w_gate, l0_w_up, l0_w_down, l0_sh_gate, l0_sh_up, l0_sh_down, l0_ln2_g, l0_ln2_b, l1_w_in, l1_lambda_re, l1_lambda_im, l1_b_re, l1_b_im, l1_c_re, l1_c_im, l1_d, l1_log_dt, l1_w_glu, l1_w_out, l1_ln1_g, l1_ln1_b, l1_router, l1_router_bias, l1_w_gate, l1_w_up, l1_w_down, l1_sh_gate, l1_sh_up, l1_sh_down, l1_ln2_g, l1_ln2_b):
    B, S, D = x.shape
    assert D == D_MODEL and S % 1024 == 0 and B % S5_BATCH == 0 and (B * S) % MOE_TILE == 0
    T = B * S
    h, hp = _mixer0(x.reshape(T, D), B, S, l0_w_in, l0_cmp_pos_k, l0_cmp_pos_v, l0_cmp_k_w1, l0_cmp_k_w2,
                    l0_cmp_v_w1, l0_cmp_v_w2, l0_w_out, l0_ln1_g, l0_ln1_b)
    h = _moe_block(h, hp, l0_router, l0_router_bias, l0_w_gate, l0_w_up, l0_w_down, l0_sh_gate, l0_sh_up,
                   l0_sh_down, l0_ln2_g, l0_ln2_b)
    h, hp = _mixer1(h.reshape(B, S, D), l1_w_in, l1_lambda_re, l1_lambda_im, l1_b_re, l1_b_im, l1_c_re,
                    l1_c_im, l1_d, l1_log_dt, l1_w_glu, l1_w_out, l1_ln1_g, l1_ln1_b)
    h = _moe_block(h.reshape(T, D), hp.reshape(T * NWORD, LANES), l1_router, l1_router_bias, l1_w_gate, l1_w_up,
                   l1_w_down, l1_sh_gate, l1_sh_up, l1_sh_down, l1_ln2_g, l1_ln2_b)
    return h.reshape(B, S, D)
```
